```python
import jax, jax.numpy as jnp
from jax import lax
import numpy as np

D_MODEL = 1024
BATCH = 8
SEQ = 4096
DEPTH = 2
DEC_BATCH = 8
DEC_SEQ = 64
PAST_LEN = 2048

CHUNK = 64
N_EVEN = (DEPTH + 1) // 2
N_ODD = DEPTH // 2
PLE_DIM = 256
RMS_EPS = 1e-6
POOL_WINDOWS = (2, 4, 8, 16)
POOL_GROUPS = 4
POOL_WIDTH = D_MODEL // 2
POOL_GROUP_DIM = POOL_WIDTH // POOL_GROUPS
POOL_STATE = max(POOL_WINDOWS) - 1
GLA_HEADS = 4
GLA_WIDTH = D_MODEL // 2
GLA_DV = GLA_WIDTH // GLA_HEADS
GLA_DK = GLA_DV // 2
GLA_KEY_WIDTH = GLA_HEADS * GLA_DK
GLA_RANK = 16
GLA_TAU = 16.0
GLA_BLOCK = 16
IN_EVEN = POOL_WIDTH + 2 * GLA_KEY_WIDTH + 2 * GLA_WIDTH + GLA_RANK
MIX_EVEN = POOL_WIDTH + GLA_WIDTH
N_Q_HEADS = 16
N_KV_HEADS = 2
Q_PER_KV = N_Q_HEADS // N_KV_HEADS
HEAD_DIM = 64
WINDOW = 128
WIN_CHUNKS = WINDOW // CHUNK
ROPE_THETA = 10000.0
QKV_ODD = (N_Q_HEADS + 2 * N_KV_HEADS) * HEAD_DIM
MIX_ODD = N_Q_HEADS * HEAD_DIM
N_EXPERTS = 32
TOP_K = 4
D_EXPERT = D_MODEL
SWIGLU_LIMIT = 7.0
SWIGLU_ALPHA = 1.702
NEG_INF = -1e30

kernel_name = "hybrid_pool_gla_swa_moe_stream_step"


def rmsnorm(x, g):
    xf = x.astype(jnp.float32)
    y = xf * lax.rsqrt(jnp.mean(xf * xf, axis=-1, keepdims=True) + RMS_EPS)
    return (y * g.astype(jnp.float32)).astype(x.dtype)


def rope(x, pos):
    half = HEAD_DIM // 2
    inv = jnp.power(jnp.float32(ROPE_THETA), -jnp.arange(half, dtype=jnp.float32) / half)
    ang = pos.astype(jnp.float32)[:, None] * inv[None, :]
    cos = jnp.cos(ang)[None, :, None, :]
    sin = jnp.sin(ang)[None, :, None, :]
    xf = x.astype(jnp.float32)
    x1, x2 = xf[..., :half], xf[..., half:]
    return jnp.concatenate([x1 * cos - x2 * sin, x2 * cos + x1 * sin], axis=-1).astype(x.dtype)


def pool_mixer(u, prev, pos0, pool_w, pool_scale):
    bsz, t_len, _ = u.shape
    uf = u.astype(jnp.float32)
    ext = jnp.concatenate([prev.astype(jnp.float32), uf], axis=1)
    cs = jnp.concatenate([jnp.zeros((bsz, 1, POOL_WIDTH), jnp.float32), jnp.cumsum(ext, axis=1)], axis=1)
    hi = cs[:, POOL_STATE + 1:]
    pos = pos0 + jnp.arange(t_len)
    means = []
    for g, w in enumerate(POOL_WINDOWS):
        sl = slice(g * POOL_GROUP_DIM, (g + 1) * POOL_GROUP_DIM)
        lo = cs[:, POOL_STATE + 1 - w: POOL_STATE + 1 - w + t_len, sl]
        cnt = jnp.minimum(w, pos + 1).astype(jnp.float32)[None, :, None]
        means.append((hi[..., sl] - lo) / cnt)
    d = (jnp.concatenate(means, axis=-1) - uf).reshape(bsz, t_len, POOL_GROUPS, POOL_GROUP_DIM)
    y = jnp.einsum('btgc,gce->btge', d, pool_w.astype(jnp.float32)).reshape(bsz, t_len, POOL_WIDTH)
    y = y * pool_scale.astype(jnp.float32)
    new_prev = ext[:, -POOL_STATE:]
    return y.astype(u.dtype), new_prev.astype(u.dtype)


def gla_scan(q, k, v, g, s0):
    bsz, t_len = q.shape[:2]
    n_blk = -(-t_len // GLA_BLOCK)
    pad = n_blk * GLA_BLOCK - t_len

    def blocks(a):
        a = jnp.pad(a.astype(jnp.float32), ((0, 0), (0, pad), (0, 0), (0, 0)))
        return a.reshape(bsz, n_blk, GLA_BLOCK, a.shape[2], a.shape[3]).transpose(1, 0, 3, 2, 4)

    qb, kb, vb, gb = blocks(q), blocks(k), blocks(v), blocks(g)
    causal = jnp.tril(jnp.ones((GLA_BLOCK, GLA_BLOCK), dtype=bool))

    def step(s, blk):
        qc, kc, vc, gc = blk
        b = jnp.cumsum(gc, axis=2)
        b_last = b[:, :, -1:]
        q_t = qc * jnp.exp(b)
        k_t = kc * jnp.exp(-b)
        scores = jnp.where(causal, jnp.einsum('bhtk,bhsk->bhts', q_t, k_t), 0.0)
        o = jnp.einsum('bhtk,bhkv->bhtv', q_t, s) + jnp.einsum('bhts,bhsv->bhtv', scores, vc)
        k_dec = kc * jnp.exp(b_last - b)
        s_new = jnp.exp(b_last[:, :, 0])[..., None] * s + jnp.einsum('bhsk,bhsv->bhkv', k_dec, vc)
        return s_new, o

    s_fin, ob = lax.scan(step, s0.astype(jnp.float32), (qb, kb, vb, gb))
    o = ob.transpose(1, 0, 3, 2, 4).reshape(bsz, n_blk * GLA_BLOCK, GLA_HEADS, GLA_DV)[:, :t_len]
    return o, s_fin


def even_mixer(xn, prev_pool, s0, pos0, w_in, pool_w, pool_scale, gla_w_gate, gla_b_gate, gla_norm, w_out):
    bsz, t_len, _ = xn.shape
    proj = xn @ w_in
    sizes = [POOL_WIDTH, GLA_KEY_WIDTH, GLA_KEY_WIDTH, GLA_WIDTH, GLA_WIDTH]
    u, q, k, v, r, z = jnp.split(proj, np.cumsum(sizes).tolist(), axis=-1)
    y_pool, new_pool = pool_mixer(u, prev_pool, pos0, pool_w, pool_scale)
    log_alpha = jax.nn.log_sigmoid((z @ gla_w_gate + gla_b_gate).astype(jnp.float32)) / GLA_TAU
    q = q.reshape(bsz, t_len, GLA_HEADS, GLA_DK).astype(jnp.float32) * (GLA_DK ** -0.5)
    k = k.reshape(bsz, t_len, GLA_HEADS, GLA_DK)
    v = v.reshape(bsz, t_len, GLA_HEADS, GLA_DV)
    o, s_new = gla_scan(q, k, v, log_alpha.reshape(bsz, t_len, GLA_HEADS, GLA_DK), s0)
    o = o * lax.rsqrt(jnp.mean(o * o, axis=-1, keepdims=True) + RMS_EPS) * gla_norm.astype(jnp.float32)
    o = o.reshape(bsz, t_len, GLA_WIDTH) * jax.nn.silu(r.astype(jnp.float32))
    mix = jnp.concatenate([y_pool, o.astype(xn.dtype)], axis=-1) @ w_out
    return mix, new_pool, s_new


def sink_softmax(s, mask, sink):
    if mask is not None:
        s = jnp.where(mask, s, NEG_INF)
    m = jnp.maximum(jnp.max(s, axis=-1, keepdims=True), sink)
    p = jnp.exp(s - m)
    return p / (jnp.sum(p, axis=-1, keepdims=True) + jnp.exp(sink - m))


def banded_attention(q, k, v, sinks):
    bsz, t_len = q.shape[:2]
    n_c = t_len // CHUNK
    qb = q.reshape(bsz, n_c, CHUNK, N_KV_HEADS, Q_PER_KV, HEAD_DIM).astype(jnp.float32)

    def band(a):
        a = a.reshape(bsz, n_c, CHUNK, N_KV_HEADS, HEAD_DIM).astype(jnp.float32)
        a = jnp.pad(a, ((0, 0), (WIN_CHUNKS, 0), (0, 0), (0, 0), (0, 0)))
        return jnp.concatenate([a[:, i:i + n_c] for i in range(WIN_CHUNKS + 1)], axis=2)

    kb, vb = band(k), band(v)
    s = jnp.einsum('bcqgrd,bckgd->bcgrqk', qb, kb) * (HEAD_DIM ** -0.5)
    key_chunk = jnp.arange(n_c)[:, None] - WIN_CHUNKS + jnp.repeat(jnp.arange(WIN_CHUNKS + 1), CHUNK)[None, :]
    mask = (key_chunk >= 0)[None, :, None, None, None, :]
    sink = sinks.astype(jnp.float32).reshape(N_KV_HEADS, Q_PER_KV)[None, None, :, :, None, None]
    p = sink_softmax(s, mask, sink)
    o = jnp.einsum('bcgrqk,bckgd->bcqgrd', p, vb)
    return o.reshape(bsz, t_len, MIX_ODD)


def cached_attention(q, k, v, k_cache, v_cache, sinks):
    bsz, t_len = q.shape[:2]
    kk = jnp.concatenate([k_cache.astype(jnp.float32), k.astype(jnp.float32)], axis=1)
    vv = jnp.concatenate([v_cache.astype(jnp.float32), v.astype(jnp.float32)], axis=1)
    qh = q.reshape(bsz, t_len, N_KV_HEADS, Q_PER_KV, HEAD_DIM).astype(jnp.float32)
    s = jnp.einsum('btgrd,bkgd->bgrtk', qh, kk) * (HEAD_DIM ** -0.5)
    sink = sinks.astype(jnp.float32).reshape(N_KV_HEADS, Q_PER_KV)[None, :, :, None, None]
    p = sink_softmax(s, None, sink)
    o = jnp.einsum('bgrtk,bkgd->btgrd', p, vv)
    return o.reshape(bsz, t_len, MIX_ODD)


def odd_mixer(xn, k_cache, v_cache, pos0, w_qkv, b_qkv, sinks, w_out, b_out):
    bsz, t_len, _ = xn.shape
    qkv = xn @ w_qkv + b_qkv
    q, k, v = jnp.split(qkv, [MIX_ODD, MIX_ODD + N_KV_HEADS * HEAD_DIM], axis=-1)
    pos = pos0 + jnp.arange(t_len)
    q = rope(q.reshape(bsz, t_len, N_Q_HEADS, HEAD_DIM), pos)
    k = rope(k.reshape(bsz, t_len, N_KV_HEADS, HEAD_DIM), pos)
    v = v.reshape(bsz, t_len, N_KV_HEADS, HEAD_DIM)
    if k_cache is None:
        o = banded_attention(q, k, v, sinks)
        new_k, new_v = k[:, -WINDOW:], v[:, -WINDOW:]
    else:
        o = cached_attention(q, k, v, k_cache, v_cache, sinks)
        w_rows = k_cache.shape[1]
        new_k = jnp.concatenate([k_cache.astype(k.dtype), k], axis=1)[:, -w_rows:]
        new_v = jnp.concatenate([v_cache.astype(v.dtype), v], axis=1)[:, -w_rows:]
    mix = o.astype(xn.dtype) @ w_out + b_out
    return mix, new_k, new_v


def moe(xn, w_router, b_router, w_gate_up, b_gate_up, w_down, b_down):
    bsz, t_len, _ = xn.shape
    xf = xn.reshape(bsz * t_len, D_MODEL)
    logits = (xf @ w_router + b_router).astype(jnp.float32)
    top_val, top_idx = lax.top_k(logits, TOP_K)
    gates = jax.nn.softmax(top_val, axis=-1)
    comb = jnp.sum(jax.nn.one_hot(top_idx, N_EXPERTS, dtype=jnp.float32) * gates[..., None], axis=1)

    def expert(acc, ew):
        wgu, bgu, wd, bd, ce = ew
        hgu = (xf @ wgu + bgu).astype(jnp.float32)
        gate = jnp.minimum(hgu[:, :D_EXPERT], SWIGLU_LIMIT)
        up = jnp.clip(hgu[:, D_EXPERT:], -SWIGLU_LIMIT, SWIGLU_LIMIT)
        act = (up + 1.0) * gate * jax.nn.sigmoid(SWIGLU_ALPHA * gate)
        out = (act.astype(xf.dtype) @ wd + bd).astype(jnp.float32)
        return acc + ce[:, None] * out, None

    acc, _ = lax.scan(expert, jnp.zeros(xf.shape, jnp.float32), (w_gate_up, b_gate_up, w_down, b_down, comb.T))
    return acc.reshape(bsz, t_len, D_MODEL).astype(xn.dtype)


def trunk(x, p, pool_st, gla_st, k_st, v_st, pos0,
          norm_mix, norm_ffn, norm_final, w_in_even, pool_w, pool_scale, gla_w_gate, gla_b_gate,
          gla_norm, w_out_even, w_qkv_odd, b_qkv_odd, attn_sinks, w_out_odd, b_out_odd,
          w_router, b_router, w_gate_up, b_gate_up, w_down, b_down, ple_proj, ple_gate):
    h = x
    new_pool, new_gla, new_k, new_v = [], [], [], []
    for i in range(DEPTH):
        xn = rmsnorm(h, norm_mix[i])
        if i % 2 == 0:
            e = i // 2
            mix, sp, sg = even_mixer(xn, pool_st[e], gla_st[e], pos0, w_in_even[e], pool_w[e], pool_scale[e],
                                     gla_w_gate[e], gla_b_gate[e], gla_norm[e], w_out_even[e])
            new_pool.append(sp)
            new_gla.append(sg)
        else:
            o = i // 2
            kc = None if k_st is None else k_st[o]
            vc = None if v_st is None else v_st[o]
            mix, sk, sv = odd_mixer(xn, kc, vc, pos0, w_qkv_odd[o], b_qkv_odd[o], attn_sinks[o],
                                    w_out_odd[o], b_out_odd[o])
            new_k.append(sk)
            new_v.append(sv)
        h = h + mix
        h = h + moe(rmsnorm(h, norm_ffn[i]), w_router[i], b_router[i], w_gate_up[i], b_gate_up[i],
                    w_down[i], b_down[i])
        gate = jax.nn.sigmoid((h @ ple_gate[i]).astype(jnp.float32))
        emb = (p[i] @ ple_proj[i]).astype(jnp.float32)
        h = h + (gate * emb).astype(h.dtype)
    y = rmsnorm(h, norm_final)
    return y, jnp.stack(new_pool), jnp.stack(new_gla), jnp.stack(new_k), jnp.stack(new_v)


def setup_inputs(seed: int = 0) -> dict:
    key = jax.random.key(seed)
    ks = jax.random.split(key, 32)
    f32 = jnp.float32
    nrm = lambda k, shape, scale=1.0: jax.random.normal(k, shape, f32) * scale
    win_rows = min(WINDOW, PAST_LEN)
    return {
        "x_prompt": nrm(ks[0], (BATCH, SEQ, D_MODEL)),
        "x_sample": nrm(ks[1], (DEC_BATCH, DEC_SEQ, D_MODEL)),
        "state_pool": nrm(ks[2], (N_EVEN, DEC_BATCH, POOL_STATE, POOL_WIDTH)),
        "state_gla": nrm(ks[3], (N_EVEN, DEC_BATCH, GLA_HEADS, GLA_DK, GLA_DV), 0.1),
        "cache_k": nrm(ks[4], (N_ODD, DEC_BATCH, win_rows, N_KV_HEADS, HEAD_DIM)),
        "cache_v": nrm(ks[5], (N_ODD, DEC_BATCH, win_rows, N_KV_HEADS, HEAD_DIM)),
        "p_prompt": nrm(ks[6], (DEPTH, BATCH, SEQ, PLE_DIM)),
        "p_sample": nrm(ks[7], (DEPTH, DEC_BATCH, DEC_SEQ, PLE_DIM)),
        "norm_mix": 1.0 + nrm(ks[8], (DEPTH, D_MODEL), 0.05),
        "norm_ffn": 1.0 + nrm(ks[9], (DEPTH, D_MODEL), 0.05),
        "norm_final": 1.0 + nrm(ks[10], (D_MODEL,), 0.05),
        "w_in_even": nrm(ks[11], (N_EVEN, D_MODEL, IN_EVEN), D_MODEL ** -0.5),
        "pool_w": nrm(ks[12], (N_EVEN, POOL_GROUPS, POOL_GROUP_DIM, POOL_GROUP_DIM), POOL_GROUP_DIM ** -0.5),
        "pool_scale": 1.0 + nrm(ks[13], (N_EVEN, POOL_WIDTH), 0.1),
        "gla_w_gate": nrm(ks[14], (N_EVEN, GLA_RANK, GLA_KEY_WIDTH), GLA_RANK ** -0.5),
        "gla_b_gate": nrm(ks[15], (N_EVEN, GLA_KEY_WIDTH), 0.1),
        "gla_norm": 1.0 + nrm(ks[16], (N_EVEN, GLA_DV), 0.05),
        "w_out_even": nrm(ks[17], (N_EVEN, MIX_EVEN, D_MODEL), MIX_EVEN ** -0.5),
        "w_qkv_odd": nrm(ks[18], (N_ODD, D_MODEL, QKV_ODD), D_MODEL ** -0.5),
        "b_qkv_odd": nrm(ks[19], (N_ODD, QKV_ODD), 0.02),
        "attn_sinks": nrm(ks[20], (N_ODD, N_Q_HEADS), 0.5),
        "w_out_odd": nrm(ks[21], (N_ODD, MIX_ODD, D_MODEL), MIX_ODD ** -0.5),
        "b_out_odd": nrm(ks[22], (N_ODD, D_MODEL), 0.02),
        "w_router": nrm(ks[23], (DEPTH, D_MODEL, N_EXPERTS), D_MODEL ** -0.5),
        "b_router": nrm(ks[24], (DEPTH, N_EXPERTS), 0.01),
        "w_gate_up": nrm(ks[25], (DEPTH, N_EXPERTS, D_MODEL, 2 * D_EXPERT), D_MODEL ** -0.5),
        "b_gate_up": nrm(ks[26], (DEPTH, N_EXPERTS, 2 * D_EXPERT), 0.01),
        "w_down": nrm(ks[27], (DEPTH, N_EXPERTS, D_EXPERT, D_MODEL), D_EXPERT ** -0.5),
        "b_down": nrm(ks[28], (DEPTH, N_EXPERTS, D_MODEL), 0.01),
        "ple_proj": nrm(ks[29], (DEPTH, PLE_DIM, D_MODEL), PLE_DIM ** -0.5),
        "ple_gate": nrm(ks[30], (DEPTH, D_MODEL, D_MODEL), D_MODEL ** -0.5),
    }


def reference(x_prompt, x_sample, state_pool, state_gla, cache_k, cache_v, p_prompt, p_sample,
              norm_mix, norm_ffn, norm_final, w_in_even, pool_w, pool_scale, gla_w_gate, gla_b_gate,
              gla_norm, w_out_even, w_qkv_odd, b_qkv_odd, attn_sinks, w_out_odd, b_out_odd,
              w_router, b_router, w_gate_up, b_gate_up, w_down, b_down, ple_proj, ple_gate):
    bsz = x_prompt.shape[0]
    pool0 = jnp.zeros((N_EVEN, bsz, POOL_STATE, POOL_WIDTH), x_prompt.dtype)
    gla0 = jnp.zeros((N_EVEN, bsz, GLA_HEADS, GLA_DK, GLA_DV), jnp.float32)
    y_prompt, pool_p, gla_p, k_p, v_p = trunk(
        x_prompt, p_prompt, pool0, gla0, None, None, 0,
        norm_mix, norm_ffn, norm_final, w_in_even, pool_w, pool_scale, gla_w_gate, gla_b_gate,
        gla_norm, w_out_even, w_qkv_odd, b_qkv_odd, attn_sinks, w_out_odd, b_out_odd,
        w_router, b_router, w_gate_up, b_gate_up, w_down, b_down, ple_proj, ple_gate)
    y_sample, pool_s, gla_s, k_s, v_s = trunk(
        x_sample, p_sample, state_pool, state_gla, cache_k, cache_v, PAST_LEN,
        norm_mix, norm_ffn, norm_final, w_in_even, pool_w, pool_scale, gla_w_gate, gla_b_gate,
        gla_norm, w_out_even, w_qkv_odd, b_qkv_odd, attn_sinks, w_out_odd, b_out_odd,
        w_router, b_router, w_gate_up, b_gate_up, w_down, b_down, ple_proj, ple_gate)
    return (y_prompt, y_sample, pool_p, gla_p, k_p, v_p, pool_s, gla_s, k_s, v_s)
```

```python
import functools

import jax
import jax.numpy as jnp
from jax import lax
from jax.experimental import pallas as pl
from jax.experimental.pallas import tpu as pltpu

F32 = jnp.float32
MXU_DTYPE = jnp.bfloat16

V7X_VMEM_BYTES = 64 * 1024 * 1024
VMEM_LIMIT = (V7X_VMEM_BYTES * 3) // 4
LANES = 128

D_MODEL = 1024
CHUNK = 64
PAST_LEN = 2048
PLE_DIM = 256
RMS_EPS = 1e-6
POOL_WINDOWS = (2, 4, 8, 16)
POOL_WIDTH = 512
POOL_GROUP_DIM = 128
POOL_STATE = 15
GLA_HEADS = 4
GLA_DK = 64
GLA_DV = 128
GLA_KEY_WIDTH = GLA_HEADS * GLA_DK
GLA_WIDTH = GLA_HEADS * GLA_DV
GLA_RANK = 16
GLA_TAU = 16.0
GLA_BLOCK = 16
IN_EVEN = POOL_WIDTH + 2 * GLA_KEY_WIDTH + 2 * GLA_WIDTH + GLA_RANK
IN_EVEN_MAIN = IN_EVEN - GLA_RANK
PROJ_WIDTH = IN_EVEN_MAIN + GLA_KEY_WIDTH
N_Q_HEADS = 16
N_KV_HEADS = 2
HEAD_DIM = 64
WINDOW = 128
ROPE_THETA = 10000.0
Q_WIDTH = N_Q_HEADS * HEAD_DIM
KV_WIDTH = N_KV_HEADS * HEAD_DIM
N_EXPERTS = 32
TOP_K = 4
SWIGLU_LIMIT = 7.0
SWIGLU_ALPHA = 1.702
NEG_INF = -1e30

TOKEN_TILE = 512
EXPERT_TILE = 512
COMBINE_TILE = 256


def _dot(a, b):
    return jnp.dot(a.astype(MXU_DTYPE), b.astype(MXU_DTYPE), preferred_element_type=F32)


def _dot_nt(a, b):
    return lax.dot_general(a.astype(MXU_DTYPE), b.astype(MXU_DTYPE), (((1,), (1,)), ((), ())),
                           preferred_element_type=F32)


def _split3(x):
    x1 = x.astype(MXU_DTYPE)
    r1 = x - x1.astype(F32)
    x2 = r1.astype(MXU_DTYPE)
    x3 = (r1 - x2.astype(F32)).astype(MXU_DTYPE)
    return x1, x2, x3


def _rms(x, g):
    return x * lax.rsqrt(jnp.mean(x * x, axis=-1, keepdims=True) + RMS_EPS) * g


def _params(n_axes=1):
    return pltpu.CompilerParams(dimension_semantics=("arbitrary",) * n_axes, vmem_limit_bytes=VMEM_LIMIT)


def _full(shape):
    return pl.BlockSpec(shape, lambda *_: (0,) * len(shape))


class _Seq:
    def __init__(self, n_prompt_seq, prompt_len, n_sample_seq):
        self.tiles_per_seq = prompt_len // CHUNK
        self.n_prompt_seq = n_prompt_seq
        self.n_prompt_tiles = n_prompt_seq * self.tiles_per_seq
        self.n_tiles = self.n_prompt_tiles + n_sample_seq
        self.n_seq = n_prompt_seq + n_sample_seq

    def is_sample(self, i):
        return i >= self.n_prompt_tiles

    def tile_in_seq(self, i):
        return jnp.where(self.is_sample(i), 0, i % self.tiles_per_seq)

    def seq(self, i):
        return jnp.where(self.is_sample(i), self.n_prompt_seq + i - self.n_prompt_tiles, i // self.tiles_per_seq)

    def last(self, i):
        return jnp.logical_or(self.is_sample(i), i % self.tiles_per_seq == self.tiles_per_seq - 1)

    def pos0(self, i):
        return jnp.where(self.is_sample(i), PAST_LEN, self.tile_in_seq(i) * CHUNK)


def _even_in_kernel(h_ref, g_ref, w_ref, wg_ref, bg_ref, out_ref):
    xn = _rms(h_ref[...], g_ref[...])
    proj = _dot(xn, w_ref[...])
    z = proj[:, IN_EVEN_MAIN:]
    a = _dot(z, wg_ref[...]) + bg_ref[...]
    log_alpha = (jnp.minimum(a, 0.0) - jnp.log1p(jnp.exp(-jnp.abs(a)))) * (1.0 / GLA_TAU)
    out_ref[:, :IN_EVEN_MAIN] = proj[:, :IN_EVEN_MAIN]
    out_ref[:, IN_EVEN_MAIN:] = log_alpha


def _even_in(h, g, w_in, wg, bg):
    n = h.shape[0]
    wp = w_in.shape[1]
    return pl.pallas_call(
        _even_in_kernel,
        out_shape=jax.ShapeDtypeStruct((n, PROJ_WIDTH), F32),
        grid=(n // TOKEN_TILE,),
        in_specs=[pl.BlockSpec((TOKEN_TILE, D_MODEL), lambda i: (i, 0)), _full((1, D_MODEL)),
                  _full((D_MODEL, wp)), _full((wp - IN_EVEN_MAIN, GLA_KEY_WIDTH)), _full((1, GLA_KEY_WIDTH))],
        out_specs=pl.BlockSpec((TOKEN_TILE, PROJ_WIDTH), lambda i: (i, 0)),
        compiler_params=_params(), name="even_in")(h, g, w_in, wg, bg)


def _pool_kernel(u_ref, init_ref, pw_ref, ps_ref, y_ref, st_ref, buf, *, sq):
    i = pl.program_id(0)

    @pl.when(sq.tile_in_seq(i) == 0)
    def _():
        buf[0:16, :] = init_ref[0]

    u = u_ref[...]
    buf[16:16 + CHUNK, :] = u
    pos = sq.pos0(i) + lax.broadcasted_iota(jnp.int32, (CHUNK, 1), 0)
    for g, w in enumerate(POOL_WINDOWS):
        sl = slice(g * POOL_GROUP_DIM, (g + 1) * POOL_GROUP_DIM)
        acc = u[:, sl]
        for j in range(1, w):
            acc = acc + buf[16 - j:16 - j + CHUNK, sl]
        cnt = jnp.minimum(w, pos + 1).astype(F32)
        d = acc / cnt - u[:, sl]
        y_ref[:, sl] = _dot(d, pw_ref[g]) * ps_ref[:, sl]
    tail = buf[CHUNK:CHUNK + 16, :]
    st_ref[0] = tail
    buf[0:16, :] = tail


def _pool(proj, init, pool_w, pool_scale, sq):
    n = proj.shape[0]
    return pl.pallas_call(
        functools.partial(_pool_kernel, sq=sq),
        out_shape=(jax.ShapeDtypeStruct((n, POOL_WIDTH), F32), jax.ShapeDtypeStruct((sq.n_seq, 16, POOL_WIDTH), F32)),
        grid=(sq.n_tiles,),
        in_specs=[pl.BlockSpec((CHUNK, POOL_WIDTH), lambda i: (i, 0)),
                  pl.BlockSpec((1, 16, POOL_WIDTH), lambda i: (sq.seq(i), 0, 0)),
                  _full((len(POOL_WINDOWS), POOL_GROUP_DIM, POOL_GROUP_DIM)), _full((1, POOL_WIDTH))],
        out_specs=(pl.BlockSpec((CHUNK, POOL_WIDTH), lambda i: (i, 0)),
                   pl.BlockSpec((1, 16, POOL_WIDTH), lambda i: (sq.seq(i), 0, 0))),
        scratch_shapes=[pltpu.VMEM((CHUNK + 16, POOL_WIDTH), F32)],
        compiler_params=_params(), name="pool")(proj, init, pool_w, pool_scale)


def _gla_kernel(q_ref, k_ref, v_ref, g_ref, s0_ref, o_ref, sout_ref, state, *, sq):
    i = pl.program_id(0)
    n_blk = CHUNK // GLA_BLOCK

    @pl.when(sq.tile_in_seq(i) == 0)
    def _():
        state[...] = s0_ref[0]

    g = g_ref[...]
    row = lax.broadcasted_iota(jnp.int32, (CHUNK, CHUNK), 0)
    col = lax.broadcasted_iota(jnp.int32, (CHUNK, CHUNK), 1)
    same = (row >> 4) == (col >> 4)
    causal = jnp.logical_and(same, col <= row)
    tri = jnp.where(causal, 1.0, 0.0).astype(MXU_DTYPE)
    ones = jnp.where(same, 1.0, 0.0).astype(MXU_DTYPE)
    g1, g2, g3 = _split3(g)
    b = _dot(tri, g1) + _dot(tri, g2) + _dot(tri, g3)
    b_last = _dot(ones, g1) + _dot(ones, g2) + _dot(ones, g3)
    q_t = q_ref[...] * (GLA_DK ** -0.5) * jnp.exp(b)
    k = k_ref[...]
    k_t = k * jnp.exp(-b)
    k_dec_t = (k * jnp.exp(b_last - b)).T
    sel = jnp.where((lax.broadcasted_iota(jnp.int32, (CHUNK, LANES), 0) >> 4)
                    == lax.broadcasted_iota(jnp.int32, (CHUNK, LANES), 1), 1.0, 0.0).astype(MXU_DTYPE)
    t1, t2, t3 = _split3(g.T)
    blk_decay = jnp.exp(_dot(t1, sel) + _dot(t2, sel) + _dot(t3, sel))
    v = v_ref[...]
    blk_of_col = lax.broadcasted_iota(jnp.int32, (GLA_DK, CHUNK), 1) >> 4
    for h in range(GLA_HEADS):
        ks = slice(h * GLA_DK, (h + 1) * GLA_DK)
        vs = slice(h * GLA_DV, (h + 1) * GLA_DV)
        vh = v[:, vs]
        scores = jnp.where(causal, _dot_nt(q_t[:, ks], k_t[:, ks]), 0.0)
        o_intra = _dot(scores, vh)
        s = state[h]
        o_inter = []
        for j in range(n_blk):
            rows = slice(j * GLA_BLOCK, (j + 1) * GLA_BLOCK)
            o_inter.append(_dot(q_t[rows, ks], s))
            kd = jnp.where(blk_of_col == j, k_dec_t[ks, :], 0.0)
            s = blk_decay[ks, j:j + 1] * s + _dot(kd, vh)
        state[h] = s
        o_ref[:, vs] = o_intra + jnp.concatenate(o_inter, axis=0)

    @pl.when(sq.last(i))
    def _():
        sout_ref[0] = state[...]


def _gla(proj, s0, sq):
    n = proj.shape[0]
    st_shape = (sq.n_seq, GLA_HEADS, GLA_DK, GLA_DV)
    st_spec = pl.BlockSpec((1, GLA_HEADS, GLA_DK, GLA_DV), lambda i: (sq.seq(i), 0, 0, 0))
    kw = GLA_KEY_WIDTH
    return pl.pallas_call(
        functools.partial(_gla_kernel, sq=sq),
        out_shape=(jax.ShapeDtypeStruct((n, GLA_WIDTH), F32), jax.ShapeDtypeStruct(st_shape, F32)),
        grid=(sq.n_tiles,),
        in_specs=[pl.BlockSpec((CHUNK, kw), lambda i: (i, POOL_WIDTH // kw)),
                  pl.BlockSpec((CHUNK, kw), lambda i: (i, POOL_WIDTH // kw + 1)),
                  pl.BlockSpec((CHUNK, GLA_WIDTH), lambda i: (i, (POOL_WIDTH + 2 * kw) // GLA_WIDTH)),
                  pl.BlockSpec((CHUNK, kw), lambda i: (i, IN_EVEN_MAIN // kw)),
                  st_spec],
        out_specs=(pl.BlockSpec((CHUNK, GLA_WIDTH), lambda i: (i, 0)), st_spec),
        scratch_shapes=[pltpu.VMEM((GLA_HEADS, GLA_DK, GLA_DV), F32)],
        compiler_params=_params(), name="gla")(proj, proj, proj, proj, s0)


def _route(h1, nffn_ref, wr_hi_ref, wr_lo_ref, br_ref, xn_ref, info_ref, cnt_ref, carry):
    tm = h1.shape[0]

    @pl.when(pl.program_id(0) == 0)
    def _():
        carry[...] = jnp.zeros_like(carry)

    xn = _rms(h1, nffn_ref[...])
    xn_ref[...] = xn
    x_hi = xn.astype(MXU_DTYPE)
    x_lo = (xn - x_hi.astype(F32)).astype(MXU_DTYPE)
    logits = (_dot(x_hi, wr_hi_ref[...]) + _dot(x_lo, wr_hi_ref[...]) + _dot(x_hi, wr_lo_ref[...])
              + br_ref[...])
    lane = lax.broadcasted_iota(jnp.int32, (tm, LANES), 1)
    vals, ids, hots = [], [], []
    for _ in range(TOP_K):
        m = jnp.max(logits, axis=-1, keepdims=True)
        ix = jnp.min(jnp.where(logits == m, lane, LANES), axis=-1, keepdims=True)
        hot = lane == ix
        vals.append(m)
        ids.append(ix)
        hots.append(hot)
        logits = jnp.where(hot, -jnp.inf, logits)
    es = [jnp.exp(v - vals[0]) for v in vals]
    den = es[0] + es[1] + es[2] + es[3]
    chosen = jnp.zeros((tm, LANES), F32)
    for hot in hots:
        chosen = chosen + jnp.where(hot, 1.0, 0.0)
    before = (lax.broadcasted_iota(jnp.int32, (tm, tm), 1) < lax.broadcasted_iota(jnp.int32, (tm, tm), 0))
    rank = _dot(jnp.where(before, 1.0, 0.0), chosen) + carry[...]
    info = jnp.zeros((tm, LANES), F32)
    for k in range(TOP_K):
        pos = jnp.sum(jnp.where(hots[k], rank, 0.0), axis=-1, keepdims=True)
        info = jnp.where(lane == k, es[k] / den, info)
        info = jnp.where(lane == TOP_K + k, ids[k].astype(F32), info)
        info = jnp.where(lane == 2 * TOP_K + k, pos, info)
    info_ref[...] = info
    carry[...] = carry[...] + jnp.sum(chosen, axis=0, keepdims=True)
    cnt_ref[...] = carry[...]


_ROUTE_OUT_SHAPES = lambda n: (jax.ShapeDtypeStruct((n, D_MODEL), F32), jax.ShapeDtypeStruct((n, D_MODEL), F32),
                               jax.ShapeDtypeStruct((n, LANES), F32), jax.ShapeDtypeStruct((1, LANES), F32))
_ROUTE_OUT_SPECS = (pl.BlockSpec((TOKEN_TILE, D_MODEL), lambda i: (i, 0)),
                    pl.BlockSpec((TOKEN_TILE, D_MODEL), lambda i: (i, 0)),
                    pl.BlockSpec((TOKEN_TILE, LANES), lambda i: (i, 0)),
                    pl.BlockSpec((1, LANES), lambda i: (0, 0)))


def _route_in_specs():
    return [_full((1, D_MODEL)), _full((D_MODEL, LANES)), _full((D_MODEL, LANES)), _full((1, LANES))]


def _even_out_kernel(yp_ref, o_ref, r_ref, h_ref, gn_ref, w_ref, nffn_ref, wr_hi_ref, wr_lo_ref, br_ref,
                     h1_ref, xn_ref, info_ref, cnt_ref, carry):
    o = o_ref[...]
    r = r_ref[...]
    parts = []
    for hd in range(GLA_HEADS):
        sl = slice(hd * GLA_DV, (hd + 1) * GLA_DV)
        oh = o[:, sl]
        oh = oh * lax.rsqrt(jnp.mean(oh * oh, axis=-1, keepdims=True) + RMS_EPS) * gn_ref[...]
        rh = r[:, sl]
        parts.append(oh * (rh * jax.nn.sigmoid(rh)))
    gla = jnp.concatenate(parts, axis=1)
    mix = _dot(yp_ref[...], w_ref[:POOL_WIDTH, :]) + _dot(gla, w_ref[POOL_WIDTH:, :])
    h1 = h_ref[...] + mix
    h1_ref[...] = h1
    _route(h1, nffn_ref, wr_hi_ref, wr_lo_ref, br_ref, xn_ref, info_ref, cnt_ref, carry)


def _even_out(y_pool, o_gla, proj, h, gla_norm, w_out, nffn, wr_hi, wr_lo, br):
    n = h.shape[0]
    row = lambda w, c: pl.BlockSpec((TOKEN_TILE, w), lambda i: (i, c))
    return pl.pallas_call(
        _even_out_kernel,
        out_shape=_ROUTE_OUT_SHAPES(n),
        grid=(n // TOKEN_TILE,),
        in_specs=[row(POOL_WIDTH, 0), row(GLA_WIDTH, 0), row(GLA_WIDTH, (IN_EVEN_MAIN - GLA_WIDTH) // GLA_WIDTH),
                  row(D_MODEL, 0), _full((1, GLA_DV)), _full((D_MODEL, D_MODEL))] + _route_in_specs(),
        out_specs=_ROUTE_OUT_SPECS,
        scratch_shapes=[pltpu.VMEM((1, LANES), F32)],
        compiler_params=_params(), name="even_out")(y_pool, o_gla, proj, h, gla_norm, w_out, nffn, wr_hi, wr_lo, br)


def _rope_tile(x, cos, sin, lo_half):
    swapped = jnp.where(lo_half, pltpu.roll(x, LANES - HEAD_DIM // 2, axis=1), pltpu.roll(x, HEAD_DIM // 2, axis=1))
    return x * cos + swapped * sin


def _odd_in_kernel(h_ref, g_ref, w_ref, b_ref, cos_ref, sin_ref, q_ref, kv_ref):
    xn = _rms(h_ref[...], g_ref[...])
    qkv = _dot(xn, w_ref[...]) + b_ref[...]
    cos = cos_ref[...]
    sin = sin_ref[...]
    lo_half = (lax.broadcasted_iota(jnp.int32, cos.shape, 1) % HEAD_DIM) < HEAD_DIM // 2
    for j in range(Q_WIDTH // LANES):
        sl = slice(j * LANES, (j + 1) * LANES)
        q_ref[:, sl] = _rope_tile(qkv[:, sl], cos, sin, lo_half).astype(q_ref.dtype)
    kv_ref[:, :KV_WIDTH] = _rope_tile(qkv[:, Q_WIDTH:Q_WIDTH + KV_WIDTH], cos, sin, lo_half)
    kv_ref[:, KV_WIDTH:] = qkv[:, Q_WIDTH + KV_WIDTH:]


def _odd_in(h, g, w_qkv, b_qkv, cos, sin, table_block):
    n = h.shape[0]
    wq = w_qkv.shape[1]
    tab = pl.BlockSpec((TOKEN_TILE, LANES), lambda i: (table_block(i), 0))
    return pl.pallas_call(
        _odd_in_kernel,
        out_shape=(jax.ShapeDtypeStruct((n, Q_WIDTH), MXU_DTYPE), jax.ShapeDtypeStruct((n, 2 * KV_WIDTH), F32)),
        grid=(n // TOKEN_TILE,),
        in_specs=[pl.BlockSpec((TOKEN_TILE, D_MODEL), lambda i: (i, 0)), _full((1, D_MODEL)),
                  _full((D_MODEL, wq)), _full((1, wq)), tab, tab],
        out_specs=(pl.BlockSpec((TOKEN_TILE, Q_WIDTH), lambda i: (i, 0)),
                   pl.BlockSpec((TOKEN_TILE, 2 * KV_WIDTH), lambda i: (i, 0))),
        compiler_params=_params(), name="odd_in")(h, g, w_qkv, b_qkv, cos, sin)


def _attn_kernel(sink_ref, q_ref, kv0_ref, kv1_ref, kv2_ref, ck0_ref, ck1_ref, cv0_ref, cv1_ref, o_ref, *, sq):
    i = pl.program_id(0)
    smp = sq.is_sample(i)
    t = sq.tile_in_seq(i)
    kv0 = kv0_ref[...]
    k_old = jnp.where(smp, ck0_ref[...], kv2_ref[:, :KV_WIDTH])
    k_mid = jnp.where(smp, ck1_ref[...], kv1_ref[:, :KV_WIDTH])
    v_old = jnp.where(smp, cv0_ref[...], kv2_ref[:, KV_WIDTH:])
    v_mid = jnp.where(smp, cv1_ref[...], kv1_ref[:, KV_WIDTH:])
    pad = jnp.zeros((CHUNK, KV_WIDTH), F32)
    keys = jnp.concatenate([k_old, k_mid, kv0[:, :KV_WIDTH], pad], axis=0)
    vals = jnp.concatenate([v_old, v_mid, kv0[:, KV_WIDTH:], pad], axis=0)
    n_keys = 4 * CHUNK
    lane = lax.broadcasted_iota(jnp.int32, (n_keys, KV_WIDTH), 1)
    lo = lane < HEAD_DIM
    keys_sw = pltpu.roll(keys, HEAD_DIM, axis=1)
    vals_sw = pltpu.roll(vals, HEAD_DIM, axis=1)
    kcol = lax.broadcasted_iota(jnp.int32, (1, n_keys), 1)
    first_valid = jnp.where(smp, 0, (2 - jnp.minimum(t, 2)) * CHUNK)
    key_ok = jnp.logical_and(kcol >= first_valid, kcol < 3 * CHUNK)
    for g in range(N_KV_HEADS):
        own, other = (keys, keys_sw) if g == 0 else (keys_sw, keys)
        vown, vother = (vals, vals_sw) if g == 0 else (vals_sw, vals)
        kb = jnp.concatenate([jnp.where(lo, own, 0.0), jnp.where(lo, 0.0, other)], axis=0).astype(MXU_DTYPE)
        vb = jnp.concatenate([jnp.where(lo, vown, 0.0), jnp.where(lo, 0.0, vother)], axis=0).astype(MXU_DTYPE)
        for pr in range(N_Q_HEADS // N_KV_HEADS // 2):
            hp = g * (N_Q_HEADS // N_KV_HEADS // 2) + pr
            sl = slice(hp * LANES, (hp + 1) * LANES)
            s = _dot_nt(q_ref[:, sl], kb) * (HEAD_DIM ** -0.5)
            halves = []
            for half in range(2):
                sh = jnp.where(key_ok, s[:, half * n_keys:(half + 1) * n_keys], NEG_INF)
                sink = sink_ref[2 * hp + half]
                m = jnp.maximum(jnp.max(sh, axis=-1, keepdims=True), sink)
                p = jnp.exp(sh - m)
                halves.append(p / (jnp.sum(p, axis=-1, keepdims=True) + jnp.exp(sink - m)))
            o_ref[:, sl] = _dot(jnp.concatenate(halves, axis=1), vb).astype(o_ref.dtype)


def _attention(sinks, q, kv, ck, cv, sq):
    n = q.shape[0]
    npt = sq.n_prompt_tiles
    prev = lambda d: (lambda i, s: (jnp.where(sq.is_sample(i), i, jnp.maximum(i - d, 0)), 0))
    cache = lambda d: (lambda i, s: (jnp.where(sq.is_sample(i), 2 * (i - npt) + d, 0), 0))
    kvspec = lambda f: pl.BlockSpec((CHUNK, 2 * KV_WIDTH), f)
    cspec = lambda f: pl.BlockSpec((CHUNK, KV_WIDTH), f)
    grid_spec = pltpu.PrefetchScalarGridSpec(
        num_scalar_prefetch=1, grid=(sq.n_tiles,),
        in_specs=[pl.BlockSpec((CHUNK, Q_WIDTH), lambda i, s: (i, 0)),
                  kvspec(prev(0)), kvspec(prev(1)), kvspec(prev(2)),
                  cspec(cache(0)), cspec(cache(1)), cspec(cache(0)), cspec(cache(1))],
        out_specs=pl.BlockSpec((CHUNK, Q_WIDTH), lambda i, s: (i, 0)))
    return pl.pallas_call(
        functools.partial(_attn_kernel, sq=sq),
        out_shape=jax.ShapeDtypeStruct((n, Q_WIDTH), MXU_DTYPE),
        grid_spec=grid_spec, compiler_params=_params(), name="attention")(sinks, q, kv, kv, kv, ck, ck, cv, cv)


def _odd_out_kernel(o_ref, h_ref, w_ref, b_ref, nffn_ref, wr_hi_ref, wr_lo_ref, br_ref,
                    h1_ref, xn_ref, info_ref, cnt_ref, carry):
    h1 = h_ref[...] + _dot(o_ref[...], w_ref[...]) + b_ref[...]
    h1_ref[...] = h1
    _route(h1, nffn_ref, wr_hi_ref, wr_lo_ref, br_ref, xn_ref, info_ref, cnt_ref, carry)


def _odd_out(o, h, w_out, b_out, nffn, wr_hi, wr_lo, br):
    n = h.shape[0]
    row = pl.BlockSpec((TOKEN_TILE, D_MODEL), lambda i: (i, 0))
    return pl.pallas_call(
        _odd_out_kernel,
        out_shape=_ROUTE_OUT_SHAPES(n),
        grid=(n // TOKEN_TILE,),
        in_specs=[row, row, _full((D_MODEL, D_MODEL)), _full((1, D_MODEL))] + _route_in_specs(),
        out_specs=_ROUTE_OUT_SPECS,
        scratch_shapes=[pltpu.VMEM((1, LANES), F32)],
        compiler_params=_params(), name="odd_out")(o, h, w_out, b_out, nffn, wr_hi, wr_lo, br)


def _row_copy(src, s, dst, d, sem):
    return pltpu.make_async_copy(src.at[pl.ds(s, 1)], dst.at[pl.ds(d, 1)], sem)


def _dispatch_kernel(dest_hbm, x_ref, xs_in, xs_out, dest_smem, sem, dsem):
    del xs_in
    i = pl.program_id(0)
    n_slots = TOKEN_TILE * TOP_K
    cp = pltpu.make_async_copy(dest_hbm.at[pl.ds(i * n_slots, n_slots)], dest_smem, dsem)
    cp.start()
    cp.wait()

    def issue(r, c):
        _row_copy(x_ref, r // TOP_K, xs_out, dest_smem[r], sem).start()
        return c

    lax.fori_loop(0, n_slots, issue, 0)

    def drain(r, c):
        _row_copy(x_ref, 0, xs_out, 0, sem).wait()
        return c

    lax.fori_loop(0, n_slots, drain, 0)


def _dispatch(dest, xn, xs_zero):
    n = xn.shape[0]
    return pl.pallas_call(
        _dispatch_kernel,
        out_shape=jax.ShapeDtypeStruct(xs_zero.shape, F32),
        grid=(n // TOKEN_TILE,),
        in_specs=[pl.BlockSpec(memory_space=pl.ANY), pl.BlockSpec((TOKEN_TILE, D_MODEL), lambda i: (i, 0)),
                  pl.BlockSpec(memory_space=pl.ANY)],
        out_specs=pl.BlockSpec(memory_space=pl.ANY),
        scratch_shapes=[pltpu.SMEM((TOKEN_TILE * TOP_K,), jnp.int32), pltpu.SemaphoreType.DMA, pltpu.SemaphoreType.DMA],
        input_output_aliases={2: 0},
        compiler_params=_params(), name="moe_dispatch")(dest, xn, xs_zero)


def _experts_kernel(te_ref, tv_ref, x_ref, wgu_ref, bgu_ref, wd_ref, bd_ref, y_ref):
    i = pl.program_id(0)

    @pl.when(tv_ref[i] != 0)
    def _():
        hgu = _dot(x_ref[...], wgu_ref[0]) + bgu_ref[0]
        gate = jnp.minimum(hgu[:, :D_MODEL], SWIGLU_LIMIT)
        up = jnp.clip(hgu[:, D_MODEL:], -SWIGLU_LIMIT, SWIGLU_LIMIT)
        act = (up + 1.0) * gate * jax.nn.sigmoid(SWIGLU_ALPHA * gate)
        y_ref[...] = _dot(act, wd_ref[0]) + bd_ref[0]

    @pl.when(tv_ref[i] == 0)
    def _():
        y_ref[...] = jnp.zeros_like(y_ref)


def _experts(tile_expert, tile_valid, xs, wgu, bgu, wd, bd):
    rows = xs.shape[0]
    grid_spec = pltpu.PrefetchScalarGridSpec(
        num_scalar_prefetch=2, grid=(rows // EXPERT_TILE,),
        in_specs=[pl.BlockSpec((EXPERT_TILE, D_MODEL), lambda i, te, tv: (i, 0)),
                  pl.BlockSpec((1, D_MODEL, 2 * D_MODEL), lambda i, te, tv: (te[i], 0, 0)),
                  pl.BlockSpec((1, 1, 2 * D_MODEL), lambda i, te, tv: (te[i], 0, 0)),
                  pl.BlockSpec((1, D_MODEL, D_MODEL), lambda i, te, tv: (te[i], 0, 0)),
                  pl.BlockSpec((1, 1, D_MODEL), lambda i, te, tv: (te[i], 0, 0))],
        out_specs=pl.BlockSpec((EXPERT_TILE, D_MODEL), lambda i, te, tv: (i, 0)))
    return pl.pallas_call(
        _experts_kernel, out_shape=jax.ShapeDtypeStruct((rows, D_MODEL), F32),
        grid_spec=grid_spec, compiler_params=_params(), name="moe_experts")(tile_expert, tile_valid, xs, wgu, bgu, wd, bd)


def _combine_kernel(dest_hbm, info_ref, h_ref, p_ref, pp_ref, pg_ref, nf_ref, ys_ref, o_ref,
                    buf, dest_smem, sem, dsem, *, final):
    i = pl.program_id(0)
    n_slots = COMBINE_TILE * TOP_K
    cp = pltpu.make_async_copy(dest_hbm.at[pl.ds(i * n_slots, n_slots)], dest_smem, dsem)
    cp.start()
    cp.wait()

    def issue(r, c):
        _row_copy(ys_ref, dest_smem[r], buf.at[r % TOP_K], r // TOP_K, sem).start()
        return c

    lax.fori_loop(0, n_slots, issue, 0)

    def drain(r, c):
        _row_copy(ys_ref, 0, buf.at[0], 0, sem).wait()
        return c

    lax.fori_loop(0, n_slots, drain, 0)
    gates = info_ref[...]
    moe = gates[:, 0:1] * buf[0]
    for k in range(1, TOP_K):
        moe = moe + gates[:, k:k + 1] * buf[k]
    h2 = h_ref[...] + moe
    h3 = h2 + jax.nn.sigmoid(_dot(h2, pg_ref[...])) * _dot(p_ref[...], pp_ref[...])
    o_ref[...] = _rms(h3, nf_ref[...]) if final else h3


def _combine(dest, info, h1, p, ple_proj, ple_gate, norm_final, ys, final):
    n = h1.shape[0]
    row = lambda w: pl.BlockSpec((COMBINE_TILE, w), lambda i: (i, 0))
    return pl.pallas_call(
        functools.partial(_combine_kernel, final=final),
        out_shape=jax.ShapeDtypeStruct((n, D_MODEL), F32),
        grid=(n // COMBINE_TILE,),
        in_specs=[pl.BlockSpec(memory_space=pl.ANY), row(LANES), row(D_MODEL), row(PLE_DIM),
                  _full((PLE_DIM, D_MODEL)), _full((D_MODEL, D_MODEL)), _full((1, D_MODEL)),
                  pl.BlockSpec(memory_space=pl.ANY)],
        out_specs=row(D_MODEL),
        scratch_shapes=[pltpu.VMEM((TOP_K, COMBINE_TILE, D_MODEL), F32), pltpu.SMEM((COMBINE_TILE * TOP_K,), jnp.int32),
                        pltpu.SemaphoreType.DMA, pltpu.SemaphoreType.DMA],
        compiler_params=_params(), name="moe_combine")(dest, info, h1, p, ple_proj, ple_gate, norm_final, ys)


def _moe_and_embed(h1, xn, info, counts, p, wgu, bgu, wd, bd, ple_proj, ple_gate, norm_final, final):
    n = h1.shape[0]
    n_tiles = (n * TOP_K + N_EXPERTS * (EXPERT_TILE - 1)) // EXPERT_TILE + 1
    ids = info[:, TOP_K:2 * TOP_K].astype(jnp.int32)
    rank = info[:, 2 * TOP_K:3 * TOP_K].astype(jnp.int32)
    cnt = counts[0, :N_EXPERTS].astype(jnp.int32)
    padded = ((cnt + EXPERT_TILE - 1) // EXPERT_TILE) * EXPERT_TILE
    ends = jnp.cumsum(padded)
    dest = ((ends - padded)[ids] + rank).reshape(-1)
    tile_start = jnp.arange(n_tiles, dtype=jnp.int32) * EXPERT_TILE
    tile_expert = jnp.minimum(jnp.searchsorted(ends, tile_start, side="right"), N_EXPERTS - 1).astype(jnp.int32)
    tile_valid = (tile_start < ends[-1]).astype(jnp.int32)
    xs = _dispatch(dest, xn, jnp.zeros((n_tiles * EXPERT_TILE, D_MODEL), F32))
    ys = _experts(tile_expert, tile_valid, xs, wgu, bgu, wd, bd)
    return _combine(dest, info, h1, p, ple_proj, ple_gate, norm_final, ys, final)


def _rope_tables(prompt_len, n_sample_seq, sample_len):
    half = HEAD_DIM // 2
    inv = jnp.power(jnp.float32(ROPE_THETA), -jnp.arange(half, dtype=F32) / half)
    pos = jnp.concatenate([jnp.arange(prompt_len), jnp.tile(PAST_LEN + jnp.arange(sample_len), n_sample_seq)])
    ang = pos.astype(F32)[:, None] * inv[None, :]
    cos = jnp.tile(jnp.cos(ang), (1, LANES // half))
    sin = jnp.tile(jnp.concatenate([-jnp.sin(ang), jnp.sin(ang)], axis=1), (1, LANES // HEAD_DIM))
    return cos, sin


def kernel(x_prompt, x_sample, state_pool, state_gla, cache_k, cache_v, p_prompt, p_sample, norm_mix, norm_ffn, norm_final, w_in_even, pool_w, pool_scale, gla_w_gate, gla_b_gate, gla_norm, w_out_even, w_qkv_odd, b_qkv_odd, attn_sinks, w_out_odd, b_out_odd, w_router, b_router, w_gate_up, b_gate_up, w_down, b_down, ple_proj, ple_gate):
    bsz, t_len, _ = x_prompt.shape
    dec_bsz, dec_len, _ = x_sample.shape
    depth = norm_mix.shape[0]
    n_p, n_s = bsz * t_len, dec_bsz * dec_len
    n = n_p + n_s
    assert dec_len == CHUNK and n_s == TOKEN_TILE and t_len % TOKEN_TILE == 0 and cache_k.shape[2] == WINDOW
    sq = _Seq(bsz, t_len, dec_bsz)
    bf = lambda a: a.astype(MXU_DTYPE)
    row = lambda a: a.reshape(1, -1)

    h = jnp.concatenate([x_prompt.reshape(n_p, D_MODEL), x_sample.reshape(n_s, D_MODEL)], axis=0)
    p_all = jnp.concatenate([p_prompt.reshape(depth, n_p, PLE_DIM), p_sample.reshape(depth, n_s, PLE_DIM)], axis=1)
    cos, sin = _rope_tables(t_len, dec_bsz, dec_len)
    tiles_per_seq = t_len // TOKEN_TILE
    table_block = lambda i: jnp.where(i < bsz * tiles_per_seq, i % tiles_per_seq, tiles_per_seq)

    pools, glas, new_k, new_v = [], [], [], []
    for i in range(depth):
        if i % 2 == 0:
            e = i // 2
            w_in = jnp.pad(bf(w_in_even[e]), ((0, 0), (0, (-IN_EVEN) % LANES)))
            wg = jnp.pad(bf(gla_w_gate[e]), ((0, w_in.shape[1] - IN_EVEN_MAIN - GLA_RANK), (0, 0)))
            proj = _even_in(h, row(norm_mix[i]), w_in, wg, row(gla_b_gate[e]))
            pool_init = jnp.concatenate(
                [jnp.zeros((bsz, 16, POOL_WIDTH), F32), jnp.pad(state_pool[e], ((0, 0), (1, 0), (0, 0)))], axis=0)
            y_pool, pool_st = _pool(proj, pool_init, bf(pool_w[e]), row(pool_scale[e]), sq)
            gla_init = jnp.concatenate([jnp.zeros((bsz,) + state_gla.shape[2:], F32), state_gla[e]], axis=0)
            o_gla, gla_st = _gla(proj, gla_init, sq)
            pools.append(pool_st[:, 1:])
            glas.append(gla_st)
            wr = jnp.pad(w_router[i], ((0, 0), (0, LANES - N_EXPERTS)))
            wr_hi = bf(wr)
            wr_lo = bf(wr - wr_hi.astype(F32))
            br = jnp.pad(row(b_router[i]), ((0, 0), (0, LANES - N_EXPERTS)), constant_values=NEG_INF)
            h1, xn, info, counts = _even_out(y_pool, o_gla, proj, h, row(gla_norm[e]), bf(w_out_even[e]),
                                             row(norm_ffn[i]), wr_hi, wr_lo, br)
        else:
            o = i // 2
            q, kv = _odd_in(h, row(norm_mix[i]), bf(w_qkv_odd[o]), row(b_qkv_odd[o]), cos, sin, table_block)
            ck = cache_k[o].reshape(dec_bsz * WINDOW, KV_WIDTH)
            cv = cache_v[o].reshape(dec_bsz * WINDOW, KV_WIDTH)
            att = _attention(attn_sinks[o], q, kv, ck, cv, sq)
            kv_p = kv[:n_p].reshape(bsz, t_len, 2 * KV_WIDTH)[:, -WINDOW:]
            kv_s = kv[n_p:].reshape(dec_bsz, dec_len, 2 * KV_WIDTH)
            hd = (N_KV_HEADS, HEAD_DIM)
            new_k.append((kv_p[..., :KV_WIDTH].reshape(bsz, WINDOW, *hd),
                          jnp.concatenate([cache_k[o], kv_s[..., :KV_WIDTH].reshape(dec_bsz, dec_len, *hd)], axis=1)[:, -WINDOW:]))
            new_v.append((kv_p[..., KV_WIDTH:].reshape(bsz, WINDOW, *hd),
                          jnp.concatenate([cache_v[o], kv_s[..., KV_WIDTH:].reshape(dec_bsz, dec_len, *hd)], axis=1)[:, -WINDOW:]))
            wr = jnp.pad(w_router[i], ((0, 0), (0, LANES - N_EXPERTS)))
            wr_hi = bf(wr)
            wr_lo = bf(wr - wr_hi.astype(F32))
            br = jnp.pad(row(b_router[i]), ((0, 0), (0, LANES - N_EXPERTS)), constant_values=NEG_INF)
            h1, xn, info, counts = _odd_out(att, h, bf(w_out_odd[o]), row(b_out_odd[o]),
                                            row(norm_ffn[i]), wr_hi, wr_lo, br)
        h = _moe_and_embed(h1, xn, info, counts, p_all[i], bf(w_gate_up[i]), b_gate_up[i][:, None, :],
                           bf(w_down[i]), b_down[i][:, None, :], bf(ple_proj[i]), bf(ple_gate[i]),
                           row(norm_final), final=(i == depth - 1))

    y_prompt = h[:n_p].reshape(bsz, t_len, D_MODEL)
    y_sample = h[n_p:].reshape(dec_bsz, dec_len, D_MODEL)
    pool_all = jnp.stack(pools)
    gla_all = jnp.stack(glas)
    return (y_prompt, y_sample, pool_all[:, :bsz], gla_all[:, :bsz],
            jnp.stack([k[0] for k in new_k]), jnp.stack([v[0] for v in new_v]),
            pool_all[:, bsz:], gla_all[:, bsz:],
            jnp.stack([k[1] for k in new_k]), jnp.stack([v[1] for v in new_v]))
```

```python
import functools

import jax
import jax.numpy as jnp
from jax import lax
from jax.experimental import pallas as pl
from jax.experimental.pallas import tpu as pltpu

F32 = jnp.float32
MXU_DTYPE = jnp.bfloat16

V7X_VMEM_BYTES = 64 * 1024 * 1024
VMEM_LIMIT = (V7X_VMEM_BYTES * 7) // 8
LANES = 128

D_MODEL = 1024
CHUNK = 64
PAST_LEN = 2048
PLE_DIM = 256
RMS_EPS = 1e-6
POOL_WINDOWS = (2, 4, 8, 16)
POOL_WIDTH = 512
POOL_GROUP_DIM = 128
POOL_STATE = 15
GLA_HEADS = 4
GLA_DK = 64
GLA_DV = 128
GLA_KEY_WIDTH = GLA_HEADS * GLA_DK
GLA_WIDTH = GLA_HEADS * GLA_DV
GLA_RANK = 16
GLA_TAU = 16.0
GLA_BLOCK = 16
IN_EVEN = POOL_WIDTH + 2 * GLA_KEY_WIDTH + 2 * GLA_WIDTH + GLA_RANK
IN_EVEN_MAIN = IN_EVEN - GLA_RANK
PROJ_WIDTH = IN_EVEN_MAIN + GLA_KEY_WIDTH
N_Q_HEADS = 16
N_KV_HEADS = 2
HEAD_DIM = 64
WINDOW = 128
ROPE_THETA = 10000.0
Q_WIDTH = N_Q_HEADS * HEAD_DIM
KV_WIDTH = N_KV_HEADS * HEAD_DIM
N_EXPERTS = 32
TOP_K = 4
SWIGLU_LIMIT = 7.0
SWIGLU_ALPHA = 1.702
NEG_INF = -1e30

TOKEN_TILE = 512
EXPERT_TILE = 512
COMBINE_TILE = 256


def _dot(a, b):
    return jnp.dot(a.astype(MXU_DTYPE), b.astype(MXU_DTYPE), preferred_element_type=F32)


def _dot_nt(a, b):
    return lax.dot_general(a.astype(MXU_DTYPE), b.astype(MXU_DTYPE), (((1,), (1,)), ((), ())),
                           preferred_element_type=F32)


def _split3(x):
    x1 = x.astype(MXU_DTYPE)
    r1 = x - x1.astype(F32)
    x2 = r1.astype(MXU_DTYPE)
    x3 = (r1 - x2.astype(F32)).astype(MXU_DTYPE)
    return x1, x2, x3


def _rms(x, g):
    return x * lax.rsqrt(jnp.mean(x * x, axis=-1, keepdims=True) + RMS_EPS) * g


ROW_TILES = D_MODEL // LANES


def _store_token_tiles(ref, x):
    for s in range(ROW_TILES):
        ref[pl.ds(s, x.shape[0], stride=ROW_TILES), :] = x[:, s * LANES:(s + 1) * LANES]


def _load_token_tiles(ref, rows):
    return jnp.concatenate([ref[pl.ds(s, rows, stride=ROW_TILES), :] for s in range(ROW_TILES)], axis=1)


def _params(n_axes=1):
    return pltpu.CompilerParams(dimension_semantics=("arbitrary",) * n_axes, vmem_limit_bytes=VMEM_LIMIT)


def _full(shape):
    return pl.BlockSpec(shape, lambda *_: (0,) * len(shape))


class _Seq:
    def __init__(self, n_prompt_seq, prompt_len, n_sample_seq):
        self.tiles_per_seq = prompt_len // CHUNK
        self.n_prompt_seq = n_prompt_seq
        self.n_prompt_tiles = n_prompt_seq * self.tiles_per_seq
        self.n_tiles = self.n_prompt_tiles + n_sample_seq
        self.n_seq = n_prompt_seq + n_sample_seq

    def is_sample(self, i):
        return i >= self.n_prompt_tiles

    def tile_in_seq(self, i):
        return jnp.where(self.is_sample(i), 0, i % self.tiles_per_seq)

    def seq(self, i):
        return jnp.where(self.is_sample(i), self.n_prompt_seq + i - self.n_prompt_tiles, i // self.tiles_per_seq)

    def last(self, i):
        return jnp.logical_or(self.is_sample(i), i % self.tiles_per_seq == self.tiles_per_seq - 1)

    def pos0(self, i):
        return jnp.where(self.is_sample(i), PAST_LEN, self.tile_in_seq(i) * CHUNK)


def _even_in_kernel(h_ref, g_ref, w_ref, wg_ref, bg_ref, out_ref):
    xn = _rms(h_ref[...], g_ref[...])
    proj = _dot(xn, w_ref[...])
    z = proj[:, IN_EVEN_MAIN:]
    a = _dot(z, wg_ref[...]) + bg_ref[...]
    log_alpha = (jnp.minimum(a, 0.0) - jnp.log1p(jnp.exp(-jnp.abs(a)))) * (1.0 / GLA_TAU)
    out_ref[:, :IN_EVEN_MAIN] = proj[:, :IN_EVEN_MAIN]
    out_ref[:, IN_EVEN_MAIN:] = log_alpha


def _even_in(h, g, w_in, wg, bg):
    n = h.shape[0]
    wp = w_in.shape[1]
    return pl.pallas_call(
        _even_in_kernel,
        out_shape=jax.ShapeDtypeStruct((n, PROJ_WIDTH), F32),
        grid=(n // TOKEN_TILE,),
        in_specs=[pl.BlockSpec((TOKEN_TILE, D_MODEL), lambda i: (i, 0)), _full((1, D_MODEL)),
                  _full((D_MODEL, wp)), _full((wp - IN_EVEN_MAIN, GLA_KEY_WIDTH)), _full((1, GLA_KEY_WIDTH))],
        out_specs=pl.BlockSpec((TOKEN_TILE, PROJ_WIDTH), lambda i: (i, 0)),
        compiler_params=_params(), name="even_in")(h, g, w_in, wg, bg)


def _pool_kernel(u_ref, init_ref, pw_ref, ps_ref, y_ref, st_ref, buf, *, sq):
    i = pl.program_id(0)

    @pl.when(sq.tile_in_seq(i) == 0)
    def _():
        buf[0:16, :] = init_ref[0]

    u = u_ref[...]
    buf[16:16 + CHUNK, :] = u
    pos = sq.pos0(i) + lax.broadcasted_iota(jnp.int32, (CHUNK, 1), 0)
    for g, w in enumerate(POOL_WINDOWS):
        sl = slice(g * POOL_GROUP_DIM, (g + 1) * POOL_GROUP_DIM)
        acc = u[:, sl]
        for j in range(1, w):
            acc = acc + buf[16 - j:16 - j + CHUNK, sl]
        cnt = jnp.minimum(w, pos + 1).astype(F32)
        d = acc / cnt - u[:, sl]
        y_ref[:, sl] = _dot(d, pw_ref[g]) * ps_ref[:, sl]
    tail = buf[CHUNK:CHUNK + 16, :]
    st_ref[0] = tail
    buf[0:16, :] = tail


def _pool(proj, init, pool_w, pool_scale, sq):
    n = proj.shape[0]
    return pl.pallas_call(
        functools.partial(_pool_kernel, sq=sq),
        out_shape=(jax.ShapeDtypeStruct((n, POOL_WIDTH), F32), jax.ShapeDtypeStruct((sq.n_seq, 16, POOL_WIDTH), F32)),
        grid=(sq.n_tiles,),
        in_specs=[pl.BlockSpec((CHUNK, POOL_WIDTH), lambda i: (i, 0)),
                  pl.BlockSpec((1, 16, POOL_WIDTH), lambda i: (sq.seq(i), 0, 0)),
                  _full((len(POOL_WINDOWS), POOL_GROUP_DIM, POOL_GROUP_DIM)), _full((1, POOL_WIDTH))],
        out_specs=(pl.BlockSpec((CHUNK, POOL_WIDTH), lambda i: (i, 0)),
                   pl.BlockSpec((1, 16, POOL_WIDTH), lambda i: (sq.seq(i), 0, 0))),
        scratch_shapes=[pltpu.VMEM((CHUNK + 16, POOL_WIDTH), F32)],
        compiler_params=_params(), name="pool")(proj, init, pool_w, pool_scale)


def _gla_kernel(q_ref, k_ref, v_ref, g_ref, s0_ref, o_ref, sout_ref, state, *, sq):
    i = pl.program_id(0)
    n_blk = CHUNK // GLA_BLOCK

    @pl.when(sq.tile_in_seq(i) == 0)
    def _():
        state[...] = s0_ref[0]

    g = g_ref[...]
    row = lax.broadcasted_iota(jnp.int32, (CHUNK, CHUNK), 0)
    col = lax.broadcasted_iota(jnp.int32, (CHUNK, CHUNK), 1)
    same = (row >> 4) == (col >> 4)
    causal = jnp.logical_and(same, col <= row)
    tri = jnp.where(causal, 1.0, 0.0).astype(MXU_DTYPE)
    ones = jnp.where(same, 1.0, 0.0).astype(MXU_DTYPE)
    g1, g2, g3 = _split3(g)
    b = _dot(tri, g1) + _dot(tri, g2) + _dot(tri, g3)
    b_last = _dot(ones, g1) + _dot(ones, g2) + _dot(ones, g3)
    q_t = q_ref[...] * (GLA_DK ** -0.5) * jnp.exp(b)
    k = k_ref[...]
    k_t = k * jnp.exp(-b)
    k_dec_t = (k * jnp.exp(b_last - b)).T
    sel = jnp.where((lax.broadcasted_iota(jnp.int32, (CHUNK, LANES), 0) >> 4)
                    == lax.broadcasted_iota(jnp.int32, (CHUNK, LANES), 1), 1.0, 0.0).astype(MXU_DTYPE)
    t1, t2, t3 = _split3(g.T)
    blk_decay = jnp.exp(_dot(t1, sel) + _dot(t2, sel) + _dot(t3, sel))
    v = v_ref[...]
    blk_of_col = lax.broadcasted_iota(jnp.int32, (GLA_DK, CHUNK), 1) >> 4
    for h in range(GLA_HEADS):
        ks = slice(h * GLA_DK, (h + 1) * GLA_DK)
        vs = slice(h * GLA_DV, (h + 1) * GLA_DV)
        vh = v[:, vs]
        scores = jnp.where(causal, _dot_nt(q_t[:, ks], k_t[:, ks]), 0.0)
        o_intra = _dot(scores, vh)
        s = state[h]
        o_inter = []
        for j in range(n_blk):
            rows = slice(j * GLA_BLOCK, (j + 1) * GLA_BLOCK)
            o_inter.append(_dot(q_t[rows, ks], s))
            kd = jnp.where(blk_of_col == j, k_dec_t[ks, :], 0.0)
            s = blk_decay[ks, j:j + 1] * s + _dot(kd, vh)
        state[h] = s
        o_ref[:, vs] = o_intra + jnp.concatenate(o_inter, axis=0)

    @pl.when(sq.last(i))
    def _():
        sout_ref[0] = state[...]


def _gla(proj, s0, sq):
    n = proj.shape[0]
    st_shape = (sq.n_seq, GLA_HEADS, GLA_DK, GLA_DV)
    st_spec = pl.BlockSpec((1, GLA_HEADS, GLA_DK, GLA_DV), lambda i: (sq.seq(i), 0, 0, 0))
    kw = GLA_KEY_WIDTH
    return pl.pallas_call(
        functools.partial(_gla_kernel, sq=sq),
        out_shape=(jax.ShapeDtypeStruct((n, GLA_WIDTH), F32), jax.ShapeDtypeStruct(st_shape, F32)),
        grid=(sq.n_tiles,),
        in_specs=[pl.BlockSpec((CHUNK, kw), lambda i: (i, POOL_WIDTH // kw)),
                  pl.BlockSpec((CHUNK, kw), lambda i: (i, POOL_WIDTH // kw + 1)),
                  pl.BlockSpec((CHUNK, GLA_WIDTH), lambda i: (i, (POOL_WIDTH + 2 * kw) // GLA_WIDTH)),
                  pl.BlockSpec((CHUNK, kw), lambda i: (i, IN_EVEN_MAIN // kw)),
                  st_spec],
        out_specs=(pl.BlockSpec((CHUNK, GLA_WIDTH), lambda i: (i, 0)), st_spec),
        scratch_shapes=[pltpu.VMEM((GLA_HEADS, GLA_DK, GLA_DV), F32)],
        compiler_params=_params(), name="gla")(proj, proj, proj, proj, s0)


def _route(h1, nffn_ref, wr_hi_ref, wr_lo_ref, br_ref, xn_ref, info_ref, cnt_ref, carry):
    tm = h1.shape[0]

    @pl.when(pl.program_id(0) == 0)
    def _():
        carry[...] = jnp.zeros_like(carry)

    xn = _rms(h1, nffn_ref[...])
    _store_token_tiles(xn_ref, xn)
    x_hi = xn.astype(MXU_DTYPE)
    x_lo = (xn - x_hi.astype(F32)).astype(MXU_DTYPE)
    logits = (_dot(x_hi, wr_hi_ref[...]) + _dot(x_lo, wr_hi_ref[...]) + _dot(x_hi, wr_lo_ref[...])
              + br_ref[...])
    lane = lax.broadcasted_iota(jnp.int32, (tm, LANES), 1)
    vals, ids, hots = [], [], []
    for _ in range(TOP_K):
        m = jnp.max(logits, axis=-1, keepdims=True)
        ix = jnp.min(jnp.where(logits == m, lane, LANES), axis=-1, keepdims=True)
        hot = lane == ix
        vals.append(m)
        ids.append(ix)
        hots.append(hot)
        logits = jnp.where(hot, -jnp.inf, logits)
    es = [jnp.exp(v - vals[0]) for v in vals]
    den = es[0] + es[1] + es[2] + es[3]
    chosen = jnp.zeros((tm, LANES), F32)
    for hot in hots:
        chosen = chosen + jnp.where(hot, 1.0, 0.0)
    before = (lax.broadcasted_iota(jnp.int32, (tm, tm), 1) < lax.broadcasted_iota(jnp.int32, (tm, tm), 0))
    rank = _dot(jnp.where(before, 1.0, 0.0), chosen) + carry[...]
    info = jnp.zeros((tm, LANES), F32)
    for k in range(TOP_K):
        pos = jnp.sum(jnp.where(hots[k], rank, 0.0), axis=-1, keepdims=True)
        info = jnp.where(lane == k, es[k] / den, info)
        info = jnp.where(lane == TOP_K + k, ids[k].astype(F32), info)
        info = jnp.where(lane == 2 * TOP_K + k, pos, info)
    info_ref[...] = info
    carry[...] = carry[...] + jnp.sum(chosen, axis=0, keepdims=True)
    cnt_ref[...] = carry[...]


_ROUTE_OUT_SHAPES = lambda n: (jax.ShapeDtypeStruct((n, D_MODEL), F32), jax.ShapeDtypeStruct((n * ROW_TILES, LANES), F32),
                               jax.ShapeDtypeStruct((n, LANES), F32), jax.ShapeDtypeStruct((1, LANES), F32))
_ROUTE_OUT_SPECS = (pl.BlockSpec((TOKEN_TILE, D_MODEL), lambda i: (i, 0)),
                    pl.BlockSpec((TOKEN_TILE * ROW_TILES, LANES), lambda i: (i, 0)),
                    pl.BlockSpec((TOKEN_TILE, LANES), lambda i: (i, 0)),
                    pl.BlockSpec((1, LANES), lambda i: (0, 0)))


def _route_in_specs():
    return [_full((1, D_MODEL)), _full((D_MODEL, LANES)), _full((D_MODEL, LANES)), _full((1, LANES))]


def _even_out_kernel(yp_ref, o_ref, r_ref, h_ref, gn_ref, w_ref, nffn_ref, wr_hi_ref, wr_lo_ref, br_ref,
                     h1_ref, xn_ref, info_ref, cnt_ref, carry):
    o = o_ref[...]
    r = r_ref[...]
    parts = []
    for hd in range(GLA_HEADS):
        sl = slice(hd * GLA_DV, (hd + 1) * GLA_DV)
        oh = o[:, sl]
        oh = oh * lax.rsqrt(jnp.mean(oh * oh, axis=-1, keepdims=True) + RMS_EPS) * gn_ref[...]
        rh = r[:, sl]
        parts.append(oh * (rh * jax.nn.sigmoid(rh)))
    gla = jnp.concatenate(parts, axis=1)
    mix = _dot(yp_ref[...], w_ref[:POOL_WIDTH, :]) + _dot(gla, w_ref[POOL_WIDTH:, :])
    h1 = h_ref[...] + mix
    h1_ref[...] = h1
    _route(h1, nffn_ref, wr_hi_ref, wr_lo_ref, br_ref, xn_ref, info_ref, cnt_ref, carry)


def _even_out(y_pool, o_gla, proj, h, gla_norm, w_out, nffn, wr_hi, wr_lo, br):
    n = h.shape[0]
    row = lambda w, c: pl.BlockSpec((TOKEN_TILE, w), lambda i: (i, c))
    return pl.pallas_call(
        _even_out_kernel,
        out_shape=_ROUTE_OUT_SHAPES(n),
        grid=(n // TOKEN_TILE,),
        in_specs=[row(POOL_WIDTH, 0), row(GLA_WIDTH, 0), row(GLA_WIDTH, (IN_EVEN_MAIN - GLA_WIDTH) // GLA_WIDTH),
                  row(D_MODEL, 0), _full((1, GLA_DV)), _full((D_MODEL, D_MODEL))] + _route_in_specs(),
        out_specs=_ROUTE_OUT_SPECS,
        scratch_shapes=[pltpu.VMEM((1, LANES), F32)],
        compiler_params=_params(), name="even_out")(y_pool, o_gla, proj, h, gla_norm, w_out, nffn, wr_hi, wr_lo, br)


def _rope_tile(x, cos, sin, lo_half):
    swapped = jnp.where(lo_half, pltpu.roll(x, LANES - HEAD_DIM // 2, axis=1), pltpu.roll(x, HEAD_DIM // 2, axis=1))
    return x * cos + swapped * sin


def _odd_in_kernel(h_ref, g_ref, w_ref, b_ref, cos_ref, sin_ref, q_ref, kv_ref):
    xn = _rms(h_ref[...], g_ref[...])
    qkv = _dot(xn, w_ref[...]) + b_ref[...]
    cos = cos_ref[...]
    sin = sin_ref[...]
    lo_half = (lax.broadcasted_iota(jnp.int32, cos.shape, 1) % HEAD_DIM) < HEAD_DIM // 2
    for j in range(Q_WIDTH // LANES):
        sl = slice(j * LANES, (j + 1) * LANES)
        q_ref[:, sl] = _rope_tile(qkv[:, sl], cos, sin, lo_half).astype(q_ref.dtype)
    kv_ref[:, :KV_WIDTH] = _rope_tile(qkv[:, Q_WIDTH:Q_WIDTH + KV_WIDTH], cos, sin, lo_half)
    kv_ref[:, KV_WIDTH:] = qkv[:, Q_WIDTH + KV_WIDTH:]


def _odd_in(h, g, w_qkv, b_qkv, cos, sin, table_block):
    n = h.shape[0]
    wq = w_qkv.shape[1]
    tab = pl.BlockSpec((TOKEN_TILE, LANES), lambda i: (table_block(i), 0))
    return pl.pallas_call(
        _odd_in_kernel,
        out_shape=(jax.ShapeDtypeStruct((n, Q_WIDTH), MXU_DTYPE), jax.ShapeDtypeStruct((n, 2 * KV_WIDTH), F32)),
        grid=(n // TOKEN_TILE,),
        in_specs=[pl.BlockSpec((TOKEN_TILE, D_MODEL), lambda i: (i, 0)), _full((1, D_MODEL)),
                  _full((D_MODEL, wq)), _full((1, wq)), tab, tab],
        out_specs=(pl.BlockSpec((TOKEN_TILE, Q_WIDTH), lambda i: (i, 0)),
                   pl.BlockSpec((TOKEN_TILE, 2 * KV_WIDTH), lambda i: (i, 0))),
        compiler_params=_params(), name="odd_in")(h, g, w_qkv, b_qkv, cos, sin)


def _attn_kernel(sink_ref, q_ref, kv0_ref, kv1_ref, kv2_ref, ck0_ref, ck1_ref, cv0_ref, cv1_ref, o_ref, *, sq):
    i = pl.program_id(0)
    smp = sq.is_sample(i)
    t = sq.tile_in_seq(i)
    kv0 = kv0_ref[...]
    k_old = jnp.where(smp, ck0_ref[...], kv2_ref[:, :KV_WIDTH])
    k_mid = jnp.where(smp, ck1_ref[...], kv1_ref[:, :KV_WIDTH])
    v_old = jnp.where(smp, cv0_ref[...], kv2_ref[:, KV_WIDTH:])
    v_mid = jnp.where(smp, cv1_ref[...], kv1_ref[:, KV_WIDTH:])
    pad = jnp.zeros((CHUNK, KV_WIDTH), F32)
    keys = jnp.concatenate([k_old, k_mid, kv0[:, :KV_WIDTH], pad], axis=0)
    vals = jnp.concatenate([v_old, v_mid, kv0[:, KV_WIDTH:], pad], axis=0)
    n_keys = 4 * CHUNK
    lane = lax.broadcasted_iota(jnp.int32, (n_keys, KV_WIDTH), 1)
    lo = lane < HEAD_DIM
    keys_sw = pltpu.roll(keys, HEAD_DIM, axis=1)
    vals_sw = pltpu.roll(vals, HEAD_DIM, axis=1)
    kcol = lax.broadcasted_iota(jnp.int32, (1, n_keys), 1)
    first_valid = jnp.where(smp, 0, (2 - jnp.minimum(t, 2)) * CHUNK)
    key_ok = jnp.logical_and(kcol >= first_valid, kcol < 3 * CHUNK)
    n_pairs = N_Q_HEADS // N_KV_HEADS // 2
    pair_of_row = lax.broadcasted_iota(jnp.int32, (n_pairs * CHUNK, 1), 0) // CHUNK
    for g in range(N_KV_HEADS):
        own, other = (keys, keys_sw) if g == 0 else (keys_sw, keys)
        vown, vother = (vals, vals_sw) if g == 0 else (vals_sw, vals)
        kb = jnp.concatenate([jnp.where(lo, own, 0.0), jnp.where(lo, 0.0, other)], axis=0).astype(MXU_DTYPE)
        vb = jnp.concatenate([jnp.where(lo, vown, 0.0), jnp.where(lo, 0.0, vother)], axis=0).astype(MXU_DTYPE)
        pairs = [slice((g * n_pairs + pr) * LANES, (g * n_pairs + pr + 1) * LANES) for pr in range(n_pairs)]
        qg = jnp.concatenate([q_ref[:, sl] for sl in pairs], axis=0)
        s = _dot_nt(qg, kb) * (HEAD_DIM ** -0.5)
        halves = []
        for half in range(2):
            sh = jnp.where(key_ok, s[:, half * n_keys:(half + 1) * n_keys], NEG_INF)
            sink = jnp.zeros((n_pairs * CHUNK, 1), F32)
            for pr in range(n_pairs):
                sink = jnp.where(pair_of_row == pr, sink_ref[2 * (g * n_pairs + pr) + half], sink)
            m = jnp.maximum(jnp.max(sh, axis=-1, keepdims=True), sink)
            p = jnp.exp(sh - m)
            halves.append(p / (jnp.sum(p, axis=-1, keepdims=True) + jnp.exp(sink - m)))
        o = _dot(jnp.concatenate(halves, axis=1), vb)
        for pr, sl in enumerate(pairs):
            o_ref[:, sl] = o[pr * CHUNK:(pr + 1) * CHUNK].astype(o_ref.dtype)


def _attention(sinks, q, kv, ck, cv, sq):
    n = q.shape[0]
    npt = sq.n_prompt_tiles
    prev = lambda d: (lambda i, s: (jnp.where(sq.is_sample(i), i, jnp.maximum(i - d, 0)), 0))
    cache = lambda d: (lambda i, s: (jnp.where(sq.is_sample(i), 2 * (i - npt) + d, 0), 0))
    kvspec = lambda f: pl.BlockSpec((CHUNK, 2 * KV_WIDTH), f)
    cspec = lambda f: pl.BlockSpec((CHUNK, KV_WIDTH), f)
    grid_spec = pltpu.PrefetchScalarGridSpec(
        num_scalar_prefetch=1, grid=(sq.n_tiles,),
        in_specs=[pl.BlockSpec((CHUNK, Q_WIDTH), lambda i, s: (i, 0)),
                  kvspec(prev(0)), kvspec(prev(1)), kvspec(prev(2)),
                  cspec(cache(0)), cspec(cache(1)), cspec(cache(0)), cspec(cache(1))],
        out_specs=pl.BlockSpec((CHUNK, Q_WIDTH), lambda i, s: (i, 0)))
    return pl.pallas_call(
        functools.partial(_attn_kernel, sq=sq),
        out_shape=jax.ShapeDtypeStruct((n, Q_WIDTH), MXU_DTYPE),
        grid_spec=grid_spec, compiler_params=_params(), name="attention")(sinks, q, kv, kv, kv, ck, ck, cv, cv)


def _odd_out_kernel(o_ref, h_ref, w_ref, b_ref, nffn_ref, wr_hi_ref, wr_lo_ref, br_ref,
                    h1_ref, xn_ref, info_ref, cnt_ref, carry):
    h1 = h_ref[...] + _dot(o_ref[...], w_ref[...]) + b_ref[...]
    h1_ref[...] = h1
    _route(h1, nffn_ref, wr_hi_ref, wr_lo_ref, br_ref, xn_ref, info_ref, cnt_ref, carry)


def _odd_out(o, h, w_out, b_out, nffn, wr_hi, wr_lo, br):
    n = h.shape[0]
    row = pl.BlockSpec((TOKEN_TILE, D_MODEL), lambda i: (i, 0))
    return pl.pallas_call(
        _odd_out_kernel,
        out_shape=_ROUTE_OUT_SHAPES(n),
        grid=(n // TOKEN_TILE,),
        in_specs=[row, row, _full((D_MODEL, D_MODEL)), _full((1, D_MODEL))] + _route_in_specs(),
        out_specs=_ROUTE_OUT_SPECS,
        scratch_shapes=[pltpu.VMEM((1, LANES), F32)],
        compiler_params=_params(), name="odd_out")(o, h, w_out, b_out, nffn, wr_hi, wr_lo, br)


def _token_copy(src, s, dst, d, sem):
    return pltpu.make_async_copy(src.at[pl.ds(pl.multiple_of(s, ROW_TILES), ROW_TILES)],
                                 dst.at[pl.ds(pl.multiple_of(d, ROW_TILES), ROW_TILES)], sem)


def _zero_fill(fill_ref, zeros, xs_out, zsem):
    zeros[...] = jnp.zeros_like(zeros)
    tile_rows = EXPERT_TILE * ROW_TILES

    def sweep(wait):
        def go(cp):
            cp.wait() if wait else cp.start()

        def tail(e, c):
            off, length = fill_ref[e], fill_ref[N_EXPERTS + e]
            for bit in range(EXPERT_TILE.bit_length() - 1):
                rows = (1 << bit) * ROW_TILES

                @pl.when((length >> bit) & 1 == 1)
                def _():
                    o = pl.multiple_of(off + (length & ((1 << bit) - 1)) * ROW_TILES, ROW_TILES)
                    go(pltpu.make_async_copy(zeros.at[pl.ds(0, rows)], xs_out.at[pl.ds(o, rows)], zsem))
            return c

        lax.fori_loop(0, N_EXPERTS, tail, 0)

        def unused(t, c):
            o = pl.multiple_of(fill_ref[2 * N_EXPERTS] + t * tile_rows, ROW_TILES)
            go(pltpu.make_async_copy(zeros, xs_out.at[pl.ds(o, tile_rows)], zsem))
            return c

        lax.fori_loop(0, fill_ref[2 * N_EXPERTS + 1], unused, 0)

    sweep(wait=False)
    sweep(wait=True)


def _dispatch_kernel(fill_ref, dest_hbm, x_ref, xs_out, dest_smem, zeros, sem, dsem, zsem):
    i = pl.program_id(0)
    n_slots = TOKEN_TILE * TOP_K

    @pl.when(i == 0)
    def _():
        _zero_fill(fill_ref, zeros, xs_out, zsem)

    cp = pltpu.make_async_copy(dest_hbm.at[pl.ds(i * n_slots, n_slots)], dest_smem, dsem)
    cp.start()
    cp.wait()

    def issue(r, c):
        _token_copy(x_ref, (r // TOP_K) * ROW_TILES, xs_out, dest_smem[r], sem).start()
        return c

    lax.fori_loop(0, n_slots, issue, 0, unroll=8)
    whole = xs_out.at[pl.ds(0, n_slots * ROW_TILES)]
    pltpu.make_async_copy(whole, whole, sem).wait()


def _dispatch(fill, dest, xt, n_rows):
    n = xt.shape[0] // ROW_TILES
    grid_spec = pltpu.PrefetchScalarGridSpec(
        num_scalar_prefetch=1, grid=(n // TOKEN_TILE,),
        in_specs=[pl.BlockSpec(memory_space=pl.ANY),
                  pl.BlockSpec((TOKEN_TILE * ROW_TILES, LANES), lambda i, f: (i, 0))],
        out_specs=pl.BlockSpec(memory_space=pl.ANY),
        scratch_shapes=[pltpu.SMEM((TOKEN_TILE * TOP_K,), jnp.int32), pltpu.VMEM((EXPERT_TILE * ROW_TILES, LANES), F32),
                        pltpu.SemaphoreType.DMA, pltpu.SemaphoreType.DMA, pltpu.SemaphoreType.DMA])
    return pl.pallas_call(
        _dispatch_kernel, out_shape=jax.ShapeDtypeStruct((n_rows * ROW_TILES, LANES), F32),
        grid_spec=grid_spec, compiler_params=_params(), name="moe_dispatch")(fill, dest, xt)


def _experts_kernel(te_ref, tr_ref, x_ref, wgu_ref, bgu_ref, wd_ref, bd_ref, y_ref, wgu_mxu, wd_mxu):
    i = pl.program_id(0)
    n_valid = tr_ref[i]

    @pl.when(jnp.logical_or(i == 0, te_ref[i] != te_ref[jnp.maximum(i - 1, 0)]))
    def _():
        wgu_mxu[...] = wgu_ref[0].astype(MXU_DTYPE)
        wd_mxu[...] = wd_ref[0].astype(MXU_DTYPE)

    @pl.when(n_valid > 0)
    def _():
        x = _load_token_tiles(x_ref, EXPERT_TILE)
        hgu = _dot(x, wgu_mxu[...]) + bgu_ref[0]
        gate = jnp.minimum(hgu[:, :D_MODEL], SWIGLU_LIMIT)
        up = jnp.clip(hgu[:, D_MODEL:], -SWIGLU_LIMIT, SWIGLU_LIMIT)
        act = (up + 1.0) * gate * jax.nn.sigmoid(SWIGLU_ALPHA * gate)
        _store_token_tiles(y_ref, _dot(act, wd_mxu[...]) + bd_ref[0])

    @pl.when(n_valid == 0)
    def _():
        y_ref[...] = jnp.zeros_like(y_ref)


def _experts(tile_expert, tile_rows, xs, wgu, bgu, wd, bd):
    rows = xs.shape[0] // ROW_TILES
    tile = pl.BlockSpec((EXPERT_TILE * ROW_TILES, LANES), lambda i, te, tr: (i, 0))
    grid_spec = pltpu.PrefetchScalarGridSpec(
        num_scalar_prefetch=2, grid=(rows // EXPERT_TILE,),
        in_specs=[tile,
                  pl.BlockSpec((1, D_MODEL, 2 * D_MODEL), lambda i, te, tr: (te[i], 0, 0)),
                  pl.BlockSpec((1, 1, 2 * D_MODEL), lambda i, te, tr: (te[i], 0, 0)),
                  pl.BlockSpec((1, D_MODEL, D_MODEL), lambda i, te, tr: (te[i], 0, 0)),
                  pl.BlockSpec((1, 1, D_MODEL), lambda i, te, tr: (te[i], 0, 0))],
        out_specs=tile,
        scratch_shapes=[pltpu.VMEM((D_MODEL, 2 * D_MODEL), MXU_DTYPE), pltpu.VMEM((D_MODEL, D_MODEL), MXU_DTYPE)])
    return pl.pallas_call(
        _experts_kernel, out_shape=jax.ShapeDtypeStruct(xs.shape, F32),
        grid_spec=grid_spec, compiler_params=_params(), name="moe_experts")(tile_expert, tile_rows, xs, wgu, bgu, wd, bd)


def _combine_kernel(dest_hbm, info_ref, h_ref, p_ref, pp_ref, pg_ref, nf_ref, ys_ref, o_ref,
                    buf, dest_smem, sems, dsem, *, final):
    i = pl.program_id(0)
    n_slots = COMBINE_TILE * TOP_K
    tile_rows = COMBINE_TILE * ROW_TILES

    def gather(step, slot):
        cp = pltpu.make_async_copy(dest_hbm.at[pl.ds(step * n_slots, n_slots)], dest_smem.at[slot], dsem)
        cp.start()
        cp.wait()

        def issue(r, c):
            _token_copy(ys_ref, dest_smem[slot, r], buf.at[slot],
                        (r % TOP_K) * tile_rows + (r // TOP_K) * ROW_TILES, sems.at[slot]).start()
            return c

        lax.fori_loop(0, n_slots, issue, 0, unroll=8)

    @pl.when(i == 0)
    def _():
        gather(0, 0)

    @pl.when(i + 1 < pl.num_programs(0))
    def _():
        gather(i + 1, (i + 1) % 2)

    slot = i % 2
    pltpu.make_async_copy(ys_ref.at[pl.ds(0, TOP_K * tile_rows)], buf.at[slot], sems.at[slot]).wait()
    gates = info_ref[...]
    moe = None
    for k in range(TOP_K):
        rows = jnp.concatenate([buf[slot, pl.ds(k * tile_rows + s, COMBINE_TILE, stride=ROW_TILES), :]
                                for s in range(ROW_TILES)], axis=1)
        moe = gates[:, k:k + 1] * rows if moe is None else moe + gates[:, k:k + 1] * rows
    h2 = h_ref[...] + moe
    h3 = h2 + jax.nn.sigmoid(_dot(h2, pg_ref[...])) * _dot(p_ref[...], pp_ref[...])
    o_ref[...] = _rms(h3, nf_ref[...]) if final else h3


def _combine(dest, info, h1, p, ple_proj, ple_gate, norm_final, ys, final):
    n = h1.shape[0]
    row = lambda w: pl.BlockSpec((COMBINE_TILE, w), lambda i: (i, 0))
    return pl.pallas_call(
        functools.partial(_combine_kernel, final=final),
        out_shape=jax.ShapeDtypeStruct((n, D_MODEL), F32),
        grid=(n // COMBINE_TILE,),
        in_specs=[pl.BlockSpec(memory_space=pl.ANY), row(LANES), row(D_MODEL), row(PLE_DIM),
                  _full((PLE_DIM, D_MODEL)), _full((D_MODEL, D_MODEL)), _full((1, D_MODEL)),
                  pl.BlockSpec(memory_space=pl.ANY)],
        out_specs=row(D_MODEL),
        scratch_shapes=[pltpu.VMEM((2, TOP_K * COMBINE_TILE * ROW_TILES, LANES), F32),
                        pltpu.SMEM((2, COMBINE_TILE * TOP_K), jnp.int32),
                        pltpu.SemaphoreType.DMA((2,)), pltpu.SemaphoreType.DMA],
        compiler_params=_params(), name="moe_combine")(dest, info, h1, p, ple_proj, ple_gate, norm_final, ys)


def _moe_and_embed(h1, xt, info, counts, p, wgu, bgu, wd, bd, ple_proj, ple_gate, norm_final, final):
    n = h1.shape[0]
    n_tiles = (n * TOP_K + N_EXPERTS * (EXPERT_TILE - 1)) // EXPERT_TILE + 1
    ids = info[:, TOP_K:2 * TOP_K].astype(jnp.int32)
    rank = info[:, 2 * TOP_K:3 * TOP_K].astype(jnp.int32)
    cnt = counts[0, :N_EXPERTS].astype(jnp.int32)
    padded = ((cnt + EXPERT_TILE - 1) // EXPERT_TILE) * EXPERT_TILE
    ends = jnp.cumsum(padded)
    starts = ends - padded
    dest = ((starts[ids] + rank) * ROW_TILES).reshape(-1)
    tile_start = jnp.arange(n_tiles, dtype=jnp.int32) * EXPERT_TILE
    tile_expert = jnp.minimum(jnp.sum((tile_start[:, None] >= ends[None, :]).astype(jnp.int32), axis=1), N_EXPERTS - 1)
    tile_rows = jnp.clip(cnt[tile_expert] - (tile_start - starts[tile_expert]), 0, EXPERT_TILE)
    tile_rows = jnp.where(tile_start < ends[-1], tile_rows, 0).astype(jnp.int32)
    fill = jnp.concatenate([(starts + cnt) * ROW_TILES, padded - cnt,
                            jnp.stack([ends[-1] * ROW_TILES, n_tiles - ends[-1] // EXPERT_TILE])]).astype(jnp.int32)
    xs = _dispatch(fill, dest, xt, n_tiles * EXPERT_TILE)
    ys = _experts(tile_expert, tile_rows, xs, wgu, bgu, wd, bd)
    return _combine(dest, info, h1, p, ple_proj, ple_gate, norm_final, ys, final)


def _rope_tables(prompt_len, n_sample_seq, sample_len):
    half = HEAD_DIM // 2
    inv = jnp.power(jnp.float32(ROPE_THETA), -jnp.arange(half, dtype=F32) / half)
    pos = jnp.concatenate([jnp.arange(prompt_len), jnp.tile(PAST_LEN + jnp.arange(sample_len), n_sample_seq)])
    ang = pos.astype(F32)[:, None] * inv[None, :]
    cos = jnp.tile(jnp.cos(ang), (1, LANES // half))
    sin = jnp.tile(jnp.concatenate([-jnp.sin(ang), jnp.sin(ang)], axis=1), (1, LANES // HEAD_DIM))
    return cos, sin


def kernel(x_prompt, x_sample, state_pool, state_gla, cache_k, cache_v, p_prompt, p_sample, norm_mix, norm_ffn, norm_final, w_in_even, pool_w, pool_scale, gla_w_gate, gla_b_gate, gla_norm, w_out_even, w_qkv_odd, b_qkv_odd, attn_sinks, w_out_odd, b_out_odd, w_router, b_router, w_gate_up, b_gate_up, w_down, b_down, ple_proj, ple_gate):
    bsz, t_len, _ = x_prompt.shape
    dec_bsz, dec_len, _ = x_sample.shape
    depth = norm_mix.shape[0]
    n_p, n_s = bsz * t_len, dec_bsz * dec_len
    n = n_p + n_s
    assert dec_len == CHUNK and n_s == TOKEN_TILE and t_len % TOKEN_TILE == 0 and cache_k.shape[2] == WINDOW
    sq = _Seq(bsz, t_len, dec_bsz)
    bf = lambda a: a.astype(MXU_DTYPE)
    row = lambda a: a.reshape(1, -1)

    h = jnp.concatenate([x_prompt.reshape(n_p, D_MODEL), x_sample.reshape(n_s, D_MODEL)], axis=0)
    p_all = jnp.concatenate([p_prompt.reshape(depth, n_p, PLE_DIM), p_sample.reshape(depth, n_s, PLE_DIM)], axis=1)
    cos, sin = _rope_tables(t_len, dec_bsz, dec_len)
    tiles_per_seq = t_len // TOKEN_TILE
    table_block = lambda i: jnp.where(i < bsz * tiles_per_seq, i % tiles_per_seq, tiles_per_seq)

    pools, glas, new_k, new_v = [], [], [], []
    for i in range(depth):
        if i % 2 == 0:
            e = i // 2
            w_in = jnp.pad(bf(w_in_even[e]), ((0, 0), (0, (-IN_EVEN) % LANES)))
            wg = jnp.pad(bf(gla_w_gate[e]), ((0, w_in.shape[1] - IN_EVEN_MAIN - GLA_RANK), (0, 0)))
            proj = _even_in(h, row(norm_mix[i]), w_in, wg, row(gla_b_gate[e]))
            pool_init = jnp.concatenate(
                [jnp.zeros((bsz, 16, POOL_WIDTH), F32), jnp.pad(state_pool[e], ((0, 0), (1, 0), (0, 0)))], axis=0)
            y_pool, pool_st = _pool(proj, pool_init, bf(pool_w[e]), row(pool_scale[e]), sq)
            gla_init = jnp.concatenate([jnp.zeros((bsz,) + state_gla.shape[2:], F32), state_gla[e]], axis=0)
            o_gla, gla_st = _gla(proj, gla_init, sq)
            pools.append(pool_st[:, 1:])
            glas.append(gla_st)
            wr = jnp.pad(w_router[i], ((0, 0), (0, LANES - N_EXPERTS)))
            wr_hi = bf(wr)
            wr_lo = bf(wr - wr_hi.astype(F32))
            br = jnp.pad(row(b_router[i]), ((0, 0), (0, LANES - N_EXPERTS)), constant_values=NEG_INF)
            h1, xn, info, counts = _even_out(y_pool, o_gla, proj, h, row(gla_norm[e]), bf(w_out_even[e]),
                                             row(norm_ffn[i]), wr_hi, wr_lo, br)
        else:
            o = i // 2
            q, kv = _odd_in(h, row(norm_mix[i]), bf(w_qkv_odd[o]), row(b_qkv_odd[o]), cos, sin, table_block)
            ck = cache_k[o].reshape(dec_bsz * WINDOW, KV_WIDTH)
            cv = cache_v[o].reshape(dec_bsz * WINDOW, KV_WIDTH)
            att = _attention(attn_sinks[o], q, kv, ck, cv, sq)
            kv_p = kv[:n_p].reshape(bsz, t_len, 2 * KV_WIDTH)[:, -WINDOW:]
            kv_s = kv[n_p:].reshape(dec_bsz, dec_len, 2 * KV_WIDTH)
            hd = (N_KV_HEADS, HEAD_DIM)
            new_k.append((kv_p[..., :KV_WIDTH].reshape(bsz, WINDOW, *hd),
                          jnp.concatenate([cache_k[o], kv_s[..., :KV_WIDTH].reshape(dec_bsz, dec_len, *hd)], axis=1)[:, -WINDOW:]))
            new_v.append((kv_p[..., KV_WIDTH:].reshape(bsz, WINDOW, *hd),
                          jnp.concatenate([cache_v[o], kv_s[..., KV_WIDTH:].reshape(dec_bsz, dec_len, *hd)], axis=1)[:, -WINDOW:]))
            wr = jnp.pad(w_router[i], ((0, 0), (0, LANES - N_EXPERTS)))
            wr_hi = bf(wr)
            wr_lo = bf(wr - wr_hi.astype(F32))
            br = jnp.pad(row(b_router[i]), ((0, 0), (0, LANES - N_EXPERTS)), constant_values=NEG_INF)
            h1, xn, info, counts = _odd_out(att, h, bf(w_out_odd[o]), row(b_out_odd[o]),
                                            row(norm_ffn[i]), wr_hi, wr_lo, br)
        h = _moe_and_embed(h1, xn, info, counts, p_all[i], w_gate_up[i], b_gate_up[i][:, None, :],
                           w_down[i], b_down[i][:, None, :], bf(ple_proj[i]), bf(ple_gate[i]),
                           row(norm_final), final=(i == depth - 1))

    y_prompt = h[:n_p].reshape(bsz, t_len, D_MODEL)
    y_sample = h[n_p:].reshape(dec_bsz, dec_len, D_MODEL)
    pool_all = jnp.stack(pools)
    gla_all = jnp.stack(glas)
    return (y_prompt, y_sample, pool_all[:, :bsz], gla_all[:, :bsz],
            jnp.stack([k[0] for k in new_k]), jnp.stack([v[0] for v in new_v]),
            pool_all[:, bsz:], gla_all[:, bsz:],
            jnp.stack([k[1] for k in new_k]), jnp.stack([v[1] for v in new_v]))
```

```python
import functools

import jax
import jax.numpy as jnp
from jax import lax
from jax.experimental import pallas as pl
from jax.experimental.pallas import tpu as pltpu

F32 = jnp.float32
MXU_DTYPE = jnp.bfloat16

V7X_VMEM_BYTES = 64 * 1024 * 1024
VMEM_LIMIT = (V7X_VMEM_BYTES * 7) // 8
LANES = 128

D_MODEL = 1024
CHUNK = 64
PAST_LEN = 2048
PLE_DIM = 256
RMS_EPS = 1e-6
POOL_WINDOWS = (2, 4, 8, 16)
POOL_WIDTH = 512
POOL_GROUP_DIM = 128
POOL_STATE = 15
GLA_HEADS = 4
GLA_DK = 64
GLA_DV = 128
GLA_KEY_WIDTH = GLA_HEADS * GLA_DK
GLA_WIDTH = GLA_HEADS * GLA_DV
GLA_RANK = 16
GLA_TAU = 16.0
GLA_BLOCK = 16
IN_EVEN = POOL_WIDTH + 2 * GLA_KEY_WIDTH + 2 * GLA_WIDTH + GLA_RANK
IN_EVEN_MAIN = IN_EVEN - GLA_RANK
PROJ_WIDTH = IN_EVEN_MAIN + GLA_KEY_WIDTH
N_Q_HEADS = 16
N_KV_HEADS = 2
HEAD_DIM = 64
WINDOW = 128
ROPE_THETA = 10000.0
Q_WIDTH = N_Q_HEADS * HEAD_DIM
KV_WIDTH = N_KV_HEADS * HEAD_DIM
N_EXPERTS = 32
TOP_K = 4
SWIGLU_LIMIT = 7.0
SWIGLU_ALPHA = 1.702
NEG_INF = -1e30

TOKEN_TILE = 512
EXPERT_TILE = 512
COMBINE_TILE = 256


def _dot(a, b):
    return jnp.dot(a.astype(MXU_DTYPE), b.astype(MXU_DTYPE), preferred_element_type=F32)


def _dot_nt(a, b):
    return lax.dot_general(a.astype(MXU_DTYPE), b.astype(MXU_DTYPE), (((1,), (1,)), ((), ())),
                           preferred_element_type=F32)


def _split3(x):
    x1 = x.astype(MXU_DTYPE)
    r1 = x - x1.astype(F32)
    x2 = r1.astype(MXU_DTYPE)
    x3 = (r1 - x2.astype(F32)).astype(MXU_DTYPE)
    return x1, x2, x3


def _rms(x, g):
    return x * lax.rsqrt(jnp.mean(x * x, axis=-1, keepdims=True) + RMS_EPS) * g


ROW_TILES = D_MODEL // LANES
TAGGED_ROWS = 2 * ROW_TILES


def _store_token_tiles(ref, x, pitch=ROW_TILES):
    for s in range(ROW_TILES):
        ref[pl.ds(s, x.shape[0], stride=pitch), :] = x[:, s * LANES:(s + 1) * LANES]


def _load_token_tiles(ref, rows, pitch=ROW_TILES):
    return jnp.concatenate([ref[pl.ds(s, rows, stride=pitch), :] for s in range(ROW_TILES)], axis=1)


def _params(n_axes=1):
    return pltpu.CompilerParams(dimension_semantics=("arbitrary",) * n_axes, vmem_limit_bytes=VMEM_LIMIT)


def _full(shape):
    return pl.BlockSpec(shape, lambda *_: (0,) * len(shape))


class _Seq:
    def __init__(self, n_prompt_seq, prompt_len, n_sample_seq):
        self.tiles_per_seq = prompt_len // CHUNK
        self.n_prompt_seq = n_prompt_seq
        self.n_prompt_tiles = n_prompt_seq * self.tiles_per_seq
        self.n_tiles = self.n_prompt_tiles + n_sample_seq
        self.n_seq = n_prompt_seq + n_sample_seq

    def is_sample(self, i):
        return i >= self.n_prompt_tiles

    def tile_in_seq(self, i):
        return jnp.where(self.is_sample(i), 0, i % self.tiles_per_seq)

    def seq(self, i):
        return jnp.where(self.is_sample(i), self.n_prompt_seq + i - self.n_prompt_tiles, i // self.tiles_per_seq)

    def last(self, i):
        return jnp.logical_or(self.is_sample(i), i % self.tiles_per_seq == self.tiles_per_seq - 1)

    def pos0(self, i):
        return jnp.where(self.is_sample(i), PAST_LEN, self.tile_in_seq(i) * CHUNK)


def _split_specs(tile, width, n_prompt, same_array):
    npt = n_prompt // tile
    off = npt if same_array else 0
    return [pl.BlockSpec((tile, width), lambda i: (jnp.minimum(i, npt - 1), 0)),
            pl.BlockSpec((tile, width), lambda i: (jnp.maximum(i - npt, 0) + off, 0))]


def _split_rows(prompt_ref, sample_ref, n_prompt_tiles):
    return jnp.where(pl.program_id(0) < n_prompt_tiles, prompt_ref[...], sample_ref[...])


def _even_in_kernel(hp_ref, hs_ref, g_ref, w_ref, wg_ref, bg_ref, out_ref, *, npt):
    xn = _rms(_split_rows(hp_ref, hs_ref, npt), g_ref[...])
    proj = _dot(xn, w_ref[...])
    z = proj[:, IN_EVEN_MAIN:]
    a = _dot(z, wg_ref[...]) + bg_ref[...]
    log_alpha = (jnp.minimum(a, 0.0) - jnp.log1p(jnp.exp(-jnp.abs(a)))) * (1.0 / GLA_TAU)
    out_ref[:, :IN_EVEN_MAIN] = proj[:, :IN_EVEN_MAIN]
    out_ref[:, IN_EVEN_MAIN:] = log_alpha


def _even_in(h_parts, n, n_p, g, w_in, wg, bg):
    wp = w_in.shape[1]
    return pl.pallas_call(
        functools.partial(_even_in_kernel, npt=n_p // TOKEN_TILE),
        out_shape=jax.ShapeDtypeStruct((n, PROJ_WIDTH), F32),
        grid=(n // TOKEN_TILE,),
        in_specs=_split_specs(TOKEN_TILE, D_MODEL, n_p, h_parts[0] is h_parts[1]) + [
            _full((1, D_MODEL)), _full((D_MODEL, wp)), _full((wp - IN_EVEN_MAIN, GLA_KEY_WIDTH)),
            _full((1, GLA_KEY_WIDTH))],
        out_specs=pl.BlockSpec((TOKEN_TILE, PROJ_WIDTH), lambda i: (i, 0)),
        compiler_params=_params(), name="even_in")(*h_parts, g, w_in, wg, bg)


def _pool_kernel(u_ref, init_ref, pw_ref, ps_ref, y_ref, st_ref, buf, *, sq):
    i = pl.program_id(0)

    @pl.when(sq.tile_in_seq(i) == 0)
    def _():
        buf[0:16, :] = init_ref[0]

    u = u_ref[...]
    buf[16:16 + CHUNK, :] = u
    pos = sq.pos0(i) + lax.broadcasted_iota(jnp.int32, (CHUNK, 1), 0)
    for g, w in enumerate(POOL_WINDOWS):
        sl = slice(g * POOL_GROUP_DIM, (g + 1) * POOL_GROUP_DIM)
        acc = u[:, sl]
        for j in range(1, w):
            acc = acc + buf[16 - j:16 - j + CHUNK, sl]
        cnt = jnp.minimum(w, pos + 1).astype(F32)
        d = acc / cnt - u[:, sl]
        y_ref[:, sl] = _dot(d, pw_ref[g]) * ps_ref[:, sl]
    tail = buf[CHUNK:CHUNK + 16, :]
    st_ref[0] = tail
    buf[0:16, :] = tail


def _pool(proj, init, pool_w, pool_scale, sq):
    n = proj.shape[0]
    return pl.pallas_call(
        functools.partial(_pool_kernel, sq=sq),
        out_shape=(jax.ShapeDtypeStruct((n, POOL_WIDTH), F32), jax.ShapeDtypeStruct((sq.n_seq, 16, POOL_WIDTH), F32)),
        grid=(sq.n_tiles,),
        in_specs=[pl.BlockSpec((CHUNK, POOL_WIDTH), lambda i: (i, 0)),
                  pl.BlockSpec((1, 16, POOL_WIDTH), lambda i: (sq.seq(i), 0, 0)),
                  _full((len(POOL_WINDOWS), POOL_GROUP_DIM, POOL_GROUP_DIM)), _full((1, POOL_WIDTH))],
        out_specs=(pl.BlockSpec((CHUNK, POOL_WIDTH), lambda i: (i, 0)),
                   pl.BlockSpec((1, 16, POOL_WIDTH), lambda i: (sq.seq(i), 0, 0))),
        scratch_shapes=[pltpu.VMEM((CHUNK + 16, POOL_WIDTH), F32)],
        compiler_params=_params(), name="pool")(proj, init, pool_w, pool_scale)


def _gla_kernel(q_ref, k_ref, v_ref, g_ref, s0_ref, o_ref, sout_ref, state, *, sq):
    i = pl.program_id(0)
    n_blk = CHUNK // GLA_BLOCK

    @pl.when(sq.tile_in_seq(i) == 0)
    def _():
        state[...] = s0_ref[0]

    g = g_ref[...]
    row = lax.broadcasted_iota(jnp.int32, (CHUNK, CHUNK), 0)
    col = lax.broadcasted_iota(jnp.int32, (CHUNK, CHUNK), 1)
    same = (row >> 4) == (col >> 4)
    causal = jnp.logical_and(same, col <= row)
    tri = jnp.where(causal, 1.0, 0.0).astype(MXU_DTYPE)
    ones = jnp.where(same, 1.0, 0.0).astype(MXU_DTYPE)
    g1, g2, g3 = _split3(g)
    b = _dot(tri, g1) + _dot(tri, g2) + _dot(tri, g3)
    b_last = _dot(ones, g1) + _dot(ones, g2) + _dot(ones, g3)
    q_t = q_ref[...] * (GLA_DK ** -0.5) * jnp.exp(b)
    k = k_ref[...]
    k_t = k * jnp.exp(-b)
    k_dec_t = (k * jnp.exp(b_last - b)).T
    sel = jnp.where((lax.broadcasted_iota(jnp.int32, (CHUNK, LANES), 0) >> 4)
                    == lax.broadcasted_iota(jnp.int32, (CHUNK, LANES), 1), 1.0, 0.0).astype(MXU_DTYPE)
    t1, t2, t3 = _split3(g.T)
    blk_decay = jnp.exp(_dot(t1, sel) + _dot(t2, sel) + _dot(t3, sel))
    v = v_ref[...]
    blk_of_col = lax.broadcasted_iota(jnp.int32, (GLA_DK, CHUNK), 1) >> 4
    for h in range(GLA_HEADS):
        ks = slice(h * GLA_DK, (h + 1) * GLA_DK)
        vs = slice(h * GLA_DV, (h + 1) * GLA_DV)
        vh = v[:, vs]
        scores = jnp.where(causal, _dot_nt(q_t[:, ks], k_t[:, ks]), 0.0)
        o_intra = _dot(scores, vh)
        s = state[h]
        o_inter = []
        for j in range(n_blk):
            rows = slice(j * GLA_BLOCK, (j + 1) * GLA_BLOCK)
            o_inter.append(_dot(q_t[rows, ks], s))
            kd = jnp.where(blk_of_col == j, k_dec_t[ks, :], 0.0)
            s = blk_decay[ks, j:j + 1] * s + _dot(kd, vh)
        state[h] = s
        o_ref[:, vs] = o_intra + jnp.concatenate(o_inter, axis=0)

    @pl.when(sq.last(i))
    def _():
        sout_ref[0] = state[...]


def _gla(proj, s0, sq):
    n = proj.shape[0]
    st_shape = (sq.n_seq, GLA_HEADS, GLA_DK, GLA_DV)
    st_spec = pl.BlockSpec((1, GLA_HEADS, GLA_DK, GLA_DV), lambda i: (sq.seq(i), 0, 0, 0))
    kw = GLA_KEY_WIDTH
    return pl.pallas_call(
        functools.partial(_gla_kernel, sq=sq),
        out_shape=(jax.ShapeDtypeStruct((n, GLA_WIDTH), F32), jax.ShapeDtypeStruct(st_shape, F32)),
        grid=(sq.n_tiles,),
        in_specs=[pl.BlockSpec((CHUNK, kw), lambda i: (i, POOL_WIDTH // kw)),
                  pl.BlockSpec((CHUNK, kw), lambda i: (i, POOL_WIDTH // kw + 1)),
                  pl.BlockSpec((CHUNK, GLA_WIDTH), lambda i: (i, (POOL_WIDTH + 2 * kw) // GLA_WIDTH)),
                  pl.BlockSpec((CHUNK, kw), lambda i: (i, IN_EVEN_MAIN // kw)),
                  st_spec],
        out_specs=(pl.BlockSpec((CHUNK, GLA_WIDTH), lambda i: (i, 0)), st_spec),
        scratch_shapes=[pltpu.VMEM((GLA_HEADS, GLA_DK, GLA_DV), F32)],
        compiler_params=_params(), name="gla")(proj, proj, proj, proj, s0)


def _route(h1, nffn_ref, wr_hi_ref, wr_lo_ref, br_ref, xn_ref, info_ref, cnt_ref, carry):
    tm = h1.shape[0]

    @pl.when(pl.program_id(0) == 0)
    def _():
        carry[...] = jnp.zeros_like(carry)

    xn = _rms(h1, nffn_ref[...])
    _store_token_tiles(xn_ref, xn, TAGGED_ROWS)
    x_hi = xn.astype(MXU_DTYPE)
    x_lo = (xn - x_hi.astype(F32)).astype(MXU_DTYPE)
    logits = (_dot(x_hi, wr_hi_ref[...]) + _dot(x_lo, wr_hi_ref[...]) + _dot(x_hi, wr_lo_ref[...])
              + br_ref[...])
    lane = lax.broadcasted_iota(jnp.int32, (tm, LANES), 1)
    vals, ids, hots = [], [], []
    for _ in range(TOP_K):
        m = jnp.max(logits, axis=-1, keepdims=True)
        ix = jnp.min(jnp.where(logits == m, lane, LANES), axis=-1, keepdims=True)
        hot = lane == ix
        vals.append(m)
        ids.append(ix)
        hots.append(hot)
        logits = jnp.where(hot, -jnp.inf, logits)
    es = [jnp.exp(v - vals[0]) for v in vals]
    den = es[0] + es[1] + es[2] + es[3]
    chosen = jnp.zeros((tm, LANES), F32)
    for hot in hots:
        chosen = chosen + jnp.where(hot, 1.0, 0.0)
    before = (lax.broadcasted_iota(jnp.int32, (tm, tm), 1) < lax.broadcasted_iota(jnp.int32, (tm, tm), 0))
    rank = _dot(jnp.where(before, 1.0, 0.0), chosen) + carry[...]
    info = jnp.zeros((tm, LANES), F32)
    for k in range(TOP_K):
        pos = jnp.sum(jnp.where(hots[k], rank, 0.0), axis=-1, keepdims=True)
        info = jnp.where(lane == k, es[k] / den, info)
        info = jnp.where(lane == TOP_K + k, ids[k].astype(F32), info)
        info = jnp.where(lane == 2 * TOP_K + k, pos, info)
    info_ref[...] = info
    token = (pl.program_id(0) * tm + lax.broadcasted_iota(jnp.int32, (tm, 1), 0)).astype(F32)
    tag = jnp.where(lane == 0, token, 0.0)
    for k in range(TOP_K):
        tag = jnp.where(lane == 1 + k, ids[k].astype(F32), tag)
    xn_ref[pl.ds(ROW_TILES, tm, stride=TAGGED_ROWS), :] = tag
    for s in range(ROW_TILES + 1, TAGGED_ROWS):
        xn_ref[pl.ds(s, tm, stride=TAGGED_ROWS), :] = jnp.zeros((tm, LANES), F32)
    carry[...] = carry[...] + jnp.sum(chosen, axis=0, keepdims=True)
    cnt_ref[...] = carry[...]


_ROUTE_OUT_SHAPES = lambda n: (jax.ShapeDtypeStruct((n, D_MODEL), F32), jax.ShapeDtypeStruct((n * TAGGED_ROWS, LANES), F32),
                               jax.ShapeDtypeStruct((n, LANES), F32), jax.ShapeDtypeStruct((1, LANES), F32))
_ROUTE_OUT_SPECS = (pl.BlockSpec((TOKEN_TILE, D_MODEL), lambda i: (i, 0)),
                    pl.BlockSpec((TOKEN_TILE * TAGGED_ROWS, LANES), lambda i: (i, 0)),
                    pl.BlockSpec((TOKEN_TILE, LANES), lambda i: (i, 0)),
                    pl.BlockSpec((1, LANES), lambda i: (0, 0)))


def _route_in_specs():
    return [_full((1, D_MODEL)), _full((D_MODEL, LANES)), _full((D_MODEL, LANES)), _full((1, LANES))]


def _even_out_kernel(yp_ref, o_ref, r_ref, hp_ref, hs_ref, gn_ref, w_ref, nffn_ref, wr_hi_ref, wr_lo_ref, br_ref,
                     h1_ref, xn_ref, info_ref, cnt_ref, carry, *, npt):
    o = o_ref[...]
    r = r_ref[...]
    parts = []
    for hd in range(GLA_HEADS):
        sl = slice(hd * GLA_DV, (hd + 1) * GLA_DV)
        oh = o[:, sl]
        oh = oh * lax.rsqrt(jnp.mean(oh * oh, axis=-1, keepdims=True) + RMS_EPS) * gn_ref[...]
        rh = r[:, sl]
        parts.append(oh * (rh * jax.nn.sigmoid(rh)))
    gla = jnp.concatenate(parts, axis=1)
    mix = _dot(yp_ref[...], w_ref[:POOL_WIDTH, :]) + _dot(gla, w_ref[POOL_WIDTH:, :])
    h1 = _split_rows(hp_ref, hs_ref, npt) + mix
    h1_ref[...] = h1
    _route(h1, nffn_ref, wr_hi_ref, wr_lo_ref, br_ref, xn_ref, info_ref, cnt_ref, carry)


def _even_out(y_pool, o_gla, proj, h_parts, n_p, gla_norm, w_out, nffn, wr_hi, wr_lo, br):
    n = proj.shape[0]
    row = lambda w, c: pl.BlockSpec((TOKEN_TILE, w), lambda i: (i, c))
    return pl.pallas_call(
        functools.partial(_even_out_kernel, npt=n_p // TOKEN_TILE),
        out_shape=_ROUTE_OUT_SHAPES(n),
        grid=(n // TOKEN_TILE,),
        in_specs=[row(POOL_WIDTH, 0), row(GLA_WIDTH, 0), row(GLA_WIDTH, (IN_EVEN_MAIN - GLA_WIDTH) // GLA_WIDTH)]
        + _split_specs(TOKEN_TILE, D_MODEL, n_p, h_parts[0] is h_parts[1])
        + [_full((1, GLA_DV)), _full((D_MODEL, D_MODEL))] + _route_in_specs(),
        out_specs=_ROUTE_OUT_SPECS,
        scratch_shapes=[pltpu.VMEM((1, LANES), F32)],
        compiler_params=_params(), name="even_out")(y_pool, o_gla, proj, *h_parts, gla_norm, w_out, nffn, wr_hi, wr_lo, br)


def _rope_tile(x, cos, sin, lo_half):
    swapped = jnp.where(lo_half, pltpu.roll(x, LANES - HEAD_DIM // 2, axis=1), pltpu.roll(x, HEAD_DIM // 2, axis=1))
    return x * cos + swapped * sin


def _odd_in_kernel(h_ref, g_ref, w_ref, b_ref, cos_ref, sin_ref, q_ref, kv_ref):
    xn = _rms(h_ref[...], g_ref[...])
    qkv = _dot(xn, w_ref[...]) + b_ref[...]
    cos = cos_ref[...]
    sin = sin_ref[...]
    lo_half = (lax.broadcasted_iota(jnp.int32, cos.shape, 1) % HEAD_DIM) < HEAD_DIM // 2
    for j in range(Q_WIDTH // LANES):
        sl = slice(j * LANES, (j + 1) * LANES)
        q_ref[:, sl] = _rope_tile(qkv[:, sl], cos, sin, lo_half).astype(q_ref.dtype)
    kv_ref[:, :KV_WIDTH] = _rope_tile(qkv[:, Q_WIDTH:Q_WIDTH + KV_WIDTH], cos, sin, lo_half)
    kv_ref[:, KV_WIDTH:] = qkv[:, Q_WIDTH + KV_WIDTH:]


def _odd_in(h, g, w_qkv, b_qkv, cos, sin, table_block):
    n = h.shape[0]
    wq = w_qkv.shape[1]
    tab = pl.BlockSpec((TOKEN_TILE, LANES), lambda i: (table_block(i), 0))
    return pl.pallas_call(
        _odd_in_kernel,
        out_shape=(jax.ShapeDtypeStruct((n, Q_WIDTH), MXU_DTYPE), jax.ShapeDtypeStruct((n, 2 * KV_WIDTH), F32)),
        grid=(n // TOKEN_TILE,),
        in_specs=[pl.BlockSpec((TOKEN_TILE, D_MODEL), lambda i: (i, 0)), _full((1, D_MODEL)),
                  _full((D_MODEL, wq)), _full((1, wq)), tab, tab],
        out_specs=(pl.BlockSpec((TOKEN_TILE, Q_WIDTH), lambda i: (i, 0)),
                   pl.BlockSpec((TOKEN_TILE, 2 * KV_WIDTH), lambda i: (i, 0))),
        compiler_params=_params(), name="odd_in")(h, g, w_qkv, b_qkv, cos, sin)


def _attn_kernel(sink_ref, q_ref, kv0_ref, kv1_ref, kv2_ref, ck0_ref, ck1_ref, cv0_ref, cv1_ref, o_ref, *, sq):
    i = pl.program_id(0)
    smp = sq.is_sample(i)
    t = sq.tile_in_seq(i)
    kv0 = kv0_ref[...]
    k_old = jnp.where(smp, ck0_ref[...], kv2_ref[:, :KV_WIDTH])
    k_mid = jnp.where(smp, ck1_ref[...], kv1_ref[:, :KV_WIDTH])
    v_old = jnp.where(smp, cv0_ref[...], kv2_ref[:, KV_WIDTH:])
    v_mid = jnp.where(smp, cv1_ref[...], kv1_ref[:, KV_WIDTH:])
    pad = jnp.zeros((CHUNK, KV_WIDTH), F32)
    keys = jnp.concatenate([k_old, k_mid, kv0[:, :KV_WIDTH], pad], axis=0)
    vals = jnp.concatenate([v_old, v_mid, kv0[:, KV_WIDTH:], pad], axis=0)
    n_keys = 4 * CHUNK
    lane = lax.broadcasted_iota(jnp.int32, (n_keys, KV_WIDTH), 1)
    lo = lane < HEAD_DIM
    keys_sw = pltpu.roll(keys, HEAD_DIM, axis=1)
    vals_sw = pltpu.roll(vals, HEAD_DIM, axis=1)
    kcol = lax.broadcasted_iota(jnp.int32, (1, n_keys), 1)
    first_valid = jnp.where(smp, 0, (2 - jnp.minimum(t, 2)) * CHUNK)
    key_ok = jnp.logical_and(kcol >= first_valid, kcol < 3 * CHUNK)
    n_pairs = N_Q_HEADS // N_KV_HEADS // 2
    pair_of_row = lax.broadcasted_iota(jnp.int32, (n_pairs * CHUNK, 1), 0) // CHUNK
    for g in range(N_KV_HEADS):
        own, other = (keys, keys_sw) if g == 0 else (keys_sw, keys)
        vown, vother = (vals, vals_sw) if g == 0 else (vals_sw, vals)
        kb = jnp.concatenate([jnp.where(lo, own, 0.0), jnp.where(lo, 0.0, other)], axis=0).astype(MXU_DTYPE)
        vb = jnp.concatenate([jnp.where(lo, vown, 0.0), jnp.where(lo, 0.0, vother)], axis=0).astype(MXU_DTYPE)
        pairs = [slice((g * n_pairs + pr) * LANES, (g * n_pairs + pr + 1) * LANES) for pr in range(n_pairs)]
        qg = jnp.concatenate([q_ref[:, sl] for sl in pairs], axis=0)
        s = _dot_nt(qg, kb) * (HEAD_DIM ** -0.5)
        halves = []
        for half in range(2):
            sh = jnp.where(key_ok, s[:, half * n_keys:(half + 1) * n_keys], NEG_INF)
            sink = jnp.zeros((n_pairs * CHUNK, 1), F32)
            for pr in range(n_pairs):
                sink = jnp.where(pair_of_row == pr, sink_ref[2 * (g * n_pairs + pr) + half], sink)
            m = jnp.maximum(jnp.max(sh, axis=-1, keepdims=True), sink)
            p = jnp.exp(sh - m)
            halves.append(p / (jnp.sum(p, axis=-1, keepdims=True) + jnp.exp(sink - m)))
        o = _dot(jnp.concatenate(halves, axis=1), vb)
        for pr, sl in enumerate(pairs):
            o_ref[:, sl] = o[pr * CHUNK:(pr + 1) * CHUNK].astype(o_ref.dtype)


def _attention(sinks, q, kv, ck, cv, sq):
    n = q.shape[0]
    npt = sq.n_prompt_tiles
    prev = lambda d: (lambda i, s: (jnp.where(sq.is_sample(i), i, jnp.maximum(i - d, 0)), 0))
    cache = lambda d: (lambda i, s: (jnp.where(sq.is_sample(i), 2 * (i - npt) + d, 0), 0))
    kvspec = lambda f: pl.BlockSpec((CHUNK, 2 * KV_WIDTH), f)
    cspec = lambda f: pl.BlockSpec((CHUNK, KV_WIDTH), f)
    grid_spec = pltpu.PrefetchScalarGridSpec(
        num_scalar_prefetch=1, grid=(sq.n_tiles,),
        in_specs=[pl.BlockSpec((CHUNK, Q_WIDTH), lambda i, s: (i, 0)),
                  kvspec(prev(0)), kvspec(prev(1)), kvspec(prev(2)),
                  cspec(cache(0)), cspec(cache(1)), cspec(cache(0)), cspec(cache(1))],
        out_specs=pl.BlockSpec((CHUNK, Q_WIDTH), lambda i, s: (i, 0)))
    return pl.pallas_call(
        functools.partial(_attn_kernel, sq=sq),
        out_shape=jax.ShapeDtypeStruct((n, Q_WIDTH), MXU_DTYPE),
        grid_spec=grid_spec, compiler_params=_params(), name="attention")(sinks, q, kv, kv, kv, ck, ck, cv, cv)


def _odd_out_kernel(o_ref, h_ref, w_ref, b_ref, nffn_ref, wr_hi_ref, wr_lo_ref, br_ref,
                    h1_ref, xn_ref, info_ref, cnt_ref, carry):
    h1 = h_ref[...] + _dot(o_ref[...], w_ref[...]) + b_ref[...]
    h1_ref[...] = h1
    _route(h1, nffn_ref, wr_hi_ref, wr_lo_ref, br_ref, xn_ref, info_ref, cnt_ref, carry)


def _odd_out(o, h, w_out, b_out, nffn, wr_hi, wr_lo, br):
    n = h.shape[0]
    row = pl.BlockSpec((TOKEN_TILE, D_MODEL), lambda i: (i, 0))
    return pl.pallas_call(
        _odd_out_kernel,
        out_shape=_ROUTE_OUT_SHAPES(n),
        grid=(n // TOKEN_TILE,),
        in_specs=[row, row, _full((D_MODEL, D_MODEL)), _full((1, D_MODEL))] + _route_in_specs(),
        out_specs=_ROUTE_OUT_SPECS,
        scratch_shapes=[pltpu.VMEM((1, LANES), F32)],
        compiler_params=_params(), name="odd_out")(o, h, w_out, b_out, nffn, wr_hi, wr_lo, br)


def _token_copy(src, s, dst, d, sem, rows=ROW_TILES):
    return pltpu.make_async_copy(src.at[pl.ds(pl.multiple_of(s, rows), rows)],
                                 dst.at[pl.ds(pl.multiple_of(d, rows), rows)], sem)


ZERO_TOKENS = EXPERT_TILE // 2


def _zero_fill(fill_ref, zeros, xs_out, zsem):
    zeros[...] = jnp.zeros_like(zeros)

    def sweep(wait):
        def go(cp):
            cp.wait() if wait else cp.start()

        def tail(e, c):
            off, length = fill_ref[e], fill_ref[N_EXPERTS + e]
            for bit in range(EXPERT_TILE.bit_length() - 1):
                rows = (1 << bit) * TAGGED_ROWS

                @pl.when((length >> bit) & 1 == 1)
                def _():
                    o = pl.multiple_of(off + (length & ((1 << bit) - 1)) * TAGGED_ROWS, TAGGED_ROWS)
                    go(pltpu.make_async_copy(zeros.at[pl.ds(0, rows)], xs_out.at[pl.ds(o, rows)], zsem))
            return c

        lax.fori_loop(0, N_EXPERTS, tail, 0)

        def unused(t, c):
            o = pl.multiple_of(fill_ref[2 * N_EXPERTS] + t * ZERO_TOKENS * TAGGED_ROWS, TAGGED_ROWS)
            go(pltpu.make_async_copy(zeros, xs_out.at[pl.ds(o, ZERO_TOKENS * TAGGED_ROWS)], zsem))
            return c

        lax.fori_loop(0, fill_ref[2 * N_EXPERTS + 1], unused, 0)

    sweep(wait=False)
    sweep(wait=True)


def _dispatch_kernel(fill_ref, dest_hbm, x_ref, xs_out, dest_smem, zeros, sem, dsem, zsem):
    i = pl.program_id(0)
    n_slots = TOKEN_TILE * TOP_K

    @pl.when(i == 0)
    def _():
        _zero_fill(fill_ref, zeros, xs_out, zsem)

    cp = pltpu.make_async_copy(dest_hbm.at[pl.ds(i * n_slots, n_slots)], dest_smem, dsem)
    cp.start()
    cp.wait()

    def issue(r, c):
        _token_copy(x_ref, (r // TOP_K) * TAGGED_ROWS, xs_out, dest_smem[r], sem, TAGGED_ROWS).start()
        return c

    lax.fori_loop(0, n_slots, issue, 0, unroll=8)
    whole = xs_out.at[pl.ds(0, n_slots * TAGGED_ROWS)]
    pltpu.make_async_copy(whole, whole, sem).wait()


def _dispatch(fill, dest, xt, n_rows):
    n = xt.shape[0] // TAGGED_ROWS
    grid_spec = pltpu.PrefetchScalarGridSpec(
        num_scalar_prefetch=1, grid=(n // TOKEN_TILE,),
        in_specs=[pl.BlockSpec(memory_space=pl.ANY),
                  pl.BlockSpec((TOKEN_TILE * TAGGED_ROWS, LANES), lambda i, f: (i, 0))],
        out_specs=pl.BlockSpec(memory_space=pl.ANY),
        scratch_shapes=[pltpu.SMEM((TOKEN_TILE * TOP_K,), jnp.int32), pltpu.VMEM((ZERO_TOKENS * TAGGED_ROWS, LANES), F32),
                        pltpu.SemaphoreType.DMA, pltpu.SemaphoreType.DMA, pltpu.SemaphoreType.DMA])
    return pl.pallas_call(
        _dispatch_kernel, out_shape=jax.ShapeDtypeStruct((n_rows * TAGGED_ROWS, LANES), F32),
        grid_spec=grid_spec, compiler_params=_params(), name="moe_dispatch")(fill, dest, xt)


def _experts_kernel(te_ref, tr_ref, x_ref, wgu_ref, bgu_ref, wd_ref, bd_ref, yt_ref,
                    wgu_mxu, wd_mxu, ybuf, place_vmem, place_smem, ysems, psem, *, n_tokens):
    i = pl.program_id(0)
    n = pl.num_programs(0)
    slot = i % 2
    tile_rows = EXPERT_TILE * ROW_TILES
    spare = yt_ref.shape[0] - 2 * tile_rows

    def rows_done(s):
        whole = yt_ref.at[pl.ds(0, tile_rows)]
        pltpu.make_async_copy(whole, whole, ysems.at[s]).wait()

    @pl.when(i == 0)
    def _():
        ybuf[0] = jnp.zeros((tile_rows, LANES), F32)
        for s in range(2):
            cp = pltpu.make_async_copy(ybuf.at[0], yt_ref.at[pl.ds(spare + s * tile_rows, tile_rows)], ysems.at[0])
            cp.start()
            cp.wait()

    @pl.when(jnp.logical_or(i == 0, te_ref[i] != te_ref[jnp.maximum(i - 1, 0)]))
    def _():
        wgu_mxu[...] = wgu_ref[0, 0].astype(MXU_DTYPE)
        wd_mxu[...] = wd_ref[0, 0].astype(MXU_DTYPE)

    @pl.when(jnp.logical_and(i >= 2, tr_ref[jnp.maximum(i - 2, 0)] > 0))
    def _():
        rows_done(slot)

    n_valid = tr_ref[i]

    @pl.when(n_valid > 0)
    def _():
        tag = x_ref[pl.ds(ROW_TILES, EXPERT_TILE, stride=TAGGED_ROWS), :]
        lane = lax.broadcasted_iota(jnp.int32, (EXPERT_TILE, LANES), 1)
        mine = jnp.logical_and(jnp.logical_and(lane >= 1, lane <= TOP_K), tag == te_ref[i].astype(F32))
        rank = jnp.sum(jnp.where(mine, (lane - 1).astype(F32), 0.0), axis=-1, keepdims=True)
        place = rank * n_tokens + tag[:, 0:1]
        hi = jnp.floor(place * (1.0 / 65536.0))
        mid = jnp.floor((place - hi * 65536.0) * (1.0 / 256.0))
        digits = jnp.where(lane == 0, place - hi * 65536.0 - mid * 256.0, jnp.where(lane == 1, mid, jnp.where(lane == 2, hi, 0.0)))
        pick = jnp.where(lax.broadcasted_iota(jnp.int32, (ROW_TILES, LANES), 0)
                         == lax.broadcasted_iota(jnp.int32, (ROW_TILES, LANES), 1), 1.0, 0.0)
        planes = _dot_nt(pick, digits)
        place_row = planes[0:1] + 256.0 * planes[1:2] + 65536.0 * planes[2:3]
        col = lax.broadcasted_iota(jnp.int32, (1, EXPERT_TILE), 1)
        first_row = jnp.where(col < n_valid, place_row.astype(jnp.int32), (spare // ROW_TILES) + slot * EXPERT_TILE + col)
        place_vmem[...] = jnp.broadcast_to(first_row * ROW_TILES, place_vmem.shape)
        to_smem = pltpu.make_async_copy(place_vmem, place_smem, psem)
        to_smem.start()

        x = _load_token_tiles(x_ref, EXPERT_TILE, TAGGED_ROWS)
        hgu = _dot(x, wgu_mxu[...]) + bgu_ref[0, 0]
        gate = jnp.minimum(hgu[:, :D_MODEL], SWIGLU_LIMIT)
        up = jnp.clip(hgu[:, D_MODEL:], -SWIGLU_LIMIT, SWIGLU_LIMIT)
        act = (up + 1.0) * gate * jax.nn.sigmoid(SWIGLU_ALPHA * gate)
        _store_token_tiles(ybuf.at[slot], _dot(act, wd_mxu[...]) + bd_ref[0, 0])
        to_smem.wait()

        def issue(j, c):
            _token_copy(ybuf.at[slot], j * ROW_TILES, yt_ref, place_smem[0, j], ysems.at[slot]).start()
            return c

        lax.fori_loop(0, EXPERT_TILE, issue, 0, unroll=8)

    @pl.when(i == n - 1)
    def _():
        @pl.when(jnp.logical_and(i >= 1, tr_ref[jnp.maximum(i - 1, 0)] > 0))
        def _():
            rows_done(1 - slot)

        @pl.when(n_valid > 0)
        def _():
            rows_done(slot)


def _experts(layer, tile_expert, tile_rows, xs, wgu, bgu, wd, bd, n_tokens):
    rows = xs.shape[0] // TAGGED_ROWS
    w = lambda shape: pl.BlockSpec((1, 1) + shape, lambda i, te, tr: (layer, te[i], 0, 0))
    grid_spec = pltpu.PrefetchScalarGridSpec(
        num_scalar_prefetch=2, grid=(rows // EXPERT_TILE,),
        in_specs=[pl.BlockSpec((EXPERT_TILE * TAGGED_ROWS, LANES), lambda i, te, tr: (i, 0)),
                  w((D_MODEL, 2 * D_MODEL)), w((1, 2 * D_MODEL)), w((D_MODEL, D_MODEL)), w((1, D_MODEL))],
        out_specs=pl.BlockSpec(memory_space=pl.ANY),
        scratch_shapes=[pltpu.VMEM((D_MODEL, 2 * D_MODEL), MXU_DTYPE), pltpu.VMEM((D_MODEL, D_MODEL), MXU_DTYPE),
                        pltpu.VMEM((2, EXPERT_TILE * ROW_TILES, LANES), F32),
                        pltpu.VMEM((ROW_TILES, EXPERT_TILE), jnp.int32), pltpu.SMEM((ROW_TILES, EXPERT_TILE), jnp.int32),
                        pltpu.SemaphoreType.DMA((2,)), pltpu.SemaphoreType.DMA])
    yt_rows = (n_tokens * TOP_K + 2 * EXPERT_TILE) * ROW_TILES
    return pl.pallas_call(
        functools.partial(_experts_kernel, n_tokens=n_tokens), out_shape=jax.ShapeDtypeStruct((yt_rows, LANES), F32),
        grid_spec=grid_spec, compiler_params=_params(), name="moe_experts")(
            tile_expert, tile_rows, xs, wgu, bgu, wd, bd)


def _combine_kernel(*refs, final, npt):
    y_refs, outs = refs[:TOP_K], refs[TOP_K + 7:]
    info_ref, h_ref, p_prompt_ref, p_sample_ref, pp_ref, pg_ref, nf_ref = refs[TOP_K:TOP_K + 7]
    i = pl.program_id(0)
    gates = info_ref[...]
    moe = None
    for k in range(TOP_K):
        rows = _load_token_tiles(y_refs[k], COMBINE_TILE)
        moe = gates[:, k:k + 1] * rows if moe is None else moe + gates[:, k:k + 1] * rows
    h2 = h_ref[...] + moe
    p = jnp.where(i < npt, p_prompt_ref[0], p_sample_ref[0])
    h3 = h2 + jax.nn.sigmoid(_dot(h2, pg_ref[...])) * _dot(p, pp_ref[...])
    if not final:
        outs[0][...] = h3
        return
    y = _rms(h3, nf_ref[...])

    @pl.when(i < npt)
    def _():
        outs[0][...] = y

    @pl.when(i >= npt)
    def _():
        outs[1][...] = y


def _combine(layer, yt, info, h1, p_prompt, p_sample, ple_proj, ple_gate, norm_final, final):
    n = h1.shape[0]
    n_p = p_prompt.shape[1]
    npt = n_p // COMBINE_TILE
    steps = n // COMBINE_TILE
    row = lambda w: pl.BlockSpec((COMBINE_TILE, w), lambda i: (i, 0))
    rank_rows = lambda k: pl.BlockSpec((COMBINE_TILE * ROW_TILES, LANES), lambda i: (k * steps + i, 0))
    prompt_rows = lambda i: jnp.minimum(i, npt - 1)
    sample_rows = lambda i: jnp.maximum(i - npt, 0)
    if final:
        out_shape = (jax.ShapeDtypeStruct((n_p, D_MODEL), F32), jax.ShapeDtypeStruct((n - n_p, D_MODEL), F32))
        out_specs = (pl.BlockSpec((COMBINE_TILE, D_MODEL), lambda i: (prompt_rows(i), 0)),
                     pl.BlockSpec((COMBINE_TILE, D_MODEL), lambda i: (sample_rows(i), 0)))
    else:
        out_shape, out_specs = jax.ShapeDtypeStruct((n, D_MODEL), F32), row(D_MODEL)
    return pl.pallas_call(
        functools.partial(_combine_kernel, final=final, npt=npt),
        out_shape=out_shape,
        grid=(steps,),
        in_specs=[rank_rows(k) for k in range(TOP_K)] + [
            row(LANES), row(D_MODEL),
            pl.BlockSpec((1, COMBINE_TILE, PLE_DIM), lambda i: (layer, prompt_rows(i), 0)),
            pl.BlockSpec((1, COMBINE_TILE, PLE_DIM), lambda i: (layer, sample_rows(i), 0)),
            _full((PLE_DIM, D_MODEL)), _full((D_MODEL, D_MODEL)), _full((1, D_MODEL))],
        out_specs=out_specs,
        compiler_params=_params(), name="moe_combine")(
            *([yt] * TOP_K), info, h1, p_prompt, p_sample, ple_proj, ple_gate, norm_final)


def _moe_and_embed(layer, h1, xt, info, counts, p_prompt, p_sample, wgu, bgu, wd, bd, ple_proj, ple_gate,
                   norm_final, final):
    n = h1.shape[0]
    n_tiles = (n * TOP_K + N_EXPERTS * (EXPERT_TILE - 1)) // EXPERT_TILE + 1
    ids = info[:, TOP_K:2 * TOP_K].astype(jnp.int32)
    rank = info[:, 2 * TOP_K:3 * TOP_K].astype(jnp.int32)
    cnt = counts[0, :N_EXPERTS].astype(jnp.int32)
    padded = ((cnt + EXPERT_TILE - 1) // EXPERT_TILE) * EXPERT_TILE
    ends = jnp.cumsum(padded)
    starts = ends - padded
    dest = ((starts[ids] + rank) * TAGGED_ROWS).reshape(-1)
    tile_start = jnp.arange(n_tiles, dtype=jnp.int32) * EXPERT_TILE
    tile_expert = jnp.minimum(jnp.sum((tile_start[:, None] >= ends[None, :]).astype(jnp.int32), axis=1), N_EXPERTS - 1)
    tile_rows = jnp.clip(cnt[tile_expert] - (tile_start - starts[tile_expert]), 0, EXPERT_TILE)
    tile_rows = jnp.where(tile_start < ends[-1], tile_rows, 0).astype(jnp.int32)
    fill = jnp.concatenate([(starts + cnt) * TAGGED_ROWS, padded - cnt,
                            jnp.stack([ends[-1] * TAGGED_ROWS,
                                       (n_tiles * EXPERT_TILE - ends[-1]) // ZERO_TOKENS])]).astype(jnp.int32)
    xs = _dispatch(fill, dest, xt, n_tiles * EXPERT_TILE)
    yt = _experts(layer, tile_expert, tile_rows, xs, wgu, bgu, wd, bd, n)
    return _combine(layer, yt, info, h1, p_prompt, p_sample, ple_proj, ple_gate, norm_final, final)


def _rope_tables(prompt_len, n_sample_seq, sample_len):
    half = HEAD_DIM // 2
    inv = jnp.power(jnp.float32(ROPE_THETA), -jnp.arange(half, dtype=F32) / half)
    pos = jnp.concatenate([jnp.arange(prompt_len), jnp.tile(PAST_LEN + jnp.arange(sample_len), n_sample_seq)])
    ang = pos.astype(F32)[:, None] * inv[None, :]
    cos = jnp.tile(jnp.cos(ang), (1, LANES // half))
    sin = jnp.tile(jnp.concatenate([-jnp.sin(ang), jnp.sin(ang)], axis=1), (1, LANES // HEAD_DIM))
    return cos, sin


def kernel(x_prompt, x_sample, state_pool, state_gla, cache_k, cache_v, p_prompt, p_sample, norm_mix, norm_ffn, norm_final, w_in_even, pool_w, pool_scale, gla_w_gate, gla_b_gate, gla_norm, w_out_even, w_qkv_odd, b_qkv_odd, attn_sinks, w_out_odd, b_out_odd, w_router, b_router, w_gate_up, b_gate_up, w_down, b_down, ple_proj, ple_gate):
    bsz, t_len, _ = x_prompt.shape
    dec_bsz, dec_len, _ = x_sample.shape
    depth = norm_mix.shape[0]
    n_p, n_s = bsz * t_len, dec_bsz * dec_len
    n = n_p + n_s
    assert dec_len == CHUNK and n_s == TOKEN_TILE and t_len % TOKEN_TILE == 0 and cache_k.shape[2] == WINDOW
    sq = _Seq(bsz, t_len, dec_bsz)
    bf = lambda a: a.astype(MXU_DTYPE)
    row = lambda a: a.reshape(1, -1)

    h_parts = (x_prompt.reshape(n_p, D_MODEL), x_sample.reshape(n_s, D_MODEL))
    p_parts = (p_prompt.reshape(depth, n_p, PLE_DIM), p_sample.reshape(depth, n_s, PLE_DIM))
    b_gu = b_gate_up.reshape(depth, N_EXPERTS, 1, 2 * D_MODEL)
    b_dn = b_down.reshape(depth, N_EXPERTS, 1, D_MODEL)
    cos, sin = _rope_tables(t_len, dec_bsz, dec_len)
    tiles_per_seq = t_len // TOKEN_TILE
    table_block = lambda i: jnp.where(i < bsz * tiles_per_seq, i % tiles_per_seq, tiles_per_seq)

    pools, glas, new_k, new_v = [], [], [], []
    for i in range(depth):
        if i % 2 == 0:
            e = i // 2
            w_in = jnp.pad(bf(w_in_even[e]), ((0, 0), (0, (-IN_EVEN) % LANES)))
            wg = jnp.pad(bf(gla_w_gate[e]), ((0, w_in.shape[1] - IN_EVEN_MAIN - GLA_RANK), (0, 0)))
            proj = _even_in(h_parts, n, n_p, row(norm_mix[i]), w_in, wg, row(gla_b_gate[e]))
            pool_init = jnp.concatenate(
                [jnp.zeros((bsz, 16, POOL_WIDTH), F32), jnp.pad(state_pool[e], ((0, 0), (1, 0), (0, 0)))], axis=0)
            y_pool, pool_st = _pool(proj, pool_init, bf(pool_w[e]), row(pool_scale[e]), sq)
            gla_init = jnp.concatenate([jnp.zeros((bsz,) + state_gla.shape[2:], F32), state_gla[e]], axis=0)
            o_gla, gla_st = _gla(proj, gla_init, sq)
            pools.append(pool_st[:, 1:])
            glas.append(gla_st)
            wr = jnp.pad(w_router[i], ((0, 0), (0, LANES - N_EXPERTS)))
            wr_hi = bf(wr)
            wr_lo = bf(wr - wr_hi.astype(F32))
            br = jnp.pad(row(b_router[i]), ((0, 0), (0, LANES - N_EXPERTS)), constant_values=NEG_INF)
            h1, xn, info, counts = _even_out(y_pool, o_gla, proj, h_parts, n_p, row(gla_norm[e]), bf(w_out_even[e]),
                                             row(norm_ffn[i]), wr_hi, wr_lo, br)
        else:
            o = i // 2
            h = h_parts[0]
            q, kv = _odd_in(h, row(norm_mix[i]), bf(w_qkv_odd[o]), row(b_qkv_odd[o]), cos, sin, table_block)
            ck = cache_k[o].reshape(dec_bsz * WINDOW, KV_WIDTH)
            cv = cache_v[o].reshape(dec_bsz * WINDOW, KV_WIDTH)
            att = _attention(attn_sinks[o], q, kv, ck, cv, sq)
            kv_p = kv[:n_p].reshape(bsz, t_len, 2 * KV_WIDTH)[:, -WINDOW:]
            kv_s = kv[n_p:].reshape(dec_bsz, dec_len, 2 * KV_WIDTH)
            hd = (N_KV_HEADS, HEAD_DIM)
            new_k.append((kv_p[..., :KV_WIDTH].reshape(bsz, WINDOW, *hd),
                          jnp.concatenate([cache_k[o], kv_s[..., :KV_WIDTH].reshape(dec_bsz, dec_len, *hd)], axis=1)[:, -WINDOW:]))
            new_v.append((kv_p[..., KV_WIDTH:].reshape(bsz, WINDOW, *hd),
                          jnp.concatenate([cache_v[o], kv_s[..., KV_WIDTH:].reshape(dec_bsz, dec_len, *hd)], axis=1)[:, -WINDOW:]))
            wr = jnp.pad(w_router[i], ((0, 0), (0, LANES - N_EXPERTS)))
            wr_hi = bf(wr)
            wr_lo = bf(wr - wr_hi.astype(F32))
            br = jnp.pad(row(b_router[i]), ((0, 0), (0, LANES - N_EXPERTS)), constant_values=NEG_INF)
            h1, xn, info, counts = _odd_out(att, h, bf(w_out_odd[o]), row(b_out_odd[o]),
                                            row(norm_ffn[i]), wr_hi, wr_lo, br)
        final = i == depth - 1
        out = _moe_and_embed(i, h1, xn, info, counts, *p_parts, w_gate_up, b_gu, w_down, b_dn,
                             bf(ple_proj[i]), bf(ple_gate[i]), row(norm_final), final)
        h_parts = out if final else (out, out)

    y_prompt = h_parts[0].reshape(bsz, t_len, D_MODEL)
    y_sample = h_parts[1].reshape(dec_bsz, dec_len, D_MODEL)
    pool_all = jnp.stack(pools)
    gla_all = jnp.stack(glas)
    return (y_prompt, y_sample, pool_all[:, :bsz], gla_all[:, :bsz],
            jnp.stack([k[0] for k in new_k]), jnp.stack([v[0] for v in new_v]),
            pool_all[:, bsz:], gla_all[:, bsz:],
            jnp.stack([k[1] for k in new_k]), jnp.stack([v[1] for v in new_v]))
```

```python
import functools

import jax
import jax.numpy as jnp
from jax import lax
from jax.experimental import pallas as pl
from jax.experimental.pallas import tpu as pltpu

F32 = jnp.float32
MXU_DTYPE = jnp.bfloat16

V7X_VMEM_BYTES = 64 * 1024 * 1024
VMEM_LIMIT = (V7X_VMEM_BYTES * 7) // 8
LANES = 128

D_MODEL = 1024
CHUNK = 64
PAST_LEN = 2048
PLE_DIM = 256
RMS_EPS = 1e-6
POOL_WINDOWS = (2, 4, 8, 16)
POOL_WIDTH = 512
POOL_GROUP_DIM = 128
POOL_STATE = 15
GLA_HEADS = 4
GLA_DK = 64
GLA_DV = 128
GLA_KEY_WIDTH = GLA_HEADS * GLA_DK
GLA_WIDTH = GLA_HEADS * GLA_DV
GLA_RANK = 16
GLA_TAU = 16.0
GLA_BLOCK = 16
IN_EVEN = POOL_WIDTH + 2 * GLA_KEY_WIDTH + 2 * GLA_WIDTH + GLA_RANK
IN_EVEN_MAIN = IN_EVEN - GLA_RANK
PROJ_WIDTH = IN_EVEN_MAIN + GLA_KEY_WIDTH
N_Q_HEADS = 16
N_KV_HEADS = 2
HEAD_DIM = 64
WINDOW = 128
ROPE_THETA = 10000.0
Q_WIDTH = N_Q_HEADS * HEAD_DIM
KV_WIDTH = N_KV_HEADS * HEAD_DIM
N_EXPERTS = 32
TOP_K = 4
SWIGLU_LIMIT = 7.0
SWIGLU_ALPHA = 1.702
NEG_INF = -1e30

TOKEN_TILE = 512
EXPERT_TILE = 512
COMBINE_TILE = 256


def _dot(a, b):
    return jnp.dot(a.astype(MXU_DTYPE), b.astype(MXU_DTYPE), preferred_element_type=F32)


def _dot_nt(a, b):
    return lax.dot_general(a.astype(MXU_DTYPE), b.astype(MXU_DTYPE), (((1,), (1,)), ((), ())),
                           preferred_element_type=F32)


def _split3(x):
    x1 = x.astype(MXU_DTYPE)
    r1 = x - x1.astype(F32)
    x2 = r1.astype(MXU_DTYPE)
    x3 = (r1 - x2.astype(F32)).astype(MXU_DTYPE)
    return x1, x2, x3


def _rms(x, g):
    return x * lax.rsqrt(jnp.mean(x * x, axis=-1, keepdims=True) + RMS_EPS) * g


ROW_TILES = D_MODEL // LANES
TAGGED_ROWS = 2 * ROW_TILES


def _store_token_tiles(ref, x, pitch=ROW_TILES):
    for s in range(ROW_TILES):
        ref[pl.ds(s, x.shape[0], stride=pitch), :] = x[:, s * LANES:(s + 1) * LANES]


def _load_token_tiles(ref, rows, pitch=ROW_TILES):
    return jnp.concatenate([ref[pl.ds(s, rows, stride=pitch), :] for s in range(ROW_TILES)], axis=1)


def _params(n_axes=1):
    return pltpu.CompilerParams(dimension_semantics=("arbitrary",) * n_axes, vmem_limit_bytes=VMEM_LIMIT)


def _full(shape):
    return pl.BlockSpec(shape, lambda *_: (0,) * len(shape))


class _Seq:
    def __init__(self, n_prompt_seq, prompt_len, n_sample_seq):
        self.tiles_per_seq = prompt_len // CHUNK
        self.n_prompt_seq = n_prompt_seq
        self.n_prompt_tiles = n_prompt_seq * self.tiles_per_seq
        self.n_tiles = self.n_prompt_tiles + n_sample_seq
        self.n_seq = n_prompt_seq + n_sample_seq

    def is_sample(self, i):
        return i >= self.n_prompt_tiles

    def tile_in_seq(self, i):
        return jnp.where(self.is_sample(i), 0, i % self.tiles_per_seq)

    def seq(self, i):
        return jnp.where(self.is_sample(i), self.n_prompt_seq + i - self.n_prompt_tiles, i // self.tiles_per_seq)

    def last(self, i):
        return jnp.logical_or(self.is_sample(i), i % self.tiles_per_seq == self.tiles_per_seq - 1)

    def pos0(self, i):
        return jnp.where(self.is_sample(i), PAST_LEN, self.tile_in_seq(i) * CHUNK)


class _Group:
    def __init__(self, n_seq, seq_len, tile, row0, pos_base):
        assert seq_len % tile == 0 and row0 % tile == 0
        self.n_seq, self.tile, self.pos_base = n_seq, tile, pos_base
        self.tiles_per_seq = seq_len // tile
        self.n_tiles = n_seq * self.tiles_per_seq
        self.block0 = row0 // tile

    def block(self, i):
        return self.block0 + i

    def seq(self, i):
        return i // self.tiles_per_seq

    def tile_in_seq(self, i):
        return i % self.tiles_per_seq

    def last(self, i):
        return i % self.tiles_per_seq == self.tiles_per_seq - 1

    def pos0(self, i):
        return self.pos_base + self.tile_in_seq(i) * self.tile


SEQ_TILE = 256


def _split_specs(tile, width, n_prompt, same_array):
    npt = n_prompt // tile
    off = npt if same_array else 0
    return [pl.BlockSpec((tile, width), lambda i: (jnp.minimum(i, npt - 1), 0)),
            pl.BlockSpec((tile, width), lambda i: (jnp.maximum(i - npt, 0) + off, 0))]


def _split_rows(prompt_ref, sample_ref, n_prompt_tiles):
    return jnp.where(pl.program_id(0) < n_prompt_tiles, prompt_ref[...], sample_ref[...])


def _even_in_kernel(hp_ref, hs_ref, g_ref, w_ref, wg_ref, bg_ref, out_ref, *, npt):
    xn = _rms(_split_rows(hp_ref, hs_ref, npt), g_ref[...])
    proj = _dot(xn, w_ref[...])
    z = proj[:, IN_EVEN_MAIN:]
    a = _dot(z, wg_ref[...]) + bg_ref[...]
    log_alpha = (jnp.minimum(a, 0.0) - jnp.log1p(jnp.exp(-jnp.abs(a)))) * (1.0 / GLA_TAU)
    out_ref[:, :IN_EVEN_MAIN] = proj[:, :IN_EVEN_MAIN]
    out_ref[:, IN_EVEN_MAIN:] = log_alpha


def _even_in(h_parts, n, n_p, g, w_in, wg, bg):
    wp = w_in.shape[1]
    return pl.pallas_call(
        functools.partial(_even_in_kernel, npt=n_p // TOKEN_TILE),
        out_shape=jax.ShapeDtypeStruct((n, PROJ_WIDTH), F32),
        grid=(n // TOKEN_TILE,),
        in_specs=_split_specs(TOKEN_TILE, D_MODEL, n_p, h_parts[0] is h_parts[1]) + [
            _full((1, D_MODEL)), _full((D_MODEL, wp)), _full((wp - IN_EVEN_MAIN, GLA_KEY_WIDTH)),
            _full((1, GLA_KEY_WIDTH))],
        out_specs=pl.BlockSpec((TOKEN_TILE, PROJ_WIDTH), lambda i: (i, 0)),
        compiler_params=_params(), name="even_in")(*h_parts, g, w_in, wg, bg)


def _pool_kernel(u_ref, init_ref, pw_ref, ps_ref, y_ref, st_ref, buf, *, grp):
    i = pl.program_id(0)
    rows = grp.tile

    @pl.when(grp.tile_in_seq(i) == 0)
    def _():
        buf[0:16, :] = init_ref[0]

    u = u_ref[...]
    buf[16:16 + rows, :] = u
    pos = grp.pos0(i) + lax.broadcasted_iota(jnp.int32, (rows, 1), 0)
    for g, w in enumerate(POOL_WINDOWS):
        sl = slice(g * POOL_GROUP_DIM, (g + 1) * POOL_GROUP_DIM)
        acc = u[:, sl]
        for j in range(1, w):
            acc = acc + buf[16 - j:16 - j + rows, sl]
        cnt = jnp.minimum(w, pos + 1).astype(F32)
        d = acc / cnt - u[:, sl]
        y_ref[:, sl] = _dot(d, pw_ref[g]) * ps_ref[:, sl]
    tail = buf[rows:rows + 16, :]
    st_ref[0] = tail
    buf[0:16, :] = tail


def _pool(proj, init, pool_w, pool_scale, grp):
    return pl.pallas_call(
        functools.partial(_pool_kernel, grp=grp),
        out_shape=(jax.ShapeDtypeStruct((grp.n_tiles * grp.tile, POOL_WIDTH), F32),
                   jax.ShapeDtypeStruct((grp.n_seq, 16, POOL_WIDTH), F32)),
        grid=(grp.n_tiles,),
        in_specs=[pl.BlockSpec((grp.tile, POOL_WIDTH), lambda i: (grp.block(i), 0)),
                  pl.BlockSpec((1, 16, POOL_WIDTH), lambda i: (grp.seq(i), 0, 0)),
                  _full((len(POOL_WINDOWS), POOL_GROUP_DIM, POOL_GROUP_DIM)), _full((1, POOL_WIDTH))],
        out_specs=(pl.BlockSpec((grp.tile, POOL_WIDTH), lambda i: (i, 0)),
                   pl.BlockSpec((1, 16, POOL_WIDTH), lambda i: (grp.seq(i), 0, 0))),
        scratch_shapes=[pltpu.VMEM((grp.tile + 16, POOL_WIDTH), F32)],
        compiler_params=_params(), name="pool")(proj, init, pool_w, pool_scale)


def _gla_kernel(q_ref, k_ref, v_ref, g_ref, s0_ref, o_ref, sout_ref, state, *, grp):
    i = pl.program_id(0)
    tile = grp.tile
    n_blk = tile // GLA_BLOCK

    @pl.when(grp.tile_in_seq(i) == 0)
    def _():
        state[...] = s0_ref[0]

    g = g_ref[...]
    row = lax.broadcasted_iota(jnp.int32, (tile, tile), 0)
    col = lax.broadcasted_iota(jnp.int32, (tile, tile), 1)
    same = (row >> 4) == (col >> 4)
    causal = jnp.logical_and(same, col <= row)
    tri = jnp.where(causal, 1.0, 0.0).astype(MXU_DTYPE)
    ones = jnp.where(same, 1.0, 0.0).astype(MXU_DTYPE)
    g1, g2, g3 = _split3(g)
    b = _dot(tri, g1) + _dot(tri, g2) + _dot(tri, g3)
    b_last = _dot(ones, g1) + _dot(ones, g2) + _dot(ones, g3)
    q_t = q_ref[...] * (GLA_DK ** -0.5) * jnp.exp(b)
    k = k_ref[...]
    k_t = k * jnp.exp(-b)
    k_dec_t = (k * jnp.exp(b_last - b)).T
    sel = jnp.where((lax.broadcasted_iota(jnp.int32, (tile, LANES), 0) >> 4)
                    == lax.broadcasted_iota(jnp.int32, (tile, LANES), 1), 1.0, 0.0).astype(MXU_DTYPE)
    t1, t2, t3 = _split3(g.T)
    blk_decay = jnp.exp(_dot(t1, sel) + _dot(t2, sel) + _dot(t3, sel))
    v = v_ref[...]
    blk_of_col = lax.broadcasted_iota(jnp.int32, (GLA_DK, tile), 1) >> 4
    for h in range(GLA_HEADS):
        ks = slice(h * GLA_DK, (h + 1) * GLA_DK)
        vs = slice(h * GLA_DV, (h + 1) * GLA_DV)
        vh = v[:, vs]
        scores = jnp.where(causal, _dot_nt(q_t[:, ks], k_t[:, ks]), 0.0)
        o_intra = _dot(scores, vh)
        s = state[h]
        o_inter = []
        for j in range(n_blk):
            rows = slice(j * GLA_BLOCK, (j + 1) * GLA_BLOCK)
            o_inter.append(_dot(q_t[rows, ks], s))
            kd = jnp.where(blk_of_col == j, k_dec_t[ks, :], 0.0)
            s = blk_decay[ks, j:j + 1] * s + _dot(kd, vh)
        state[h] = s
        o_ref[:, vs] = o_intra + jnp.concatenate(o_inter, axis=0)

    @pl.when(grp.last(i))
    def _():
        sout_ref[0] = state[...]


def _gla(proj, s0, grp):
    st_shape = (grp.n_seq, GLA_HEADS, GLA_DK, GLA_DV)
    st_spec = pl.BlockSpec((1, GLA_HEADS, GLA_DK, GLA_DV), lambda i: (grp.seq(i), 0, 0, 0))
    kw = GLA_KEY_WIDTH
    cols = lambda width, c: pl.BlockSpec((grp.tile, width), lambda i: (grp.block(i), c))
    return pl.pallas_call(
        functools.partial(_gla_kernel, grp=grp),
        out_shape=(jax.ShapeDtypeStruct((grp.n_tiles * grp.tile, GLA_WIDTH), F32), jax.ShapeDtypeStruct(st_shape, F32)),
        grid=(grp.n_tiles,),
        in_specs=[cols(kw, POOL_WIDTH // kw), cols(kw, POOL_WIDTH // kw + 1),
                  cols(GLA_WIDTH, (POOL_WIDTH + 2 * kw) // GLA_WIDTH), cols(kw, IN_EVEN_MAIN // kw), st_spec],
        out_specs=(pl.BlockSpec((grp.tile, GLA_WIDTH), lambda i: (i, 0)), st_spec),
        scratch_shapes=[pltpu.VMEM((GLA_HEADS, GLA_DK, GLA_DV), F32)],
        compiler_params=_params(), name="gla")(proj, proj, proj, proj, s0)


def _route(h1, nffn_ref, wr_hi_ref, wr_lo_ref, br_ref, xn_ref, info_ref, cnt_ref, carry):
    tm = h1.shape[0]

    @pl.when(pl.program_id(0) == 0)
    def _():
        carry[...] = jnp.zeros_like(carry)

    xn = _rms(h1, nffn_ref[...])
    _store_token_tiles(xn_ref, xn, TAGGED_ROWS)
    x_hi = xn.astype(MXU_DTYPE)
    x_lo = (xn - x_hi.astype(F32)).astype(MXU_DTYPE)
    logits = (_dot(x_hi, wr_hi_ref[...]) + _dot(x_lo, wr_hi_ref[...]) + _dot(x_hi, wr_lo_ref[...])
              + br_ref[...])
    lane = lax.broadcasted_iota(jnp.int32, (tm, LANES), 1)
    vals, ids, hots = [], [], []
    for _ in range(TOP_K):
        m = jnp.max(logits, axis=-1, keepdims=True)
        ix = jnp.min(jnp.where(logits == m, lane, LANES), axis=-1, keepdims=True)
        hot = lane == ix
        vals.append(m)
        ids.append(ix)
        hots.append(hot)
        logits = jnp.where(hot, -jnp.inf, logits)
    es = [jnp.exp(v - vals[0]) for v in vals]
    den = es[0] + es[1] + es[2] + es[3]
    chosen = jnp.zeros((tm, LANES), F32)
    for hot in hots:
        chosen = chosen + jnp.where(hot, 1.0, 0.0)
    before = (lax.broadcasted_iota(jnp.int32, (tm, tm), 1) < lax.broadcasted_iota(jnp.int32, (tm, tm), 0))
    rank = _dot(jnp.where(before, 1.0, 0.0), chosen) + carry[...]
    info = jnp.zeros((tm, LANES), F32)
    for k in range(TOP_K):
        pos = jnp.sum(jnp.where(hots[k], rank, 0.0), axis=-1, keepdims=True)
        info = jnp.where(lane == k, es[k] / den, info)
        info = jnp.where(lane == TOP_K + k, ids[k].astype(F32), info)
        info = jnp.where(lane == 2 * TOP_K + k, pos, info)
    info_ref[...] = info
    token = (pl.program_id(0) * tm + lax.broadcasted_iota(jnp.int32, (tm, 1), 0)).astype(F32)
    tag = jnp.where(lane == 0, token, 0.0)
    for k in range(TOP_K):
        tag = jnp.where(lane == 1 + k, ids[k].astype(F32), tag)
    xn_ref[pl.ds(ROW_TILES, tm, stride=TAGGED_ROWS), :] = tag
    for s in range(ROW_TILES + 1, TAGGED_ROWS):
        xn_ref[pl.ds(s, tm, stride=TAGGED_ROWS), :] = jnp.zeros((tm, LANES), F32)
    carry[...] = carry[...] + jnp.sum(chosen, axis=0, keepdims=True)
    cnt_ref[...] = carry[...]


_ROUTE_OUT_SHAPES = lambda n: (jax.ShapeDtypeStruct((n, D_MODEL), F32), jax.ShapeDtypeStruct((n * TAGGED_ROWS, LANES), F32),
                               jax.ShapeDtypeStruct((n, LANES), F32), jax.ShapeDtypeStruct((1, LANES), F32))
_ROUTE_OUT_SPECS = (pl.BlockSpec((TOKEN_TILE, D_MODEL), lambda i: (i, 0)),
                    pl.BlockSpec((TOKEN_TILE * TAGGED_ROWS, LANES), lambda i: (i, 0)),
                    pl.BlockSpec((TOKEN_TILE, LANES), lambda i: (i, 0)),
                    pl.BlockSpec((1, LANES), lambda i: (0, 0)))


def _route_in_specs():
    return [_full((1, D_MODEL)), _full((D_MODEL, LANES)), _full((D_MODEL, LANES)), _full((1, LANES))]


def _even_out_kernel(ypp_ref, yps_ref, op_ref, os_ref, r_ref, hp_ref, hs_ref, gn_ref, w_ref, nffn_ref, wr_hi_ref,
                     wr_lo_ref, br_ref, h1_ref, xn_ref, info_ref, cnt_ref, carry, *, npt):
    o = _split_rows(op_ref, os_ref, npt)
    r = r_ref[...]
    parts = []
    for hd in range(GLA_HEADS):
        sl = slice(hd * GLA_DV, (hd + 1) * GLA_DV)
        oh = o[:, sl]
        oh = oh * lax.rsqrt(jnp.mean(oh * oh, axis=-1, keepdims=True) + RMS_EPS) * gn_ref[...]
        rh = r[:, sl]
        parts.append(oh * (rh * jax.nn.sigmoid(rh)))
    gla = jnp.concatenate(parts, axis=1)
    mix = _dot(_split_rows(ypp_ref, yps_ref, npt), w_ref[:POOL_WIDTH, :]) + _dot(gla, w_ref[POOL_WIDTH:, :])
    h1 = _split_rows(hp_ref, hs_ref, npt) + mix
    h1_ref[...] = h1
    _route(h1, nffn_ref, wr_hi_ref, wr_lo_ref, br_ref, xn_ref, info_ref, cnt_ref, carry)


def _even_out(y_pool_parts, o_gla_parts, proj, h_parts, n_p, gla_norm, w_out, nffn, wr_hi, wr_lo, br):
    n = proj.shape[0]
    return pl.pallas_call(
        functools.partial(_even_out_kernel, npt=n_p // TOKEN_TILE),
        out_shape=_ROUTE_OUT_SHAPES(n),
        grid=(n // TOKEN_TILE,),
        in_specs=_split_specs(TOKEN_TILE, POOL_WIDTH, n_p, False) + _split_specs(TOKEN_TILE, GLA_WIDTH, n_p, False)
        + [pl.BlockSpec((TOKEN_TILE, GLA_WIDTH), lambda i: (i, (IN_EVEN_MAIN - GLA_WIDTH) // GLA_WIDTH))]
        + _split_specs(TOKEN_TILE, D_MODEL, n_p, h_parts[0] is h_parts[1])
        + [_full((1, GLA_DV)), _full((D_MODEL, D_MODEL))] + _route_in_specs(),
        out_specs=_ROUTE_OUT_SPECS,
        scratch_shapes=[pltpu.VMEM((1, LANES), F32)],
        compiler_params=_params(), name="even_out")(
            *y_pool_parts, *o_gla_parts, proj, *h_parts, gla_norm, w_out, nffn, wr_hi, wr_lo, br)


def _rope_tile(x, cos, sin, lo_half):
    swapped = jnp.where(lo_half, pltpu.roll(x, LANES - HEAD_DIM // 2, axis=1), pltpu.roll(x, HEAD_DIM // 2, axis=1))
    return x * cos + swapped * sin


def _odd_in_kernel(h_ref, g_ref, w_ref, b_ref, cos_ref, sin_ref, q_ref, kv_ref):
    xn = _rms(h_ref[...], g_ref[...])
    qkv = _dot(xn, w_ref[...]) + b_ref[...]
    cos = cos_ref[...]
    sin = sin_ref[...]
    lo_half = (lax.broadcasted_iota(jnp.int32, cos.shape, 1) % HEAD_DIM) < HEAD_DIM // 2
    for j in range(Q_WIDTH // LANES):
        sl = slice(j * LANES, (j + 1) * LANES)
        q_ref[:, sl] = _rope_tile(qkv[:, sl], cos, sin, lo_half).astype(q_ref.dtype)
    kv_ref[:, :KV_WIDTH] = _rope_tile(qkv[:, Q_WIDTH:Q_WIDTH + KV_WIDTH], cos, sin, lo_half)
    kv_ref[:, KV_WIDTH:] = qkv[:, Q_WIDTH + KV_WIDTH:]


def _odd_in(h, g, w_qkv, b_qkv, cos, sin, table_block):
    n = h.shape[0]
    wq = w_qkv.shape[1]
    tab = pl.BlockSpec((TOKEN_TILE, LANES), lambda i: (table_block(i), 0))
    return pl.pallas_call(
        _odd_in_kernel,
        out_shape=(jax.ShapeDtypeStruct((n, Q_WIDTH), MXU_DTYPE), jax.ShapeDtypeStruct((n, 2 * KV_WIDTH), F32)),
        grid=(n // TOKEN_TILE,),
        in_specs=[pl.BlockSpec((TOKEN_TILE, D_MODEL), lambda i: (i, 0)), _full((1, D_MODEL)),
                  _full((D_MODEL, wq)), _full((1, wq)), tab, tab],
        out_specs=(pl.BlockSpec((TOKEN_TILE, Q_WIDTH), lambda i: (i, 0)),
                   pl.BlockSpec((TOKEN_TILE, 2 * KV_WIDTH), lambda i: (i, 0))),
        compiler_params=_params(), name="odd_in")(h, g, w_qkv, b_qkv, cos, sin)


def _attn_kernel(sink_ref, q_ref, kv0_ref, kv1_ref, kv2_ref, ck0_ref, ck1_ref, cv0_ref, cv1_ref, o_ref, *, sq):
    i = pl.program_id(0)
    smp = sq.is_sample(i)
    t = sq.tile_in_seq(i)
    kv0 = kv0_ref[...]
    k_old = jnp.where(smp, ck0_ref[...], kv2_ref[:, :KV_WIDTH])
    k_mid = jnp.where(smp, ck1_ref[...], kv1_ref[:, :KV_WIDTH])
    v_old = jnp.where(smp, cv0_ref[...], kv2_ref[:, KV_WIDTH:])
    v_mid = jnp.where(smp, cv1_ref[...], kv1_ref[:, KV_WIDTH:])
    pad = jnp.zeros((CHUNK, KV_WIDTH), F32)
    keys = jnp.concatenate([k_old, k_mid, kv0[:, :KV_WIDTH], pad], axis=0)
    vals = jnp.concatenate([v_old, v_mid, kv0[:, KV_WIDTH:], pad], axis=0)
    n_keys = 4 * CHUNK
    lane = lax.broadcasted_iota(jnp.int32, (n_keys, KV_WIDTH), 1)
    lo = lane < HEAD_DIM
    keys_sw = pltpu.roll(keys, HEAD_DIM, axis=1)
    vals_sw = pltpu.roll(vals, HEAD_DIM, axis=1)
    kcol = lax.broadcasted_iota(jnp.int32, (1, n_keys), 1)
    first_valid = jnp.where(smp, 0, (2 - jnp.minimum(t, 2)) * CHUNK)
    key_ok = jnp.logical_and(kcol >= first_valid, kcol < 3 * CHUNK)
    n_pairs = N_Q_HEADS // N_KV_HEADS // 2
    pair_of_row = lax.broadcasted_iota(jnp.int32, (n_pairs * CHUNK, 1), 0) // CHUNK
    for g in range(N_KV_HEADS):
        own, other = (keys, keys_sw) if g == 0 else (keys_sw, keys)
        vown, vother = (vals, vals_sw) if g == 0 else (vals_sw, vals)
        kb = jnp.concatenate([jnp.where(lo, own, 0.0), jnp.where(lo, 0.0, other)], axis=0).astype(MXU_DTYPE)
        vb = jnp.concatenate([jnp.where(lo, vown, 0.0), jnp.where(lo, 0.0, vother)], axis=0).astype(MXU_DTYPE)
        pairs = [slice((g * n_pairs + pr) * LANES, (g * n_pairs + pr + 1) * LANES) for pr in range(n_pairs)]
        qg = jnp.concatenate([q_ref[:, sl] for sl in pairs], axis=0)
        s = _dot_nt(qg, kb) * (HEAD_DIM ** -0.5)
        halves = []
        for half in range(2):
            sh = jnp.where(key_ok, s[:, half * n_keys:(half + 1) * n_keys], NEG_INF)
            sink = jnp.zeros((n_pairs * CHUNK, 1), F32)
            for pr in range(n_pairs):
                sink = jnp.where(pair_of_row == pr, sink_ref[2 * (g * n_pairs + pr) + half], sink)
            m = jnp.maximum(jnp.max(sh, axis=-1, keepdims=True), sink)
            p = jnp.exp(sh - m)
            halves.append(p / (jnp.sum(p, axis=-1, keepdims=True) + jnp.exp(sink - m)))
        o = _dot(jnp.concatenate(halves, axis=1), vb)
        for pr, sl in enumerate(pairs):
            o_ref[:, sl] = o[pr * CHUNK:(pr + 1) * CHUNK].astype(o_ref.dtype)


def _attention(sinks, q, kv, ck, cv, sq):
    n = q.shape[0]
    npt = sq.n_prompt_tiles
    prev = lambda d: (lambda i, s: (jnp.where(sq.is_sample(i), i, jnp.maximum(i - d, 0)), 0))
    cache = lambda d: (lambda i, s: (jnp.where(sq.is_sample(i), 2 * (i - npt) + d, 0), 0))
    kvspec = lambda f: pl.BlockSpec((CHUNK, 2 * KV_WIDTH), f)
    cspec = lambda f: pl.BlockSpec((CHUNK, KV_WIDTH), f)
    grid_spec = pltpu.PrefetchScalarGridSpec(
        num_scalar_prefetch=1, grid=(sq.n_tiles,),
        in_specs=[pl.BlockSpec((CHUNK, Q_WIDTH), lambda i, s: (i, 0)),
                  kvspec(prev(0)), kvspec(prev(1)), kvspec(prev(2)),
                  cspec(cache(0)), cspec(cache(1)), cspec(cache(0)), cspec(cache(1))],
        out_specs=pl.BlockSpec((CHUNK, Q_WIDTH), lambda i, s: (i, 0)))
    return pl.pallas_call(
        functools.partial(_attn_kernel, sq=sq),
        out_shape=jax.ShapeDtypeStruct((n, Q_WIDTH), MXU_DTYPE),
        grid_spec=grid_spec, compiler_params=_params(), name="attention")(sinks, q, kv, kv, kv, ck, ck, cv, cv)


def _odd_out_kernel(o_ref, h_ref, w_ref, b_ref, nffn_ref, wr_hi_ref, wr_lo_ref, br_ref,
                    h1_ref, xn_ref, info_ref, cnt_ref, carry):
    h1 = h_ref[...] + _dot(o_ref[...], w_ref[...]) + b_ref[...]
    h1_ref[...] = h1
    _route(h1, nffn_ref, wr_hi_ref, wr_lo_ref, br_ref, xn_ref, info_ref, cnt_ref, carry)


def _odd_out(o, h, w_out, b_out, nffn, wr_hi, wr_lo, br):
    n = h.shape[0]
    row = pl.BlockSpec((TOKEN_TILE, D_MODEL), lambda i: (i, 0))
    return pl.pallas_call(
        _odd_out_kernel,
        out_shape=_ROUTE_OUT_SHAPES(n),
        grid=(n // TOKEN_TILE,),
        in_specs=[row, row, _full((D_MODEL, D_MODEL)), _full((1, D_MODEL))] + _route_in_specs(),
        out_specs=_ROUTE_OUT_SPECS,
        scratch_shapes=[pltpu.VMEM((1, LANES), F32)],
        compiler_params=_params(), name="odd_out")(o, h, w_out, b_out, nffn, wr_hi, wr_lo, br)


def _token_copy(src, s, dst, d, sem, rows=ROW_TILES):
    return pltpu.make_async_copy(src.at[pl.ds(pl.multiple_of(s, rows), rows)],
                                 dst.at[pl.ds(pl.multiple_of(d, rows), rows)], sem)


ZERO_TOKENS = EXPERT_TILE // 2


def _zero_fill(fill_ref, zeros, xs_out, zsem):
    zeros[...] = jnp.zeros_like(zeros)

    def sweep(wait):
        def go(cp):
            cp.wait() if wait else cp.start()

        def tail(e, c):
            off, length = fill_ref[e], fill_ref[N_EXPERTS + e]
            for bit in range(EXPERT_TILE.bit_length() - 1):
                rows = (1 << bit) * TAGGED_ROWS

                @pl.when((length >> bit) & 1 == 1)
                def _():
                    o = pl.multiple_of(off + (length & ((1 << bit) - 1)) * TAGGED_ROWS, TAGGED_ROWS)
                    go(pltpu.make_async_copy(zeros.at[pl.ds(0, rows)], xs_out.at[pl.ds(o, rows)], zsem))
            return c

        lax.fori_loop(0, N_EXPERTS, tail, 0)

        def unused(t, c):
            o = pl.multiple_of(fill_ref[2 * N_EXPERTS] + t * ZERO_TOKENS * TAGGED_ROWS, TAGGED_ROWS)
            go(pltpu.make_async_copy(zeros, xs_out.at[pl.ds(o, ZERO_TOKENS * TAGGED_ROWS)], zsem))
            return c

        lax.fori_loop(0, fill_ref[2 * N_EXPERTS + 1], unused, 0)

    sweep(wait=False)
    sweep(wait=True)


def _dispatch_kernel(fill_ref, dest_hbm, x_ref, xs_out, dest_smem, zeros, sem, dsems, zsem):
    i = pl.program_id(0)
    n_slots = TOKEN_TILE * TOP_K

    @pl.when(i == 0)
    def _():
        _zero_fill(fill_ref, zeros, xs_out, zsem)

    slot = i % 2

    def dest_rows(step, s):
        return pltpu.make_async_copy(dest_hbm.at[pl.ds(step * n_slots, n_slots)], dest_smem.at[s], dsems.at[s])

    @pl.when(i == 0)
    def _():
        dest_rows(0, 0).start()

    dest_rows(i, slot).wait()

    @pl.when(i + 1 < pl.num_programs(0))
    def _():
        dest_rows(i + 1, 1 - slot).start()

    def issue(t, c):
        for k in range(TOP_K):
            _token_copy(x_ref, t * TAGGED_ROWS, xs_out, dest_smem[slot, t * TOP_K + k], sem, TAGGED_ROWS).start()
        return c

    lax.fori_loop(0, TOKEN_TILE, issue, 0, unroll=2)
    whole = xs_out.at[pl.ds(0, n_slots * TAGGED_ROWS)]
    pltpu.make_async_copy(whole, whole, sem).wait()


def _dispatch(fill, dest, xt, n_rows):
    n = xt.shape[0] // TAGGED_ROWS
    grid_spec = pltpu.PrefetchScalarGridSpec(
        num_scalar_prefetch=1, grid=(n // TOKEN_TILE,),
        in_specs=[pl.BlockSpec(memory_space=pl.ANY),
                  pl.BlockSpec((TOKEN_TILE * TAGGED_ROWS, LANES), lambda i, f: (i, 0))],
        out_specs=pl.BlockSpec(memory_space=pl.ANY),
        scratch_shapes=[pltpu.SMEM((2, TOKEN_TILE * TOP_K), jnp.int32), pltpu.VMEM((ZERO_TOKENS * TAGGED_ROWS, LANES), F32),
                        pltpu.SemaphoreType.DMA, pltpu.SemaphoreType.DMA((2,)), pltpu.SemaphoreType.DMA])
    return pl.pallas_call(
        _dispatch_kernel, out_shape=jax.ShapeDtypeStruct((n_rows * TAGGED_ROWS, LANES), F32),
        grid_spec=grid_spec, compiler_params=_params(), name="moe_dispatch")(fill, dest, xt)


def _experts_kernel(te_ref, tr_ref, x_ref, wgu_ref, bgu_ref, wd_ref, bd_ref, yt_ref,
                    wgu_mxu, wd_mxu, ybuf, place_vmem, place_smem, ysems, psem, *, n_tokens):
    i = pl.program_id(0)
    n = pl.num_programs(0)
    slot = i % 2
    tile_rows = EXPERT_TILE * ROW_TILES
    spare = yt_ref.shape[0] - 2 * tile_rows

    def rows_done(s):
        whole = yt_ref.at[pl.ds(0, tile_rows)]
        pltpu.make_async_copy(whole, whole, ysems.at[s]).wait()

    @pl.when(i == 0)
    def _():
        ybuf[0] = jnp.zeros((tile_rows, LANES), F32)
        for s in range(2):
            cp = pltpu.make_async_copy(ybuf.at[0], yt_ref.at[pl.ds(spare + s * tile_rows, tile_rows)], ysems.at[0])
            cp.start()
            cp.wait()

    @pl.when(jnp.logical_or(i == 0, te_ref[i] != te_ref[jnp.maximum(i - 1, 0)]))
    def _():
        wgu_mxu[...] = wgu_ref[0, 0].astype(MXU_DTYPE)
        wd_mxu[...] = wd_ref[0, 0].astype(MXU_DTYPE)

    @pl.when(jnp.logical_and(i >= 2, tr_ref[jnp.maximum(i - 2, 0)] > 0))
    def _():
        rows_done(slot)

    n_valid = tr_ref[i]

    @pl.when(n_valid > 0)
    def _():
        tag = x_ref[pl.ds(ROW_TILES, EXPERT_TILE, stride=TAGGED_ROWS), :]
        lane = lax.broadcasted_iota(jnp.int32, (EXPERT_TILE, LANES), 1)
        mine = jnp.logical_and(jnp.logical_and(lane >= 1, lane <= TOP_K), tag == te_ref[i].astype(F32))
        rank = jnp.sum(jnp.where(mine, (lane - 1).astype(F32), 0.0), axis=-1, keepdims=True)
        place = rank * n_tokens + tag[:, 0:1]
        hi = jnp.floor(place * (1.0 / 65536.0))
        mid = jnp.floor((place - hi * 65536.0) * (1.0 / 256.0))
        digits = jnp.where(lane == 0, place - hi * 65536.0 - mid * 256.0, jnp.where(lane == 1, mid, jnp.where(lane == 2, hi, 0.0)))
        pick = jnp.where(lax.broadcasted_iota(jnp.int32, (ROW_TILES, LANES), 0)
                         == lax.broadcasted_iota(jnp.int32, (ROW_TILES, LANES), 1), 1.0, 0.0)
        planes = _dot_nt(pick, digits)
        place_row = planes[0:1] + 256.0 * planes[1:2] + 65536.0 * planes[2:3]
        col = lax.broadcasted_iota(jnp.int32, (1, EXPERT_TILE), 1)
        first_row = jnp.where(col < n_valid, place_row.astype(jnp.int32), (spare // ROW_TILES) + slot * EXPERT_TILE + col)
        place_vmem[...] = jnp.broadcast_to(first_row * ROW_TILES, place_vmem.shape)
        to_smem = pltpu.make_async_copy(place_vmem, place_smem, psem)
        to_smem.start()

        x = _load_token_tiles(x_ref, EXPERT_TILE, TAGGED_ROWS)
        hgu = _dot(x, wgu_mxu[...]) + bgu_ref[0, 0]
        gate = jnp.minimum(hgu[:, :D_MODEL], SWIGLU_LIMIT)
        up = jnp.clip(hgu[:, D_MODEL:], -SWIGLU_LIMIT, SWIGLU_LIMIT)
        act = (up + 1.0) * gate * jax.nn.sigmoid(SWIGLU_ALPHA * gate)
        _store_token_tiles(ybuf.at[slot], _dot(act, wd_mxu[...]) + bd_ref[0, 0])
        to_smem.wait()

        def issue(j, c):
            _token_copy(ybuf.at[slot], j * ROW_TILES, yt_ref, place_smem[0, j], ysems.at[slot]).start()
            return c

        lax.fori_loop(0, EXPERT_TILE, issue, 0, unroll=8)

    @pl.when(i == n - 1)
    def _():
        @pl.when(jnp.logical_and(i >= 1, tr_ref[jnp.maximum(i - 1, 0)] > 0))
        def _():
            rows_done(1 - slot)

        @pl.when(n_valid > 0)
        def _():
            rows_done(slot)


def _experts(layer, tile_expert, tile_rows, xs, wgu, bgu, wd, bd, n_tokens):
    rows = xs.shape[0] // TAGGED_ROWS
    w = lambda shape: pl.BlockSpec((1, 1) + shape, lambda i, te, tr: (layer, te[i], 0, 0))
    grid_spec = pltpu.PrefetchScalarGridSpec(
        num_scalar_prefetch=2, grid=(rows // EXPERT_TILE,),
        in_specs=[pl.BlockSpec((EXPERT_TILE * TAGGED_ROWS, LANES), lambda i, te, tr: (i, 0)),
                  w((D_MODEL, 2 * D_MODEL)), w((1, 2 * D_MODEL)), w((D_MODEL, D_MODEL)), w((1, D_MODEL))],
        out_specs=pl.BlockSpec(memory_space=pl.ANY),
        scratch_shapes=[pltpu.VMEM((D_MODEL, 2 * D_MODEL), MXU_DTYPE), pltpu.VMEM((D_MODEL, D_MODEL), MXU_DTYPE),
                        pltpu.VMEM((2, EXPERT_TILE * ROW_TILES, LANES), F32),
                        pltpu.VMEM((ROW_TILES, EXPERT_TILE), jnp.int32), pltpu.SMEM((ROW_TILES, EXPERT_TILE), jnp.int32),
                        pltpu.SemaphoreType.DMA((2,)), pltpu.SemaphoreType.DMA])
    yt_rows = (n_tokens * TOP_K + 2 * EXPERT_TILE) * ROW_TILES
    return pl.pallas_call(
        functools.partial(_experts_kernel, n_tokens=n_tokens), out_shape=jax.ShapeDtypeStruct((yt_rows, LANES), F32),
        grid_spec=grid_spec, compiler_params=_params(), name="moe_experts")(
            tile_expert, tile_rows, xs, wgu, bgu, wd, bd)


def _combine_kernel(*refs, final, npt):
    y_refs, outs = refs[:TOP_K], refs[TOP_K + 7:]
    info_ref, h_ref, p_prompt_ref, p_sample_ref, pp_ref, pg_ref, nf_ref = refs[TOP_K:TOP_K + 7]
    i = pl.program_id(0)
    gates = info_ref[...]
    moe = None
    for k in range(TOP_K):
        rows = _load_token_tiles(y_refs[k], COMBINE_TILE)
        moe = gates[:, k:k + 1] * rows if moe is None else moe + gates[:, k:k + 1] * rows
    h2 = h_ref[...] + moe
    p = jnp.where(i < npt, p_prompt_ref[0], p_sample_ref[0])
    h3 = h2 + jax.nn.sigmoid(_dot(h2, pg_ref[...])) * _dot(p, pp_ref[...])
    if not final:
        outs[0][...] = h3
        return
    y = _rms(h3, nf_ref[...])

    @pl.when(i < npt)
    def _():
        outs[0][...] = y

    @pl.when(i >= npt)
    def _():
        outs[1][...] = y


def _combine(layer, yt, info, h1, p_prompt, p_sample, ple_proj, ple_gate, norm_final, final):
    n = h1.shape[0]
    n_p = p_prompt.shape[1]
    npt = n_p // COMBINE_TILE
    steps = n // COMBINE_TILE
    row = lambda w: pl.BlockSpec((COMBINE_TILE, w), lambda i: (i, 0))
    rank_rows = lambda k: pl.BlockSpec((COMBINE_TILE * ROW_TILES, LANES), lambda i: (k * steps + i, 0))
    prompt_rows = lambda i: jnp.minimum(i, npt - 1)
    sample_rows = lambda i: jnp.maximum(i - npt, 0)
    if final:
        out_shape = (jax.ShapeDtypeStruct((n_p, D_MODEL), F32), jax.ShapeDtypeStruct((n - n_p, D_MODEL), F32))
        out_specs = (pl.BlockSpec((COMBINE_TILE, D_MODEL), lambda i: (prompt_rows(i), 0)),
                     pl.BlockSpec((COMBINE_TILE, D_MODEL), lambda i: (sample_rows(i), 0)))
    else:
        out_shape, out_specs = jax.ShapeDtypeStruct((n, D_MODEL), F32), row(D_MODEL)
    return pl.pallas_call(
        functools.partial(_combine_kernel, final=final, npt=npt),
        out_shape=out_shape,
        grid=(steps,),
        in_specs=[rank_rows(k) for k in range(TOP_K)] + [
            row(LANES), row(D_MODEL),
            pl.BlockSpec((1, COMBINE_TILE, PLE_DIM), lambda i: (layer, prompt_rows(i), 0)),
            pl.BlockSpec((1, COMBINE_TILE, PLE_DIM), lambda i: (layer, sample_rows(i), 0)),
            _full((PLE_DIM, D_MODEL)), _full((D_MODEL, D_MODEL)), _full((1, D_MODEL))],
        out_specs=out_specs,
        compiler_params=_params(), name="moe_combine")(
            *([yt] * TOP_K), info, h1, p_prompt, p_sample, ple_proj, ple_gate, norm_final)


def _moe_and_embed(layer, h1, xt, info, counts, p_prompt, p_sample, wgu, bgu, wd, bd, ple_proj, ple_gate,
                   norm_final, final):
    n = h1.shape[0]
    n_tiles = (n * TOP_K + N_EXPERTS * (EXPERT_TILE - 1)) // EXPERT_TILE + 1
    ids = info[:, TOP_K:2 * TOP_K].astype(jnp.int32)
    rank = info[:, 2 * TOP_K:3 * TOP_K].astype(jnp.int32)
    cnt = counts[0, :N_EXPERTS].astype(jnp.int32)
    padded = ((cnt + EXPERT_TILE - 1) // EXPERT_TILE) * EXPERT_TILE
    ends = jnp.cumsum(padded)
    starts = ends - padded
    dest = ((starts[ids] + rank) * TAGGED_ROWS).reshape(-1)
    tile_start = jnp.arange(n_tiles, dtype=jnp.int32) * EXPERT_TILE
    tile_expert = jnp.minimum(jnp.sum((tile_start[:, None] >= ends[None, :]).astype(jnp.int32), axis=1), N_EXPERTS - 1)
    tile_rows = jnp.clip(cnt[tile_expert] - (tile_start - starts[tile_expert]), 0, EXPERT_TILE)
    tile_rows = jnp.where(tile_start < ends[-1], tile_rows, 0).astype(jnp.int32)
    fill = jnp.concatenate([(starts + cnt) * TAGGED_ROWS, padded - cnt,
                            jnp.stack([ends[-1] * TAGGED_ROWS,
                                       (n_tiles * EXPERT_TILE - ends[-1]) // ZERO_TOKENS])]).astype(jnp.int32)
    xs = _dispatch(fill, dest, xt, n_tiles * EXPERT_TILE)
    yt = _experts(layer, tile_expert, tile_rows, xs, wgu, bgu, wd, bd, n)
    return _combine(layer, yt, info, h1, p_prompt, p_sample, ple_proj, ple_gate, norm_final, final)


def _rope_tables(prompt_len, n_sample_seq, sample_len):
    half = HEAD_DIM // 2
    inv = jnp.power(jnp.float32(ROPE_THETA), -jnp.arange(half, dtype=F32) / half)
    pos = jnp.concatenate([jnp.arange(prompt_len), jnp.tile(PAST_LEN + jnp.arange(sample_len), n_sample_seq)])
    ang = pos.astype(F32)[:, None] * inv[None, :]
    cos = jnp.tile(jnp.cos(ang), (1, LANES // half))
    sin = jnp.tile(jnp.concatenate([-jnp.sin(ang), jnp.sin(ang)], axis=1), (1, LANES // HEAD_DIM))
    return cos, sin


def kernel(x_prompt, x_sample, state_pool, state_gla, cache_k, cache_v, p_prompt, p_sample, norm_mix, norm_ffn, norm_final, w_in_even, pool_w, pool_scale, gla_w_gate, gla_b_gate, gla_norm, w_out_even, w_qkv_odd, b_qkv_odd, attn_sinks, w_out_odd, b_out_odd, w_router, b_router, w_gate_up, b_gate_up, w_down, b_down, ple_proj, ple_gate):
    bsz, t_len, _ = x_prompt.shape
    dec_bsz, dec_len, _ = x_sample.shape
    depth = norm_mix.shape[0]
    n_p, n_s = bsz * t_len, dec_bsz * dec_len
    n = n_p + n_s
    assert dec_len == CHUNK and n_s == TOKEN_TILE and t_len % TOKEN_TILE == 0 and cache_k.shape[2] == WINDOW
    sq = _Seq(bsz, t_len, dec_bsz)
    grp_p = _Group(bsz, t_len, min(SEQ_TILE, t_len), 0, 0)
    grp_s = _Group(dec_bsz, dec_len, dec_len, n_p, PAST_LEN)
    bf = lambda a: a.astype(MXU_DTYPE)
    row = lambda a: a.reshape(1, -1)

    h_parts = (x_prompt.reshape(n_p, D_MODEL), x_sample.reshape(n_s, D_MODEL))
    p_parts = (p_prompt.reshape(depth, n_p, PLE_DIM), p_sample.reshape(depth, n_s, PLE_DIM))
    b_gu = b_gate_up.reshape(depth, N_EXPERTS, 1, 2 * D_MODEL)
    b_dn = b_down.reshape(depth, N_EXPERTS, 1, D_MODEL)
    cos, sin = _rope_tables(t_len, dec_bsz, dec_len)
    tiles_per_seq = t_len // TOKEN_TILE
    table_block = lambda i: jnp.where(i < bsz * tiles_per_seq, i % tiles_per_seq, tiles_per_seq)

    pools, glas, new_k, new_v = [], [], [], []
    for i in range(depth):
        if i % 2 == 0:
            e = i // 2
            w_in = jnp.pad(bf(w_in_even[e]), ((0, 0), (0, (-IN_EVEN) % LANES)))
            wg = jnp.pad(bf(gla_w_gate[e]), ((0, w_in.shape[1] - IN_EVEN_MAIN - GLA_RANK), (0, 0)))
            proj = _even_in(h_parts, n, n_p, row(norm_mix[i]), w_in, wg, row(gla_b_gate[e]))
            pw, ps = bf(pool_w[e]), row(pool_scale[e])
            y_pool_p, pool_p = _pool(proj, jnp.zeros((bsz, 16, POOL_WIDTH), F32), pw, ps, grp_p)
            y_pool_s, pool_s = _pool(proj, jnp.pad(state_pool[e], ((0, 0), (1, 0), (0, 0))), pw, ps, grp_s)
            o_gla_p, gla_p = _gla(proj, jnp.zeros((bsz,) + state_gla.shape[2:], F32), grp_p)
            o_gla_s, gla_s = _gla(proj, state_gla[e], grp_s)
            pools.append((pool_p[:, 1:], pool_s[:, 1:]))
            glas.append((gla_p, gla_s))
            wr = jnp.pad(w_router[i], ((0, 0), (0, LANES - N_EXPERTS)))
            wr_hi = bf(wr)
            wr_lo = bf(wr - wr_hi.astype(F32))
            br = jnp.pad(row(b_router[i]), ((0, 0), (0, LANES - N_EXPERTS)), constant_values=NEG_INF)
            h1, xn, info, counts = _even_out((y_pool_p, y_pool_s), (o_gla_p, o_gla_s), proj, h_parts, n_p,
                                             row(gla_norm[e]), bf(w_out_even[e]), row(norm_ffn[i]), wr_hi, wr_lo, br)
        else:
            o = i // 2
            h = h_parts[0]
            q, kv = _odd_in(h, row(norm_mix[i]), bf(w_qkv_odd[o]), row(b_qkv_odd[o]), cos, sin, table_block)
            ck = cache_k[o].reshape(dec_bsz * WINDOW, KV_WIDTH)
            cv = cache_v[o].reshape(dec_bsz * WINDOW, KV_WIDTH)
            att = _attention(attn_sinks[o], q, kv, ck, cv, sq)
            kv_p = kv[:n_p].reshape(bsz, t_len, 2 * KV_WIDTH)[:, -WINDOW:]
            kv_s = kv[n_p:].reshape(dec_bsz, dec_len, 2 * KV_WIDTH)
            hd = (N_KV_HEADS, HEAD_DIM)
            new_k.append((kv_p[..., :KV_WIDTH].reshape(bsz, WINDOW, *hd),
                          jnp.concatenate([cache_k[o], kv_s[..., :KV_WIDTH].reshape(dec_bsz, dec_len, *hd)], axis=1)[:, -WINDOW:]))
            new_v.append((kv_p[..., KV_WIDTH:].reshape(bsz, WINDOW, *hd),
                          jnp.concatenate([cache_v[o], kv_s[..., KV_WIDTH:].reshape(dec_bsz, dec_len, *hd)], axis=1)[:, -WINDOW:]))
            wr = jnp.pad(w_router[i], ((0, 0), (0, LANES - N_EXPERTS)))
            wr_hi = bf(wr)
            wr_lo = bf(wr - wr_hi.astype(F32))
            br = jnp.pad(row(b_router[i]), ((0, 0), (0, LANES - N_EXPERTS)), constant_values=NEG_INF)
            h1, xn, info, counts = _odd_out(att, h, bf(w_out_odd[o]), row(b_out_odd[o]),
                                            row(norm_ffn[i]), wr_hi, wr_lo, br)
        final = i == depth - 1
        out = _moe_and_embed(i, h1, xn, info, counts, *p_parts, w_gate_up, b_gu, w_down, b_dn,
                             bf(ple_proj[i]), bf(ple_gate[i]), row(norm_final), final)
        h_parts = out if final else (out, out)

    y_prompt = h_parts[0].reshape(bsz, t_len, D_MODEL)
    y_sample = h_parts[1].reshape(dec_bsz, dec_len, D_MODEL)
    part = lambda pairs, j: jnp.stack([p[j] for p in pairs])
    return (y_prompt, y_sample, part(pools, 0), part(glas, 0), part(new_k, 0), part(new_v, 0),
            part(pools, 1), part(glas, 1), part(new_k, 1), part(new_v, 1))
```

```python
import functools

import jax
import jax.numpy as jnp
from jax import lax
from jax.experimental import pallas as pl
from jax.experimental.pallas import tpu as pltpu

F32 = jnp.float32
MXU_DTYPE = jnp.bfloat16

V7X_VMEM_BYTES = 64 * 1024 * 1024
VMEM_LIMIT = (V7X_VMEM_BYTES * 7) // 8
LANES = 128

D_MODEL = 1024
CHUNK = 64
PAST_LEN = 2048
PLE_DIM = 256
RMS_EPS = 1e-6
POOL_WINDOWS = (2, 4, 8, 16)
POOL_WIDTH = 512
POOL_GROUP_DIM = 128
POOL_STATE = 15
GLA_HEADS = 4
GLA_DK = 64
GLA_DV = 128
GLA_KEY_WIDTH = GLA_HEADS * GLA_DK
GLA_WIDTH = GLA_HEADS * GLA_DV
GLA_RANK = 16
GLA_TAU = 16.0
GLA_BLOCK = 16
IN_EVEN = POOL_WIDTH + 2 * GLA_KEY_WIDTH + 2 * GLA_WIDTH + GLA_RANK
IN_EVEN_MAIN = IN_EVEN - GLA_RANK
PROJ_WIDTH = IN_EVEN_MAIN + GLA_KEY_WIDTH
N_Q_HEADS = 16
N_KV_HEADS = 2
HEAD_DIM = 64
WINDOW = 128
ROPE_THETA = 10000.0
Q_WIDTH = N_Q_HEADS * HEAD_DIM
KV_WIDTH = N_KV_HEADS * HEAD_DIM
N_EXPERTS = 32
TOP_K = 4
SWIGLU_LIMIT = 7.0
SWIGLU_ALPHA = 1.702
NEG_INF = -1e30

TOKEN_TILE = 512
EXPERT_TILE = 512
COMBINE_TILE = 256


def _dot(a, b):
    return jnp.dot(a.astype(MXU_DTYPE), b.astype(MXU_DTYPE), preferred_element_type=F32)


def _dot_nt(a, b):
    return lax.dot_general(a.astype(MXU_DTYPE), b.astype(MXU_DTYPE), (((1,), (1,)), ((), ())),
                           preferred_element_type=F32)


def _split3(x):
    x1 = x.astype(MXU_DTYPE)
    r1 = x - x1.astype(F32)
    x2 = r1.astype(MXU_DTYPE)
    x3 = (r1 - x2.astype(F32)).astype(MXU_DTYPE)
    return x1, x2, x3


def _rms(x, g):
    return x * lax.rsqrt(jnp.mean(x * x, axis=-1, keepdims=True) + RMS_EPS) * g


ROW_TILES = D_MODEL // LANES
TAGGED_ROWS = 2 * ROW_TILES
TAG_DIGITS = 3


def _store_token_tiles(ref, x, pitch=ROW_TILES):
    for s in range(ROW_TILES):
        ref[pl.ds(s, x.shape[0], stride=pitch), :] = x[:, s * LANES:(s + 1) * LANES]


def _load_token_tiles(ref, rows, pitch=ROW_TILES):
    return jnp.concatenate([ref[pl.ds(s, rows, stride=pitch), :] for s in range(ROW_TILES)], axis=1)


def _params(n_axes=1):
    return pltpu.CompilerParams(dimension_semantics=("arbitrary",) * n_axes, vmem_limit_bytes=VMEM_LIMIT)


def _full(shape):
    return pl.BlockSpec(shape, lambda *_: (0,) * len(shape))


class _Seq:
    def __init__(self, n_prompt_seq, prompt_len, n_sample_seq):
        self.tiles_per_seq = prompt_len // CHUNK
        self.n_prompt_seq = n_prompt_seq
        self.n_prompt_tiles = n_prompt_seq * self.tiles_per_seq
        self.n_tiles = self.n_prompt_tiles + n_sample_seq
        self.n_seq = n_prompt_seq + n_sample_seq

    def is_sample(self, i):
        return i >= self.n_prompt_tiles

    def tile_in_seq(self, i):
        return jnp.where(self.is_sample(i), 0, i % self.tiles_per_seq)

    def seq(self, i):
        return jnp.where(self.is_sample(i), self.n_prompt_seq + i - self.n_prompt_tiles, i // self.tiles_per_seq)

    def last(self, i):
        return jnp.logical_or(self.is_sample(i), i % self.tiles_per_seq == self.tiles_per_seq - 1)

    def pos0(self, i):
        return jnp.where(self.is_sample(i), PAST_LEN, self.tile_in_seq(i) * CHUNK)


class _Group:
    def __init__(self, n_seq, seq_len, tile, row0, pos_base):
        assert seq_len % tile == 0 and row0 % tile == 0
        self.n_seq, self.tile, self.pos_base = n_seq, tile, pos_base
        self.tiles_per_seq = seq_len // tile
        self.n_tiles = n_seq * self.tiles_per_seq
        self.block0 = row0 // tile

    def block(self, i):
        return self.block0 + i

    def seq(self, i):
        return i // self.tiles_per_seq

    def tile_in_seq(self, i):
        return i % self.tiles_per_seq

    def last(self, i):
        return i % self.tiles_per_seq == self.tiles_per_seq - 1

    def pos0(self, i):
        return self.pos_base + self.tile_in_seq(i) * self.tile


SEQ_TILE = 256


def _split_specs(tile, width, n_prompt, same_array):
    npt = n_prompt // tile
    off = npt if same_array else 0
    return [pl.BlockSpec((tile, width), lambda i: (jnp.minimum(i, npt - 1), 0)),
            pl.BlockSpec((tile, width), lambda i: (jnp.maximum(i - npt, 0) + off, 0))]


def _split_rows(prompt_ref, sample_ref, n_prompt_tiles):
    return jnp.where(pl.program_id(0) < n_prompt_tiles, prompt_ref[...], sample_ref[...])


def _even_in_kernel(hp_ref, hs_ref, g_ref, w_ref, wg_ref, bg_ref, out_ref, *, npt):
    xn = _rms(_split_rows(hp_ref, hs_ref, npt), g_ref[...])
    proj = _dot(xn, w_ref[...])
    z = proj[:, IN_EVEN_MAIN:]
    a = _dot(z, wg_ref[...]) + bg_ref[...]
    log_alpha = (jnp.minimum(a, 0.0) - jnp.log1p(jnp.exp(-jnp.abs(a)))) * (1.0 / GLA_TAU)
    out_ref[:, :IN_EVEN_MAIN] = proj[:, :IN_EVEN_MAIN]
    out_ref[:, IN_EVEN_MAIN:] = log_alpha


def _even_in(h_parts, n, n_p, g, w_in, wg, bg):
    wp = w_in.shape[1]
    return pl.pallas_call(
        functools.partial(_even_in_kernel, npt=n_p // TOKEN_TILE),
        out_shape=jax.ShapeDtypeStruct((n, PROJ_WIDTH), F32),
        grid=(n // TOKEN_TILE,),
        in_specs=_split_specs(TOKEN_TILE, D_MODEL, n_p, h_parts[0] is h_parts[1]) + [
            _full((1, D_MODEL)), _full((D_MODEL, wp)), _full((wp - IN_EVEN_MAIN, GLA_KEY_WIDTH)),
            _full((1, GLA_KEY_WIDTH))],
        out_specs=pl.BlockSpec((TOKEN_TILE, PROJ_WIDTH), lambda i: (i, 0)),
        compiler_params=_params(), name="even_in")(*h_parts, g, w_in, wg, bg)


def _pool_kernel(u_ref, init_ref, pw_ref, ps_ref, y_ref, st_ref, buf, *, grp):
    i = pl.program_id(0)
    rows = grp.tile

    @pl.when(grp.tile_in_seq(i) == 0)
    def _():
        buf[0:16, :] = init_ref[0]

    u = u_ref[...]
    buf[16:16 + rows, :] = u
    pos = grp.pos0(i) + lax.broadcasted_iota(jnp.int32, (rows, 1), 0)
    for g, w in enumerate(POOL_WINDOWS):
        sl = slice(g * POOL_GROUP_DIM, (g + 1) * POOL_GROUP_DIM)
        acc = u[:, sl]
        for j in range(1, w):
            acc = acc + buf[16 - j:16 - j + rows, sl]
        cnt = jnp.minimum(w, pos + 1).astype(F32)
        d = acc / cnt - u[:, sl]
        y_ref[:, sl] = _dot(d, pw_ref[g]) * ps_ref[:, sl]
    tail = buf[rows:rows + 16, :]
    st_ref[0] = tail
    buf[0:16, :] = tail


def _pool(proj, init, pool_w, pool_scale, grp):
    return pl.pallas_call(
        functools.partial(_pool_kernel, grp=grp),
        out_shape=(jax.ShapeDtypeStruct((grp.n_tiles * grp.tile, POOL_WIDTH), F32),
                   jax.ShapeDtypeStruct((grp.n_seq, 16, POOL_WIDTH), F32)),
        grid=(grp.n_tiles,),
        in_specs=[pl.BlockSpec((grp.tile, POOL_WIDTH), lambda i: (grp.block(i), 0)),
                  pl.BlockSpec((1, 16, POOL_WIDTH), lambda i: (grp.seq(i), 0, 0)),
                  _full((len(POOL_WINDOWS), POOL_GROUP_DIM, POOL_GROUP_DIM)), _full((1, POOL_WIDTH))],
        out_specs=(pl.BlockSpec((grp.tile, POOL_WIDTH), lambda i: (i, 0)),
                   pl.BlockSpec((1, 16, POOL_WIDTH), lambda i: (grp.seq(i), 0, 0))),
        scratch_shapes=[pltpu.VMEM((grp.tile + 16, POOL_WIDTH), F32)],
        compiler_params=_params(), name="pool")(proj, init, pool_w, pool_scale)


def _gla_kernel(q_ref, k_ref, v_ref, g_ref, s0_ref, o_ref, sout_ref, state, before, *, grp):
    i = pl.program_id(0)
    tile = grp.tile
    n_blk = tile // GLA_BLOCK

    @pl.when(grp.tile_in_seq(i) == 0)
    def _():
        state[...] = s0_ref[0]

    g = g_ref[...]
    row = lax.broadcasted_iota(jnp.int32, (tile, tile), 0)
    col = lax.broadcasted_iota(jnp.int32, (tile, tile), 1)
    same = (row >> 4) == (col >> 4)
    causal = jnp.logical_and(same, col <= row)
    tri = jnp.where(causal, 1.0, 0.0).astype(MXU_DTYPE)
    ones = jnp.where(same, 1.0, 0.0).astype(MXU_DTYPE)
    g1, g2, g3 = _split3(g)
    b = _dot(tri, g1) + _dot(tri, g2) + _dot(tri, g3)
    b_last = _dot(ones, g1) + _dot(ones, g2) + _dot(ones, g3)
    q_t = q_ref[...] * (GLA_DK ** -0.5) * jnp.exp(b)
    k = k_ref[...]
    k_t = k * jnp.exp(-b)
    k_dec_t = (k * jnp.exp(b_last - b)).T
    sel = jnp.where((lax.broadcasted_iota(jnp.int32, (tile, LANES), 0) >> 4)
                    == lax.broadcasted_iota(jnp.int32, (tile, LANES), 1), 1.0, 0.0).astype(MXU_DTYPE)
    t1, t2, t3 = _split3(g.T)
    blk_decay = jnp.exp(_dot(t1, sel) + _dot(t2, sel) + _dot(t3, sel))
    v = v_ref[...]
    stacked = n_blk * GLA_DK
    dk_bits, blk_bits = GLA_DK.bit_length() - 1, GLA_BLOCK.bit_length() - 1
    upd_live = (lax.broadcasted_iota(jnp.int32, (stacked, tile), 0) >> dk_bits
                == lax.broadcasted_iota(jnp.int32, (stacked, tile), 1) >> blk_bits)
    qry_live = (lax.broadcasted_iota(jnp.int32, (tile, stacked), 0) >> blk_bits
                == lax.broadcasted_iota(jnp.int32, (tile, stacked), 1) >> dk_bits)
    low_half = lax.broadcasted_iota(jnp.int32, (tile, LANES), 1) < GLA_DK
    for h in range(GLA_HEADS):
        ks = slice(h * GLA_DK, (h + 1) * GLA_DK)
        vs = slice(h * GLA_DV, (h + 1) * GLA_DV)
        vh = v[:, vs]
        scores = jnp.where(causal, _dot_nt(q_t[:, ks], k_t[:, ks]), 0.0)
        o = _dot(scores, vh)
        upd = _dot(jnp.where(upd_live, jnp.concatenate([k_dec_t[ks, :]] * n_blk, axis=0), 0.0), vh)
        s = state[h]
        for j in range(n_blk):
            rows = slice(j * GLA_DK, (j + 1) * GLA_DK)
            before[rows, :] = s
            s = blk_decay[ks, j:j + 1] * s + upd[rows, :]
        state[h] = s
        pair = q_t[:, (h // 2) * LANES:(h // 2 + 1) * LANES]
        swapped = pltpu.roll(pair, GLA_DK, axis=1)
        both = jnp.where(low_half, pair, swapped) if h % 2 == 0 else jnp.where(low_half, swapped, pair)
        q_exp = jnp.where(qry_live, jnp.concatenate([both] * (stacked // LANES), axis=1), 0.0)
        o_ref[:, vs] = o + _dot(q_exp, before[...])

    @pl.when(grp.last(i))
    def _():
        sout_ref[0] = state[...]


def _gla(proj, s0, grp):
    st_shape = (grp.n_seq, GLA_HEADS, GLA_DK, GLA_DV)
    st_spec = pl.BlockSpec((1, GLA_HEADS, GLA_DK, GLA_DV), lambda i: (grp.seq(i), 0, 0, 0))
    kw = GLA_KEY_WIDTH
    cols = lambda width, c: pl.BlockSpec((grp.tile, width), lambda i: (grp.block(i), c))
    return pl.pallas_call(
        functools.partial(_gla_kernel, grp=grp),
        out_shape=(jax.ShapeDtypeStruct((grp.n_tiles * grp.tile, GLA_WIDTH), F32), jax.ShapeDtypeStruct(st_shape, F32)),
        grid=(grp.n_tiles,),
        in_specs=[cols(kw, POOL_WIDTH // kw), cols(kw, POOL_WIDTH // kw + 1),
                  cols(GLA_WIDTH, (POOL_WIDTH + 2 * kw) // GLA_WIDTH), cols(kw, IN_EVEN_MAIN // kw), st_spec],
        out_specs=(pl.BlockSpec((grp.tile, GLA_WIDTH), lambda i: (i, 0)), st_spec),
        scratch_shapes=[pltpu.VMEM((GLA_HEADS, GLA_DK, GLA_DV), F32),
                        pltpu.VMEM((grp.tile // GLA_BLOCK * GLA_DK, GLA_DV), F32)],
        compiler_params=_params(), name="gla")(proj, proj, proj, proj, s0)


def _route(h1, nffn_ref, wr_hi_ref, wr_lo_ref, br_ref, xn_ref, info_ref, cnt_ref, carry):
    tm = h1.shape[0]

    @pl.when(pl.program_id(0) == 0)
    def _():
        carry[...] = jnp.zeros_like(carry)

    xn = _rms(h1, nffn_ref[...])
    _store_token_tiles(xn_ref, xn, TAGGED_ROWS)
    x_hi = xn.astype(MXU_DTYPE)
    x_lo = (xn - x_hi.astype(F32)).astype(MXU_DTYPE)
    logits = (_dot(x_hi, wr_hi_ref[...]) + _dot(x_lo, wr_hi_ref[...]) + _dot(x_hi, wr_lo_ref[...])
              + br_ref[...])
    lane = lax.broadcasted_iota(jnp.int32, (tm, LANES), 1)
    lane_f = lane.astype(F32)
    vals, ids, hots = [], [], []
    for _ in range(TOP_K):
        m = jnp.max(logits, axis=-1, keepdims=True)
        ix = jnp.min(jnp.where(logits == m, lane_f, float(LANES)), axis=-1, keepdims=True)
        hot = lane_f == ix
        vals.append(m)
        ids.append(ix)
        hots.append(hot)
        logits = jnp.where(hot, -jnp.inf, logits)
    es = [jnp.exp(v - vals[0]) for v in vals]
    den = es[0] + es[1] + es[2] + es[3]
    chosen = jnp.zeros((tm, LANES), F32)
    for hot in hots:
        chosen = chosen + jnp.where(hot, 1.0, 0.0)
    before = (lax.broadcasted_iota(jnp.int32, (tm, tm), 1) < lax.broadcasted_iota(jnp.int32, (tm, tm), 0))
    rank = _dot(jnp.where(before, 1.0, 0.0), chosen) + carry[...]
    info = jnp.zeros((tm, LANES), F32)
    for k in range(TOP_K):
        pos = jnp.sum(jnp.where(hots[k], rank, 0.0), axis=-1, keepdims=True)
        info = jnp.where(lane == k, es[k] / den, info)
        info = jnp.where(lane == TOP_K + k, ids[k].astype(F32), info)
        info = jnp.where(lane == 2 * TOP_K + k, pos, info)
    info_ref[...] = info
    token = pl.program_id(0) * tm + lax.broadcasted_iota(jnp.int32, (tm, 1), 0)
    tag = jnp.zeros((tm, LANES), F32)
    for d in range(TAG_DIGITS):
        tag = jnp.where(lane == d, ((token >> (8 * d)) & 255).astype(F32), tag)
    for k in range(TOP_K):
        tag = jnp.where(lane == TAG_DIGITS + k, ids[k].astype(F32), tag)
    xn_ref[pl.ds(ROW_TILES, tm, stride=TAGGED_ROWS), :] = tag.astype(xn_ref.dtype)
    for s in range(ROW_TILES + 1, TAGGED_ROWS):
        xn_ref[pl.ds(s, tm, stride=TAGGED_ROWS), :] = jnp.zeros((tm, LANES), xn_ref.dtype)
    carry[...] = carry[...] + jnp.sum(chosen, axis=0, keepdims=True)
    cnt_ref[...] = carry[...]


_ROUTE_OUT_SHAPES = lambda n: (jax.ShapeDtypeStruct((n, D_MODEL), F32), jax.ShapeDtypeStruct((n * TAGGED_ROWS, LANES), F32),
                               jax.ShapeDtypeStruct((n, LANES), F32), jax.ShapeDtypeStruct((1, LANES), F32))
_ROUTE_OUT_SPECS = (pl.BlockSpec((TOKEN_TILE, D_MODEL), lambda i: (i, 0)),
                    pl.BlockSpec((TOKEN_TILE * TAGGED_ROWS, LANES), lambda i: (i, 0)),
                    pl.BlockSpec((TOKEN_TILE, LANES), lambda i: (i, 0)),
                    pl.BlockSpec((1, LANES), lambda i: (0, 0)))


def _route_in_specs():
    return [_full((1, D_MODEL)), _full((D_MODEL, LANES)), _full((D_MODEL, LANES)), _full((1, LANES))]


def _even_out_kernel(ypp_ref, yps_ref, op_ref, os_ref, r_ref, hp_ref, hs_ref, gn_ref, w_ref, nffn_ref, wr_hi_ref,
                     wr_lo_ref, br_ref, h1_ref, xn_ref, info_ref, cnt_ref, carry, *, npt):
    o = _split_rows(op_ref, os_ref, npt)
    r = r_ref[...]
    parts = []
    for hd in range(GLA_HEADS):
        sl = slice(hd * GLA_DV, (hd + 1) * GLA_DV)
        oh = o[:, sl]
        oh = oh * lax.rsqrt(jnp.mean(oh * oh, axis=-1, keepdims=True) + RMS_EPS) * gn_ref[...]
        rh = r[:, sl]
        parts.append(oh * (rh * jax.nn.sigmoid(rh)))
    gla = jnp.concatenate(parts, axis=1)
    mix = _dot(_split_rows(ypp_ref, yps_ref, npt), w_ref[:POOL_WIDTH, :]) + _dot(gla, w_ref[POOL_WIDTH:, :])
    h1 = _split_rows(hp_ref, hs_ref, npt) + mix
    h1_ref[...] = h1
    _route(h1, nffn_ref, wr_hi_ref, wr_lo_ref, br_ref, xn_ref, info_ref, cnt_ref, carry)


def _even_out(y_pool_parts, o_gla_parts, proj, h_parts, n_p, gla_norm, w_out, nffn, wr_hi, wr_lo, br):
    n = proj.shape[0]
    return pl.pallas_call(
        functools.partial(_even_out_kernel, npt=n_p // TOKEN_TILE),
        out_shape=_ROUTE_OUT_SHAPES(n),
        grid=(n // TOKEN_TILE,),
        in_specs=_split_specs(TOKEN_TILE, POOL_WIDTH, n_p, False) + _split_specs(TOKEN_TILE, GLA_WIDTH, n_p, False)
        + [pl.BlockSpec((TOKEN_TILE, GLA_WIDTH), lambda i: (i, (IN_EVEN_MAIN - GLA_WIDTH) // GLA_WIDTH))]
        + _split_specs(TOKEN_TILE, D_MODEL, n_p, h_parts[0] is h_parts[1])
        + [_full((1, GLA_DV)), _full((D_MODEL, D_MODEL))] + _route_in_specs(),
        out_specs=_ROUTE_OUT_SPECS,
        scratch_shapes=[pltpu.VMEM((1, LANES), F32)],
        compiler_params=_params(), name="even_out")(
            *y_pool_parts, *o_gla_parts, proj, *h_parts, gla_norm, w_out, nffn, wr_hi, wr_lo, br)


def _rope_tile(x, cos, sin, lo_half):
    swapped = jnp.where(lo_half, pltpu.roll(x, LANES - HEAD_DIM // 2, axis=1), pltpu.roll(x, HEAD_DIM // 2, axis=1))
    return x * cos + swapped * sin


def _odd_in_kernel(h_ref, g_ref, w_ref, b_ref, cos_ref, sin_ref, q_ref, kv_ref):
    xn = _rms(h_ref[...], g_ref[...])
    qkv = _dot(xn, w_ref[...]) + b_ref[...]
    cos = cos_ref[...]
    sin = sin_ref[...]
    lo_half = (lax.broadcasted_iota(jnp.int32, cos.shape, 1) % HEAD_DIM) < HEAD_DIM // 2
    for j in range(Q_WIDTH // LANES):
        sl = slice(j * LANES, (j + 1) * LANES)
        q_ref[:, sl] = _rope_tile(qkv[:, sl], cos, sin, lo_half).astype(q_ref.dtype)
    kv_ref[:, :KV_WIDTH] = _rope_tile(qkv[:, Q_WIDTH:Q_WIDTH + KV_WIDTH], cos, sin, lo_half)
    kv_ref[:, KV_WIDTH:] = qkv[:, Q_WIDTH + KV_WIDTH:]


def _odd_in(h, g, w_qkv, b_qkv, cos, sin, table_block):
    n = h.shape[0]
    wq = w_qkv.shape[1]
    tab = pl.BlockSpec((TOKEN_TILE, LANES), lambda i: (table_block(i), 0))
    return pl.pallas_call(
        _odd_in_kernel,
        out_shape=(jax.ShapeDtypeStruct((n, Q_WIDTH), MXU_DTYPE), jax.ShapeDtypeStruct((n, 2 * KV_WIDTH), F32)),
        grid=(n // TOKEN_TILE,),
        in_specs=[pl.BlockSpec((TOKEN_TILE, D_MODEL), lambda i: (i, 0)), _full((1, D_MODEL)),
                  _full((D_MODEL, wq)), _full((1, wq)), tab, tab],
        out_specs=(pl.BlockSpec((TOKEN_TILE, Q_WIDTH), lambda i: (i, 0)),
                   pl.BlockSpec((TOKEN_TILE, 2 * KV_WIDTH), lambda i: (i, 0))),
        compiler_params=_params(), name="odd_in")(h, g, w_qkv, b_qkv, cos, sin)


def _attn_kernel(sink_ref, q_ref, kv0_ref, kv1_ref, kv2_ref, ck0_ref, ck1_ref, cv0_ref, cv1_ref, o_ref, *, sq):
    i = pl.program_id(0)
    smp = sq.is_sample(i)
    t = sq.tile_in_seq(i)
    kv0 = kv0_ref[...]
    k_old = jnp.where(smp, ck0_ref[...], kv2_ref[:, :KV_WIDTH])
    k_mid = jnp.where(smp, ck1_ref[...], kv1_ref[:, :KV_WIDTH])
    v_old = jnp.where(smp, cv0_ref[...], kv2_ref[:, KV_WIDTH:])
    v_mid = jnp.where(smp, cv1_ref[...], kv1_ref[:, KV_WIDTH:])
    pad = jnp.zeros((CHUNK, KV_WIDTH), F32)
    keys = jnp.concatenate([k_old, k_mid, kv0[:, :KV_WIDTH], pad], axis=0)
    vals = jnp.concatenate([v_old, v_mid, kv0[:, KV_WIDTH:], pad], axis=0)
    n_keys = 4 * CHUNK
    lane = lax.broadcasted_iota(jnp.int32, (n_keys, KV_WIDTH), 1)
    lo = lane < HEAD_DIM
    keys_sw = pltpu.roll(keys, HEAD_DIM, axis=1)
    vals_sw = pltpu.roll(vals, HEAD_DIM, axis=1)
    kcol = lax.broadcasted_iota(jnp.int32, (1, n_keys), 1)
    first_valid = jnp.where(smp, 0, (2 - jnp.minimum(t, 2)) * CHUNK)
    key_ok = jnp.logical_and(kcol >= first_valid, kcol < 3 * CHUNK)
    n_pairs = N_Q_HEADS // N_KV_HEADS // 2
    pair_of_row = lax.broadcasted_iota(jnp.int32, (n_pairs * CHUNK, 1), 0) // CHUNK
    for g in range(N_KV_HEADS):
        own, other = (keys, keys_sw) if g == 0 else (keys_sw, keys)
        vown, vother = (vals, vals_sw) if g == 0 else (vals_sw, vals)
        kb = jnp.concatenate([jnp.where(lo, own, 0.0), jnp.where(lo, 0.0, other)], axis=0).astype(MXU_DTYPE)
        vb = jnp.concatenate([jnp.where(lo, vown, 0.0), jnp.where(lo, 0.0, vother)], axis=0).astype(MXU_DTYPE)
        pairs = [slice((g * n_pairs + pr) * LANES, (g * n_pairs + pr + 1) * LANES) for pr in range(n_pairs)]
        qg = jnp.concatenate([q_ref[:, sl] for sl in pairs], axis=0)
        s = _dot_nt(qg, kb) * (HEAD_DIM ** -0.5)
        halves = []
        for half in range(2):
            sh = jnp.where(key_ok, s[:, half * n_keys:(half + 1) * n_keys], NEG_INF)
            sink = jnp.zeros((n_pairs * CHUNK, 1), F32)
            for pr in range(n_pairs):
                sink = jnp.where(pair_of_row == pr, sink_ref[2 * (g * n_pairs + pr) + half], sink)
            m = jnp.maximum(jnp.max(sh, axis=-1, keepdims=True), sink)
            p = jnp.exp(sh - m)
            halves.append(p * (1.0 / (jnp.sum(p, axis=-1, keepdims=True) + jnp.exp(sink - m))))
        o = _dot(jnp.concatenate(halves, axis=1), vb)
        for pr, sl in enumerate(pairs):
            o_ref[:, sl] = o[pr * CHUNK:(pr + 1) * CHUNK].astype(o_ref.dtype)


def _attention(sinks, q, kv, ck, cv, sq):
    n = q.shape[0]
    npt = sq.n_prompt_tiles
    prev = lambda d: (lambda i, s: (jnp.where(sq.is_sample(i), i, jnp.maximum(i - d, 0)), 0))
    cache = lambda d: (lambda i, s: (jnp.where(sq.is_sample(i), 2 * (i - npt) + d, 0), 0))
    kvspec = lambda f: pl.BlockSpec((CHUNK, 2 * KV_WIDTH), f)
    cspec = lambda f: pl.BlockSpec((CHUNK, KV_WIDTH), f)
    grid_spec = pltpu.PrefetchScalarGridSpec(
        num_scalar_prefetch=1, grid=(sq.n_tiles,),
        in_specs=[pl.BlockSpec((CHUNK, Q_WIDTH), lambda i, s: (i, 0)),
                  kvspec(prev(0)), kvspec(prev(1)), kvspec(prev(2)),
                  cspec(cache(0)), cspec(cache(1)), cspec(cache(0)), cspec(cache(1))],
        out_specs=pl.BlockSpec((CHUNK, Q_WIDTH), lambda i, s: (i, 0)))
    return pl.pallas_call(
        functools.partial(_attn_kernel, sq=sq),
        out_shape=jax.ShapeDtypeStruct((n, Q_WIDTH), MXU_DTYPE),
        grid_spec=grid_spec, compiler_params=_params(), name="attention")(sinks, q, kv, kv, kv, ck, ck, cv, cv)


def _odd_out_kernel(o_ref, h_ref, w_ref, b_ref, nffn_ref, wr_hi_ref, wr_lo_ref, br_ref,
                    h1_ref, xn_ref, info_ref, cnt_ref, carry):
    h1 = h_ref[...] + _dot(o_ref[...], w_ref[...]) + b_ref[...]
    h1_ref[...] = h1
    _route(h1, nffn_ref, wr_hi_ref, wr_lo_ref, br_ref, xn_ref, info_ref, cnt_ref, carry)


def _odd_out(o, h, w_out, b_out, nffn, wr_hi, wr_lo, br):
    n = h.shape[0]
    row = pl.BlockSpec((TOKEN_TILE, D_MODEL), lambda i: (i, 0))
    return pl.pallas_call(
        _odd_out_kernel,
        out_shape=_ROUTE_OUT_SHAPES(n),
        grid=(n // TOKEN_TILE,),
        in_specs=[row, row, _full((D_MODEL, D_MODEL)), _full((1, D_MODEL))] + _route_in_specs(),
        out_specs=_ROUTE_OUT_SPECS,
        scratch_shapes=[pltpu.VMEM((1, LANES), F32)],
        compiler_params=_params(), name="odd_out")(o, h, w_out, b_out, nffn, wr_hi, wr_lo, br)


def _token_copy(src, s, dst, d, sem, rows=ROW_TILES):
    return pltpu.make_async_copy(src.at[pl.ds(pl.multiple_of(s, rows), rows)],
                                 dst.at[pl.ds(pl.multiple_of(d, rows), rows)], sem)


ZERO_TOKENS = EXPERT_TILE // 2


def _zero_fill(fill_ref, zeros, xs_out, zsem):
    zeros[...] = jnp.zeros_like(zeros)

    def sweep(wait):
        def go(cp):
            cp.wait() if wait else cp.start()

        def tail(e, c):
            off, length = fill_ref[e], fill_ref[N_EXPERTS + e]
            for bit in range(EXPERT_TILE.bit_length() - 1):
                rows = (1 << bit) * TAGGED_ROWS

                @pl.when((length >> bit) & 1 == 1)
                def _():
                    o = pl.multiple_of(off + (length & ((1 << bit) - 1)) * TAGGED_ROWS, TAGGED_ROWS)
                    go(pltpu.make_async_copy(zeros.at[pl.ds(0, rows)], xs_out.at[pl.ds(o, rows)], zsem))
            return c

        lax.fori_loop(0, N_EXPERTS, tail, 0)

        def unused(t, c):
            o = pl.multiple_of(fill_ref[2 * N_EXPERTS] + t * ZERO_TOKENS * TAGGED_ROWS, TAGGED_ROWS)
            go(pltpu.make_async_copy(zeros, xs_out.at[pl.ds(o, ZERO_TOKENS * TAGGED_ROWS)], zsem))
            return c

        lax.fori_loop(0, fill_ref[2 * N_EXPERTS + 1], unused, 0)

    sweep(wait=False)
    sweep(wait=True)


def _dispatch_kernel(fill_ref, dest_hbm, x_ref, xs_out, dest_smem, zeros, sem, dsems, zsem):
    i = pl.program_id(0)
    n_slots = TOKEN_TILE * TOP_K

    @pl.when(i == 0)
    def _():
        _zero_fill(fill_ref, zeros, xs_out, zsem)

    slot = i % 2

    def dest_rows(step, s):
        return pltpu.make_async_copy(dest_hbm.at[pl.ds(step * n_slots, n_slots)], dest_smem.at[s], dsems.at[s])

    @pl.when(i == 0)
    def _():
        dest_rows(0, 0).start()

    dest_rows(i, slot).wait()

    @pl.when(i + 1 < pl.num_programs(0))
    def _():
        dest_rows(i + 1, 1 - slot).start()

    def issue(t, c):
        for k in range(TOP_K):
            _token_copy(x_ref, t * TAGGED_ROWS, xs_out, dest_smem[slot, t * TOP_K + k], sem, TAGGED_ROWS).start()
        return c

    lax.fori_loop(0, TOKEN_TILE, issue, 0, unroll=2)
    whole = xs_out.at[pl.ds(0, n_slots * TAGGED_ROWS)]
    pltpu.make_async_copy(whole, whole, sem).wait()


def _dispatch(fill, dest, xt, n_rows):
    n = xt.shape[0] // TAGGED_ROWS
    grid_spec = pltpu.PrefetchScalarGridSpec(
        num_scalar_prefetch=1, grid=(n // TOKEN_TILE,),
        in_specs=[pl.BlockSpec(memory_space=pl.ANY),
                  pl.BlockSpec((TOKEN_TILE * TAGGED_ROWS, LANES), lambda i, f: (i, 0))],
        out_specs=pl.BlockSpec(memory_space=pl.ANY),
        scratch_shapes=[pltpu.SMEM((2, TOKEN_TILE * TOP_K), jnp.int32), pltpu.VMEM((ZERO_TOKENS * TAGGED_ROWS, LANES), xt.dtype),
                        pltpu.SemaphoreType.DMA, pltpu.SemaphoreType.DMA((2,)), pltpu.SemaphoreType.DMA])
    return pl.pallas_call(
        _dispatch_kernel, out_shape=jax.ShapeDtypeStruct((n_rows * TAGGED_ROWS, LANES), xt.dtype),
        grid_spec=grid_spec, compiler_params=_params(), name="moe_dispatch")(fill, dest, xt)


def _experts_kernel(te_ref, tr_ref, x_ref, wgu_ref, bgu_ref, wd_ref, bd_ref, yt_ref,
                    wgu_mxu, wd_mxu, ybuf, place_vmem, place_smem, ysems, psem, *, n_tokens):
    i = pl.program_id(0)
    slot = i % 2
    prev = 1 - slot
    tile_rows = EXPERT_TILE * ROW_TILES
    spare = yt_ref.shape[0] - 2 * tile_rows
    col = lax.broadcasted_iota(jnp.int32, (1, EXPERT_TILE), 1)

    def rows_done(s):
        whole = yt_ref.at[pl.ds(0, tile_rows)]
        pltpu.make_async_copy(whole, whole, ysems.at[s]).wait()

    def places_to_smem(first_rows, s):
        place_vmem[...] = jnp.broadcast_to(first_rows, place_vmem.shape)
        return pltpu.make_async_copy(place_vmem, place_smem.at[s], psem)

    def start_row_copy(j, s):
        _token_copy(ybuf.at[s], j * ROW_TILES, yt_ref, place_smem[s, 0, j], ysems.at[s]).start()

    @pl.when(i == 0)
    def _():
        ybuf[...] = jnp.zeros(ybuf.shape, F32)
        for s in range(2):
            cp = pltpu.make_async_copy(ybuf.at[0], yt_ref.at[pl.ds(spare + s * tile_rows, tile_rows)], ysems.at[0])
            cp.start()
            cp.wait()
        cp = places_to_smem(spare + tile_rows + col * ROW_TILES, 1)
        cp.start()
        cp.wait()

    @pl.when(jnp.logical_or(i == 0, te_ref[i] != te_ref[jnp.maximum(i - 1, 0)]))
    def _():
        wgu_mxu[...] = wgu_ref[0, 0].astype(MXU_DTYPE)
        wd_mxu[...] = wd_ref[0, 0].astype(MXU_DTYPE)

    n_valid = tr_ref[i]
    after_last = jnp.logical_and(n_valid == 0, jnp.logical_and(i >= 1, tr_ref[jnp.maximum(i - 1, 0)] > 0))

    @pl.when(jnp.logical_and(i >= 1, jnp.logical_or(n_valid > 0, after_last)))
    def _():
        rows_done(slot)

    @pl.when(n_valid > 0)
    def _():
        for j in range(EXPERT_TILE):
            start_row_copy(j, prev)
        tag = x_ref[pl.ds(ROW_TILES, EXPERT_TILE, stride=TAGGED_ROWS), :].astype(F32)
        lane = lax.broadcasted_iota(jnp.int32, (EXPERT_TILE, LANES), 1)
        is_id = jnp.logical_and(lane >= TAG_DIGITS, lane < TAG_DIGITS + TOP_K)
        mine = jnp.logical_and(is_id, tag == te_ref[i].astype(F32))
        rank = jnp.sum(jnp.where(mine, (lane - TAG_DIGITS).astype(F32), 0.0), axis=-1, keepdims=True)
        token = tag[:, 0:1] + 256.0 * tag[:, 1:2] + 65536.0 * tag[:, 2:3]
        place = rank * n_tokens + token
        hi = jnp.floor(place * (1.0 / 65536.0))
        mid = jnp.floor((place - hi * 65536.0) * (1.0 / 256.0))
        digits = jnp.where(lane == 0, place - hi * 65536.0 - mid * 256.0, jnp.where(lane == 1, mid, jnp.where(lane == 2, hi, 0.0)))
        pick = jnp.where(lax.broadcasted_iota(jnp.int32, (ROW_TILES, LANES), 0)
                         == lax.broadcasted_iota(jnp.int32, (ROW_TILES, LANES), 1), 1.0, 0.0)
        planes = _dot_nt(pick, digits)
        place_row = (planes[0:1] + 256.0 * planes[1:2] + 65536.0 * planes[2:3]).astype(jnp.int32) * ROW_TILES
        to_smem = places_to_smem(jnp.where(col < n_valid, place_row, spare + slot * tile_rows + col * ROW_TILES), slot)
        to_smem.start()

        x = _load_token_tiles(x_ref, EXPERT_TILE, TAGGED_ROWS)
        hgu = _dot(x, wgu_mxu[...]) + bgu_ref[0, 0]
        gate = jnp.minimum(hgu[:, :D_MODEL], SWIGLU_LIMIT)
        up = jnp.clip(hgu[:, D_MODEL:], -SWIGLU_LIMIT, SWIGLU_LIMIT)
        act = (up + 1.0) * gate * jax.nn.sigmoid(SWIGLU_ALPHA * gate)
        _store_token_tiles(ybuf.at[slot], _dot(act, wd_mxu[...]) + bd_ref[0, 0])
        to_smem.wait()

    @pl.when(after_last)
    def _():
        def issue(j, c):
            start_row_copy(j, prev)
            return c

        lax.fori_loop(0, EXPERT_TILE, issue, 0, unroll=8)
        rows_done(prev)


def _experts(layer, tile_expert, tile_rows, xs, wgu, bgu, wd, bd, n_tokens):
    rows = xs.shape[0] // TAGGED_ROWS
    w = lambda shape: pl.BlockSpec((1, 1) + shape, lambda i, te, tr: (layer, te[i], 0, 0))
    grid_spec = pltpu.PrefetchScalarGridSpec(
        num_scalar_prefetch=2, grid=(rows // EXPERT_TILE,),
        in_specs=[pl.BlockSpec((EXPERT_TILE * TAGGED_ROWS, LANES), lambda i, te, tr: (i, 0)),
                  w((D_MODEL, 2 * D_MODEL)), w((1, 2 * D_MODEL)), w((D_MODEL, D_MODEL)), w((1, D_MODEL))],
        out_specs=pl.BlockSpec(memory_space=pl.ANY),
        scratch_shapes=[pltpu.VMEM((D_MODEL, 2 * D_MODEL), MXU_DTYPE), pltpu.VMEM((D_MODEL, D_MODEL), MXU_DTYPE),
                        pltpu.VMEM((2, EXPERT_TILE * ROW_TILES, LANES), F32),
                        pltpu.VMEM((ROW_TILES, EXPERT_TILE), jnp.int32), pltpu.SMEM((2, ROW_TILES, EXPERT_TILE), jnp.int32),
                        pltpu.SemaphoreType.DMA((2,)), pltpu.SemaphoreType.DMA])
    yt_rows = (n_tokens * TOP_K + 2 * EXPERT_TILE) * ROW_TILES
    return pl.pallas_call(
        functools.partial(_experts_kernel, n_tokens=n_tokens), out_shape=jax.ShapeDtypeStruct((yt_rows, LANES), F32),
        grid_spec=grid_spec, compiler_params=_params(), name="moe_experts")(
            tile_expert, tile_rows, xs, wgu, bgu, wd, bd)


def _combine_kernel(*refs, final, npt):
    y_refs, outs = refs[:TOP_K], refs[TOP_K + 7:]
    info_ref, h_ref, p_prompt_ref, p_sample_ref, pp_ref, pg_ref, nf_ref = refs[TOP_K:TOP_K + 7]
    i = pl.program_id(0)
    gates = info_ref[...]
    moe = None
    for k in range(TOP_K):
        rows = _load_token_tiles(y_refs[k], COMBINE_TILE)
        moe = gates[:, k:k + 1] * rows if moe is None else moe + gates[:, k:k + 1] * rows
    h2 = h_ref[...] + moe
    p = jnp.where(i < npt, p_prompt_ref[0], p_sample_ref[0])
    h3 = h2 + jax.nn.sigmoid(_dot(h2, pg_ref[...])) * _dot(p, pp_ref[...])
    if not final:
        outs[0][...] = h3
        return
    y = _rms(h3, nf_ref[...])

    @pl.when(i < npt)
    def _():
        outs[0][...] = y

    @pl.when(i >= npt)
    def _():
        outs[1][...] = y


def _combine(layer, yt, info, h1, p_prompt, p_sample, ple_proj, ple_gate, norm_final, final):
    n = h1.shape[0]
    n_p = p_prompt.shape[1]
    npt = n_p // COMBINE_TILE
    steps = n // COMBINE_TILE
    row = lambda w: pl.BlockSpec((COMBINE_TILE, w), lambda i: (i, 0))
    rank_rows = lambda k: pl.BlockSpec((COMBINE_TILE * ROW_TILES, LANES), lambda i: (k * steps + i, 0))
    prompt_rows = lambda i: jnp.minimum(i, npt - 1)
    sample_rows = lambda i: jnp.maximum(i - npt, 0)
    if final:
        out_shape = (jax.ShapeDtypeStruct((n_p, D_MODEL), F32), jax.ShapeDtypeStruct((n - n_p, D_MODEL), F32))
        out_specs = (pl.BlockSpec((COMBINE_TILE, D_MODEL), lambda i: (prompt_rows(i), 0)),
                     pl.BlockSpec((COMBINE_TILE, D_MODEL), lambda i: (sample_rows(i), 0)))
    else:
        out_shape, out_specs = jax.ShapeDtypeStruct((n, D_MODEL), F32), row(D_MODEL)
    return pl.pallas_call(
        functools.partial(_combine_kernel, final=final, npt=npt),
        out_shape=out_shape,
        grid=(steps,),
        in_specs=[rank_rows(k) for k in range(TOP_K)] + [
            row(LANES), row(D_MODEL),
            pl.BlockSpec((1, COMBINE_TILE, PLE_DIM), lambda i: (layer, prompt_rows(i), 0)),
            pl.BlockSpec((1, COMBINE_TILE, PLE_DIM), lambda i: (layer, sample_rows(i), 0)),
            _full((PLE_DIM, D_MODEL)), _full((D_MODEL, D_MODEL)), _full((1, D_MODEL))],
        out_specs=out_specs,
        compiler_params=_params(), name="moe_combine")(
            *([yt] * TOP_K), info, h1, p_prompt, p_sample, ple_proj, ple_gate, norm_final)


def _moe_and_embed(layer, h1, xt, info, counts, p_prompt, p_sample, wgu, bgu, wd, bd, ple_proj, ple_gate,
                   norm_final, final):
    n = h1.shape[0]
    n_tiles = (n * TOP_K + N_EXPERTS * (EXPERT_TILE - 1)) // EXPERT_TILE + 1
    ids = info[:, TOP_K:2 * TOP_K].astype(jnp.int32)
    rank = info[:, 2 * TOP_K:3 * TOP_K].astype(jnp.int32)
    cnt = counts[0, :N_EXPERTS].astype(jnp.int32)
    padded = ((cnt + EXPERT_TILE - 1) // EXPERT_TILE) * EXPERT_TILE
    ends = jnp.cumsum(padded)
    starts = ends - padded
    dest = ((starts[ids] + rank) * TAGGED_ROWS).reshape(-1)
    tile_start = jnp.arange(n_tiles, dtype=jnp.int32) * EXPERT_TILE
    tile_expert = jnp.minimum(jnp.sum((tile_start[:, None] >= ends[None, :]).astype(jnp.int32), axis=1), N_EXPERTS - 1)
    tile_rows = jnp.clip(cnt[tile_expert] - (tile_start - starts[tile_expert]), 0, EXPERT_TILE)
    tile_rows = jnp.where(tile_start < ends[-1], tile_rows, 0).astype(jnp.int32)
    fill = jnp.concatenate([(starts + cnt) * TAGGED_ROWS, padded - cnt,
                            jnp.stack([ends[-1] * TAGGED_ROWS,
                                       (n_tiles * EXPERT_TILE - ends[-1]) // ZERO_TOKENS])]).astype(jnp.int32)
    xs = _dispatch(fill, dest, xt, n_tiles * EXPERT_TILE)
    yt = _experts(layer, tile_expert, tile_rows, xs, wgu, bgu, wd, bd, n)
    return _combine(layer, yt, info, h1, p_prompt, p_sample, ple_proj, ple_gate, norm_final, final)


def _rope_tables(prompt_len, n_sample_seq, sample_len):
    half = HEAD_DIM // 2
    inv = jnp.power(jnp.float32(ROPE_THETA), -jnp.arange(half, dtype=F32) / half)
    pos = jnp.concatenate([jnp.arange(prompt_len), jnp.tile(PAST_LEN + jnp.arange(sample_len), n_sample_seq)])
    ang = pos.astype(F32)[:, None] * inv[None, :]
    cos = jnp.tile(jnp.cos(ang), (1, LANES // half))
    sin = jnp.tile(jnp.concatenate([-jnp.sin(ang), jnp.sin(ang)], axis=1), (1, LANES // HEAD_DIM))
    return cos, sin


def kernel(x_prompt, x_sample, state_pool, state_gla, cache_k, cache_v, p_prompt, p_sample, norm_mix, norm_ffn, norm_final, w_in_even, pool_w, pool_scale, gla_w_gate, gla_b_gate, gla_norm, w_out_even, w_qkv_odd, b_qkv_odd, attn_sinks, w_out_odd, b_out_odd, w_router, b_router, w_gate_up, b_gate_up, w_down, b_down, ple_proj, ple_gate):
    bsz, t_len, _ = x_prompt.shape
    dec_bsz, dec_len, _ = x_sample.shape
    depth = norm_mix.shape[0]
    n_p, n_s = bsz * t_len, dec_bsz * dec_len
    n = n_p + n_s
    assert dec_len == CHUNK and n_s == TOKEN_TILE and t_len % TOKEN_TILE == 0 and cache_k.shape[2] == WINDOW
    sq = _Seq(bsz, t_len, dec_bsz)
    grp_p = _Group(bsz, t_len, min(SEQ_TILE, t_len), 0, 0)
    grp_s = _Group(dec_bsz, dec_len, dec_len, n_p, PAST_LEN)
    bf = lambda a: a.astype(MXU_DTYPE)
    row = lambda a: a.reshape(1, -1)

    h_parts = (x_prompt.reshape(n_p, D_MODEL), x_sample.reshape(n_s, D_MODEL))
    p_parts = (p_prompt.reshape(depth, n_p, PLE_DIM), p_sample.reshape(depth, n_s, PLE_DIM))
    b_gu = b_gate_up.reshape(depth, N_EXPERTS, 1, 2 * D_MODEL)
    b_dn = b_down.reshape(depth, N_EXPERTS, 1, D_MODEL)
    cos, sin = _rope_tables(t_len, dec_bsz, dec_len)
    tiles_per_seq = t_len // TOKEN_TILE
    table_block = lambda i: jnp.where(i < bsz * tiles_per_seq, i % tiles_per_seq, tiles_per_seq)

    pools, glas, new_k, new_v = [], [], [], []
    for i in range(depth):
        if i % 2 == 0:
            e = i // 2
            w_in = jnp.pad(bf(w_in_even[e]), ((0, 0), (0, (-IN_EVEN) % LANES)))
            wg = jnp.pad(bf(gla_w_gate[e]), ((0, w_in.shape[1] - IN_EVEN_MAIN - GLA_RANK), (0, 0)))
            proj = _even_in(h_parts, n, n_p, row(norm_mix[i]), w_in, wg, row(gla_b_gate[e]))
            pw, ps = bf(pool_w[e]), row(pool_scale[e])
            y_pool_p, pool_p = _pool(proj, jnp.zeros((bsz, 16, POOL_WIDTH), F32), pw, ps, grp_p)
            y_pool_s, pool_s = _pool(proj, jnp.pad(state_pool[e], ((0, 0), (1, 0), (0, 0))), pw, ps, grp_s)
            o_gla_p, gla_p = _gla(proj, jnp.zeros((bsz,) + state_gla.shape[2:], F32), grp_p)
            o_gla_s, gla_s = _gla(proj, state_gla[e], grp_s)
            pools.append((pool_p[:, 1:], pool_s[:, 1:]))
            glas.append((gla_p, gla_s))
            wr = jnp.pad(w_router[i], ((0, 0), (0, LANES - N_EXPERTS)))
            wr_hi = bf(wr)
            wr_lo = bf(wr - wr_hi.astype(F32))
            br = jnp.pad(row(b_router[i]), ((0, 0), (0, LANES - N_EXPERTS)), constant_values=NEG_INF)
            h1, xn, info, counts = _even_out((y_pool_p, y_pool_s), (o_gla_p, o_gla_s), proj, h_parts, n_p,
                                             row(gla_norm[e]), bf(w_out_even[e]), row(norm_ffn[i]), wr_hi, wr_lo, br)
        else:
            o = i // 2
            h = h_parts[0]
            q, kv = _odd_in(h, row(norm_mix[i]), bf(w_qkv_odd[o]), row(b_qkv_odd[o]), cos, sin, table_block)
            ck = cache_k[o].reshape(dec_bsz * WINDOW, KV_WIDTH)
            cv = cache_v[o].reshape(dec_bsz * WINDOW, KV_WIDTH)
            att = _attention(attn_sinks[o], q, kv, ck, cv, sq)
            kv_p = kv[:n_p].reshape(bsz, t_len, 2 * KV_WIDTH)[:, -WINDOW:]
            kv_s = kv[n_p:].reshape(dec_bsz, dec_len, 2 * KV_WIDTH)
            hd = (N_KV_HEADS, HEAD_DIM)
            new_k.append((kv_p[..., :KV_WIDTH].reshape(bsz, WINDOW, *hd),
                          jnp.concatenate([cache_k[o], kv_s[..., :KV_WIDTH].reshape(dec_bsz, dec_len, *hd)], axis=1)[:, -WINDOW:]))
            new_v.append((kv_p[..., KV_WIDTH:].reshape(bsz, WINDOW, *hd),
                          jnp.concatenate([cache_v[o], kv_s[..., KV_WIDTH:].reshape(dec_bsz, dec_len, *hd)], axis=1)[:, -WINDOW:]))
            wr = jnp.pad(w_router[i], ((0, 0), (0, LANES - N_EXPERTS)))
            wr_hi = bf(wr)
            wr_lo = bf(wr - wr_hi.astype(F32))
            br = jnp.pad(row(b_router[i]), ((0, 0), (0, LANES - N_EXPERTS)), constant_values=NEG_INF)
            h1, xn, info, counts = _odd_out(att, h, bf(w_out_odd[o]), row(b_out_odd[o]),
                                            row(norm_ffn[i]), wr_hi, wr_lo, br)
        final = i == depth - 1
        out = _moe_and_embed(i, h1, xn, info, counts, *p_parts, w_gate_up, b_gu, w_down, b_dn,
                             bf(ple_proj[i]), bf(ple_gate[i]), row(norm_final), final)
        h_parts = out if final else (out, out)

    y_prompt = h_parts[0].reshape(bsz, t_len, D_MODEL)
    y_sample = h_parts[1].reshape(dec_bsz, dec_len, D_MODEL)
    part = lambda pairs, j: jnp.stack([p[j] for p in pairs])
    return (y_prompt, y_sample, part(pools, 0), part(glas, 0), part(new_k, 0), part(new_v, 0),
            part(pools, 1), part(glas, 1), part(new_k, 1), part(new_v, 1))
```

```python
import functools

import jax
import jax.numpy as jnp
from jax import lax
from jax.experimental import pallas as pl
from jax.experimental.pallas import tpu as pltpu

F32 = jnp.float32
MXU_DTYPE = jnp.bfloat16

V7X_VMEM_BYTES = 64 * 1024 * 1024
VMEM_LIMIT = (V7X_VMEM_BYTES * 7) // 8
LANES = 128

D_MODEL = 1024
CHUNK = 64
PAST_LEN = 2048
PLE_DIM = 256
RMS_EPS = 1e-6
POOL_WINDOWS = (2, 4, 8, 16)
POOL_WIDTH = 512
POOL_GROUP_DIM = 128
POOL_STATE = 15
GLA_HEADS = 4
GLA_DK = 64
GLA_DV = 128
GLA_KEY_WIDTH = GLA_HEADS * GLA_DK
GLA_WIDTH = GLA_HEADS * GLA_DV
GLA_RANK = 16
GLA_TAU = 16.0
GLA_BLOCK = 16
IN_EVEN = POOL_WIDTH + 2 * GLA_KEY_WIDTH + 2 * GLA_WIDTH + GLA_RANK
IN_EVEN_MAIN = IN_EVEN - GLA_RANK
PROJ_WIDTH = IN_EVEN_MAIN + GLA_KEY_WIDTH
N_Q_HEADS = 16
N_KV_HEADS = 2
HEAD_DIM = 64
WINDOW = 128
ROPE_THETA = 10000.0
Q_WIDTH = N_Q_HEADS * HEAD_DIM
KV_WIDTH = N_KV_HEADS * HEAD_DIM
N_EXPERTS = 32
TOP_K = 4
SWIGLU_LIMIT = 7.0
SWIGLU_ALPHA = 1.702
NEG_INF = -1e30

TOKEN_TILE = 512
EXPERT_TILE = 512
COMBINE_TILE = 256


def _dot(a, b):
    return jnp.dot(a.astype(MXU_DTYPE), b.astype(MXU_DTYPE), preferred_element_type=F32)


def _dot_nt(a, b):
    return lax.dot_general(a.astype(MXU_DTYPE), b.astype(MXU_DTYPE), (((1,), (1,)), ((), ())),
                           preferred_element_type=F32)


def _split3(x):
    x1 = x.astype(MXU_DTYPE)
    r1 = x - x1.astype(F32)
    x2 = r1.astype(MXU_DTYPE)
    x3 = (r1 - x2.astype(F32)).astype(MXU_DTYPE)
    return x1, x2, x3


def _rms(x, g):
    return x * lax.rsqrt(jnp.mean(x * x, axis=-1, keepdims=True) + RMS_EPS) * g


ROW_TILES = D_MODEL // LANES
TAGGED_ROWS = 2 * ROW_TILES
TAG_DIGITS = 3


def _store_token_tiles(ref, x, pitch=ROW_TILES):
    for s in range(ROW_TILES):
        ref[pl.ds(s, x.shape[0], stride=pitch), :] = x[:, s * LANES:(s + 1) * LANES]


def _load_token_tiles(ref, rows, pitch=ROW_TILES):
    return jnp.concatenate([ref[pl.ds(s, rows, stride=pitch), :] for s in range(ROW_TILES)], axis=1)


def _params(n_axes=1):
    return pltpu.CompilerParams(dimension_semantics=("arbitrary",) * n_axes, vmem_limit_bytes=VMEM_LIMIT)


def _full(shape):
    return pl.BlockSpec(shape, lambda *_: (0,) * len(shape))


class _Seq:
    def __init__(self, n_prompt_seq, prompt_len, n_sample_seq):
        self.tiles_per_seq = prompt_len // CHUNK
        self.n_prompt_seq = n_prompt_seq
        self.n_prompt_tiles = n_prompt_seq * self.tiles_per_seq
        self.n_tiles = self.n_prompt_tiles + n_sample_seq
        self.n_seq = n_prompt_seq + n_sample_seq

    def is_sample(self, i):
        return i >= self.n_prompt_tiles

    def tile_in_seq(self, i):
        return jnp.where(self.is_sample(i), 0, i % self.tiles_per_seq)

    def seq(self, i):
        return jnp.where(self.is_sample(i), self.n_prompt_seq + i - self.n_prompt_tiles, i // self.tiles_per_seq)

    def last(self, i):
        return jnp.logical_or(self.is_sample(i), i % self.tiles_per_seq == self.tiles_per_seq - 1)

    def pos0(self, i):
        return jnp.where(self.is_sample(i), PAST_LEN, self.tile_in_seq(i) * CHUNK)


class _Group:
    def __init__(self, n_seq, seq_len, tile, row0, pos_base):
        assert seq_len % tile == 0 and row0 % tile == 0
        self.n_seq, self.tile, self.pos_base = n_seq, tile, pos_base
        self.tiles_per_seq = seq_len // tile
        self.n_tiles = n_seq * self.tiles_per_seq
        self.block0 = row0 // tile

    def block(self, i):
        return self.block0 + i

    def seq(self, i):
        return i // self.tiles_per_seq

    def tile_in_seq(self, i):
        return i % self.tiles_per_seq

    def last(self, i):
        return i % self.tiles_per_seq == self.tiles_per_seq - 1

    def pos0(self, i):
        return self.pos_base + self.tile_in_seq(i) * self.tile


SEQ_TILE = 256


def _split_specs(tile, width, n_prompt, same_array):
    npt = n_prompt // tile
    off = npt if same_array else 0
    return [pl.BlockSpec((tile, width), lambda i: (jnp.minimum(i, npt - 1), 0)),
            pl.BlockSpec((tile, width), lambda i: (jnp.maximum(i - npt, 0) + off, 0))]


def _split_rows(prompt_ref, sample_ref, n_prompt_tiles):
    return jnp.where(pl.program_id(0) < n_prompt_tiles, prompt_ref[...], sample_ref[...])


def _even_in_kernel(hp_ref, hs_ref, g_ref, w_ref, wg_ref, bg_ref, out_ref, *, npt):
    xn = _rms(_split_rows(hp_ref, hs_ref, npt), g_ref[...])
    proj = _dot(xn, w_ref[...])
    z = proj[:, IN_EVEN_MAIN:]
    a = _dot(z, wg_ref[...]) + bg_ref[...]
    log_alpha = (jnp.minimum(a, 0.0) - jnp.log1p(jnp.exp(-jnp.abs(a)))) * (1.0 / GLA_TAU)
    out_ref[:, :IN_EVEN_MAIN] = proj[:, :IN_EVEN_MAIN]
    out_ref[:, IN_EVEN_MAIN:] = log_alpha


def _even_in(h_parts, n, n_p, g, w_in, wg, bg):
    wp = w_in.shape[1]
    return pl.pallas_call(
        functools.partial(_even_in_kernel, npt=n_p // TOKEN_TILE),
        out_shape=jax.ShapeDtypeStruct((n, PROJ_WIDTH), F32),
        grid=(n // TOKEN_TILE,),
        in_specs=_split_specs(TOKEN_TILE, D_MODEL, n_p, h_parts[0] is h_parts[1]) + [
            _full((1, D_MODEL)), _full((D_MODEL, wp)), _full((wp - IN_EVEN_MAIN, GLA_KEY_WIDTH)),
            _full((1, GLA_KEY_WIDTH))],
        out_specs=pl.BlockSpec((TOKEN_TILE, PROJ_WIDTH), lambda i: (i, 0)),
        compiler_params=_params(), name="even_in")(*h_parts, g, w_in, wg, bg)


def _pool_kernel(u_ref, init_ref, pw_ref, ps_ref, y_ref, st_ref, buf, *, grp):
    i = pl.program_id(0)
    rows = grp.tile

    @pl.when(grp.tile_in_seq(i) == 0)
    def _():
        buf[0:16, :] = init_ref[0]

    u = u_ref[...]
    buf[16:16 + rows, :] = u
    pos = grp.pos0(i) + lax.broadcasted_iota(jnp.int32, (rows, 1), 0)
    for g, w in enumerate(POOL_WINDOWS):
        sl = slice(g * POOL_GROUP_DIM, (g + 1) * POOL_GROUP_DIM)
        acc = u[:, sl]
        for j in range(1, w):
            acc = acc + buf[16 - j:16 - j + rows, sl]
        cnt = jnp.minimum(w, pos + 1).astype(F32)
        d = acc / cnt - u[:, sl]
        y_ref[:, sl] = _dot(d, pw_ref[g]) * ps_ref[:, sl]
    tail = buf[rows:rows + 16, :]
    st_ref[0] = tail
    buf[0:16, :] = tail


def _pool(proj, init, pool_w, pool_scale, grp):
    return pl.pallas_call(
        functools.partial(_pool_kernel, grp=grp),
        out_shape=(jax.ShapeDtypeStruct((grp.n_tiles * grp.tile, POOL_WIDTH), F32),
                   jax.ShapeDtypeStruct((grp.n_seq, 16, POOL_WIDTH), F32)),
        grid=(grp.n_tiles,),
        in_specs=[pl.BlockSpec((grp.tile, POOL_WIDTH), lambda i: (grp.block(i), 0)),
                  pl.BlockSpec((1, 16, POOL_WIDTH), lambda i: (grp.seq(i), 0, 0)),
                  _full((len(POOL_WINDOWS), POOL_GROUP_DIM, POOL_GROUP_DIM)), _full((1, POOL_WIDTH))],
        out_specs=(pl.BlockSpec((grp.tile, POOL_WIDTH), lambda i: (i, 0)),
                   pl.BlockSpec((1, 16, POOL_WIDTH), lambda i: (grp.seq(i), 0, 0))),
        scratch_shapes=[pltpu.VMEM((grp.tile + 16, POOL_WIDTH), F32)],
        compiler_params=_params(), name="pool")(proj, init, pool_w, pool_scale)


def _gla_kernel(q_ref, k_ref, v_ref, g_ref, s0_ref, o_ref, sout_ref, state, before, *, grp):
    i = pl.program_id(0)
    tile = grp.tile
    n_blk = tile // GLA_BLOCK

    @pl.when(grp.tile_in_seq(i) == 0)
    def _():
        state[...] = s0_ref[0]

    g = g_ref[...]
    row = lax.broadcasted_iota(jnp.int32, (tile, tile), 0)
    col = lax.broadcasted_iota(jnp.int32, (tile, tile), 1)
    same = (row >> 4) == (col >> 4)
    causal = jnp.logical_and(same, col <= row)
    tri = jnp.where(causal, 1.0, 0.0).astype(MXU_DTYPE)
    ones = jnp.where(same, 1.0, 0.0).astype(MXU_DTYPE)
    g1, g2, g3 = _split3(g)
    b = _dot(tri, g1) + _dot(tri, g2) + _dot(tri, g3)
    b_last = _dot(ones, g1) + _dot(ones, g2) + _dot(ones, g3)
    q_t = q_ref[...] * (GLA_DK ** -0.5) * jnp.exp(b)
    k = k_ref[...]
    k_t = k * jnp.exp(-b)
    k_dec_t = (k * jnp.exp(b_last - b)).T
    sel = jnp.where((lax.broadcasted_iota(jnp.int32, (tile, LANES), 0) >> 4)
                    == lax.broadcasted_iota(jnp.int32, (tile, LANES), 1), 1.0, 0.0).astype(MXU_DTYPE)
    t1, t2, t3 = _split3(g.T)
    blk_decay = jnp.exp(_dot(t1, sel) + _dot(t2, sel) + _dot(t3, sel))
    v = v_ref[...]
    stacked = n_blk * GLA_DK
    dk_bits, blk_bits = GLA_DK.bit_length() - 1, GLA_BLOCK.bit_length() - 1
    upd_live = (lax.broadcasted_iota(jnp.int32, (stacked, tile), 0) >> dk_bits
                == lax.broadcasted_iota(jnp.int32, (stacked, tile), 1) >> blk_bits)
    qry_live = (lax.broadcasted_iota(jnp.int32, (tile, stacked), 0) >> blk_bits
                == lax.broadcasted_iota(jnp.int32, (tile, stacked), 1) >> dk_bits)
    low_half = lax.broadcasted_iota(jnp.int32, (tile, LANES), 1) < GLA_DK
    for h in range(GLA_HEADS):
        ks = slice(h * GLA_DK, (h + 1) * GLA_DK)
        vs = slice(h * GLA_DV, (h + 1) * GLA_DV)
        vh = v[:, vs]
        scores = jnp.where(causal, _dot_nt(q_t[:, ks], k_t[:, ks]), 0.0)
        o = _dot(scores, vh)
        upd = _dot(jnp.where(upd_live, jnp.concatenate([k_dec_t[ks, :]] * n_blk, axis=0), 0.0), vh)
        s = state[h]
        for j in range(n_blk):
            rows = slice(j * GLA_DK, (j + 1) * GLA_DK)
            before[rows, :] = s
            s = blk_decay[ks, j:j + 1] * s + upd[rows, :]
        state[h] = s
        pair = q_t[:, (h // 2) * LANES:(h // 2 + 1) * LANES]
        swapped = pltpu.roll(pair, GLA_DK, axis=1)
        both = jnp.where(low_half, pair, swapped) if h % 2 == 0 else jnp.where(low_half, swapped, pair)
        q_exp = jnp.where(qry_live, jnp.concatenate([both] * (stacked // LANES), axis=1), 0.0)
        o_ref[:, vs] = o + _dot(q_exp, before[...])

    @pl.when(grp.last(i))
    def _():
        sout_ref[0] = state[...]


def _gla(proj, s0, grp):
    st_shape = (grp.n_seq, GLA_HEADS, GLA_DK, GLA_DV)
    st_spec = pl.BlockSpec((1, GLA_HEADS, GLA_DK, GLA_DV), lambda i: (grp.seq(i), 0, 0, 0))
    kw = GLA_KEY_WIDTH
    cols = lambda width, c: pl.BlockSpec((grp.tile, width), lambda i: (grp.block(i), c))
    return pl.pallas_call(
        functools.partial(_gla_kernel, grp=grp),
        out_shape=(jax.ShapeDtypeStruct((grp.n_tiles * grp.tile, GLA_WIDTH), F32), jax.ShapeDtypeStruct(st_shape, F32)),
        grid=(grp.n_tiles,),
        in_specs=[cols(kw, POOL_WIDTH // kw), cols(kw, POOL_WIDTH // kw + 1),
                  cols(GLA_WIDTH, (POOL_WIDTH + 2 * kw) // GLA_WIDTH), cols(kw, IN_EVEN_MAIN // kw), st_spec],
        out_specs=(pl.BlockSpec((grp.tile, GLA_WIDTH), lambda i: (i, 0)), st_spec),
        scratch_shapes=[pltpu.VMEM((GLA_HEADS, GLA_DK, GLA_DV), F32),
                        pltpu.VMEM((grp.tile // GLA_BLOCK * GLA_DK, GLA_DV), F32)],
        compiler_params=_params(), name="gla")(proj, proj, proj, proj, s0)


def _route(h1, nffn_ref, wr_hi_ref, wr_lo_ref, br_ref, xn_ref, info_ref, cnt_ref, carry):
    tm = h1.shape[0]

    @pl.when(pl.program_id(0) == 0)
    def _():
        carry[...] = jnp.zeros_like(carry)

    xn = _rms(h1, nffn_ref[...])
    x_hi = xn.astype(MXU_DTYPE)
    xn_ref[:, :ROW_TILES, :] = x_hi.reshape(tm, ROW_TILES, LANES)
    x_lo = (xn - x_hi.astype(F32)).astype(MXU_DTYPE)
    logits = (_dot(x_hi, wr_hi_ref[...]) + _dot(x_lo, wr_hi_ref[...]) + _dot(x_hi, wr_lo_ref[...])
              + br_ref[...])
    lane = lax.broadcasted_iota(jnp.int32, (tm, LANES), 1)
    lane_f = lane.astype(F32)
    vals, ids, hots = [], [], []
    for _ in range(TOP_K):
        m = jnp.max(logits, axis=-1, keepdims=True)
        ix = jnp.min(jnp.where(logits == m, lane_f, float(LANES)), axis=-1, keepdims=True)
        hot = lane_f == ix
        vals.append(m)
        ids.append(ix)
        hots.append(hot)
        logits = jnp.where(hot, -jnp.inf, logits)
    es = [jnp.exp(v - vals[0]) for v in vals]
    den = es[0] + es[1] + es[2] + es[3]
    chosen = jnp.zeros((tm, LANES), F32)
    for hot in hots:
        chosen = chosen + jnp.where(hot, 1.0, 0.0)
    before = (lax.broadcasted_iota(jnp.int32, (tm, tm), 1) < lax.broadcasted_iota(jnp.int32, (tm, tm), 0))
    rank = _dot(jnp.where(before, 1.0, 0.0), chosen) + carry[...]
    info = jnp.zeros((tm, LANES), F32)
    for k in range(TOP_K):
        pos = jnp.sum(jnp.where(hots[k], rank, 0.0), axis=-1, keepdims=True)
        info = jnp.where(lane == k, es[k] / den, info)
        info = jnp.where(lane == TOP_K + k, ids[k].astype(F32), info)
        info = jnp.where(lane == 2 * TOP_K + k, pos, info)
    info_ref[...] = info
    token = pl.program_id(0) * tm + lax.broadcasted_iota(jnp.int32, (tm, 1), 0)
    tag = jnp.zeros((tm, LANES), F32)
    for d in range(TAG_DIGITS):
        tag = jnp.where(lane == d, ((token >> (8 * d)) & 255).astype(F32), tag)
    for k in range(TOP_K):
        tag = jnp.where(lane == TAG_DIGITS + k, ids[k].astype(F32), tag)
    tag_row = jnp.concatenate([tag, jnp.zeros((tm, D_MODEL - LANES), F32)], axis=1).astype(xn_ref.dtype)
    xn_ref[:, ROW_TILES:, :] = tag_row.reshape(tm, ROW_TILES, LANES)
    carry[...] = carry[...] + jnp.sum(chosen, axis=0, keepdims=True)
    cnt_ref[...] = carry[...]


_ROUTE_OUT_SHAPES = lambda n: (jax.ShapeDtypeStruct((n, D_MODEL), F32), jax.ShapeDtypeStruct((n, TAGGED_ROWS, LANES), MXU_DTYPE),
                               jax.ShapeDtypeStruct((n, LANES), F32), jax.ShapeDtypeStruct((1, LANES), F32))
_ROUTE_OUT_SPECS = (pl.BlockSpec((TOKEN_TILE, D_MODEL), lambda i: (i, 0)),
                    pl.BlockSpec((TOKEN_TILE, TAGGED_ROWS, LANES), lambda i: (i, 0, 0)),
                    pl.BlockSpec((TOKEN_TILE, LANES), lambda i: (i, 0)),
                    pl.BlockSpec((1, LANES), lambda i: (0, 0)))


def _route_in_specs():
    return [_full((1, D_MODEL)), _full((D_MODEL, LANES)), _full((D_MODEL, LANES)), _full((1, LANES))]


def _even_out_kernel(ypp_ref, yps_ref, op_ref, os_ref, r_ref, hp_ref, hs_ref, gn_ref, w_ref, nffn_ref, wr_hi_ref,
                     wr_lo_ref, br_ref, h1_ref, xn_ref, info_ref, cnt_ref, carry, *, npt):
    o = _split_rows(op_ref, os_ref, npt)
    r = r_ref[...]
    parts = []
    for hd in range(GLA_HEADS):
        sl = slice(hd * GLA_DV, (hd + 1) * GLA_DV)
        oh = o[:, sl]
        oh = oh * lax.rsqrt(jnp.mean(oh * oh, axis=-1, keepdims=True) + RMS_EPS) * gn_ref[...]
        rh = r[:, sl]
        parts.append(oh * (rh * jax.nn.sigmoid(rh)))
    gla = jnp.concatenate(parts, axis=1)
    mix = _dot(_split_rows(ypp_ref, yps_ref, npt), w_ref[:POOL_WIDTH, :]) + _dot(gla, w_ref[POOL_WIDTH:, :])
    h1 = _split_rows(hp_ref, hs_ref, npt) + mix
    h1_ref[...] = h1
    _route(h1, nffn_ref, wr_hi_ref, wr_lo_ref, br_ref, xn_ref, info_ref, cnt_ref, carry)


def _even_out(y_pool_parts, o_gla_parts, proj, h_parts, n_p, gla_norm, w_out, nffn, wr_hi, wr_lo, br):
    n = proj.shape[0]
    return pl.pallas_call(
        functools.partial(_even_out_kernel, npt=n_p // TOKEN_TILE),
        out_shape=_ROUTE_OUT_SHAPES(n),
        grid=(n // TOKEN_TILE,),
        in_specs=_split_specs(TOKEN_TILE, POOL_WIDTH, n_p, False) + _split_specs(TOKEN_TILE, GLA_WIDTH, n_p, False)
        + [pl.BlockSpec((TOKEN_TILE, GLA_WIDTH), lambda i: (i, (IN_EVEN_MAIN - GLA_WIDTH) // GLA_WIDTH))]
        + _split_specs(TOKEN_TILE, D_MODEL, n_p, h_parts[0] is h_parts[1])
        + [_full((1, GLA_DV)), _full((D_MODEL, D_MODEL))] + _route_in_specs(),
        out_specs=_ROUTE_OUT_SPECS,
        scratch_shapes=[pltpu.VMEM((1, LANES), F32)],
        compiler_params=_params(), name="even_out")(
            *y_pool_parts, *o_gla_parts, proj, *h_parts, gla_norm, w_out, nffn, wr_hi, wr_lo, br)


def _rope_tile(x, cos, sin, lo_half):
    swapped = jnp.where(lo_half, pltpu.roll(x, LANES - HEAD_DIM // 2, axis=1), pltpu.roll(x, HEAD_DIM // 2, axis=1))
    return x * cos + swapped * sin


def _odd_in_kernel(h_ref, g_ref, w_ref, b_ref, cos_ref, sin_ref, q_ref, kv_ref):
    xn = _rms(h_ref[...], g_ref[...])
    qkv = _dot(xn, w_ref[...]) + b_ref[...]
    cos = cos_ref[...]
    sin = sin_ref[...]
    lo_half = (lax.broadcasted_iota(jnp.int32, cos.shape, 1) % HEAD_DIM) < HEAD_DIM // 2
    for j in range(Q_WIDTH // LANES):
        sl = slice(j * LANES, (j + 1) * LANES)
        q_ref[:, sl] = _rope_tile(qkv[:, sl], cos, sin, lo_half).astype(q_ref.dtype)
    kv_ref[:, :KV_WIDTH] = _rope_tile(qkv[:, Q_WIDTH:Q_WIDTH + KV_WIDTH], cos, sin, lo_half)
    kv_ref[:, KV_WIDTH:] = qkv[:, Q_WIDTH + KV_WIDTH:]


def _odd_in(h, g, w_qkv, b_qkv, cos, sin, table_block):
    n = h.shape[0]
    wq = w_qkv.shape[1]
    tab = pl.BlockSpec((TOKEN_TILE, LANES), lambda i: (table_block(i), 0))
    return pl.pallas_call(
        _odd_in_kernel,
        out_shape=(jax.ShapeDtypeStruct((n, Q_WIDTH), MXU_DTYPE), jax.ShapeDtypeStruct((n, 2 * KV_WIDTH), F32)),
        grid=(n // TOKEN_TILE,),
        in_specs=[pl.BlockSpec((TOKEN_TILE, D_MODEL), lambda i: (i, 0)), _full((1, D_MODEL)),
                  _full((D_MODEL, wq)), _full((1, wq)), tab, tab],
        out_specs=(pl.BlockSpec((TOKEN_TILE, Q_WIDTH), lambda i: (i, 0)),
                   pl.BlockSpec((TOKEN_TILE, 2 * KV_WIDTH), lambda i: (i, 0))),
        compiler_params=_params(), name="odd_in")(h, g, w_qkv, b_qkv, cos, sin)


def _attn_kernel(sink_ref, q_ref, kv0_ref, kv1_ref, kv2_ref, ck0_ref, ck1_ref, cv0_ref, cv1_ref, o_ref, *, sq):
    i = pl.program_id(0)
    smp = sq.is_sample(i)
    t = sq.tile_in_seq(i)
    kv0 = kv0_ref[...]
    k_old = jnp.where(smp, ck0_ref[...], kv2_ref[:, :KV_WIDTH])
    k_mid = jnp.where(smp, ck1_ref[...], kv1_ref[:, :KV_WIDTH])
    v_old = jnp.where(smp, cv0_ref[...], kv2_ref[:, KV_WIDTH:])
    v_mid = jnp.where(smp, cv1_ref[...], kv1_ref[:, KV_WIDTH:])
    pad = jnp.zeros((CHUNK, KV_WIDTH), F32)
    keys = jnp.concatenate([k_old, k_mid, kv0[:, :KV_WIDTH], pad], axis=0)
    vals = jnp.concatenate([v_old, v_mid, kv0[:, KV_WIDTH:], pad], axis=0)
    n_keys = 4 * CHUNK
    lane = lax.broadcasted_iota(jnp.int32, (n_keys, KV_WIDTH), 1)
    lo = lane < HEAD_DIM
    keys_sw = pltpu.roll(keys, HEAD_DIM, axis=1)
    vals_sw = pltpu.roll(vals, HEAD_DIM, axis=1)
    kcol = lax.broadcasted_iota(jnp.int32, (1, n_keys), 1)
    first_valid = jnp.where(smp, 0, (2 - jnp.minimum(t, 2)) * CHUNK)
    key_ok = jnp.logical_and(kcol >= first_valid, kcol < 3 * CHUNK)
    n_pairs = N_Q_HEADS // N_KV_HEADS // 2
    pair_of_row = lax.broadcasted_iota(jnp.int32, (n_pairs * CHUNK, 1), 0) // CHUNK
    for g in range(N_KV_HEADS):
        own, other = (keys, keys_sw) if g == 0 else (keys_sw, keys)
        vown, vother = (vals, vals_sw) if g == 0 else (vals_sw, vals)
        kb = jnp.concatenate([jnp.where(lo, own, 0.0), jnp.where(lo, 0.0, other)], axis=0).astype(MXU_DTYPE)
        vb = jnp.concatenate([jnp.where(lo, vown, 0.0), jnp.where(lo, 0.0, vother)], axis=0).astype(MXU_DTYPE)
        pairs = [slice((g * n_pairs + pr) * LANES, (g * n_pairs + pr + 1) * LANES) for pr in range(n_pairs)]
        qg = jnp.concatenate([q_ref[:, sl] for sl in pairs], axis=0)
        s = _dot_nt(qg, kb) * (HEAD_DIM ** -0.5)
        halves = []
        for half in range(2):
            sh = jnp.where(key_ok, s[:, half * n_keys:(half + 1) * n_keys], NEG_INF)
            sink = jnp.zeros((n_pairs * CHUNK, 1), F32)
            for pr in range(n_pairs):
                sink = jnp.where(pair_of_row == pr, sink_ref[2 * (g * n_pairs + pr) + half], sink)
            m = jnp.maximum(jnp.max(sh, axis=-1, keepdims=True), sink)
            p = jnp.exp(sh - m)
            halves.append(p * (1.0 / (jnp.sum(p, axis=-1, keepdims=True) + jnp.exp(sink - m))))
        o = _dot(jnp.concatenate(halves, axis=1), vb)
        for pr, sl in enumerate(pairs):
            o_ref[:, sl] = o[pr * CHUNK:(pr + 1) * CHUNK].astype(o_ref.dtype)


def _attention(sinks, q, kv, ck, cv, sq):
    n = q.shape[0]
    npt = sq.n_prompt_tiles
    prev = lambda d: (lambda i, s: (jnp.where(sq.is_sample(i), i, jnp.maximum(i - d, 0)), 0))
    cache = lambda d: (lambda i, s: (jnp.where(sq.is_sample(i), 2 * (i - npt) + d, 0), 0))
    kvspec = lambda f: pl.BlockSpec((CHUNK, 2 * KV_WIDTH), f)
    cspec = lambda f: pl.BlockSpec((CHUNK, KV_WIDTH), f)
    grid_spec = pltpu.PrefetchScalarGridSpec(
        num_scalar_prefetch=1, grid=(sq.n_tiles,),
        in_specs=[pl.BlockSpec((CHUNK, Q_WIDTH), lambda i, s: (i, 0)),
                  kvspec(prev(0)), kvspec(prev(1)), kvspec(prev(2)),
                  cspec(cache(0)), cspec(cache(1)), cspec(cache(0)), cspec(cache(1))],
        out_specs=pl.BlockSpec((CHUNK, Q_WIDTH), lambda i, s: (i, 0)))
    return pl.pallas_call(
        functools.partial(_attn_kernel, sq=sq),
        out_shape=jax.ShapeDtypeStruct((n, Q_WIDTH), MXU_DTYPE),
        grid_spec=grid_spec, compiler_params=_params(), name="attention")(sinks, q, kv, kv, kv, ck, ck, cv, cv)


def _odd_out_kernel(o_ref, h_ref, w_ref, b_ref, nffn_ref, wr_hi_ref, wr_lo_ref, br_ref,
                    h1_ref, xn_ref, info_ref, cnt_ref, carry):
    h1 = h_ref[...] + _dot(o_ref[...], w_ref[...]) + b_ref[...]
    h1_ref[...] = h1
    _route(h1, nffn_ref, wr_hi_ref, wr_lo_ref, br_ref, xn_ref, info_ref, cnt_ref, carry)


def _odd_out(o, h, w_out, b_out, nffn, wr_hi, wr_lo, br):
    n = h.shape[0]
    row = pl.BlockSpec((TOKEN_TILE, D_MODEL), lambda i: (i, 0))
    return pl.pallas_call(
        _odd_out_kernel,
        out_shape=_ROUTE_OUT_SHAPES(n),
        grid=(n // TOKEN_TILE,),
        in_specs=[row, row, _full((D_MODEL, D_MODEL)), _full((1, D_MODEL))] + _route_in_specs(),
        out_specs=_ROUTE_OUT_SPECS,
        scratch_shapes=[pltpu.VMEM((1, LANES), F32)],
        compiler_params=_params(), name="odd_out")(o, h, w_out, b_out, nffn, wr_hi, wr_lo, br)


def _token_copy(src, s, dst, d, sem, rows=ROW_TILES):
    return pltpu.make_async_copy(src.at[pl.ds(pl.multiple_of(s, rows), rows)],
                                 dst.at[pl.ds(pl.multiple_of(d, rows), rows)], sem)


ZERO_TOKENS = EXPERT_TILE // 2


def _zero_fill(fill_ref, zeros, xs_out, zsem):
    zeros[...] = jnp.zeros_like(zeros)

    def sweep(wait):
        def go(cp):
            cp.wait() if wait else cp.start()

        def tail(e, c):
            off, length = fill_ref[e], fill_ref[N_EXPERTS + e]
            for bit in range(EXPERT_TILE.bit_length() - 1):
                rows = (1 << bit) * TAGGED_ROWS

                @pl.when((length >> bit) & 1 == 1)
                def _():
                    o = pl.multiple_of(off + (length & ((1 << bit) - 1)) * TAGGED_ROWS, TAGGED_ROWS)
                    go(pltpu.make_async_copy(zeros.at[pl.ds(0, rows)], xs_out.at[pl.ds(o, rows)], zsem))
            return c

        lax.fori_loop(0, N_EXPERTS, tail, 0)

        def unused(t, c):
            o = pl.multiple_of(fill_ref[2 * N_EXPERTS] + t * ZERO_TOKENS * TAGGED_ROWS, TAGGED_ROWS)
            go(pltpu.make_async_copy(zeros, xs_out.at[pl.ds(o, ZERO_TOKENS * TAGGED_ROWS)], zsem))
            return c

        lax.fori_loop(0, fill_ref[2 * N_EXPERTS + 1], unused, 0)

    sweep(wait=False)
    sweep(wait=True)


def _dispatch_kernel(fill_ref, dest_hbm, x_ref, xs_out, dest_smem, zeros, sem, dsems, zsem):
    i = pl.program_id(0)
    n_slots = TOKEN_TILE * TOP_K

    @pl.when(i == 0)
    def _():
        _zero_fill(fill_ref, zeros, xs_out, zsem)

    slot = i % 2

    def dest_rows(step, s):
        return pltpu.make_async_copy(dest_hbm.at[pl.ds(step * n_slots, n_slots)], dest_smem.at[s], dsems.at[s])

    @pl.when(i == 0)
    def _():
        dest_rows(0, 0).start()

    dest_rows(i, slot).wait()

    @pl.when(i + 1 < pl.num_programs(0))
    def _():
        dest_rows(i + 1, 1 - slot).start()

    def issue(t, c):
        for k in range(TOP_K):
            _token_copy(x_ref, t * TAGGED_ROWS, xs_out, dest_smem[slot, t * TOP_K + k], sem, TAGGED_ROWS).start()
        return c

    lax.fori_loop(0, TOKEN_TILE, issue, 0, unroll=2)
    whole = xs_out.at[pl.ds(0, n_slots * TAGGED_ROWS)]
    pltpu.make_async_copy(whole, whole, sem).wait()


def _dispatch(fill, dest, xt, n_rows):
    n = xt.shape[0] // TAGGED_ROWS
    grid_spec = pltpu.PrefetchScalarGridSpec(
        num_scalar_prefetch=1, grid=(n // TOKEN_TILE,),
        in_specs=[pl.BlockSpec(memory_space=pl.ANY),
                  pl.BlockSpec((TOKEN_TILE * TAGGED_ROWS, LANES), lambda i, f: (i, 0))],
        out_specs=pl.BlockSpec(memory_space=pl.ANY),
        scratch_shapes=[pltpu.SMEM((2, TOKEN_TILE * TOP_K), jnp.int32), pltpu.VMEM((ZERO_TOKENS * TAGGED_ROWS, LANES), xt.dtype),
                        pltpu.SemaphoreType.DMA, pltpu.SemaphoreType.DMA((2,)), pltpu.SemaphoreType.DMA])
    return pl.pallas_call(
        _dispatch_kernel, out_shape=jax.ShapeDtypeStruct((n_rows * TAGGED_ROWS, LANES), xt.dtype),
        grid_spec=grid_spec, compiler_params=_params(), name="moe_dispatch")(fill, dest, xt)


def _experts_kernel(te_ref, tr_ref, x_ref, wgu_ref, bgu_ref, wd_ref, bd_ref, yt_ref,
                    wgu_mxu, wd_mxu, ybuf, place_vmem, place_smem, ysems, psem, *, n_tokens):
    i = pl.program_id(0)
    slot = i % 2
    prev = 1 - slot
    tile_rows = EXPERT_TILE * ROW_TILES
    spare = yt_ref.shape[0] - 2 * tile_rows
    col = lax.broadcasted_iota(jnp.int32, (1, EXPERT_TILE), 1)

    def rows_done(s):
        whole = yt_ref.at[pl.ds(0, tile_rows)]
        pltpu.make_async_copy(whole, whole, ysems.at[s]).wait()

    def places_to_smem(first_rows, s):
        place_vmem[...] = jnp.broadcast_to(first_rows, place_vmem.shape)
        return pltpu.make_async_copy(place_vmem, place_smem.at[s], psem)

    def start_row_copy(j, s):
        _token_copy(ybuf.at[s], j * ROW_TILES, yt_ref, place_smem[s, 0, j], ysems.at[s]).start()

    @pl.when(i == 0)
    def _():
        ybuf[...] = jnp.zeros(ybuf.shape, F32)
        for s in range(2):
            cp = pltpu.make_async_copy(ybuf.at[0], yt_ref.at[pl.ds(spare + s * tile_rows, tile_rows)], ysems.at[0])
            cp.start()
            cp.wait()
        cp = places_to_smem(spare + tile_rows + col * ROW_TILES, 1)
        cp.start()
        cp.wait()

    @pl.when(jnp.logical_or(i == 0, te_ref[i] != te_ref[jnp.maximum(i - 1, 0)]))
    def _():
        wgu_mxu[...] = wgu_ref[0, 0].astype(MXU_DTYPE)
        wd_mxu[...] = wd_ref[0, 0].astype(MXU_DTYPE)

    n_valid = tr_ref[i]
    after_last = jnp.logical_and(n_valid == 0, jnp.logical_and(i >= 1, tr_ref[jnp.maximum(i - 1, 0)] > 0))

    @pl.when(jnp.logical_and(i >= 1, jnp.logical_or(n_valid > 0, after_last)))
    def _():
        rows_done(slot)

    @pl.when(n_valid > 0)
    def _():
        for j in range(EXPERT_TILE):
            start_row_copy(j, prev)
        tag = jnp.sum(x_ref[:, ROW_TILES:, :].astype(F32), axis=1)
        lane = lax.broadcasted_iota(jnp.int32, (EXPERT_TILE, LANES), 1)
        is_id = jnp.logical_and(lane >= TAG_DIGITS, lane < TAG_DIGITS + TOP_K)
        mine = jnp.logical_and(is_id, tag == te_ref[i].astype(F32))
        weight = jnp.where(lane == 0, 1.0, jnp.where(lane == 1, 256.0, 65536.0))
        terms = jnp.where(lane < TAG_DIGITS, tag * weight, jnp.where(mine, ((lane - TAG_DIGITS) * n_tokens).astype(F32), 0.0))
        place = jnp.sum(terms, axis=-1, keepdims=True)
        hi = jnp.floor(place * (1.0 / 65536.0))
        mid = jnp.floor((place - hi * 65536.0) * (1.0 / 256.0))
        digits = jnp.where(lane == 0, place - hi * 65536.0 - mid * 256.0, jnp.where(lane == 1, mid, jnp.where(lane == 2, hi, 0.0)))
        pick = jnp.where(lax.broadcasted_iota(jnp.int32, (ROW_TILES, LANES), 0)
                         == lax.broadcasted_iota(jnp.int32, (ROW_TILES, LANES), 1), 1.0, 0.0)
        planes = _dot_nt(pick, digits)
        place_row = (planes[0:1] + 256.0 * planes[1:2] + 65536.0 * planes[2:3]).astype(jnp.int32) * ROW_TILES
        to_smem = places_to_smem(jnp.where(col < n_valid, place_row, spare + slot * tile_rows + col * ROW_TILES), slot)
        to_smem.start()

        x = x_ref[:, :ROW_TILES, :].reshape(EXPERT_TILE, D_MODEL)
        hgu = _dot(x, wgu_mxu[...]) + bgu_ref[0, 0]
        gate = jnp.minimum(hgu[:, :D_MODEL], SWIGLU_LIMIT)
        up = jnp.clip(hgu[:, D_MODEL:], -SWIGLU_LIMIT, SWIGLU_LIMIT)
        act = (up + 1.0) * gate * jax.nn.sigmoid(SWIGLU_ALPHA * gate)
        _store_token_tiles(ybuf.at[slot], _dot(act, wd_mxu[...]) + bd_ref[0, 0])
        to_smem.wait()

    @pl.when(after_last)
    def _():
        def issue(j, c):
            start_row_copy(j, prev)
            return c

        lax.fori_loop(0, EXPERT_TILE, issue, 0, unroll=8)
        rows_done(prev)


def _experts(layer, tile_expert, tile_rows, xs, wgu, bgu, wd, bd, n_tokens):
    rows = xs.shape[0]
    w = lambda shape: pl.BlockSpec((1, 1) + shape, lambda i, te, tr: (layer, te[i], 0, 0))
    grid_spec = pltpu.PrefetchScalarGridSpec(
        num_scalar_prefetch=2, grid=(rows // EXPERT_TILE,),
        in_specs=[pl.BlockSpec((EXPERT_TILE, TAGGED_ROWS, LANES), lambda i, te, tr: (i, 0, 0)),
                  w((D_MODEL, 2 * D_MODEL)), w((1, 2 * D_MODEL)), w((D_MODEL, D_MODEL)), w((1, D_MODEL))],
        out_specs=pl.BlockSpec(memory_space=pl.ANY),
        scratch_shapes=[pltpu.VMEM((D_MODEL, 2 * D_MODEL), MXU_DTYPE), pltpu.VMEM((D_MODEL, D_MODEL), MXU_DTYPE),
                        pltpu.VMEM((2, EXPERT_TILE * ROW_TILES, LANES), F32),
                        pltpu.VMEM((ROW_TILES, EXPERT_TILE), jnp.int32), pltpu.SMEM((2, ROW_TILES, EXPERT_TILE), jnp.int32),
                        pltpu.SemaphoreType.DMA((2,)), pltpu.SemaphoreType.DMA])
    yt_rows = (n_tokens * TOP_K + 2 * EXPERT_TILE) * ROW_TILES
    return pl.pallas_call(
        functools.partial(_experts_kernel, n_tokens=n_tokens), out_shape=jax.ShapeDtypeStruct((yt_rows, LANES), F32),
        grid_spec=grid_spec, compiler_params=_params(), name="moe_experts")(
            tile_expert, tile_rows, xs, wgu, bgu, wd, bd)


def _combine_kernel(*refs, final, npt):
    y_refs, outs = refs[:TOP_K], refs[TOP_K + 7:]
    info_ref, h_ref, p_prompt_ref, p_sample_ref, pp_ref, pg_ref, nf_ref = refs[TOP_K:TOP_K + 7]
    i = pl.program_id(0)
    gates = info_ref[...]
    moe = None
    for k in range(TOP_K):
        rows = _load_token_tiles(y_refs[k], COMBINE_TILE)
        moe = gates[:, k:k + 1] * rows if moe is None else moe + gates[:, k:k + 1] * rows
    h2 = h_ref[...] + moe
    p = jnp.where(i < npt, p_prompt_ref[0], p_sample_ref[0])
    h3 = h2 + jax.nn.sigmoid(_dot(h2, pg_ref[...])) * _dot(p, pp_ref[...])
    if not final:
        outs[0][...] = h3
        return
    y = _rms(h3, nf_ref[...])

    @pl.when(i < npt)
    def _():
        outs[0][...] = y

    @pl.when(i >= npt)
    def _():
        outs[1][...] = y


def _combine(layer, yt, info, h1, p_prompt, p_sample, ple_proj, ple_gate, norm_final, final):
    n = h1.shape[0]
    n_p = p_prompt.shape[1]
    npt = n_p // COMBINE_TILE
    steps = n // COMBINE_TILE
    row = lambda w: pl.BlockSpec((COMBINE_TILE, w), lambda i: (i, 0))
    rank_rows = lambda k: pl.BlockSpec((COMBINE_TILE * ROW_TILES, LANES), lambda i: (k * steps + i, 0))
    prompt_rows = lambda i: jnp.minimum(i, npt - 1)
    sample_rows = lambda i: jnp.maximum(i - npt, 0)
    if final:
        out_shape = (jax.ShapeDtypeStruct((n_p, D_MODEL), F32), jax.ShapeDtypeStruct((n - n_p, D_MODEL), F32))
        out_specs = (pl.BlockSpec((COMBINE_TILE, D_MODEL), lambda i: (prompt_rows(i), 0)),
                     pl.BlockSpec((COMBINE_TILE, D_MODEL), lambda i: (sample_rows(i), 0)))
    else:
        out_shape, out_specs = jax.ShapeDtypeStruct((n, D_MODEL), F32), row(D_MODEL)
    return pl.pallas_call(
        functools.partial(_combine_kernel, final=final, npt=npt),
        out_shape=out_shape,
        grid=(steps,),
        in_specs=[rank_rows(k) for k in range(TOP_K)] + [
            row(LANES), row(D_MODEL),
            pl.BlockSpec((1, COMBINE_TILE, PLE_DIM), lambda i: (layer, prompt_rows(i), 0)),
            pl.BlockSpec((1, COMBINE_TILE, PLE_DIM), lambda i: (layer, sample_rows(i), 0)),
            _full((PLE_DIM, D_MODEL)), _full((D_MODEL, D_MODEL)), _full((1, D_MODEL))],
        out_specs=out_specs,
        compiler_params=_params(), name="moe_combine")(
            *([yt] * TOP_K), info, h1, p_prompt, p_sample, ple_proj, ple_gate, norm_final)


def _moe_and_embed(layer, h1, xt, info, counts, p_prompt, p_sample, wgu, bgu, wd, bd, ple_proj, ple_gate,
                   norm_final, final):
    n = h1.shape[0]
    n_tiles = (n * TOP_K + N_EXPERTS * (EXPERT_TILE - 1)) // EXPERT_TILE + 1
    ids = info[:, TOP_K:2 * TOP_K].astype(jnp.int32)
    rank = info[:, 2 * TOP_K:3 * TOP_K].astype(jnp.int32)
    cnt = counts[0, :N_EXPERTS].astype(jnp.int32)
    padded = ((cnt + EXPERT_TILE - 1) // EXPERT_TILE) * EXPERT_TILE
    ends = jnp.cumsum(padded)
    starts = ends - padded
    dest = ((starts[ids] + rank) * TAGGED_ROWS).reshape(-1)
    tile_start = jnp.arange(n_tiles, dtype=jnp.int32) * EXPERT_TILE
    tile_expert = jnp.minimum(jnp.sum((tile_start[:, None] >= ends[None, :]).astype(jnp.int32), axis=1), N_EXPERTS - 1)
    tile_rows = jnp.clip(cnt[tile_expert] - (tile_start - starts[tile_expert]), 0, EXPERT_TILE)
    tile_rows = jnp.where(tile_start < ends[-1], tile_rows, 0).astype(jnp.int32)
    fill = jnp.concatenate([(starts + cnt) * TAGGED_ROWS, padded - cnt,
                            jnp.stack([ends[-1] * TAGGED_ROWS,
                                       (n_tiles * EXPERT_TILE - ends[-1]) // ZERO_TOKENS])]).astype(jnp.int32)
    xs = _dispatch(fill, dest, xt.reshape(n * TAGGED_ROWS, LANES), n_tiles * EXPERT_TILE)
    yt = _experts(layer, tile_expert, tile_rows, xs.reshape(-1, TAGGED_ROWS, LANES), wgu, bgu, wd, bd, n)
    return _combine(layer, yt, info, h1, p_prompt, p_sample, ple_proj, ple_gate, norm_final, final)


def _rope_tables(prompt_len, n_sample_seq, sample_len):
    half = HEAD_DIM // 2
    inv = jnp.power(jnp.float32(ROPE_THETA), -jnp.arange(half, dtype=F32) / half)
    pos = jnp.concatenate([jnp.arange(prompt_len), jnp.tile(PAST_LEN + jnp.arange(sample_len), n_sample_seq)])
    ang = pos.astype(F32)[:, None] * inv[None, :]
    cos = jnp.tile(jnp.cos(ang), (1, LANES // half))
    sin = jnp.tile(jnp.concatenate([-jnp.sin(ang), jnp.sin(ang)], axis=1), (1, LANES // HEAD_DIM))
    return cos, sin


def kernel(x_prompt, x_sample, state_pool, state_gla, cache_k, cache_v, p_prompt, p_sample, norm_mix, norm_ffn, norm_final, w_in_even, pool_w, pool_scale, gla_w_gate, gla_b_gate, gla_norm, w_out_even, w_qkv_odd, b_qkv_odd, attn_sinks, w_out_odd, b_out_odd, w_router, b_router, w_gate_up, b_gate_up, w_down, b_down, ple_proj, ple_gate):
    bsz, t_len, _ = x_prompt.shape
    dec_bsz, dec_len, _ = x_sample.shape
    depth = norm_mix.shape[0]
    n_p, n_s = bsz * t_len, dec_bsz * dec_len
    n = n_p + n_s
    assert dec_len == CHUNK and n_s == TOKEN_TILE and t_len % TOKEN_TILE == 0 and cache_k.shape[2] == WINDOW
    sq = _Seq(bsz, t_len, dec_bsz)
    grp_p = _Group(bsz, t_len, min(SEQ_TILE, t_len), 0, 0)
    grp_s = _Group(dec_bsz, dec_len, dec_len, n_p, PAST_LEN)
    bf = lambda a: a.astype(MXU_DTYPE)
    row = lambda a: a.reshape(1, -1)

    h_parts = (x_prompt.reshape(n_p, D_MODEL), x_sample.reshape(n_s, D_MODEL))
    p_parts = (p_prompt.reshape(depth, n_p, PLE_DIM), p_sample.reshape(depth, n_s, PLE_DIM))
    b_gu = b_gate_up.reshape(depth, N_EXPERTS, 1, 2 * D_MODEL)
    b_dn = b_down.reshape(depth, N_EXPERTS, 1, D_MODEL)
    cos, sin = _rope_tables(t_len, dec_bsz, dec_len)
    tiles_per_seq = t_len // TOKEN_TILE
    table_block = lambda i: jnp.where(i < bsz * tiles_per_seq, i % tiles_per_seq, tiles_per_seq)

    pools, glas, new_k, new_v = [], [], [], []
    for i in range(depth):
        if i % 2 == 0:
            e = i // 2
            w_in = jnp.pad(bf(w_in_even[e]), ((0, 0), (0, (-IN_EVEN) % LANES)))
            wg = jnp.pad(bf(gla_w_gate[e]), ((0, w_in.shape[1] - IN_EVEN_MAIN - GLA_RANK), (0, 0)))
            proj = _even_in(h_parts, n, n_p, row(norm_mix[i]), w_in, wg, row(gla_b_gate[e]))
            pw, ps = bf(pool_w[e]), row(pool_scale[e])
            y_pool_p, pool_p = _pool(proj, jnp.zeros((bsz, 16, POOL_WIDTH), F32), pw, ps, grp_p)
            y_pool_s, pool_s = _pool(proj, jnp.pad(state_pool[e], ((0, 0), (1, 0), (0, 0))), pw, ps, grp_s)
            o_gla_p, gla_p = _gla(proj, jnp.zeros((bsz,) + state_gla.shape[2:], F32), grp_p)
            o_gla_s, gla_s = _gla(proj, state_gla[e], grp_s)
            pools.append((pool_p[:, 1:], pool_s[:, 1:]))
            glas.append((gla_p, gla_s))
            wr = jnp.pad(w_router[i], ((0, 0), (0, LANES - N_EXPERTS)))
            wr_hi = bf(wr)
            wr_lo = bf(wr - wr_hi.astype(F32))
            br = jnp.pad(row(b_router[i]), ((0, 0), (0, LANES - N_EXPERTS)), constant_values=NEG_INF)
            h1, xn, info, counts = _even_out((y_pool_p, y_pool_s), (o_gla_p, o_gla_s), proj, h_parts, n_p,
                                             row(gla_norm[e]), bf(w_out_even[e]), row(norm_ffn[i]), wr_hi, wr_lo, br)
        else:
            o = i // 2
            h = h_parts[0]
            q, kv = _odd_in(h, row(norm_mix[i]), bf(w_qkv_odd[o]), row(b_qkv_odd[o]), cos, sin, table_block)
            ck = cache_k[o].reshape(dec_bsz * WINDOW, KV_WIDTH)
            cv = cache_v[o].reshape(dec_bsz * WINDOW, KV_WIDTH)
            att = _attention(attn_sinks[o], q, kv, ck, cv, sq)
            kv_p = kv[:n_p].reshape(bsz, t_len, 2 * KV_WIDTH)[:, -WINDOW:]
            kv_s = kv[n_p:].reshape(dec_bsz, dec_len, 2 * KV_WIDTH)
            hd = (N_KV_HEADS, HEAD_DIM)
            new_k.append((kv_p[..., :KV_WIDTH].reshape(bsz, WINDOW, *hd),
                          jnp.concatenate([cache_k[o], kv_s[..., :KV_WIDTH].reshape(dec_bsz, dec_len, *hd)], axis=1)[:, -WINDOW:]))
            new_v.append((kv_p[..., KV_WIDTH:].reshape(bsz, WINDOW, *hd),
                          jnp.concatenate([cache_v[o], kv_s[..., KV_WIDTH:].reshape(dec_bsz, dec_len, *hd)], axis=1)[:, -WINDOW:]))
            wr = jnp.pad(w_router[i], ((0, 0), (0, LANES - N_EXPERTS)))
            wr_hi = bf(wr)
            wr_lo = bf(wr - wr_hi.astype(F32))
            br = jnp.pad(row(b_router[i]), ((0, 0), (0, LANES - N_EXPERTS)), constant_values=NEG_INF)
            h1, xn, info, counts = _odd_out(att, h, bf(w_out_odd[o]), row(b_out_odd[o]),
                                            row(norm_ffn[i]), wr_hi, wr_lo, br)
        final = i == depth - 1
        out = _moe_and_embed(i, h1, xn, info, counts, *p_parts, w_gate_up, b_gu, w_down, b_dn,
                             bf(ple_proj[i]), bf(ple_gate[i]), row(norm_final), final)
        h_parts = out if final else (out, out)

    y_prompt = h_parts[0].reshape(bsz, t_len, D_MODEL)
    y_sample = h_parts[1].reshape(dec_bsz, dec_len, D_MODEL)
    part = lambda pairs, j: jnp.stack([p[j] for p in pairs])
    return (y_prompt, y_sample, part(pools, 0), part(glas, 0), part(new_k, 0), part(new_v, 0),
            part(pools, 1), part(glas, 1), part(new_k, 1), part(new_v, 1))
```

```python
import functools

import jax
import jax.numpy as jnp
from jax import lax
from jax.experimental import pallas as pl
from jax.experimental.pallas import tpu as pltpu

F32 = jnp.float32
MXU_DTYPE = jnp.bfloat16

V7X_VMEM_BYTES = 64 * 1024 * 1024
VMEM_LIMIT = (V7X_VMEM_BYTES * 7) // 8
LANES = 128

D_MODEL = 1024
CHUNK = 64
PAST_LEN = 2048
PLE_DIM = 256
RMS_EPS = 1e-6
POOL_WINDOWS = (2, 4, 8, 16)
POOL_WIDTH = 512
POOL_GROUP_DIM = 128
POOL_STATE = 15
GLA_HEADS = 4
GLA_DK = 64
GLA_DV = 128
GLA_KEY_WIDTH = GLA_HEADS * GLA_DK
GLA_WIDTH = GLA_HEADS * GLA_DV
GLA_RANK = 16
GLA_TAU = 16.0
GLA_BLOCK = 16
IN_EVEN = POOL_WIDTH + 2 * GLA_KEY_WIDTH + 2 * GLA_WIDTH + GLA_RANK
IN_EVEN_MAIN = IN_EVEN - GLA_RANK
PROJ_WIDTH = IN_EVEN_MAIN + GLA_KEY_WIDTH
N_Q_HEADS = 16
N_KV_HEADS = 2
HEAD_DIM = 64
WINDOW = 128
ROPE_THETA = 10000.0
Q_WIDTH = N_Q_HEADS * HEAD_DIM
KV_WIDTH = N_KV_HEADS * HEAD_DIM
N_EXPERTS = 32
TOP_K = 4
SWIGLU_LIMIT = 7.0
SWIGLU_ALPHA = 1.702
NEG_INF = -1e30

TOKEN_TILE = 512
EXPERT_TILE = 512
COMBINE_TILE = 256


def _dot(a, b):
    return jnp.dot(a.astype(MXU_DTYPE), b.astype(MXU_DTYPE), preferred_element_type=F32)


def _dot_nt(a, b):
    return lax.dot_general(a.astype(MXU_DTYPE), b.astype(MXU_DTYPE), (((1,), (1,)), ((), ())),
                           preferred_element_type=F32)


def _split3(x):
    x1 = x.astype(MXU_DTYPE)
    r1 = x - x1.astype(F32)
    x2 = r1.astype(MXU_DTYPE)
    x3 = (r1 - x2.astype(F32)).astype(MXU_DTYPE)
    return x1, x2, x3


def _rms(x, g):
    return x * lax.rsqrt(jnp.mean(x * x, axis=-1, keepdims=True) + RMS_EPS) * g


ROW_TILES = D_MODEL // LANES
TAGGED_ROWS = 2 * ROW_TILES
TAG_DIGITS = 3


def _store_token_tiles(ref, x, pitch=ROW_TILES):
    for s in range(ROW_TILES):
        ref[pl.ds(s, x.shape[0], stride=pitch), :] = x[:, s * LANES:(s + 1) * LANES]


def _load_token_tiles(ref, rows, pitch=ROW_TILES):
    return jnp.concatenate([ref[pl.ds(s, rows, stride=pitch), :] for s in range(ROW_TILES)], axis=1)


def _params(n_axes=1):
    return pltpu.CompilerParams(dimension_semantics=("arbitrary",) * n_axes, vmem_limit_bytes=VMEM_LIMIT)


def _full(shape):
    return pl.BlockSpec(shape, lambda *_: (0,) * len(shape))


class _Seq:
    def __init__(self, n_prompt_seq, prompt_len, n_sample_seq):
        self.tiles_per_seq = prompt_len // CHUNK
        self.n_prompt_seq = n_prompt_seq
        self.n_prompt_tiles = n_prompt_seq * self.tiles_per_seq
        self.n_tiles = self.n_prompt_tiles + n_sample_seq
        self.n_seq = n_prompt_seq + n_sample_seq

    def is_sample(self, i):
        return i >= self.n_prompt_tiles

    def tile_in_seq(self, i):
        return jnp.where(self.is_sample(i), 0, i % self.tiles_per_seq)

    def seq(self, i):
        return jnp.where(self.is_sample(i), self.n_prompt_seq + i - self.n_prompt_tiles, i // self.tiles_per_seq)

    def last(self, i):
        return jnp.logical_or(self.is_sample(i), i % self.tiles_per_seq == self.tiles_per_seq - 1)

    def pos0(self, i):
        return jnp.where(self.is_sample(i), PAST_LEN, self.tile_in_seq(i) * CHUNK)


class _Group:
    def __init__(self, n_seq, seq_len, tile, row0, pos_base):
        assert seq_len % tile == 0 and row0 % tile == 0
        self.n_seq, self.tile, self.pos_base = n_seq, tile, pos_base
        self.tiles_per_seq = seq_len // tile
        self.n_tiles = n_seq * self.tiles_per_seq
        self.block0 = row0 // tile

    def block(self, i):
        return self.block0 + i

    def seq(self, i):
        return i // self.tiles_per_seq

    def tile_in_seq(self, i):
        return i % self.tiles_per_seq

    def last(self, i):
        return i % self.tiles_per_seq == self.tiles_per_seq - 1

    def pos0(self, i):
        return self.pos_base + self.tile_in_seq(i) * self.tile


SEQ_TILE = 256


def _split_specs(tile, width, n_prompt, same_array):
    npt = n_prompt // tile
    off = npt if same_array else 0
    return [pl.BlockSpec((tile, width), lambda i: (jnp.minimum(i, npt - 1), 0)),
            pl.BlockSpec((tile, width), lambda i: (jnp.maximum(i - npt, 0) + off, 0))]


def _split_rows(prompt_ref, sample_ref, n_prompt_tiles):
    return jnp.where(pl.program_id(0) < n_prompt_tiles, prompt_ref[...], sample_ref[...])


def _even_in_kernel(hp_ref, hs_ref, g_ref, w_ref, wg_ref, bg_ref, out_ref, *, npt):
    xn = _rms(_split_rows(hp_ref, hs_ref, npt), g_ref[...])
    proj = _dot(xn, w_ref[...])
    z = proj[:, IN_EVEN_MAIN:]
    a = _dot(z, wg_ref[...]) + bg_ref[...]
    log_alpha = (jnp.minimum(a, 0.0) - jnp.log1p(jnp.exp(-jnp.abs(a)))) * (1.0 / GLA_TAU)
    out_ref[:, :IN_EVEN_MAIN] = proj[:, :IN_EVEN_MAIN]
    out_ref[:, IN_EVEN_MAIN:] = log_alpha


def _even_in(h_parts, n, n_p, g, w_in, wg, bg):
    wp = w_in.shape[1]
    return pl.pallas_call(
        functools.partial(_even_in_kernel, npt=n_p // TOKEN_TILE),
        out_shape=jax.ShapeDtypeStruct((n, PROJ_WIDTH), F32),
        grid=(n // TOKEN_TILE,),
        in_specs=_split_specs(TOKEN_TILE, D_MODEL, n_p, h_parts[0] is h_parts[1]) + [
            _full((1, D_MODEL)), _full((D_MODEL, wp)), _full((wp - IN_EVEN_MAIN, GLA_KEY_WIDTH)),
            _full((1, GLA_KEY_WIDTH))],
        out_specs=pl.BlockSpec((TOKEN_TILE, PROJ_WIDTH), lambda i: (i, 0)),
        compiler_params=_params(), name="even_in")(*h_parts, g, w_in, wg, bg)


def _pool_kernel(u_ref, init_ref, pw_ref, ps_ref, y_ref, st_ref, buf, *, grp):
    i = pl.program_id(0)
    rows = grp.tile

    @pl.when(grp.tile_in_seq(i) == 0)
    def _():
        buf[0:16, :] = init_ref[0]

    u = u_ref[...]
    buf[16:16 + rows, :] = u
    pos = grp.pos0(i) + lax.broadcasted_iota(jnp.int32, (rows, 1), 0)
    for g, w in enumerate(POOL_WINDOWS):
        sl = slice(g * POOL_GROUP_DIM, (g + 1) * POOL_GROUP_DIM)
        acc = u[:, sl]
        for j in range(1, w):
            acc = acc + buf[16 - j:16 - j + rows, sl]
        cnt = jnp.minimum(w, pos + 1).astype(F32)
        d = acc / cnt - u[:, sl]
        y_ref[:, sl] = _dot(d, pw_ref[g]) * ps_ref[:, sl]
    tail = buf[rows:rows + 16, :]
    st_ref[0] = tail
    buf[0:16, :] = tail


def _pool(proj, init, pool_w, pool_scale, grp):
    return pl.pallas_call(
        functools.partial(_pool_kernel, grp=grp),
        out_shape=(jax.ShapeDtypeStruct((grp.n_tiles * grp.tile, POOL_WIDTH), F32),
                   jax.ShapeDtypeStruct((grp.n_seq, 16, POOL_WIDTH), F32)),
        grid=(grp.n_tiles,),
        in_specs=[pl.BlockSpec((grp.tile, POOL_WIDTH), lambda i: (grp.block(i), 0)),
                  pl.BlockSpec((1, 16, POOL_WIDTH), lambda i: (grp.seq(i), 0, 0)),
                  _full((len(POOL_WINDOWS), POOL_GROUP_DIM, POOL_GROUP_DIM)), _full((1, POOL_WIDTH))],
        out_specs=(pl.BlockSpec((grp.tile, POOL_WIDTH), lambda i: (i, 0)),
                   pl.BlockSpec((1, 16, POOL_WIDTH), lambda i: (grp.seq(i), 0, 0))),
        scratch_shapes=[pltpu.VMEM((grp.tile + 16, POOL_WIDTH), F32)],
        compiler_params=_params(), name="pool")(proj, init, pool_w, pool_scale)


def _gla_kernel(q_ref, k_ref, v_ref, g_ref, s0_ref, o_ref, sout_ref, state, before, *, grp):
    i = pl.program_id(0)
    tile = grp.tile
    n_blk = tile // GLA_BLOCK

    @pl.when(grp.tile_in_seq(i) == 0)
    def _():
        state[...] = s0_ref[0]

    g = g_ref[...]
    row = lax.broadcasted_iota(jnp.int32, (tile, tile), 0)
    col = lax.broadcasted_iota(jnp.int32, (tile, tile), 1)
    same = (row >> 4) == (col >> 4)
    causal = jnp.logical_and(same, col <= row)
    tri = jnp.where(causal, 1.0, 0.0).astype(MXU_DTYPE)
    ones = jnp.where(same, 1.0, 0.0).astype(MXU_DTYPE)
    g1, g2, g3 = _split3(g)
    b = _dot(tri, g1) + _dot(tri, g2) + _dot(tri, g3)
    b_last = _dot(ones, g1) + _dot(ones, g2) + _dot(ones, g3)
    q_t = q_ref[...] * (GLA_DK ** -0.5) * jnp.exp(b)
    k = k_ref[...]
    k_t = k * jnp.exp(-b)
    k_dec_t = (k * jnp.exp(b_last - b)).T
    sel = jnp.where((lax.broadcasted_iota(jnp.int32, (tile, LANES), 0) >> 4)
                    == lax.broadcasted_iota(jnp.int32, (tile, LANES), 1), 1.0, 0.0).astype(MXU_DTYPE)
    t1, t2, t3 = _split3(g.T)
    blk_decay = jnp.exp(_dot(t1, sel) + _dot(t2, sel) + _dot(t3, sel))
    v = v_ref[...]
    stacked = n_blk * GLA_DK
    dk_bits, blk_bits = GLA_DK.bit_length() - 1, GLA_BLOCK.bit_length() - 1
    upd_live = (lax.broadcasted_iota(jnp.int32, (stacked, tile), 0) >> dk_bits
                == lax.broadcasted_iota(jnp.int32, (stacked, tile), 1) >> blk_bits)
    qry_live = (lax.broadcasted_iota(jnp.int32, (tile, stacked), 0) >> blk_bits
                == lax.broadcasted_iota(jnp.int32, (tile, stacked), 1) >> dk_bits)
    low_half = lax.broadcasted_iota(jnp.int32, (tile, LANES), 1) < GLA_DK
    for h in range(GLA_HEADS):
        ks = slice(h * GLA_DK, (h + 1) * GLA_DK)
        vs = slice(h * GLA_DV, (h + 1) * GLA_DV)
        vh = v[:, vs]
        scores = jnp.where(causal, _dot_nt(q_t[:, ks], k_t[:, ks]), 0.0)
        o = _dot(scores, vh)
        upd = _dot(jnp.where(upd_live, jnp.concatenate([k_dec_t[ks, :]] * n_blk, axis=0), 0.0), vh)
        s = state[h]
        for j in range(n_blk):
            rows = slice(j * GLA_DK, (j + 1) * GLA_DK)
            before[rows, :] = s
            s = blk_decay[ks, j:j + 1] * s + upd[rows, :]
        state[h] = s
        pair = q_t[:, (h // 2) * LANES:(h // 2 + 1) * LANES]
        swapped = pltpu.roll(pair, GLA_DK, axis=1)
        both = jnp.where(low_half, pair, swapped) if h % 2 == 0 else jnp.where(low_half, swapped, pair)
        q_exp = jnp.where(qry_live, jnp.concatenate([both] * (stacked // LANES), axis=1), 0.0)
        o_ref[:, vs] = o + _dot(q_exp, before[...])

    @pl.when(grp.last(i))
    def _():
        sout_ref[0] = state[...]


def _gla(proj, s0, grp):
    st_shape = (grp.n_seq, GLA_HEADS, GLA_DK, GLA_DV)
    st_spec = pl.BlockSpec((1, GLA_HEADS, GLA_DK, GLA_DV), lambda i: (grp.seq(i), 0, 0, 0))
    kw = GLA_KEY_WIDTH
    cols = lambda width, c: pl.BlockSpec((grp.tile, width), lambda i: (grp.block(i), c))
    return pl.pallas_call(
        functools.partial(_gla_kernel, grp=grp),
        out_shape=(jax.ShapeDtypeStruct((grp.n_tiles * grp.tile, GLA_WIDTH), F32), jax.ShapeDtypeStruct(st_shape, F32)),
        grid=(grp.n_tiles,),
        in_specs=[cols(kw, POOL_WIDTH // kw), cols(kw, POOL_WIDTH // kw + 1),
                  cols(GLA_WIDTH, (POOL_WIDTH + 2 * kw) // GLA_WIDTH), cols(kw, IN_EVEN_MAIN // kw), st_spec],
        out_specs=(pl.BlockSpec((grp.tile, GLA_WIDTH), lambda i: (i, 0)), st_spec),
        scratch_shapes=[pltpu.VMEM((GLA_HEADS, GLA_DK, GLA_DV), F32),
                        pltpu.VMEM((grp.tile // GLA_BLOCK * GLA_DK, GLA_DV), F32)],
        compiler_params=_params(), name="gla")(proj, proj, proj, proj, s0)


def _route(h1, nffn_ref, wr_hi_ref, wr_lo_ref, br_ref, xn_ref, info_ref, cnt_ref, carry):
    tm = h1.shape[0]

    @pl.when(pl.program_id(0) == 0)
    def _():
        carry[...] = jnp.zeros_like(carry)

    xn = _rms(h1, nffn_ref[...])
    x_hi = xn.astype(MXU_DTYPE)
    xn_ref[:, :ROW_TILES, :] = x_hi.reshape(tm, ROW_TILES, LANES)
    x_lo = (xn - x_hi.astype(F32)).astype(MXU_DTYPE)
    logits = (_dot(x_hi, wr_hi_ref[...]) + _dot(x_lo, wr_hi_ref[...]) + _dot(x_hi, wr_lo_ref[...])
              + br_ref[...])
    lane = lax.broadcasted_iota(jnp.int32, (tm, LANES), 1)
    lane_f = lane.astype(F32)
    vals, ids, hots = [], [], []
    for _ in range(TOP_K):
        m = jnp.max(logits, axis=-1, keepdims=True)
        ix = jnp.min(jnp.where(logits == m, lane_f, float(LANES)), axis=-1, keepdims=True)
        hot = lane_f == ix
        vals.append(m)
        ids.append(ix)
        hots.append(hot)
        logits = jnp.where(hot, -jnp.inf, logits)
    es = [jnp.exp(v - vals[0]) for v in vals]
    den = es[0] + es[1] + es[2] + es[3]
    chosen = jnp.zeros((tm, LANES), F32)
    for hot in hots:
        chosen = chosen + jnp.where(hot, 1.0, 0.0)
    before = (lax.broadcasted_iota(jnp.int32, (tm, tm), 1) < lax.broadcasted_iota(jnp.int32, (tm, tm), 0))
    rank = _dot(jnp.where(before, 1.0, 0.0), chosen) + carry[...]
    info = jnp.zeros((tm, LANES), F32)
    for k in range(TOP_K):
        pos = jnp.sum(jnp.where(hots[k], rank, 0.0), axis=-1, keepdims=True)
        info = jnp.where(lane == k, es[k] / den, info)
        info = jnp.where(lane == TOP_K + k, ids[k].astype(F32), info)
        info = jnp.where(lane == 2 * TOP_K + k, pos, info)
    info_ref[...] = info
    token = pl.program_id(0) * tm + lax.broadcasted_iota(jnp.int32, (tm, 1), 0)
    tag = jnp.zeros((tm, LANES), F32)
    for d in range(TAG_DIGITS):
        tag = jnp.where(lane == d, ((token >> (8 * d)) & 255).astype(F32), tag)
    for k in range(TOP_K):
        tag = jnp.where(lane == TAG_DIGITS + k, ids[k].astype(F32), tag)
    tag_row = jnp.concatenate([tag, jnp.zeros((tm, D_MODEL - LANES), F32)], axis=1).astype(xn_ref.dtype)
    xn_ref[:, ROW_TILES:, :] = tag_row.reshape(tm, ROW_TILES, LANES)
    carry[...] = carry[...] + jnp.sum(chosen, axis=0, keepdims=True)
    cnt_ref[...] = carry[...]


_ROUTE_OUT_SHAPES = lambda n: (jax.ShapeDtypeStruct((n, D_MODEL), F32), jax.ShapeDtypeStruct((n, TAGGED_ROWS, LANES), MXU_DTYPE),
                               jax.ShapeDtypeStruct((n, LANES), F32), jax.ShapeDtypeStruct((1, LANES), F32))
_ROUTE_OUT_SPECS = (pl.BlockSpec((TOKEN_TILE, D_MODEL), lambda i: (i, 0)),
                    pl.BlockSpec((TOKEN_TILE, TAGGED_ROWS, LANES), lambda i: (i, 0, 0)),
                    pl.BlockSpec((TOKEN_TILE, LANES), lambda i: (i, 0)),
                    pl.BlockSpec((1, LANES), lambda i: (0, 0)))


def _route_in_specs():
    return [_full((1, D_MODEL)), _full((D_MODEL, LANES)), _full((D_MODEL, LANES)), _full((1, LANES))]


def _even_out_kernel(ypp_ref, yps_ref, op_ref, os_ref, r_ref, hp_ref, hs_ref, gn_ref, w_ref, nffn_ref, wr_hi_ref,
                     wr_lo_ref, br_ref, h1_ref, xn_ref, info_ref, cnt_ref, carry, *, npt):
    o = _split_rows(op_ref, os_ref, npt)
    r = r_ref[...]
    parts = []
    for hd in range(GLA_HEADS):
        sl = slice(hd * GLA_DV, (hd + 1) * GLA_DV)
        oh = o[:, sl]
        oh = oh * lax.rsqrt(jnp.mean(oh * oh, axis=-1, keepdims=True) + RMS_EPS) * gn_ref[...]
        rh = r[:, sl]
        parts.append(oh * (rh * jax.nn.sigmoid(rh)))
    gla = jnp.concatenate(parts, axis=1)
    mix = _dot(_split_rows(ypp_ref, yps_ref, npt), w_ref[:POOL_WIDTH, :]) + _dot(gla, w_ref[POOL_WIDTH:, :])
    h1 = _split_rows(hp_ref, hs_ref, npt) + mix
    h1_ref[...] = h1
    _route(h1, nffn_ref, wr_hi_ref, wr_lo_ref, br_ref, xn_ref, info_ref, cnt_ref, carry)


def _even_out(y_pool_parts, o_gla_parts, proj, h_parts, n_p, gla_norm, w_out, nffn, wr_hi, wr_lo, br):
    n = proj.shape[0]
    return pl.pallas_call(
        functools.partial(_even_out_kernel, npt=n_p // TOKEN_TILE),
        out_shape=_ROUTE_OUT_SHAPES(n),
        grid=(n // TOKEN_TILE,),
        in_specs=_split_specs(TOKEN_TILE, POOL_WIDTH, n_p, False) + _split_specs(TOKEN_TILE, GLA_WIDTH, n_p, False)
        + [pl.BlockSpec((TOKEN_TILE, GLA_WIDTH), lambda i: (i, (IN_EVEN_MAIN - GLA_WIDTH) // GLA_WIDTH))]
        + _split_specs(TOKEN_TILE, D_MODEL, n_p, h_parts[0] is h_parts[1])
        + [_full((1, GLA_DV)), _full((D_MODEL, D_MODEL))] + _route_in_specs(),
        out_specs=_ROUTE_OUT_SPECS,
        scratch_shapes=[pltpu.VMEM((1, LANES), F32)],
        compiler_params=_params(), name="even_out")(
            *y_pool_parts, *o_gla_parts, proj, *h_parts, gla_norm, w_out, nffn, wr_hi, wr_lo, br)


def _rope_tile(x, cos, sin, lo_half):
    swapped = jnp.where(lo_half, pltpu.roll(x, LANES - HEAD_DIM // 2, axis=1), pltpu.roll(x, HEAD_DIM // 2, axis=1))
    return x * cos + swapped * sin


def _odd_in_kernel(h_ref, g_ref, w_ref, b_ref, cos_ref, sin_ref, q_ref, kv_ref):
    xn = _rms(h_ref[...], g_ref[...])
    qkv = _dot(xn, w_ref[...]) + b_ref[...]
    cos = cos_ref[...]
    sin = sin_ref[...]
    lo_half = (lax.broadcasted_iota(jnp.int32, cos.shape, 1) % HEAD_DIM) < HEAD_DIM // 2
    for j in range(Q_WIDTH // LANES):
        sl = slice(j * LANES, (j + 1) * LANES)
        q_ref[:, sl] = _rope_tile(qkv[:, sl], cos, sin, lo_half).astype(q_ref.dtype)
    kv_ref[:, :KV_WIDTH] = _rope_tile(qkv[:, Q_WIDTH:Q_WIDTH + KV_WIDTH], cos, sin, lo_half)
    kv_ref[:, KV_WIDTH:] = qkv[:, Q_WIDTH + KV_WIDTH:]


def _odd_in(h, g, w_qkv, b_qkv, cos, sin, table_block):
    n = h.shape[0]
    wq = w_qkv.shape[1]
    tab = pl.BlockSpec((TOKEN_TILE, LANES), lambda i: (table_block(i), 0))
    return pl.pallas_call(
        _odd_in_kernel,
        out_shape=(jax.ShapeDtypeStruct((n, Q_WIDTH), MXU_DTYPE), jax.ShapeDtypeStruct((n, 2 * KV_WIDTH), F32)),
        grid=(n // TOKEN_TILE,),
        in_specs=[pl.BlockSpec((TOKEN_TILE, D_MODEL), lambda i: (i, 0)), _full((1, D_MODEL)),
                  _full((D_MODEL, wq)), _full((1, wq)), tab, tab],
        out_specs=(pl.BlockSpec((TOKEN_TILE, Q_WIDTH), lambda i: (i, 0)),
                   pl.BlockSpec((TOKEN_TILE, 2 * KV_WIDTH), lambda i: (i, 0))),
        compiler_params=_params(), name="odd_in")(h, g, w_qkv, b_qkv, cos, sin)


def _attn_kernel(sink_ref, q_ref, kv0_ref, kv1_ref, kv2_ref, ck0_ref, ck1_ref, cv0_ref, cv1_ref, o_ref, *, sq):
    i = pl.program_id(0)
    smp = sq.is_sample(i)
    t = sq.tile_in_seq(i)
    kv0 = kv0_ref[...]
    k_old = jnp.where(smp, ck0_ref[...], kv2_ref[:, :KV_WIDTH])
    k_mid = jnp.where(smp, ck1_ref[...], kv1_ref[:, :KV_WIDTH])
    v_old = jnp.where(smp, cv0_ref[...], kv2_ref[:, KV_WIDTH:])
    v_mid = jnp.where(smp, cv1_ref[...], kv1_ref[:, KV_WIDTH:])
    pad = jnp.zeros((CHUNK, KV_WIDTH), F32)
    keys = jnp.concatenate([k_old, k_mid, kv0[:, :KV_WIDTH], pad], axis=0)
    vals = jnp.concatenate([v_old, v_mid, kv0[:, KV_WIDTH:], pad], axis=0)
    n_keys = 4 * CHUNK
    lane = lax.broadcasted_iota(jnp.int32, (n_keys, KV_WIDTH), 1)
    lo = lane < HEAD_DIM
    keys_sw = pltpu.roll(keys, HEAD_DIM, axis=1)
    vals_sw = pltpu.roll(vals, HEAD_DIM, axis=1)
    kcol = lax.broadcasted_iota(jnp.int32, (1, n_keys), 1)
    first_valid = jnp.where(smp, 0, (2 - jnp.minimum(t, 2)) * CHUNK)
    key_ok = jnp.logical_and(kcol >= first_valid, kcol < 3 * CHUNK)
    n_pairs = N_Q_HEADS // N_KV_HEADS // 2
    pair_of_row = lax.broadcasted_iota(jnp.int32, (n_pairs * CHUNK, 1), 0) // CHUNK
    for g in range(N_KV_HEADS):
        own, other = (keys, keys_sw) if g == 0 else (keys_sw, keys)
        vown, vother = (vals, vals_sw) if g == 0 else (vals_sw, vals)
        kb = jnp.concatenate([jnp.where(lo, own, 0.0), jnp.where(lo, 0.0, other)], axis=0).astype(MXU_DTYPE)
        vb = jnp.concatenate([jnp.where(lo, vown, 0.0), jnp.where(lo, 0.0, vother)], axis=0).astype(MXU_DTYPE)
        pairs = [slice((g * n_pairs + pr) * LANES, (g * n_pairs + pr + 1) * LANES) for pr in range(n_pairs)]
        qg = jnp.concatenate([q_ref[:, sl] for sl in pairs], axis=0)
        s = _dot_nt(qg, kb) * (HEAD_DIM ** -0.5)
        halves = []
        for half in range(2):
            sh = jnp.where(key_ok, s[:, half * n_keys:(half + 1) * n_keys], NEG_INF)
            sink = jnp.zeros((n_pairs * CHUNK, 1), F32)
            for pr in range(n_pairs):
                sink = jnp.where(pair_of_row == pr, sink_ref[2 * (g * n_pairs + pr) + half], sink)
            m = jnp.maximum(jnp.max(sh, axis=-1, keepdims=True), sink)
            p = jnp.exp(sh - m)
            halves.append(p * (1.0 / (jnp.sum(p, axis=-1, keepdims=True) + jnp.exp(sink - m))))
        o = _dot(jnp.concatenate(halves, axis=1), vb)
        for pr, sl in enumerate(pairs):
            o_ref[:, sl] = o[pr * CHUNK:(pr + 1) * CHUNK].astype(o_ref.dtype)


def _attention(sinks, q, kv, ck, cv, sq):
    n = q.shape[0]
    npt = sq.n_prompt_tiles
    prev = lambda d: (lambda i, s: (jnp.where(sq.is_sample(i), i, jnp.maximum(i - d, 0)), 0))
    cache = lambda d: (lambda i, s: (jnp.where(sq.is_sample(i), 2 * (i - npt) + d, 0), 0))
    kvspec = lambda f: pl.BlockSpec((CHUNK, 2 * KV_WIDTH), f)
    cspec = lambda f: pl.BlockSpec((CHUNK, KV_WIDTH), f)
    grid_spec = pltpu.PrefetchScalarGridSpec(
        num_scalar_prefetch=1, grid=(sq.n_tiles,),
        in_specs=[pl.BlockSpec((CHUNK, Q_WIDTH), lambda i, s: (i, 0)),
                  kvspec(prev(0)), kvspec(prev(1)), kvspec(prev(2)),
                  cspec(cache(0)), cspec(cache(1)), cspec(cache(0)), cspec(cache(1))],
        out_specs=pl.BlockSpec((CHUNK, Q_WIDTH), lambda i, s: (i, 0)))
    return pl.pallas_call(
        functools.partial(_attn_kernel, sq=sq),
        out_shape=jax.ShapeDtypeStruct((n, Q_WIDTH), MXU_DTYPE),
        grid_spec=grid_spec, compiler_params=_params(), name="attention")(sinks, q, kv, kv, kv, ck, ck, cv, cv)


def _odd_out_kernel(o_ref, h_ref, w_ref, b_ref, nffn_ref, wr_hi_ref, wr_lo_ref, br_ref,
                    h1_ref, xn_ref, info_ref, cnt_ref, carry):
    h1 = h_ref[...] + _dot(o_ref[...], w_ref[...]) + b_ref[...]
    h1_ref[...] = h1
    _route(h1, nffn_ref, wr_hi_ref, wr_lo_ref, br_ref, xn_ref, info_ref, cnt_ref, carry)


def _odd_out(o, h, w_out, b_out, nffn, wr_hi, wr_lo, br):
    n = h.shape[0]
    row = pl.BlockSpec((TOKEN_TILE, D_MODEL), lambda i: (i, 0))
    return pl.pallas_call(
        _odd_out_kernel,
        out_shape=_ROUTE_OUT_SHAPES(n),
        grid=(n // TOKEN_TILE,),
        in_specs=[row, row, _full((D_MODEL, D_MODEL)), _full((1, D_MODEL))] + _route_in_specs(),
        out_specs=_ROUTE_OUT_SPECS,
        scratch_shapes=[pltpu.VMEM((1, LANES), F32)],
        compiler_params=_params(), name="odd_out")(o, h, w_out, b_out, nffn, wr_hi, wr_lo, br)


def _token_copy(src, s, dst, d, sem, rows=ROW_TILES):
    return pltpu.make_async_copy(src.at[pl.ds(pl.multiple_of(s, rows), rows)],
                                 dst.at[pl.ds(pl.multiple_of(d, rows), rows)], sem)


ZERO_TOKENS = EXPERT_TILE // 2
HIDDEN_BLOCK = 512


def _zero_fill(fill_ref, zeros, xs_out, zsem):
    zeros[...] = jnp.zeros_like(zeros)

    def sweep(wait):
        def go(cp):
            cp.wait() if wait else cp.start()

        def tail(e, c):
            off, length = fill_ref[e], fill_ref[N_EXPERTS + e]
            for bit in range(EXPERT_TILE.bit_length() - 1):
                rows = (1 << bit) * TAGGED_ROWS

                @pl.when((length >> bit) & 1 == 1)
                def _():
                    o = pl.multiple_of(off + (length & ((1 << bit) - 1)) * TAGGED_ROWS, TAGGED_ROWS)
                    go(pltpu.make_async_copy(zeros.at[pl.ds(0, rows)], xs_out.at[pl.ds(o, rows)], zsem))
            return c

        lax.fori_loop(0, N_EXPERTS, tail, 0)

        def unused(t, c):
            o = pl.multiple_of(fill_ref[2 * N_EXPERTS] + t * ZERO_TOKENS * TAGGED_ROWS, TAGGED_ROWS)
            go(pltpu.make_async_copy(zeros, xs_out.at[pl.ds(o, ZERO_TOKENS * TAGGED_ROWS)], zsem))
            return c

        lax.fori_loop(0, fill_ref[2 * N_EXPERTS + 1], unused, 0)

    sweep(wait=False)
    sweep(wait=True)


def _dispatch_kernel(fill_ref, dest_hbm, x_ref, xs_out, dest_smem, zeros, sem, dsems, zsem):
    i = pl.program_id(0)
    n_slots = TOKEN_TILE * TOP_K

    @pl.when(i == 0)
    def _():
        _zero_fill(fill_ref, zeros, xs_out, zsem)

    slot = i % 2

    def dest_rows(step, s):
        return pltpu.make_async_copy(dest_hbm.at[pl.ds(step * n_slots, n_slots)], dest_smem.at[s], dsems.at[s])

    @pl.when(i == 0)
    def _():
        dest_rows(0, 0).start()

    dest_rows(i, slot).wait()

    @pl.when(i + 1 < pl.num_programs(0))
    def _():
        dest_rows(i + 1, 1 - slot).start()

    def issue(t, c):
        for k in range(TOP_K):
            _token_copy(x_ref, t * TAGGED_ROWS, xs_out, dest_smem[slot, t * TOP_K + k], sem, TAGGED_ROWS).start()
        return c

    lax.fori_loop(0, TOKEN_TILE, issue, 0, unroll=2)
    whole = xs_out.at[pl.ds(0, n_slots * TAGGED_ROWS)]
    pltpu.make_async_copy(whole, whole, sem).wait()


def _dispatch(fill, dest, xt, n_rows):
    n = xt.shape[0] // TAGGED_ROWS
    grid_spec = pltpu.PrefetchScalarGridSpec(
        num_scalar_prefetch=1, grid=(n // TOKEN_TILE,),
        in_specs=[pl.BlockSpec(memory_space=pl.ANY),
                  pl.BlockSpec((TOKEN_TILE * TAGGED_ROWS, LANES), lambda i, f: (i, 0))],
        out_specs=pl.BlockSpec(memory_space=pl.ANY),
        scratch_shapes=[pltpu.SMEM((2, TOKEN_TILE * TOP_K), jnp.int32), pltpu.VMEM((ZERO_TOKENS * TAGGED_ROWS, LANES), xt.dtype),
                        pltpu.SemaphoreType.DMA, pltpu.SemaphoreType.DMA((2,)), pltpu.SemaphoreType.DMA])
    return pl.pallas_call(
        _dispatch_kernel, out_shape=jax.ShapeDtypeStruct((n_rows * TAGGED_ROWS, LANES), xt.dtype),
        grid_spec=grid_spec, compiler_params=_params(), name="moe_dispatch")(fill, dest, xt)


def _experts_kernel(te_ref, tr_ref, x_ref, wgu_ref, bgu_ref, wd_ref, bd_ref, yt_ref,
                    wgu_mxu, wd_mxu, ybuf, place_vmem, place_smem, ysems, psem, *, n_tokens):
    i = pl.program_id(0)
    slot = i % 2
    prev = 1 - slot
    tile_rows = EXPERT_TILE * ROW_TILES
    spare = yt_ref.shape[0] - 2 * tile_rows
    col = lax.broadcasted_iota(jnp.int32, (1, EXPERT_TILE), 1)

    def rows_done(s):
        whole = yt_ref.at[pl.ds(0, tile_rows)]
        pltpu.make_async_copy(whole, whole, ysems.at[s]).wait()

    def places_to_smem(first_rows, s):
        place_vmem[...] = jnp.broadcast_to(first_rows, place_vmem.shape)
        return pltpu.make_async_copy(place_vmem, place_smem.at[s], psem)

    def start_row_copy(j, s):
        _token_copy(ybuf.at[s], j * ROW_TILES, yt_ref, place_smem[s, 0, j], ysems.at[s]).start()

    @pl.when(i == 0)
    def _():
        ybuf[...] = jnp.zeros(ybuf.shape, F32)
        for s in (1, 0):
            cp = pltpu.make_async_copy(ybuf.at[0], yt_ref.at[pl.ds(spare + s * tile_rows, tile_rows)], ysems.at[s])
            cp.start()
            if s == 1:
                cp.wait()
        cp = places_to_smem(spare + tile_rows + col * ROW_TILES, 1)
        cp.start()
        cp.wait()

    @pl.when(jnp.logical_or(i == 0, te_ref[i] != te_ref[jnp.maximum(i - 1, 0)]))
    def _():
        wgu_mxu[...] = wgu_ref[0, 0].astype(MXU_DTYPE)
        wd_mxu[...] = wd_ref[0, 0].astype(MXU_DTYPE)

    n_valid = tr_ref[i]
    after_last = jnp.logical_and(n_valid == 0, jnp.logical_and(i >= 1, tr_ref[jnp.maximum(i - 1, 0)] > 0))

    @pl.when(n_valid > 0)
    def _():
        for j in range(EXPERT_TILE):
            start_row_copy(j, prev)
        tag = jnp.sum(x_ref[:, ROW_TILES:, :].astype(F32), axis=1)
        lane = lax.broadcasted_iota(jnp.int32, (EXPERT_TILE, LANES), 1)
        is_id = jnp.logical_and(lane >= TAG_DIGITS, lane < TAG_DIGITS + TOP_K)
        mine = jnp.logical_and(is_id, tag == te_ref[i].astype(F32))
        weight = jnp.where(lane == 0, 1.0, jnp.where(lane == 1, 256.0, 65536.0))
        terms = jnp.where(lane < TAG_DIGITS, tag * weight, jnp.where(mine, ((lane - TAG_DIGITS) * n_tokens).astype(F32), 0.0))
        place = jnp.sum(terms, axis=-1, keepdims=True)
        hi = jnp.floor(place * (1.0 / 65536.0))
        mid = jnp.floor((place - hi * 65536.0) * (1.0 / 256.0))
        digits = jnp.where(lane == 0, place - hi * 65536.0 - mid * 256.0, jnp.where(lane == 1, mid, jnp.where(lane == 2, hi, 0.0)))
        pick = jnp.where(lax.broadcasted_iota(jnp.int32, (ROW_TILES, LANES), 0)
                         == lax.broadcasted_iota(jnp.int32, (ROW_TILES, LANES), 1), 1.0, 0.0)
        planes = _dot_nt(pick, digits)
        place_row = (planes[0:1] + 256.0 * planes[1:2] + 65536.0 * planes[2:3]).astype(jnp.int32) * ROW_TILES
        to_smem = places_to_smem(jnp.where(col < n_valid, place_row, spare + slot * tile_rows + col * ROW_TILES), slot)
        to_smem.start()

        x = x_ref[:, :ROW_TILES, :].reshape(EXPERT_TILE, D_MODEL)
        out = None
        for c in range(D_MODEL // HIDDEN_BLOCK):
            cols = slice(c * HIDDEN_BLOCK, (c + 1) * HIDDEN_BLOCK)
            ucols = slice(D_MODEL + c * HIDDEN_BLOCK, D_MODEL + (c + 1) * HIDDEN_BLOCK)
            gate = jnp.minimum(_dot(x, wgu_mxu[:, cols]) + bgu_ref[0, 0, :, cols], SWIGLU_LIMIT)
            up = jnp.clip(_dot(x, wgu_mxu[:, ucols]) + bgu_ref[0, 0, :, ucols], -SWIGLU_LIMIT, SWIGLU_LIMIT)
            act = (up + 1.0) * gate * jax.nn.sigmoid(SWIGLU_ALPHA * gate)
            part = _dot(act, wd_mxu[cols, :])
            out = part if out is None else out + part
        out = out + bd_ref[0, 0]
        rows_done(slot)
        _store_token_tiles(ybuf.at[slot], out)
        to_smem.wait()

    @pl.when(after_last)
    def _():
        rows_done(slot)
        def issue(j, c):
            start_row_copy(j, prev)
            return c

        lax.fori_loop(0, EXPERT_TILE, issue, 0, unroll=8)
        rows_done(prev)


def _experts(layer, tile_expert, tile_rows, xs, wgu, bgu, wd, bd, n_tokens):
    rows = xs.shape[0]
    w = lambda shape: pl.BlockSpec((1, 1) + shape, lambda i, te, tr: (layer, te[i], 0, 0))
    grid_spec = pltpu.PrefetchScalarGridSpec(
        num_scalar_prefetch=2, grid=(rows // EXPERT_TILE,),
        in_specs=[pl.BlockSpec((EXPERT_TILE, TAGGED_ROWS, LANES), lambda i, te, tr: (i, 0, 0)),
                  w((D_MODEL, 2 * D_MODEL)), w((1, 2 * D_MODEL)), w((D_MODEL, D_MODEL)), w((1, D_MODEL))],
        out_specs=pl.BlockSpec(memory_space=pl.ANY),
        scratch_shapes=[pltpu.VMEM((D_MODEL, 2 * D_MODEL), MXU_DTYPE), pltpu.VMEM((D_MODEL, D_MODEL), MXU_DTYPE),
                        pltpu.VMEM((2, EXPERT_TILE * ROW_TILES, LANES), F32),
                        pltpu.VMEM((ROW_TILES, EXPERT_TILE), jnp.int32), pltpu.SMEM((2, ROW_TILES, EXPERT_TILE), jnp.int32),
                        pltpu.SemaphoreType.DMA((2,)), pltpu.SemaphoreType.DMA])
    yt_rows = (n_tokens * TOP_K + 2 * EXPERT_TILE) * ROW_TILES
    return pl.pallas_call(
        functools.partial(_experts_kernel, n_tokens=n_tokens), out_shape=jax.ShapeDtypeStruct((yt_rows, LANES), F32),
        grid_spec=grid_spec, compiler_params=_params(), name="moe_experts")(
            tile_expert, tile_rows, xs, wgu, bgu, wd, bd)


def _combine_kernel(*refs, final, npt):
    y_refs, outs = refs[:TOP_K], refs[TOP_K + 7:]
    info_ref, h_ref, p_prompt_ref, p_sample_ref, pp_ref, pg_ref, nf_ref = refs[TOP_K:TOP_K + 7]
    i = pl.program_id(0)
    gates = info_ref[...]
    moe = None
    for k in range(TOP_K):
        rows = _load_token_tiles(y_refs[k], COMBINE_TILE)
        moe = gates[:, k:k + 1] * rows if moe is None else moe + gates[:, k:k + 1] * rows
    h2 = h_ref[...] + moe
    p = jnp.where(i < npt, p_prompt_ref[0], p_sample_ref[0])
    h3 = h2 + jax.nn.sigmoid(_dot(h2, pg_ref[...])) * _dot(p, pp_ref[...])
    if not final:
        outs[0][...] = h3
        return
    y = _rms(h3, nf_ref[...])

    @pl.when(i < npt)
    def _():
        outs[0][...] = y

    @pl.when(i >= npt)
    def _():
        outs[1][...] = y


def _combine(layer, yt, info, h1, p_prompt, p_sample, ple_proj, ple_gate, norm_final, final):
    n = h1.shape[0]
    n_p = p_prompt.shape[1]
    npt = n_p // COMBINE_TILE
    steps = n // COMBINE_TILE
    row = lambda w: pl.BlockSpec((COMBINE_TILE, w), lambda i: (i, 0))
    rank_rows = lambda k: pl.BlockSpec((COMBINE_TILE * ROW_TILES, LANES), lambda i: (k * steps + i, 0))
    prompt_rows = lambda i: jnp.minimum(i, npt - 1)
    sample_rows = lambda i: jnp.maximum(i - npt, 0)
    if final:
        out_shape = (jax.ShapeDtypeStruct((n_p, D_MODEL), F32), jax.ShapeDtypeStruct((n - n_p, D_MODEL), F32))
        out_specs = (pl.BlockSpec((COMBINE_TILE, D_MODEL), lambda i: (prompt_rows(i), 0)),
                     pl.BlockSpec((COMBINE_TILE, D_MODEL), lambda i: (sample_rows(i), 0)))
    else:
        out_shape, out_specs = jax.ShapeDtypeStruct((n, D_MODEL), F32), row(D_MODEL)
    return pl.pallas_call(
        functools.partial(_combine_kernel, final=final, npt=npt),
        out_shape=out_shape,
        grid=(steps,),
        in_specs=[rank_rows(k) for k in range(TOP_K)] + [
            row(LANES), row(D_MODEL),
            pl.BlockSpec((1, COMBINE_TILE, PLE_DIM), lambda i: (layer, prompt_rows(i), 0)),
            pl.BlockSpec((1, COMBINE_TILE, PLE_DIM), lambda i: (layer, sample_rows(i), 0)),
            _full((PLE_DIM, D_MODEL)), _full((D_MODEL, D_MODEL)), _full((1, D_MODEL))],
        out_specs=out_specs,
        compiler_params=_params(), name="moe_combine")(
            *([yt] * TOP_K), info, h1, p_prompt, p_sample, ple_proj, ple_gate, norm_final)


def _moe_and_embed(layer, h1, xt, info, counts, p_prompt, p_sample, wgu, bgu, wd, bd, ple_proj, ple_gate,
                   norm_final, final):
    n = h1.shape[0]
    n_tiles = (n * TOP_K + N_EXPERTS * (EXPERT_TILE - 1)) // EXPERT_TILE + 1
    ids = info[:, TOP_K:2 * TOP_K].astype(jnp.int32)
    rank = info[:, 2 * TOP_K:3 * TOP_K].astype(jnp.int32)
    cnt = counts[0, :N_EXPERTS].astype(jnp.int32)
    padded = ((cnt + EXPERT_TILE - 1) // EXPERT_TILE) * EXPERT_TILE
    ends = jnp.cumsum(padded)
    starts = ends - padded
    dest = ((starts[ids] + rank) * TAGGED_ROWS).reshape(-1)
    tile_start = jnp.arange(n_tiles, dtype=jnp.int32) * EXPERT_TILE
    tile_expert = jnp.minimum(jnp.sum((tile_start[:, None] >= ends[None, :]).astype(jnp.int32), axis=1), N_EXPERTS - 1)
    tile_rows = jnp.clip(cnt[tile_expert] - (tile_start - starts[tile_expert]), 0, EXPERT_TILE)
    tile_rows = jnp.where(tile_start < ends[-1], tile_rows, 0).astype(jnp.int32)
    fill = jnp.concatenate([(starts + cnt) * TAGGED_ROWS, padded - cnt,
                            jnp.stack([ends[-1] * TAGGED_ROWS,
                                       (n_tiles * EXPERT_TILE - ends[-1]) // ZERO_TOKENS])]).astype(jnp.int32)
    xs = _dispatch(fill, dest, xt.reshape(n * TAGGED_ROWS, LANES), n_tiles * EXPERT_TILE)
    yt = _experts(layer, tile_expert, tile_rows, xs.reshape(-1, TAGGED_ROWS, LANES), wgu, bgu, wd, bd, n)
    return _combine(layer, yt, info, h1, p_prompt, p_sample, ple_proj, ple_gate, norm_final, final)


def _rope_tables(prompt_len, n_sample_seq, sample_len):
    half = HEAD_DIM // 2
    inv = jnp.power(jnp.float32(ROPE_THETA), -jnp.arange(half, dtype=F32) / half)
    pos = jnp.concatenate([jnp.arange(prompt_len), jnp.tile(PAST_LEN + jnp.arange(sample_len), n_sample_seq)])
    ang = pos.astype(F32)[:, None] * inv[None, :]
    cos = jnp.tile(jnp.cos(ang), (1, LANES // half))
    sin = jnp.tile(jnp.concatenate([-jnp.sin(ang), jnp.sin(ang)], axis=1), (1, LANES // HEAD_DIM))
    return cos, sin


def kernel(x_prompt, x_sample, state_pool, state_gla, cache_k, cache_v, p_prompt, p_sample, norm_mix, norm_ffn, norm_final, w_in_even, pool_w, pool_scale, gla_w_gate, gla_b_gate, gla_norm, w_out_even, w_qkv_odd, b_qkv_odd, attn_sinks, w_out_odd, b_out_odd, w_router, b_router, w_gate_up, b_gate_up, w_down, b_down, ple_proj, ple_gate):
    bsz, t_len, _ = x_prompt.shape
    dec_bsz, dec_len, _ = x_sample.shape
    depth = norm_mix.shape[0]
    n_p, n_s = bsz * t_len, dec_bsz * dec_len
    n = n_p + n_s
    assert dec_len == CHUNK and n_s == TOKEN_TILE and t_len % TOKEN_TILE == 0 and cache_k.shape[2] == WINDOW
    sq = _Seq(bsz, t_len, dec_bsz)
    grp_p = _Group(bsz, t_len, min(SEQ_TILE, t_len), 0, 0)
    grp_s = _Group(dec_bsz, dec_len, dec_len, n_p, PAST_LEN)
    bf = lambda a: a.astype(MXU_DTYPE)
    row = lambda a: a.reshape(1, -1)

    h_parts = (x_prompt.reshape(n_p, D_MODEL), x_sample.reshape(n_s, D_MODEL))
    p_parts = (p_prompt.reshape(depth, n_p, PLE_DIM), p_sample.reshape(depth, n_s, PLE_DIM))
    b_gu = b_gate_up.reshape(depth, N_EXPERTS, 1, 2 * D_MODEL)
    b_dn = b_down.reshape(depth, N_EXPERTS, 1, D_MODEL)
    cos, sin = _rope_tables(t_len, dec_bsz, dec_len)
    tiles_per_seq = t_len // TOKEN_TILE
    table_block = lambda i: jnp.where(i < bsz * tiles_per_seq, i % tiles_per_seq, tiles_per_seq)

    pools, glas, new_k, new_v = [], [], [], []
    for i in range(depth):
        if i % 2 == 0:
            e = i // 2
            w_in = jnp.pad(bf(w_in_even[e]), ((0, 0), (0, (-IN_EVEN) % LANES)))
            wg = jnp.pad(bf(gla_w_gate[e]), ((0, w_in.shape[1] - IN_EVEN_MAIN - GLA_RANK), (0, 0)))
            proj = _even_in(h_parts, n, n_p, row(norm_mix[i]), w_in, wg, row(gla_b_gate[e]))
            pw, ps = bf(pool_w[e]), row(pool_scale[e])
            y_pool_p, pool_p = _pool(proj, jnp.zeros((bsz, 16, POOL_WIDTH), F32), pw, ps, grp_p)
            y_pool_s, pool_s = _pool(proj, jnp.pad(state_pool[e], ((0, 0), (1, 0), (0, 0))), pw, ps, grp_s)
            o_gla_p, gla_p = _gla(proj, jnp.zeros((bsz,) + state_gla.shape[2:], F32), grp_p)
            o_gla_s, gla_s = _gla(proj, state_gla[e], grp_s)
            pools.append((pool_p[:, 1:], pool_s[:, 1:]))
            glas.append((gla_p, gla_s))
            wr = jnp.pad(w_router[i], ((0, 0), (0, LANES - N_EXPERTS)))
            wr_hi = bf(wr)
            wr_lo = bf(wr - wr_hi.astype(F32))
            br = jnp.pad(row(b_router[i]), ((0, 0), (0, LANES - N_EXPERTS)), constant_values=NEG_INF)
            h1, xn, info, counts = _even_out((y_pool_p, y_pool_s), (o_gla_p, o_gla_s), proj, h_parts, n_p,
                                             row(gla_norm[e]), bf(w_out_even[e]), row(norm_ffn[i]), wr_hi, wr_lo, br)
        else:
            o = i // 2
            h = h_parts[0]
            q, kv = _odd_in(h, row(norm_mix[i]), bf(w_qkv_odd[o]), row(b_qkv_odd[o]), cos, sin, table_block)
            ck = cache_k[o].reshape(dec_bsz * WINDOW, KV_WIDTH)
            cv = cache_v[o].reshape(dec_bsz * WINDOW, KV_WIDTH)
            att = _attention(attn_sinks[o], q, kv, ck, cv, sq)
            kv_p = kv[:n_p].reshape(bsz, t_len, 2 * KV_WIDTH)[:, -WINDOW:]
            kv_s = kv[n_p:].reshape(dec_bsz, dec_len, 2 * KV_WIDTH)
            hd = (N_KV_HEADS, HEAD_DIM)
            new_k.append((kv_p[..., :KV_WIDTH].reshape(bsz, WINDOW, *hd),
                          jnp.concatenate([cache_k[o], kv_s[..., :KV_WIDTH].reshape(dec_bsz, dec_len, *hd)], axis=1)[:, -WINDOW:]))
            new_v.append((kv_p[..., KV_WIDTH:].reshape(bsz, WINDOW, *hd),
                          jnp.concatenate([cache_v[o], kv_s[..., KV_WIDTH:].reshape(dec_bsz, dec_len, *hd)], axis=1)[:, -WINDOW:]))
            wr = jnp.pad(w_router[i], ((0, 0), (0, LANES - N_EXPERTS)))
            wr_hi = bf(wr)
            wr_lo = bf(wr - wr_hi.astype(F32))
            br = jnp.pad(row(b_router[i]), ((0, 0), (0, LANES - N_EXPERTS)), constant_values=NEG_INF)
            h1, xn, info, counts = _odd_out(att, h, bf(w_out_odd[o]), row(b_out_odd[o]),
                                            row(norm_ffn[i]), wr_hi, wr_lo, br)
        final = i == depth - 1
        out = _moe_and_embed(i, h1, xn, info, counts, *p_parts, w_gate_up, b_gu, w_down, b_dn,
                             bf(ple_proj[i]), bf(ple_gate[i]), row(norm_final), final)
        h_parts = out if final else (out, out)

    y_prompt = h_parts[0].reshape(bsz, t_len, D_MODEL)
    y_sample = h_parts[1].reshape(dec_bsz, dec_len, D_MODEL)
    part = lambda pairs, j: jnp.stack([p[j] for p in pairs])
    return (y_prompt, y_sample, part(pools, 0), part(glas, 0), part(new_k, 0), part(new_v, 0),
            part(pools, 1), part(glas, 1), part(new_k, 1), part(new_v, 1))
```

```python
import functools

import jax
import jax.numpy as jnp
from jax import lax
from jax.experimental import pallas as pl
from jax.experimental.pallas import tpu as pltpu

F32 = jnp.float32
MXU_DTYPE = jnp.bfloat16

V7X_VMEM_BYTES = 64 * 1024 * 1024
VMEM_LIMIT = (V7X_VMEM_BYTES * 7) // 8
LANES = 128

D_MODEL = 1024
CHUNK = 64
PAST_LEN = 2048
PLE_DIM = 256
RMS_EPS = 1e-6
POOL_WINDOWS = (2, 4, 8, 16)
POOL_WIDTH = 512
POOL_GROUP_DIM = 128
POOL_STATE = 15
GLA_HEADS = 4
GLA_DK = 64
GLA_DV = 128
GLA_KEY_WIDTH = GLA_HEADS * GLA_DK
GLA_WIDTH = GLA_HEADS * GLA_DV
GLA_RANK = 16
GLA_TAU = 16.0
GLA_BLOCK = 16
IN_EVEN = POOL_WIDTH + 2 * GLA_KEY_WIDTH + 2 * GLA_WIDTH + GLA_RANK
IN_EVEN_MAIN = IN_EVEN - GLA_RANK
PROJ_WIDTH = IN_EVEN_MAIN + GLA_KEY_WIDTH
N_Q_HEADS = 16
N_KV_HEADS = 2
HEAD_DIM = 64
WINDOW = 128
ROPE_THETA = 10000.0
Q_WIDTH = N_Q_HEADS * HEAD_DIM
KV_WIDTH = N_KV_HEADS * HEAD_DIM
N_EXPERTS = 32
TOP_K = 4
SWIGLU_LIMIT = 7.0
SWIGLU_ALPHA = 1.702
NEG_INF = -1e30

TOKEN_TILE = 512
EXPERT_TILE = 512
COMBINE_TILE = 256


def _dot(a, b):
    return jnp.dot(a.astype(MXU_DTYPE), b.astype(MXU_DTYPE), preferred_element_type=F32)


def _dot_nt(a, b):
    return lax.dot_general(a.astype(MXU_DTYPE), b.astype(MXU_DTYPE), (((1,), (1,)), ((), ())),
                           preferred_element_type=F32)


def _split3(x):
    x1 = x.astype(MXU_DTYPE)
    r1 = x - x1.astype(F32)
    x2 = r1.astype(MXU_DTYPE)
    x3 = (r1 - x2.astype(F32)).astype(MXU_DTYPE)
    return x1, x2, x3


def _rms(x, g):
    return x * lax.rsqrt(jnp.mean(x * x, axis=-1, keepdims=True) + RMS_EPS) * g


ROW_TILES = D_MODEL // LANES
TAGGED_ROWS = 2 * ROW_TILES
TAG_DIGITS = 3


def _store_token_tiles(ref, x, pitch=ROW_TILES):
    for s in range(ROW_TILES):
        ref[pl.ds(s, x.shape[0], stride=pitch), :] = x[:, s * LANES:(s + 1) * LANES]


def _load_token_tiles(ref, rows, pitch=ROW_TILES):
    return jnp.concatenate([ref[pl.ds(s, rows, stride=pitch), :] for s in range(ROW_TILES)], axis=1)


def _params(n_axes=1):
    return pltpu.CompilerParams(dimension_semantics=("arbitrary",) * n_axes, vmem_limit_bytes=VMEM_LIMIT)


def _full(shape):
    return pl.BlockSpec(shape, lambda *_: (0,) * len(shape))


class _Seq:
    def __init__(self, n_prompt_seq, prompt_len, n_sample_seq):
        self.tiles_per_seq = prompt_len // CHUNK
        self.n_prompt_seq = n_prompt_seq
        self.n_prompt_tiles = n_prompt_seq * self.tiles_per_seq
        self.n_tiles = self.n_prompt_tiles + n_sample_seq
        self.n_seq = n_prompt_seq + n_sample_seq

    def is_sample(self, i):
        return i >= self.n_prompt_tiles

    def tile_in_seq(self, i):
        return jnp.where(self.is_sample(i), 0, i % self.tiles_per_seq)

    def seq(self, i):
        return jnp.where(self.is_sample(i), self.n_prompt_seq + i - self.n_prompt_tiles, i // self.tiles_per_seq)

    def last(self, i):
        return jnp.logical_or(self.is_sample(i), i % self.tiles_per_seq == self.tiles_per_seq - 1)

    def pos0(self, i):
        return jnp.where(self.is_sample(i), PAST_LEN, self.tile_in_seq(i) * CHUNK)


class _Group:
    def __init__(self, n_seq, seq_len, tile, row0, pos_base):
        assert seq_len % tile == 0 and row0 % tile == 0
        self.n_seq, self.tile, self.pos_base = n_seq, tile, pos_base
        self.tiles_per_seq = seq_len // tile
        self.n_tiles = n_seq * self.tiles_per_seq
        self.block0 = row0 // tile

    def block(self, i):
        return self.block0 + i

    def seq(self, i):
        return i // self.tiles_per_seq

    def tile_in_seq(self, i):
        return i % self.tiles_per_seq

    def last(self, i):
        return i % self.tiles_per_seq == self.tiles_per_seq - 1

    def pos0(self, i):
        return self.pos_base + self.tile_in_seq(i) * self.tile


SEQ_TILE = 256


def _split_specs(tile, width, n_prompt, same_array):
    npt = n_prompt // tile
    off = npt if same_array else 0
    return [pl.BlockSpec((tile, width), lambda i: (jnp.minimum(i, npt - 1), 0)),
            pl.BlockSpec((tile, width), lambda i: (jnp.maximum(i - npt, 0) + off, 0))]


def _split_rows(prompt_ref, sample_ref, n_prompt_tiles):
    return jnp.where(pl.program_id(0) < n_prompt_tiles, prompt_ref[...], sample_ref[...])


def _even_in_kernel(hp_ref, hs_ref, g_ref, w_ref, wg_ref, bg_ref, out_ref, *, npt):
    xn = _rms(_split_rows(hp_ref, hs_ref, npt), g_ref[...])
    proj = _dot(xn, w_ref[...])
    z = proj[:, IN_EVEN_MAIN:]
    a = _dot(z, wg_ref[...]) + bg_ref[...]
    log_alpha = (jnp.minimum(a, 0.0) - jnp.log1p(jnp.exp(-jnp.abs(a)))) * (1.0 / GLA_TAU)
    out_ref[:, :IN_EVEN_MAIN] = proj[:, :IN_EVEN_MAIN]
    out_ref[:, IN_EVEN_MAIN:] = log_alpha


def _even_in(h_parts, n, n_p, g, w_in, wg, bg):
    wp = w_in.shape[1]
    return pl.pallas_call(
        functools.partial(_even_in_kernel, npt=n_p // TOKEN_TILE),
        out_shape=jax.ShapeDtypeStruct((n, PROJ_WIDTH), F32),
        grid=(n // TOKEN_TILE,),
        in_specs=_split_specs(TOKEN_TILE, D_MODEL, n_p, h_parts[0] is h_parts[1]) + [
            _full((1, D_MODEL)), _full((D_MODEL, wp)), _full((wp - IN_EVEN_MAIN, GLA_KEY_WIDTH)),
            _full((1, GLA_KEY_WIDTH))],
        out_specs=pl.BlockSpec((TOKEN_TILE, PROJ_WIDTH), lambda i: (i, 0)),
        compiler_params=_params(), name="even_in")(*h_parts, g, w_in, wg, bg)


def _pool_kernel(u_ref, init_ref, pw_ref, ps_ref, y_ref, st_ref, buf, *, grp):
    i = pl.program_id(0)
    rows = grp.tile

    @pl.when(grp.tile_in_seq(i) == 0)
    def _():
        buf[0:16, :] = init_ref[0]

    u = u_ref[...]
    buf[16:16 + rows, :] = u
    pos = grp.pos0(i) + lax.broadcasted_iota(jnp.int32, (rows, 1), 0)
    for g, w in enumerate(POOL_WINDOWS):
        sl = slice(g * POOL_GROUP_DIM, (g + 1) * POOL_GROUP_DIM)
        acc = u[:, sl]
        for j in range(1, w):
            acc = acc + buf[16 - j:16 - j + rows, sl]
        cnt = jnp.minimum(w, pos + 1).astype(F32)
        d = acc / cnt - u[:, sl]
        y_ref[:, sl] = _dot(d, pw_ref[g]) * ps_ref[:, sl]
    tail = buf[rows:rows + 16, :]
    st_ref[0] = tail
    buf[0:16, :] = tail


def _pool(proj, init, pool_w, pool_scale, grp):
    return pl.pallas_call(
        functools.partial(_pool_kernel, grp=grp),
        out_shape=(jax.ShapeDtypeStruct((grp.n_tiles * grp.tile, POOL_WIDTH), F32),
                   jax.ShapeDtypeStruct((grp.n_seq, 16, POOL_WIDTH), F32)),
        grid=(grp.n_tiles,),
        in_specs=[pl.BlockSpec((grp.tile, POOL_WIDTH), lambda i: (grp.block(i), 0)),
                  pl.BlockSpec((1, 16, POOL_WIDTH), lambda i: (grp.seq(i), 0, 0)),
                  _full((len(POOL_WINDOWS), POOL_GROUP_DIM, POOL_GROUP_DIM)), _full((1, POOL_WIDTH))],
        out_specs=(pl.BlockSpec((grp.tile, POOL_WIDTH), lambda i: (i, 0)),
                   pl.BlockSpec((1, 16, POOL_WIDTH), lambda i: (grp.seq(i), 0, 0))),
        scratch_shapes=[pltpu.VMEM((grp.tile + 16, POOL_WIDTH), F32)],
        compiler_params=_params(), name="pool")(proj, init, pool_w, pool_scale)


def _gla_kernel(q_ref, k_ref, v_ref, g_ref, s0_ref, o_ref, sout_ref, state, before, *, grp):
    i = pl.program_id(0)
    tile = grp.tile
    n_blk = tile // GLA_BLOCK

    @pl.when(grp.tile_in_seq(i) == 0)
    def _():
        state[...] = s0_ref[0]

    g = g_ref[...]
    row = lax.broadcasted_iota(jnp.int32, (tile, tile), 0)
    col = lax.broadcasted_iota(jnp.int32, (tile, tile), 1)
    same = (row >> 4) == (col >> 4)
    causal = jnp.logical_and(same, col <= row)
    tri = jnp.where(causal, 1.0, 0.0).astype(MXU_DTYPE)
    ones = jnp.where(same, 1.0, 0.0).astype(MXU_DTYPE)
    g1, g2, g3 = _split3(g)
    b = _dot(tri, g1) + _dot(tri, g2) + _dot(tri, g3)
    b_last = _dot(ones, g1) + _dot(ones, g2) + _dot(ones, g3)
    q_t = q_ref[...] * (GLA_DK ** -0.5) * jnp.exp(b)
    k = k_ref[...]
    k_t = k * jnp.exp(-b)
    k_dec_t = (k * jnp.exp(b_last - b)).T
    sel = jnp.where((lax.broadcasted_iota(jnp.int32, (tile, LANES), 0) >> 4)
                    == lax.broadcasted_iota(jnp.int32, (tile, LANES), 1), 1.0, 0.0).astype(MXU_DTYPE)
    t1, t2, t3 = _split3(g.T)
    blk_decay = jnp.exp(_dot(t1, sel) + _dot(t2, sel) + _dot(t3, sel))
    v = v_ref[...]
    stacked = n_blk * GLA_DK
    dk_bits, blk_bits = GLA_DK.bit_length() - 1, GLA_BLOCK.bit_length() - 1
    upd_live = (lax.broadcasted_iota(jnp.int32, (stacked, tile), 0) >> dk_bits
                == lax.broadcasted_iota(jnp.int32, (stacked, tile), 1) >> blk_bits)
    qry_live = (lax.broadcasted_iota(jnp.int32, (tile, stacked), 0) >> blk_bits
                == lax.broadcasted_iota(jnp.int32, (tile, stacked), 1) >> dk_bits)
    low_half = lax.broadcasted_iota(jnp.int32, (tile, LANES), 1) < GLA_DK
    for h in range(GLA_HEADS):
        ks = slice(h * GLA_DK, (h + 1) * GLA_DK)
        vs = slice(h * GLA_DV, (h + 1) * GLA_DV)
        vh = v[:, vs]
        scores = jnp.where(causal, _dot_nt(q_t[:, ks], k_t[:, ks]), 0.0)
        o = _dot(scores, vh)
        upd = _dot(jnp.where(upd_live, jnp.concatenate([k_dec_t[ks, :]] * n_blk, axis=0), 0.0), vh)
        s = state[h]
        for j in range(n_blk):
            rows = slice(j * GLA_DK, (j + 1) * GLA_DK)
            before[rows, :] = s
            s = blk_decay[ks, j:j + 1] * s + upd[rows, :]
        state[h] = s
        pair = q_t[:, (h // 2) * LANES:(h // 2 + 1) * LANES]
        swapped = pltpu.roll(pair, GLA_DK, axis=1)
        both = jnp.where(low_half, pair, swapped) if h % 2 == 0 else jnp.where(low_half, swapped, pair)
        q_exp = jnp.where(qry_live, jnp.concatenate([both] * (stacked // LANES), axis=1), 0.0)
        o_ref[:, vs] = o + _dot(q_exp, before[...])

    @pl.when(grp.last(i))
    def _():
        sout_ref[0] = state[...]


def _gla(proj, s0, grp):
    st_shape = (grp.n_seq, GLA_HEADS, GLA_DK, GLA_DV)
    st_spec = pl.BlockSpec((1, GLA_HEADS, GLA_DK, GLA_DV), lambda i: (grp.seq(i), 0, 0, 0))
    kw = GLA_KEY_WIDTH
    cols = lambda width, c: pl.BlockSpec((grp.tile, width), lambda i: (grp.block(i), c))
    return pl.pallas_call(
        functools.partial(_gla_kernel, grp=grp),
        out_shape=(jax.ShapeDtypeStruct((grp.n_tiles * grp.tile, GLA_WIDTH), F32), jax.ShapeDtypeStruct(st_shape, F32)),
        grid=(grp.n_tiles,),
        in_specs=[cols(kw, POOL_WIDTH // kw), cols(kw, POOL_WIDTH // kw + 1),
                  cols(GLA_WIDTH, (POOL_WIDTH + 2 * kw) // GLA_WIDTH), cols(kw, IN_EVEN_MAIN // kw), st_spec],
        out_specs=(pl.BlockSpec((grp.tile, GLA_WIDTH), lambda i: (i, 0)), st_spec),
        scratch_shapes=[pltpu.VMEM((GLA_HEADS, GLA_DK, GLA_DV), F32),
                        pltpu.VMEM((grp.tile // GLA_BLOCK * GLA_DK, GLA_DV), F32)],
        compiler_params=_params(), name="gla")(proj, proj, proj, proj, s0)


def _route(h1, nffn_ref, wr_hi_ref, wr_lo_ref, br_ref, xn_ref, info_ref, cnt_ref, carry):
    tm = h1.shape[0]

    @pl.when(pl.program_id(0) == 0)
    def _():
        carry[...] = jnp.zeros_like(carry)

    xn = _rms(h1, nffn_ref[...])
    x_hi = xn.astype(MXU_DTYPE)
    xn_ref[:, :ROW_TILES, :] = x_hi.reshape(tm, ROW_TILES, LANES)
    x_lo = (xn - x_hi.astype(F32)).astype(MXU_DTYPE)
    logits = (_dot(x_hi, wr_hi_ref[...]) + _dot(x_lo, wr_hi_ref[...]) + _dot(x_hi, wr_lo_ref[...])
              + br_ref[...])
    lane = lax.broadcasted_iota(jnp.int32, (tm, LANES), 1)
    lane_f = lane.astype(F32)
    vals, ids, hots = [], [], []
    for _ in range(TOP_K):
        m = jnp.max(logits, axis=-1, keepdims=True)
        ix = jnp.min(jnp.where(logits == m, lane_f, float(LANES)), axis=-1, keepdims=True)
        hot = lane_f == ix
        vals.append(m)
        ids.append(ix)
        hots.append(hot)
        logits = jnp.where(hot, -jnp.inf, logits)
    es = [jnp.exp(v - vals[0]) for v in vals]
    den = es[0] + es[1] + es[2] + es[3]
    chosen = jnp.zeros((tm, LANES), F32)
    for hot in hots:
        chosen = chosen + jnp.where(hot, 1.0, 0.0)
    before = (lax.broadcasted_iota(jnp.int32, (tm, tm), 1) < lax.broadcasted_iota(jnp.int32, (tm, tm), 0))
    rank = _dot(jnp.where(before, 1.0, 0.0), chosen) + carry[...]
    info = jnp.zeros((tm, LANES), F32)
    for k in range(TOP_K):
        pos = jnp.sum(jnp.where(hots[k], rank, 0.0), axis=-1, keepdims=True)
        info = jnp.where(lane == k, es[k] / den, info)
        info = jnp.where(lane == TOP_K + k, ids[k].astype(F32), info)
        info = jnp.where(lane == 2 * TOP_K + k, pos, info)
    info_ref[...] = info
    token = pl.program_id(0) * tm + lax.broadcasted_iota(jnp.int32, (tm, 1), 0)
    tag = jnp.zeros((tm, LANES), F32)
    for d in range(TAG_DIGITS):
        tag = jnp.where(lane == d, ((token >> (8 * d)) & 255).astype(F32), tag)
    for k in range(TOP_K):
        tag = jnp.where(lane == TAG_DIGITS + k, ids[k].astype(F32), tag)
    tag_row = jnp.concatenate([tag, jnp.zeros((tm, D_MODEL - LANES), F32)], axis=1).astype(xn_ref.dtype)
    xn_ref[:, ROW_TILES:, :] = tag_row.reshape(tm, ROW_TILES, LANES)
    carry[...] = carry[...] + jnp.sum(chosen, axis=0, keepdims=True)
    cnt_ref[...] = carry[...]


_ROUTE_OUT_SHAPES = lambda n: (jax.ShapeDtypeStruct((n, D_MODEL), F32), jax.ShapeDtypeStruct((n, TAGGED_ROWS, LANES), MXU_DTYPE),
                               jax.ShapeDtypeStruct((n, LANES), F32), jax.ShapeDtypeStruct((1, LANES), F32))
_ROUTE_OUT_SPECS = (pl.BlockSpec((TOKEN_TILE, D_MODEL), lambda i: (i, 0)),
                    pl.BlockSpec((TOKEN_TILE, TAGGED_ROWS, LANES), lambda i: (i, 0, 0)),
                    pl.BlockSpec((TOKEN_TILE, LANES), lambda i: (i, 0)),
                    pl.BlockSpec((1, LANES), lambda i: (0, 0)))


def _route_in_specs():
    return [_full((1, D_MODEL)), _full((D_MODEL, LANES)), _full((D_MODEL, LANES)), _full((1, LANES))]


def _even_out_kernel(ypp_ref, yps_ref, op_ref, os_ref, r_ref, hp_ref, hs_ref, gn_ref, w_ref, nffn_ref, wr_hi_ref,
                     wr_lo_ref, br_ref, h1_ref, xn_ref, info_ref, cnt_ref, carry, *, npt):
    o = _split_rows(op_ref, os_ref, npt)
    r = r_ref[...]
    parts = []
    for hd in range(GLA_HEADS):
        sl = slice(hd * GLA_DV, (hd + 1) * GLA_DV)
        oh = o[:, sl]
        oh = oh * lax.rsqrt(jnp.mean(oh * oh, axis=-1, keepdims=True) + RMS_EPS) * gn_ref[...]
        rh = r[:, sl]
        parts.append(oh * (rh * jax.nn.sigmoid(rh)))
    gla = jnp.concatenate(parts, axis=1)
    mix = _dot(_split_rows(ypp_ref, yps_ref, npt), w_ref[:POOL_WIDTH, :]) + _dot(gla, w_ref[POOL_WIDTH:, :])
    h1 = _split_rows(hp_ref, hs_ref, npt) + mix
    h1_ref[...] = h1
    _route(h1, nffn_ref, wr_hi_ref, wr_lo_ref, br_ref, xn_ref, info_ref, cnt_ref, carry)


def _even_out(y_pool_parts, o_gla_parts, proj, h_parts, n_p, gla_norm, w_out, nffn, wr_hi, wr_lo, br):
    n = proj.shape[0]
    return pl.pallas_call(
        functools.partial(_even_out_kernel, npt=n_p // TOKEN_TILE),
        out_shape=_ROUTE_OUT_SHAPES(n),
        grid=(n // TOKEN_TILE,),
        in_specs=_split_specs(TOKEN_TILE, POOL_WIDTH, n_p, False) + _split_specs(TOKEN_TILE, GLA_WIDTH, n_p, False)
        + [pl.BlockSpec((TOKEN_TILE, GLA_WIDTH), lambda i: (i, (IN_EVEN_MAIN - GLA_WIDTH) // GLA_WIDTH))]
        + _split_specs(TOKEN_TILE, D_MODEL, n_p, h_parts[0] is h_parts[1])
        + [_full((1, GLA_DV)), _full((D_MODEL, D_MODEL))] + _route_in_specs(),
        out_specs=_ROUTE_OUT_SPECS,
        scratch_shapes=[pltpu.VMEM((1, LANES), F32)],
        compiler_params=_params(), name="even_out")(
            *y_pool_parts, *o_gla_parts, proj, *h_parts, gla_norm, w_out, nffn, wr_hi, wr_lo, br)


def _rope_tile(x, cos, sin, lo_half):
    swapped = jnp.where(lo_half, pltpu.roll(x, LANES - HEAD_DIM // 2, axis=1), pltpu.roll(x, HEAD_DIM // 2, axis=1))
    return x * cos + swapped * sin


def _odd_in_kernel(h_ref, g_ref, w_ref, b_ref, cos_ref, sin_ref, q_ref, kv_ref):
    xn = _rms(h_ref[...], g_ref[...])
    qkv = _dot(xn, w_ref[...]) + b_ref[...]
    cos = cos_ref[...]
    sin = sin_ref[...]
    lo_half = (lax.broadcasted_iota(jnp.int32, cos.shape, 1) % HEAD_DIM) < HEAD_DIM // 2
    for j in range(Q_WIDTH // LANES):
        sl = slice(j * LANES, (j + 1) * LANES)
        q_ref[:, sl] = _rope_tile(qkv[:, sl], cos, sin, lo_half).astype(q_ref.dtype)
    kv_ref[:, :KV_WIDTH] = _rope_tile(qkv[:, Q_WIDTH:Q_WIDTH + KV_WIDTH], cos, sin, lo_half)
    kv_ref[:, KV_WIDTH:] = qkv[:, Q_WIDTH + KV_WIDTH:]


def _odd_in(h, g, w_qkv, b_qkv, cos, sin, table_block):
    n = h.shape[0]
    wq = w_qkv.shape[1]
    tab = pl.BlockSpec((TOKEN_TILE, LANES), lambda i: (table_block(i), 0))
    return pl.pallas_call(
        _odd_in_kernel,
        out_shape=(jax.ShapeDtypeStruct((n, Q_WIDTH), MXU_DTYPE), jax.ShapeDtypeStruct((n, 2 * KV_WIDTH), F32)),
        grid=(n // TOKEN_TILE,),
        in_specs=[pl.BlockSpec((TOKEN_TILE, D_MODEL), lambda i: (i, 0)), _full((1, D_MODEL)),
                  _full((D_MODEL, wq)), _full((1, wq)), tab, tab],
        out_specs=(pl.BlockSpec((TOKEN_TILE, Q_WIDTH), lambda i: (i, 0)),
                   pl.BlockSpec((TOKEN_TILE, 2 * KV_WIDTH), lambda i: (i, 0))),
        compiler_params=_params(), name="odd_in")(h, g, w_qkv, b_qkv, cos, sin)


def _attn_kernel(sink_ref, q_ref, kv0_ref, kv1_ref, kv2_ref, ck0_ref, ck1_ref, cv0_ref, cv1_ref, o_ref, *, sq):
    i = pl.program_id(0)
    smp = sq.is_sample(i)
    t = sq.tile_in_seq(i)
    kv0 = kv0_ref[...]
    k_old = jnp.where(smp, ck0_ref[...], kv2_ref[:, :KV_WIDTH])
    k_mid = jnp.where(smp, ck1_ref[...], kv1_ref[:, :KV_WIDTH])
    v_old = jnp.where(smp, cv0_ref[...], kv2_ref[:, KV_WIDTH:])
    v_mid = jnp.where(smp, cv1_ref[...], kv1_ref[:, KV_WIDTH:])
    pad = jnp.zeros((CHUNK, KV_WIDTH), F32)
    keys = jnp.concatenate([k_old, k_mid, kv0[:, :KV_WIDTH], pad], axis=0)
    vals = jnp.concatenate([v_old, v_mid, kv0[:, KV_WIDTH:], pad], axis=0)
    n_keys = 4 * CHUNK
    lane = lax.broadcasted_iota(jnp.int32, (n_keys, KV_WIDTH), 1)
    lo = lane < HEAD_DIM
    keys_sw = pltpu.roll(keys, HEAD_DIM, axis=1)
    vals_sw = pltpu.roll(vals, HEAD_DIM, axis=1)
    kcol = lax.broadcasted_iota(jnp.int32, (1, n_keys), 1)
    first_valid = jnp.where(smp, 0, (2 - jnp.minimum(t, 2)) * CHUNK)
    key_ok = jnp.logical_and(kcol >= first_valid, kcol < 3 * CHUNK)
    n_pairs = N_Q_HEADS // N_KV_HEADS // 2
    pair_of_row = lax.broadcasted_iota(jnp.int32, (n_pairs * CHUNK, 1), 0) // CHUNK
    for g in range(N_KV_HEADS):
        own, other = (keys, keys_sw) if g == 0 else (keys_sw, keys)
        vown, vother = (vals, vals_sw) if g == 0 else (vals_sw, vals)
        kb = jnp.concatenate([jnp.where(lo, own, 0.0), jnp.where(lo, 0.0, other)], axis=0).astype(MXU_DTYPE)
        vb = jnp.concatenate([jnp.where(lo, vown, 0.0), jnp.where(lo, 0.0, vother)], axis=0).astype(MXU_DTYPE)
        pairs = [slice((g * n_pairs + pr) * LANES, (g * n_pairs + pr + 1) * LANES) for pr in range(n_pairs)]
        qg = jnp.concatenate([q_ref[:, sl] for sl in pairs], axis=0)
        s = _dot_nt(qg, kb) * (HEAD_DIM ** -0.5)
        halves = []
        for half in range(2):
            sh = jnp.where(key_ok, s[:, half * n_keys:(half + 1) * n_keys], NEG_INF)
            sink = jnp.zeros((n_pairs * CHUNK, 1), F32)
            for pr in range(n_pairs):
                sink = jnp.where(pair_of_row == pr, sink_ref[2 * (g * n_pairs + pr) + half], sink)
            m = jnp.maximum(jnp.max(sh, axis=-1, keepdims=True), sink)
            p = jnp.exp(sh - m)
            halves.append(p * (1.0 / (jnp.sum(p, axis=-1, keepdims=True) + jnp.exp(sink - m))))
        o = _dot(jnp.concatenate(halves, axis=1), vb)
        for pr, sl in enumerate(pairs):
            o_ref[:, sl] = o[pr * CHUNK:(pr + 1) * CHUNK].astype(o_ref.dtype)


def _attention(sinks, q, kv, ck, cv, sq):
    n = q.shape[0]
    npt = sq.n_prompt_tiles
    prev = lambda d: (lambda i, s: (jnp.where(sq.is_sample(i), i, jnp.maximum(i - d, 0)), 0))
    cache = lambda d: (lambda i, s: (jnp.where(sq.is_sample(i), 2 * (i - npt) + d, 0), 0))
    kvspec = lambda f: pl.BlockSpec((CHUNK, 2 * KV_WIDTH), f)
    cspec = lambda f: pl.BlockSpec((CHUNK, KV_WIDTH), f)
    grid_spec = pltpu.PrefetchScalarGridSpec(
        num_scalar_prefetch=1, grid=(sq.n_tiles,),
        in_specs=[pl.BlockSpec((CHUNK, Q_WIDTH), lambda i, s: (i, 0)),
                  kvspec(prev(0)), kvspec(prev(1)), kvspec(prev(2)),
                  cspec(cache(0)), cspec(cache(1)), cspec(cache(0)), cspec(cache(1))],
        out_specs=pl.BlockSpec((CHUNK, Q_WIDTH), lambda i, s: (i, 0)))
    return pl.pallas_call(
        functools.partial(_attn_kernel, sq=sq),
        out_shape=jax.ShapeDtypeStruct((n, Q_WIDTH), MXU_DTYPE),
        grid_spec=grid_spec, compiler_params=_params(), name="attention")(sinks, q, kv, kv, kv, ck, ck, cv, cv)


def _odd_out_kernel(o_ref, h_ref, w_ref, b_ref, nffn_ref, wr_hi_ref, wr_lo_ref, br_ref,
                    h1_ref, xn_ref, info_ref, cnt_ref, carry):
    h1 = h_ref[...] + _dot(o_ref[...], w_ref[...]) + b_ref[...]
    h1_ref[...] = h1
    _route(h1, nffn_ref, wr_hi_ref, wr_lo_ref, br_ref, xn_ref, info_ref, cnt_ref, carry)


def _odd_out(o, h, w_out, b_out, nffn, wr_hi, wr_lo, br):
    n = h.shape[0]
    row = pl.BlockSpec((TOKEN_TILE, D_MODEL), lambda i: (i, 0))
    return pl.pallas_call(
        _odd_out_kernel,
        out_shape=_ROUTE_OUT_SHAPES(n),
        grid=(n // TOKEN_TILE,),
        in_specs=[row, row, _full((D_MODEL, D_MODEL)), _full((1, D_MODEL))] + _route_in_specs(),
        out_specs=_ROUTE_OUT_SPECS,
        scratch_shapes=[pltpu.VMEM((1, LANES), F32)],
        compiler_params=_params(), name="odd_out")(o, h, w_out, b_out, nffn, wr_hi, wr_lo, br)


def _token_copy(src, s, dst, d, sem, rows=ROW_TILES):
    return pltpu.make_async_copy(src.at[pl.ds(pl.multiple_of(s, rows), rows)],
                                 dst.at[pl.ds(pl.multiple_of(d, rows), rows)], sem)


ZERO_TOKENS = EXPERT_TILE // 2
HIDDEN_BLOCK = 512


def _zero_fill(fill_ref, zeros, xs_out, zsem):
    zeros[...] = jnp.zeros_like(zeros)

    def sweep(wait):
        def go(cp):
            cp.wait() if wait else cp.start()

        def tail(e, c):
            off, length = fill_ref[e], fill_ref[N_EXPERTS + e]
            for bit in range(EXPERT_TILE.bit_length() - 1):
                rows = (1 << bit) * TAGGED_ROWS

                @pl.when((length >> bit) & 1 == 1)
                def _():
                    o = pl.multiple_of(off + (length & ((1 << bit) - 1)) * TAGGED_ROWS, TAGGED_ROWS)
                    go(pltpu.make_async_copy(zeros.at[pl.ds(0, rows)], xs_out.at[pl.ds(o, rows)], zsem))
            return c

        lax.fori_loop(0, N_EXPERTS, tail, 0)

        def unused(t, c):
            o = pl.multiple_of(fill_ref[2 * N_EXPERTS] + t * ZERO_TOKENS * TAGGED_ROWS, TAGGED_ROWS)
            go(pltpu.make_async_copy(zeros, xs_out.at[pl.ds(o, ZERO_TOKENS * TAGGED_ROWS)], zsem))
            return c

        lax.fori_loop(0, fill_ref[2 * N_EXPERTS + 1], unused, 0)

    sweep(wait=False)
    sweep(wait=True)


def _dispatch_kernel(fill_ref, dest_hbm, x_ref, xs_out, dest_smem, zeros, sem, dsems, zsem):
    i = pl.program_id(0)
    n_slots = TOKEN_TILE * TOP_K

    @pl.when(i == 0)
    def _():
        _zero_fill(fill_ref, zeros, xs_out, zsem)

    slot = i % 2

    def dest_rows(step, s):
        return pltpu.make_async_copy(dest_hbm.at[pl.ds(step * n_slots, n_slots)], dest_smem.at[s], dsems.at[s])

    @pl.when(i == 0)
    def _():
        dest_rows(0, 0).start()

    dest_rows(i, slot).wait()

    @pl.when(i + 1 < pl.num_programs(0))
    def _():
        dest_rows(i + 1, 1 - slot).start()

    def issue(t, c):
        for k in range(TOP_K):
            _token_copy(x_ref, t * TAGGED_ROWS, xs_out, dest_smem[slot, t * TOP_K + k], sem, TAGGED_ROWS).start()
        return c

    lax.fori_loop(0, TOKEN_TILE, issue, 0, unroll=2)
    whole = xs_out.at[pl.ds(0, n_slots * TAGGED_ROWS)]
    pltpu.make_async_copy(whole, whole, sem).wait()


def _dispatch(fill, dest, xt, n_rows):
    n = xt.shape[0] // TAGGED_ROWS
    grid_spec = pltpu.PrefetchScalarGridSpec(
        num_scalar_prefetch=1, grid=(n // TOKEN_TILE,),
        in_specs=[pl.BlockSpec(memory_space=pl.ANY),
                  pl.BlockSpec((TOKEN_TILE * TAGGED_ROWS, LANES), lambda i, f: (i, 0))],
        out_specs=pl.BlockSpec(memory_space=pl.ANY),
        scratch_shapes=[pltpu.SMEM((2, TOKEN_TILE * TOP_K), jnp.int32), pltpu.VMEM((ZERO_TOKENS * TAGGED_ROWS, LANES), xt.dtype),
                        pltpu.SemaphoreType.DMA, pltpu.SemaphoreType.DMA((2,)), pltpu.SemaphoreType.DMA])
    return pl.pallas_call(
        _dispatch_kernel, out_shape=jax.ShapeDtypeStruct((n_rows * TAGGED_ROWS, LANES), xt.dtype),
        grid_spec=grid_spec, compiler_params=_params(), name="moe_dispatch")(fill, dest, xt)


def _experts_kernel(te_ref, tr_ref, x_ref, wgu_ref, bgu_ref, wd_ref, bd_ref, yt_ref,
                    wgu_mxu, wd_mxu, ybuf, place_vmem, place_smem, ysems, psem, *, n_tokens):
    i = pl.program_id(0)
    slot = i % 2
    prev = 1 - slot
    tile_rows = EXPERT_TILE * ROW_TILES
    spare = yt_ref.shape[0] - 2 * tile_rows
    col = lax.broadcasted_iota(jnp.int32, (1, EXPERT_TILE), 1)

    def rows_done(s):
        whole = yt_ref.at[pl.ds(0, tile_rows)]
        pltpu.make_async_copy(whole, whole, ysems.at[s]).wait()

    def places_to_smem(first_rows, s):
        place_vmem[...] = jnp.broadcast_to(first_rows, place_vmem.shape)
        return pltpu.make_async_copy(place_vmem, place_smem.at[s], psem)

    def start_row_copy(j, s):
        _token_copy(ybuf.at[s], j * ROW_TILES, yt_ref, place_smem[s, 0, j], ysems.at[s]).start()

    @pl.when(i == 0)
    def _():
        ybuf[...] = jnp.zeros(ybuf.shape, F32)
        for s in (1, 0):
            cp = pltpu.make_async_copy(ybuf.at[0], yt_ref.at[pl.ds(spare + s * tile_rows, tile_rows)], ysems.at[s])
            cp.start()
            if s == 1:
                cp.wait()
        cp = places_to_smem(spare + tile_rows + col * ROW_TILES, 1)
        cp.start()
        cp.wait()

    @pl.when(jnp.logical_or(i == 0, te_ref[i] != te_ref[jnp.maximum(i - 1, 0)]))
    def _():
        wgu_mxu[...] = wgu_ref[0, 0].astype(MXU_DTYPE)
        wd_mxu[...] = wd_ref[0, 0].astype(MXU_DTYPE)

    n_valid = tr_ref[i]
    after_last = jnp.logical_and(n_valid == 0, jnp.logical_and(i >= 1, tr_ref[jnp.maximum(i - 1, 0)] > 0))

    @pl.when(n_valid > 0)
    def _():
        rows_done(slot)
        for j in range(EXPERT_TILE):
            start_row_copy(j, prev)
        wide = x_ref[...].reshape(EXPERT_TILE, TAGGED_ROWS * LANES)
        tag = wide[:, D_MODEL:D_MODEL + LANES].astype(F32)
        lane = lax.broadcasted_iota(jnp.int32, (EXPERT_TILE, LANES), 1)
        is_id = jnp.logical_and(lane >= TAG_DIGITS, lane < TAG_DIGITS + TOP_K)
        mine = jnp.logical_and(is_id, tag == te_ref[i].astype(F32))
        weight = jnp.where(lane == 0, 1.0, jnp.where(lane == 1, 256.0, 65536.0))
        terms = jnp.where(lane < TAG_DIGITS, tag * weight, jnp.where(mine, ((lane - TAG_DIGITS) * n_tokens).astype(F32), 0.0))
        place = jnp.sum(terms, axis=-1, keepdims=True)
        hi = jnp.floor(place * (1.0 / 65536.0))
        mid = jnp.floor((place - hi * 65536.0) * (1.0 / 256.0))
        digits = jnp.where(lane == 0, place - hi * 65536.0 - mid * 256.0, jnp.where(lane == 1, mid, jnp.where(lane == 2, hi, 0.0)))
        pick = jnp.where(lax.broadcasted_iota(jnp.int32, (ROW_TILES, LANES), 0)
                         == lax.broadcasted_iota(jnp.int32, (ROW_TILES, LANES), 1), 1.0, 0.0)
        planes = _dot_nt(pick, digits)
        place_row = (planes[0:1] + 256.0 * planes[1:2] + 65536.0 * planes[2:3]).astype(jnp.int32) * ROW_TILES
        to_smem = places_to_smem(jnp.where(col < n_valid, place_row, spare + slot * tile_rows + col * ROW_TILES), slot)
        to_smem.start()

        x = wide[:, :D_MODEL]
        out = None
        for c in range(D_MODEL // HIDDEN_BLOCK):
            cols = slice(c * HIDDEN_BLOCK, (c + 1) * HIDDEN_BLOCK)
            ucols = slice(D_MODEL + c * HIDDEN_BLOCK, D_MODEL + (c + 1) * HIDDEN_BLOCK)
            gate = jnp.minimum(_dot(x, wgu_mxu[:, cols]) + bgu_ref[0, 0, :, cols], SWIGLU_LIMIT)
            up = jnp.clip(_dot(x, wgu_mxu[:, ucols]) + bgu_ref[0, 0, :, ucols], -SWIGLU_LIMIT, SWIGLU_LIMIT)
            act = (up + 1.0) * gate * jax.nn.sigmoid(SWIGLU_ALPHA * gate)
            part = _dot(act, wd_mxu[cols, :])
            out = part if out is None else out + part
        _store_token_tiles(ybuf.at[slot], out + bd_ref[0, 0])
        to_smem.wait()

    @pl.when(after_last)
    def _():
        rows_done(slot)
        def issue(j, c):
            start_row_copy(j, prev)
            return c

        lax.fori_loop(0, EXPERT_TILE, issue, 0, unroll=8)
        rows_done(prev)


def _experts(layer, tile_expert, tile_rows, xs, wgu, bgu, wd, bd, n_tokens):
    rows = xs.shape[0]
    w = lambda shape: pl.BlockSpec((1, 1) + shape, lambda i, te, tr: (layer, te[i], 0, 0))
    grid_spec = pltpu.PrefetchScalarGridSpec(
        num_scalar_prefetch=2, grid=(rows // EXPERT_TILE,),
        in_specs=[pl.BlockSpec((EXPERT_TILE, TAGGED_ROWS, LANES), lambda i, te, tr: (i, 0, 0)),
                  w((D_MODEL, 2 * D_MODEL)), w((1, 2 * D_MODEL)), w((D_MODEL, D_MODEL)), w((1, D_MODEL))],
        out_specs=pl.BlockSpec(memory_space=pl.ANY),
        scratch_shapes=[pltpu.VMEM((D_MODEL, 2 * D_MODEL), MXU_DTYPE), pltpu.VMEM((D_MODEL, D_MODEL), MXU_DTYPE),
                        pltpu.VMEM((2, EXPERT_TILE * ROW_TILES, LANES), F32),
                        pltpu.VMEM((ROW_TILES, EXPERT_TILE), jnp.int32), pltpu.SMEM((2, ROW_TILES, EXPERT_TILE), jnp.int32),
                        pltpu.SemaphoreType.DMA((2,)), pltpu.SemaphoreType.DMA])
    yt_rows = (n_tokens * TOP_K + 2 * EXPERT_TILE) * ROW_TILES
    return pl.pallas_call(
        functools.partial(_experts_kernel, n_tokens=n_tokens), out_shape=jax.ShapeDtypeStruct((yt_rows, LANES), F32),
        grid_spec=grid_spec, compiler_params=_params(), name="moe_experts")(
            tile_expert, tile_rows, xs, wgu, bgu, wd, bd)


def _combine_kernel(*refs, final, npt):
    y_refs, outs = refs[:TOP_K], refs[TOP_K + 7:]
    info_ref, h_ref, p_prompt_ref, p_sample_ref, pp_ref, pg_ref, nf_ref = refs[TOP_K:TOP_K + 7]
    i = pl.program_id(0)
    gates = info_ref[...]
    moe = None
    for k in range(TOP_K):
        rows = _load_token_tiles(y_refs[k], COMBINE_TILE)
        moe = gates[:, k:k + 1] * rows if moe is None else moe + gates[:, k:k + 1] * rows
    h2 = h_ref[...] + moe
    p = jnp.where(i < npt, p_prompt_ref[0], p_sample_ref[0])
    h3 = h2 + jax.nn.sigmoid(_dot(h2, pg_ref[...])) * _dot(p, pp_ref[...])
    if not final:
        outs[0][...] = h3
        return
    y = _rms(h3, nf_ref[...])

    @pl.when(i < npt)
    def _():
        outs[0][...] = y

    @pl.when(i >= npt)
    def _():
        outs[1][...] = y


def _combine(layer, yt, info, h1, p_prompt, p_sample, ple_proj, ple_gate, norm_final, final):
    n = h1.shape[0]
    n_p = p_prompt.shape[1]
    npt = n_p // COMBINE_TILE
    steps = n // COMBINE_TILE
    row = lambda w: pl.BlockSpec((COMBINE_TILE, w), lambda i: (i, 0))
    rank_rows = lambda k: pl.BlockSpec((COMBINE_TILE * ROW_TILES, LANES), lambda i: (k * steps + i, 0))
    prompt_rows = lambda i: jnp.minimum(i, npt - 1)
    sample_rows = lambda i: jnp.maximum(i - npt, 0)
    if final:
        out_shape = (jax.ShapeDtypeStruct((n_p, D_MODEL), F32), jax.ShapeDtypeStruct((n - n_p, D_MODEL), F32))
        out_specs = (pl.BlockSpec((COMBINE_TILE, D_MODEL), lambda i: (prompt_rows(i), 0)),
                     pl.BlockSpec((COMBINE_TILE, D_MODEL), lambda i: (sample_rows(i), 0)))
    else:
        out_shape, out_specs = jax.ShapeDtypeStruct((n, D_MODEL), F32), row(D_MODEL)
    return pl.pallas_call(
        functools.partial(_combine_kernel, final=final, npt=npt),
        out_shape=out_shape,
        grid=(steps,),
        in_specs=[rank_rows(k) for k in range(TOP_K)] + [
            row(LANES), row(D_MODEL),
            pl.BlockSpec((1, COMBINE_TILE, PLE_DIM), lambda i: (layer, prompt_rows(i), 0)),
            pl.BlockSpec((1, COMBINE_TILE, PLE_DIM), lambda i: (layer, sample_rows(i), 0)),
            _full((PLE_DIM, D_MODEL)), _full((D_MODEL, D_MODEL)), _full((1, D_MODEL))],
        out_specs=out_specs,
        compiler_params=_params(), name="moe_combine")(
            *([yt] * TOP_K), info, h1, p_prompt, p_sample, ple_proj, ple_gate, norm_final)


def _moe_and_embed(layer, h1, xt, info, counts, p_prompt, p_sample, wgu, bgu, wd, bd, ple_proj, ple_gate,
                   norm_final, final):
    n = h1.shape[0]
    n_tiles = (n * TOP_K + N_EXPERTS * (EXPERT_TILE - 1)) // EXPERT_TILE + 1
    ids = info[:, TOP_K:2 * TOP_K].astype(jnp.int32)
    rank = info[:, 2 * TOP_K:3 * TOP_K].astype(jnp.int32)
    cnt = counts[0, :N_EXPERTS].astype(jnp.int32)
    padded = ((cnt + EXPERT_TILE - 1) // EXPERT_TILE) * EXPERT_TILE
    ends = jnp.cumsum(padded)
    starts = ends - padded
    dest = ((starts[ids] + rank) * TAGGED_ROWS).reshape(-1)
    tile_start = jnp.arange(n_tiles, dtype=jnp.int32) * EXPERT_TILE
    tile_expert = jnp.minimum(jnp.sum((tile_start[:, None] >= ends[None, :]).astype(jnp.int32), axis=1), N_EXPERTS - 1)
    tile_rows = jnp.clip(cnt[tile_expert] - (tile_start - starts[tile_expert]), 0, EXPERT_TILE)
    tile_rows = jnp.where(tile_start < ends[-1], tile_rows, 0).astype(jnp.int32)
    fill = jnp.concatenate([(starts + cnt) * TAGGED_ROWS, padded - cnt,
                            jnp.stack([ends[-1] * TAGGED_ROWS,
                                       (n_tiles * EXPERT_TILE - ends[-1]) // ZERO_TOKENS])]).astype(jnp.int32)
    xs = _dispatch(fill, dest, xt.reshape(n * TAGGED_ROWS, LANES), n_tiles * EXPERT_TILE)
    yt = _experts(layer, tile_expert, tile_rows, xs.reshape(-1, TAGGED_ROWS, LANES), wgu, bgu, wd, bd, n)
    return _combine(layer, yt, info, h1, p_prompt, p_sample, ple_proj, ple_gate, norm_final, final)


def _rope_tables(prompt_len, n_sample_seq, sample_len):
    half = HEAD_DIM // 2
    inv = jnp.power(jnp.float32(ROPE_THETA), -jnp.arange(half, dtype=F32) / half)
    pos = jnp.concatenate([jnp.arange(prompt_len), jnp.tile(PAST_LEN + jnp.arange(sample_len), n_sample_seq)])
    ang = pos.astype(F32)[:, None] * inv[None, :]
    cos = jnp.tile(jnp.cos(ang), (1, LANES // half))
    sin = jnp.tile(jnp.concatenate([-jnp.sin(ang), jnp.sin(ang)], axis=1), (1, LANES // HEAD_DIM))
    return cos, sin


def kernel(x_prompt, x_sample, state_pool, state_gla, cache_k, cache_v, p_prompt, p_sample, norm_mix, norm_ffn, norm_final, w_in_even, pool_w, pool_scale, gla_w_gate, gla_b_gate, gla_norm, w_out_even, w_qkv_odd, b_qkv_odd, attn_sinks, w_out_odd, b_out_odd, w_router, b_router, w_gate_up, b_gate_up, w_down, b_down, ple_proj, ple_gate):
    bsz, t_len, _ = x_prompt.shape
    dec_bsz, dec_len, _ = x_sample.shape
    depth = norm_mix.shape[0]
    n_p, n_s = bsz * t_len, dec_bsz * dec_len
    n = n_p + n_s
    assert dec_len == CHUNK and n_s == TOKEN_TILE and t_len % TOKEN_TILE == 0 and cache_k.shape[2] == WINDOW
    sq = _Seq(bsz, t_len, dec_bsz)
    grp_p = _Group(bsz, t_len, min(SEQ_TILE, t_len), 0, 0)
    grp_s = _Group(dec_bsz, dec_len, dec_len, n_p, PAST_LEN)
    bf = lambda a: a.astype(MXU_DTYPE)
    row = lambda a: a.reshape(1, -1)

    h_parts = (x_prompt.reshape(n_p, D_MODEL), x_sample.reshape(n_s, D_MODEL))
    p_parts = (p_prompt.reshape(depth, n_p, PLE_DIM), p_sample.reshape(depth, n_s, PLE_DIM))
    b_gu = b_gate_up.reshape(depth, N_EXPERTS, 1, 2 * D_MODEL)
    b_dn = b_down.reshape(depth, N_EXPERTS, 1, D_MODEL)
    cos, sin = _rope_tables(t_len, dec_bsz, dec_len)
    tiles_per_seq = t_len // TOKEN_TILE
    table_block = lambda i: jnp.where(i < bsz * tiles_per_seq, i % tiles_per_seq, tiles_per_seq)

    pools, glas, new_k, new_v = [], [], [], []
    for i in range(depth):
        if i % 2 == 0:
            e = i // 2
            w_in = jnp.pad(bf(w_in_even[e]), ((0, 0), (0, (-IN_EVEN) % LANES)))
            wg = jnp.pad(bf(gla_w_gate[e]), ((0, w_in.shape[1] - IN_EVEN_MAIN - GLA_RANK), (0, 0)))
            proj = _even_in(h_parts, n, n_p, row(norm_mix[i]), w_in, wg, row(gla_b_gate[e]))
            pw, ps = bf(pool_w[e]), row(pool_scale[e])
            y_pool_p, pool_p = _pool(proj, jnp.zeros((bsz, 16, POOL_WIDTH), F32), pw, ps, grp_p)
            y_pool_s, pool_s = _pool(proj, jnp.pad(state_pool[e], ((0, 0), (1, 0), (0, 0))), pw, ps, grp_s)
            o_gla_p, gla_p = _gla(proj, jnp.zeros((bsz,) + state_gla.shape[2:], F32), grp_p)
            o_gla_s, gla_s = _gla(proj, state_gla[e], grp_s)
            pools.append((pool_p[:, 1:], pool_s[:, 1:]))
            glas.append((gla_p, gla_s))
            wr = jnp.pad(w_router[i], ((0, 0), (0, LANES - N_EXPERTS)))
            wr_hi = bf(wr)
            wr_lo = bf(wr - wr_hi.astype(F32))
            br = jnp.pad(row(b_router[i]), ((0, 0), (0, LANES - N_EXPERTS)), constant_values=NEG_INF)
            h1, xn, info, counts = _even_out((y_pool_p, y_pool_s), (o_gla_p, o_gla_s), proj, h_parts, n_p,
                                             row(gla_norm[e]), bf(w_out_even[e]), row(norm_ffn[i]), wr_hi, wr_lo, br)
        else:
            o = i // 2
            h = h_parts[0]
            q, kv = _odd_in(h, row(norm_mix[i]), bf(w_qkv_odd[o]), row(b_qkv_odd[o]), cos, sin, table_block)
            ck = cache_k[o].reshape(dec_bsz * WINDOW, KV_WIDTH)
            cv = cache_v[o].reshape(dec_bsz * WINDOW, KV_WIDTH)
            att = _attention(attn_sinks[o], q, kv, ck, cv, sq)
            kv_p = kv[:n_p].reshape(bsz, t_len, 2 * KV_WIDTH)[:, -WINDOW:]
            kv_s = kv[n_p:].reshape(dec_bsz, dec_len, 2 * KV_WIDTH)
            hd = (N_KV_HEADS, HEAD_DIM)
            new_k.append((kv_p[..., :KV_WIDTH].reshape(bsz, WINDOW, *hd),
                          jnp.concatenate([cache_k[o], kv_s[..., :KV_WIDTH].reshape(dec_bsz, dec_len, *hd)], axis=1)[:, -WINDOW:]))
            new_v.append((kv_p[..., KV_WIDTH:].reshape(bsz, WINDOW, *hd),
                          jnp.concatenate([cache_v[o], kv_s[..., KV_WIDTH:].reshape(dec_bsz, dec_len, *hd)], axis=1)[:, -WINDOW:]))
            wr = jnp.pad(w_router[i], ((0, 0), (0, LANES - N_EXPERTS)))
            wr_hi = bf(wr)
            wr_lo = bf(wr - wr_hi.astype(F32))
            br = jnp.pad(row(b_router[i]), ((0, 0), (0, LANES - N_EXPERTS)), constant_values=NEG_INF)
            h1, xn, info, counts = _odd_out(att, h, bf(w_out_odd[o]), row(b_out_odd[o]),
                                            row(norm_ffn[i]), wr_hi, wr_lo, br)
        final = i == depth - 1
        out = _moe_and_embed(i, h1, xn, info, counts, *p_parts, w_gate_up, b_gu, w_down, b_dn,
                             bf(ple_proj[i]), bf(ple_gate[i]), row(norm_final), final)
        h_parts = out if final else (out, out)

    y_prompt = h_parts[0].reshape(bsz, t_len, D_MODEL)
    y_sample = h_parts[1].reshape(dec_bsz, dec_len, D_MODEL)
    part = lambda pairs, j: jnp.stack([p[j] for p in pairs])
    return (y_prompt, y_sample, part(pools, 0), part(glas, 0), part(new_k, 0), part(new_v, 0),
            part(pools, 1), part(glas, 1), part(new_k, 1), part(new_v, 1))
```

```python
import functools

import jax
import jax.numpy as jnp
from jax import lax
from jax.experimental import pallas as pl
from jax.experimental.pallas import tpu as pltpu

F32 = jnp.float32
MXU_DTYPE = jnp.bfloat16

V7X_VMEM_BYTES = 64 * 1024 * 1024
VMEM_LIMIT = (V7X_VMEM_BYTES * 7) // 8
LANES = 128

D_MODEL = 1024
CHUNK = 64
PAST_LEN = 2048
PLE_DIM = 256
RMS_EPS = 1e-6
POOL_WINDOWS = (2, 4, 8, 16)
POOL_WIDTH = 512
POOL_GROUP_DIM = 128
POOL_STATE = 15
GLA_HEADS = 4
GLA_DK = 64
GLA_DV = 128
GLA_KEY_WIDTH = GLA_HEADS * GLA_DK
GLA_WIDTH = GLA_HEADS * GLA_DV
GLA_RANK = 16
GLA_TAU = 16.0
GLA_BLOCK = 16
IN_EVEN = POOL_WIDTH + 2 * GLA_KEY_WIDTH + 2 * GLA_WIDTH + GLA_RANK
IN_EVEN_MAIN = IN_EVEN - GLA_RANK
PROJ_WIDTH = IN_EVEN_MAIN + GLA_KEY_WIDTH
N_Q_HEADS = 16
N_KV_HEADS = 2
HEAD_DIM = 64
WINDOW = 128
ROPE_THETA = 10000.0
Q_WIDTH = N_Q_HEADS * HEAD_DIM
KV_WIDTH = N_KV_HEADS * HEAD_DIM
N_EXPERTS = 32
TOP_K = 4
SWIGLU_LIMIT = 7.0
SWIGLU_ALPHA = 1.702
NEG_INF = -1e30

TOKEN_TILE = 512
EXPERT_TILE = 512
COMBINE_TILE = 256


def _dot(a, b):
    return jnp.dot(a.astype(MXU_DTYPE), b.astype(MXU_DTYPE), preferred_element_type=F32)


def _dot_nt(a, b):
    return lax.dot_general(a.astype(MXU_DTYPE), b.astype(MXU_DTYPE), (((1,), (1,)), ((), ())),
                           preferred_element_type=F32)


def _split3(x):
    x1 = x.astype(MXU_DTYPE)
    r1 = x - x1.astype(F32)
    x2 = r1.astype(MXU_DTYPE)
    x3 = (r1 - x2.astype(F32)).astype(MXU_DTYPE)
    return x1, x2, x3


def _rms(x, g):
    return x * lax.rsqrt(jnp.mean(x * x, axis=-1, keepdims=True) + RMS_EPS) * g


ROW_TILES = D_MODEL // LANES
TAGGED_ROWS = 2 * ROW_TILES
TAG_DIGITS = 3


def _store_token_tiles(ref, x, pitch=ROW_TILES):
    for s in range(ROW_TILES):
        ref[pl.ds(s, x.shape[0], stride=pitch), :] = x[:, s * LANES:(s + 1) * LANES]


def _load_token_tiles(ref, rows, pitch=ROW_TILES):
    return jnp.concatenate([ref[pl.ds(s, rows, stride=pitch), :] for s in range(ROW_TILES)], axis=1)


def _params(n_axes=1):
    return pltpu.CompilerParams(dimension_semantics=("arbitrary",) * n_axes, vmem_limit_bytes=VMEM_LIMIT)


def _full(shape):
    return pl.BlockSpec(shape, lambda *_: (0,) * len(shape))


class _Seq:
    def __init__(self, n_prompt_seq, prompt_len, n_sample_seq):
        self.tiles_per_seq = prompt_len // CHUNK
        self.n_prompt_seq = n_prompt_seq
        self.n_prompt_tiles = n_prompt_seq * self.tiles_per_seq
        self.n_tiles = self.n_prompt_tiles + n_sample_seq
        self.n_seq = n_prompt_seq + n_sample_seq

    def is_sample(self, i):
        return i >= self.n_prompt_tiles

    def tile_in_seq(self, i):
        return jnp.where(self.is_sample(i), 0, i % self.tiles_per_seq)

    def seq(self, i):
        return jnp.where(self.is_sample(i), self.n_prompt_seq + i - self.n_prompt_tiles, i // self.tiles_per_seq)

    def last(self, i):
        return jnp.logical_or(self.is_sample(i), i % self.tiles_per_seq == self.tiles_per_seq - 1)

    def pos0(self, i):
        return jnp.where(self.is_sample(i), PAST_LEN, self.tile_in_seq(i) * CHUNK)


class _Group:
    def __init__(self, n_seq, seq_len, tile, row0, pos_base):
        assert seq_len % tile == 0 and row0 % tile == 0
        self.n_seq, self.tile, self.pos_base = n_seq, tile, pos_base
        self.tiles_per_seq = seq_len // tile
        self.n_tiles = n_seq * self.tiles_per_seq
        self.block0 = row0 // tile

    def block(self, i):
        return self.block0 + i

    def seq(self, i):
        return i // self.tiles_per_seq

    def tile_in_seq(self, i):
        return i % self.tiles_per_seq

    def last(self, i):
        return i % self.tiles_per_seq == self.tiles_per_seq - 1

    def pos0(self, i):
        return self.pos_base + self.tile_in_seq(i) * self.tile


SEQ_TILE = 256


def _split_specs(tile, width, n_prompt, same_array):
    npt = n_prompt // tile
    off = npt if same_array else 0
    return [pl.BlockSpec((tile, width), lambda i: (jnp.minimum(i, npt - 1), 0)),
            pl.BlockSpec((tile, width), lambda i: (jnp.maximum(i - npt, 0) + off, 0))]


def _split_rows(prompt_ref, sample_ref, n_prompt_tiles):
    return jnp.where(pl.program_id(0) < n_prompt_tiles, prompt_ref[...], sample_ref[...])


def _even_in_kernel(hp_ref, hs_ref, g_ref, w_ref, wg_ref, bg_ref, out_ref, *, npt):
    xn = _rms(_split_rows(hp_ref, hs_ref, npt), g_ref[...])
    proj = _dot(xn, w_ref[...])
    z = proj[:, IN_EVEN_MAIN:]
    a = _dot(z, wg_ref[...]) + bg_ref[...]
    log_alpha = (jnp.minimum(a, 0.0) - jnp.log1p(jnp.exp(-jnp.abs(a)))) * (1.0 / GLA_TAU)
    out_ref[:, :IN_EVEN_MAIN] = proj[:, :IN_EVEN_MAIN]
    out_ref[:, IN_EVEN_MAIN:] = log_alpha


def _even_in(h_parts, n, n_p, g, w_in, wg, bg):
    wp = w_in.shape[1]
    return pl.pallas_call(
        functools.partial(_even_in_kernel, npt=n_p // TOKEN_TILE),
        out_shape=jax.ShapeDtypeStruct((n, PROJ_WIDTH), F32),
        grid=(n // TOKEN_TILE,),
        in_specs=_split_specs(TOKEN_TILE, D_MODEL, n_p, h_parts[0] is h_parts[1]) + [
            _full((1, D_MODEL)), _full((D_MODEL, wp)), _full((wp - IN_EVEN_MAIN, GLA_KEY_WIDTH)),
            _full((1, GLA_KEY_WIDTH))],
        out_specs=pl.BlockSpec((TOKEN_TILE, PROJ_WIDTH), lambda i: (i, 0)),
        compiler_params=_params(), name="even_in")(*h_parts, g, w_in, wg, bg)


def _pool_kernel(u_ref, init_ref, pw_ref, ps_ref, y_ref, st_ref, buf, *, grp):
    i = pl.program_id(0)
    rows = grp.tile

    @pl.when(grp.tile_in_seq(i) == 0)
    def _():
        buf[0:16, :] = init_ref[0]

    u = u_ref[...]
    buf[16:16 + rows, :] = u
    pos = grp.pos0(i) + lax.broadcasted_iota(jnp.int32, (rows, 1), 0)
    for g, w in enumerate(POOL_WINDOWS):
        sl = slice(g * POOL_GROUP_DIM, (g + 1) * POOL_GROUP_DIM)
        acc = u[:, sl]
        for j in range(1, w):
            acc = acc + buf[16 - j:16 - j + rows, sl]
        cnt = jnp.minimum(w, pos + 1).astype(F32)
        d = acc / cnt - u[:, sl]
        y_ref[:, sl] = _dot(d, pw_ref[g]) * ps_ref[:, sl]
    tail = buf[rows:rows + 16, :]
    st_ref[0] = tail
    buf[0:16, :] = tail


def _pool(proj, init, pool_w, pool_scale, grp):
    return pl.pallas_call(
        functools.partial(_pool_kernel, grp=grp),
        out_shape=(jax.ShapeDtypeStruct((grp.n_tiles * grp.tile, POOL_WIDTH), F32),
                   jax.ShapeDtypeStruct((grp.n_seq, 16, POOL_WIDTH), F32)),
        grid=(grp.n_tiles,),
        in_specs=[pl.BlockSpec((grp.tile, POOL_WIDTH), lambda i: (grp.block(i), 0)),
                  pl.BlockSpec((1, 16, POOL_WIDTH), lambda i: (grp.seq(i), 0, 0)),
                  _full((len(POOL_WINDOWS), POOL_GROUP_DIM, POOL_GROUP_DIM)), _full((1, POOL_WIDTH))],
        out_specs=(pl.BlockSpec((grp.tile, POOL_WIDTH), lambda i: (i, 0)),
                   pl.BlockSpec((1, 16, POOL_WIDTH), lambda i: (grp.seq(i), 0, 0))),
        scratch_shapes=[pltpu.VMEM((grp.tile + 16, POOL_WIDTH), F32)],
        compiler_params=_params(), name="pool")(proj, init, pool_w, pool_scale)


def _gla_kernel(q_ref, k_ref, v_ref, g_ref, s0_ref, o_ref, sout_ref, state, before, *, grp):
    i = pl.program_id(0)
    tile = grp.tile
    n_blk = tile // GLA_BLOCK

    @pl.when(grp.tile_in_seq(i) == 0)
    def _():
        state[...] = s0_ref[0]

    g = g_ref[...]
    row = lax.broadcasted_iota(jnp.int32, (tile, tile), 0)
    col = lax.broadcasted_iota(jnp.int32, (tile, tile), 1)
    same = (row >> 4) == (col >> 4)
    causal = jnp.logical_and(same, col <= row)
    tri = jnp.where(causal, 1.0, 0.0).astype(MXU_DTYPE)
    ones = jnp.where(same, 1.0, 0.0).astype(MXU_DTYPE)
    g1, g2, g3 = _split3(g)
    b = _dot(tri, g1) + _dot(tri, g2) + _dot(tri, g3)
    b_last = _dot(ones, g1) + _dot(ones, g2) + _dot(ones, g3)
    q_t = q_ref[...] * (GLA_DK ** -0.5) * jnp.exp(b)
    k = k_ref[...]
    k_t = k * jnp.exp(-b)
    k_dec_t = (k * jnp.exp(b_last - b)).T
    sel = jnp.where((lax.broadcasted_iota(jnp.int32, (tile, LANES), 0) >> 4)
                    == lax.broadcasted_iota(jnp.int32, (tile, LANES), 1), 1.0, 0.0).astype(MXU_DTYPE)
    t1, t2, t3 = _split3(g.T)
    blk_decay = jnp.exp(_dot(t1, sel) + _dot(t2, sel) + _dot(t3, sel))
    v = v_ref[...]
    stacked = n_blk * GLA_DK
    dk_bits, blk_bits = GLA_DK.bit_length() - 1, GLA_BLOCK.bit_length() - 1
    upd_live = (lax.broadcasted_iota(jnp.int32, (stacked, tile), 0) >> dk_bits
                == lax.broadcasted_iota(jnp.int32, (stacked, tile), 1) >> blk_bits)
    qry_live = (lax.broadcasted_iota(jnp.int32, (tile, stacked), 0) >> blk_bits
                == lax.broadcasted_iota(jnp.int32, (tile, stacked), 1) >> dk_bits)
    low_half = lax.broadcasted_iota(jnp.int32, (tile, LANES), 1) < GLA_DK
    for h in range(GLA_HEADS):
        ks = slice(h * GLA_DK, (h + 1) * GLA_DK)
        vs = slice(h * GLA_DV, (h + 1) * GLA_DV)
        vh = v[:, vs]
        scores = jnp.where(causal, _dot_nt(q_t[:, ks], k_t[:, ks]), 0.0)
        o = _dot(scores, vh)
        upd = _dot(jnp.where(upd_live, jnp.concatenate([k_dec_t[ks, :]] * n_blk, axis=0), 0.0), vh)
        s = state[h]
        for j in range(n_blk):
            rows = slice(j * GLA_DK, (j + 1) * GLA_DK)
            before[rows, :] = s
            s = blk_decay[ks, j:j + 1] * s + upd[rows, :]
        state[h] = s
        pair = q_t[:, (h // 2) * LANES:(h // 2 + 1) * LANES]
        swapped = pltpu.roll(pair, GLA_DK, axis=1)
        both = jnp.where(low_half, pair, swapped) if h % 2 == 0 else jnp.where(low_half, swapped, pair)
        q_exp = jnp.where(qry_live, jnp.concatenate([both] * (stacked // LANES), axis=1), 0.0)
        o_ref[:, vs] = o + _dot(q_exp, before[...])

    @pl.when(grp.last(i))
    def _():
        sout_ref[0] = state[...]


def _gla(proj, s0, grp):
    st_shape = (grp.n_seq, GLA_HEADS, GLA_DK, GLA_DV)
    st_spec = pl.BlockSpec((1, GLA_HEADS, GLA_DK, GLA_DV), lambda i: (grp.seq(i), 0, 0, 0))
    kw = GLA_KEY_WIDTH
    cols = lambda width, c: pl.BlockSpec((grp.tile, width), lambda i: (grp.block(i), c))
    return pl.pallas_call(
        functools.partial(_gla_kernel, grp=grp),
        out_shape=(jax.ShapeDtypeStruct((grp.n_tiles * grp.tile, GLA_WIDTH), F32), jax.ShapeDtypeStruct(st_shape, F32)),
        grid=(grp.n_tiles,),
        in_specs=[cols(kw, POOL_WIDTH // kw), cols(kw, POOL_WIDTH // kw + 1),
                  cols(GLA_WIDTH, (POOL_WIDTH + 2 * kw) // GLA_WIDTH), cols(kw, IN_EVEN_MAIN // kw), st_spec],
        out_specs=(pl.BlockSpec((grp.tile, GLA_WIDTH), lambda i: (i, 0)), st_spec),
        scratch_shapes=[pltpu.VMEM((GLA_HEADS, GLA_DK, GLA_DV), F32),
                        pltpu.VMEM((grp.tile // GLA_BLOCK * GLA_DK, GLA_DV), F32)],
        compiler_params=_params(), name="gla")(proj, proj, proj, proj, s0)


def _route(h1, nffn_ref, wr_hi_ref, wr_lo_ref, br_ref, xn_ref, info_ref, cnt_ref, carry):
    tm = h1.shape[0]

    @pl.when(pl.program_id(0) == 0)
    def _():
        carry[...] = jnp.zeros_like(carry)

    xn = _rms(h1, nffn_ref[...])
    x_hi = xn.astype(MXU_DTYPE)
    xn_ref[:, :ROW_TILES, :] = x_hi.reshape(tm, ROW_TILES, LANES)
    x_lo = (xn - x_hi.astype(F32)).astype(MXU_DTYPE)
    logits = (_dot(x_hi, wr_hi_ref[...]) + _dot(x_lo, wr_hi_ref[...]) + _dot(x_hi, wr_lo_ref[...])
              + br_ref[...])
    lane = lax.broadcasted_iota(jnp.int32, (tm, LANES), 1)
    lane_f = lane.astype(F32)
    vals, ids, hots = [], [], []
    for _ in range(TOP_K):
        m = jnp.max(logits, axis=-1, keepdims=True)
        ix = jnp.min(jnp.where(logits == m, lane_f, float(LANES)), axis=-1, keepdims=True)
        hot = lane_f == ix
        vals.append(m)
        ids.append(ix)
        hots.append(hot)
        logits = jnp.where(hot, -jnp.inf, logits)
    es = [jnp.exp(v - vals[0]) for v in vals]
    den = es[0] + es[1] + es[2] + es[3]
    chosen = jnp.zeros((tm, LANES), F32)
    for hot in hots:
        chosen = chosen + jnp.where(hot, 1.0, 0.0)
    before = (lax.broadcasted_iota(jnp.int32, (tm, tm), 1) < lax.broadcasted_iota(jnp.int32, (tm, tm), 0))
    rank = _dot(jnp.where(before, 1.0, 0.0), chosen) + carry[...]
    info = jnp.zeros((tm, LANES), F32)
    for k in range(TOP_K):
        pos = jnp.sum(jnp.where(hots[k], rank, 0.0), axis=-1, keepdims=True)
        info = jnp.where(lane == k, es[k] / den, info)
        info = jnp.where(lane == TOP_K + k, ids[k].astype(F32), info)
        info = jnp.where(lane == 2 * TOP_K + k, pos, info)
    info_ref[...] = info
    token = pl.program_id(0) * tm + lax.broadcasted_iota(jnp.int32, (tm, 1), 0)
    tag = jnp.zeros((tm, LANES), F32)
    for d in range(TAG_DIGITS):
        tag = jnp.where(lane == d, ((token >> (8 * d)) & 255).astype(F32), tag)
    for k in range(TOP_K):
        tag = jnp.where(lane == TAG_DIGITS + k, ids[k].astype(F32), tag)
    tag_row = jnp.concatenate([tag, jnp.zeros((tm, D_MODEL - LANES), F32)], axis=1).astype(xn_ref.dtype)
    xn_ref[:, ROW_TILES:, :] = tag_row.reshape(tm, ROW_TILES, LANES)
    carry[...] = carry[...] + jnp.sum(chosen, axis=0, keepdims=True)
    cnt_ref[...] = carry[...]


_ROUTE_OUT_SHAPES = lambda n: (jax.ShapeDtypeStruct((n, D_MODEL), F32), jax.ShapeDtypeStruct((n, TAGGED_ROWS, LANES), MXU_DTYPE),
                               jax.ShapeDtypeStruct((n, LANES), F32), jax.ShapeDtypeStruct((1, LANES), F32))
_ROUTE_OUT_SPECS = (pl.BlockSpec((TOKEN_TILE, D_MODEL), lambda i: (i, 0)),
                    pl.BlockSpec((TOKEN_TILE, TAGGED_ROWS, LANES), lambda i: (i, 0, 0)),
                    pl.BlockSpec((TOKEN_TILE, LANES), lambda i: (i, 0)),
                    pl.BlockSpec((1, LANES), lambda i: (0, 0)))


def _route_in_specs():
    return [_full((1, D_MODEL)), _full((D_MODEL, LANES)), _full((D_MODEL, LANES)), _full((1, LANES))]


def _even_out_kernel(ypp_ref, yps_ref, op_ref, os_ref, r_ref, hp_ref, hs_ref, gn_ref, w_ref, nffn_ref, wr_hi_ref,
                     wr_lo_ref, br_ref, h1_ref, xn_ref, info_ref, cnt_ref, carry, *, npt):
    o = _split_rows(op_ref, os_ref, npt)
    r = r_ref[...]
    parts = []
    for hd in range(GLA_HEADS):
        sl = slice(hd * GLA_DV, (hd + 1) * GLA_DV)
        oh = o[:, sl]
        oh = oh * lax.rsqrt(jnp.mean(oh * oh, axis=-1, keepdims=True) + RMS_EPS) * gn_ref[...]
        rh = r[:, sl]
        parts.append(oh * (rh * jax.nn.sigmoid(rh)))
    gla = jnp.concatenate(parts, axis=1)
    mix = _dot(_split_rows(ypp_ref, yps_ref, npt), w_ref[:POOL_WIDTH, :]) + _dot(gla, w_ref[POOL_WIDTH:, :])
    h1 = _split_rows(hp_ref, hs_ref, npt) + mix
    h1_ref[...] = h1
    _route(h1, nffn_ref, wr_hi_ref, wr_lo_ref, br_ref, xn_ref, info_ref, cnt_ref, carry)


def _even_out(y_pool_parts, o_gla_parts, proj, h_parts, n_p, gla_norm, w_out, nffn, wr_hi, wr_lo, br):
    n = proj.shape[0]
    return pl.pallas_call(
        functools.partial(_even_out_kernel, npt=n_p // TOKEN_TILE),
        out_shape=_ROUTE_OUT_SHAPES(n),
        grid=(n // TOKEN_TILE,),
        in_specs=_split_specs(TOKEN_TILE, POOL_WIDTH, n_p, False) + _split_specs(TOKEN_TILE, GLA_WIDTH, n_p, False)
        + [pl.BlockSpec((TOKEN_TILE, GLA_WIDTH), lambda i: (i, (IN_EVEN_MAIN - GLA_WIDTH) // GLA_WIDTH))]
        + _split_specs(TOKEN_TILE, D_MODEL, n_p, h_parts[0] is h_parts[1])
        + [_full((1, GLA_DV)), _full((D_MODEL, D_MODEL))] + _route_in_specs(),
        out_specs=_ROUTE_OUT_SPECS,
        scratch_shapes=[pltpu.VMEM((1, LANES), F32)],
        compiler_params=_params(), name="even_out")(
            *y_pool_parts, *o_gla_parts, proj, *h_parts, gla_norm, w_out, nffn, wr_hi, wr_lo, br)


def _rope_tile(x, cos, sin, lo_half):
    swapped = jnp.where(lo_half, pltpu.roll(x, LANES - HEAD_DIM // 2, axis=1), pltpu.roll(x, HEAD_DIM // 2, axis=1))
    return x * cos + swapped * sin


def _odd_in_kernel(h_ref, g_ref, w_ref, b_ref, cos_ref, sin_ref, q_ref, kv_ref):
    xn = _rms(h_ref[...], g_ref[...])
    qkv = _dot(xn, w_ref[...]) + b_ref[...]
    cos = cos_ref[...]
    sin = sin_ref[...]
    lo_half = (lax.broadcasted_iota(jnp.int32, cos.shape, 1) % HEAD_DIM) < HEAD_DIM // 2
    for j in range(Q_WIDTH // LANES):
        sl = slice(j * LANES, (j + 1) * LANES)
        q_ref[:, sl] = _rope_tile(qkv[:, sl], cos, sin, lo_half).astype(q_ref.dtype)
    kv_ref[:, :KV_WIDTH] = _rope_tile(qkv[:, Q_WIDTH:Q_WIDTH + KV_WIDTH], cos, sin, lo_half)
    kv_ref[:, KV_WIDTH:] = qkv[:, Q_WIDTH + KV_WIDTH:]


def _odd_in(h, g, w_qkv, b_qkv, cos, sin, table_block):
    n = h.shape[0]
    wq = w_qkv.shape[1]
    tab = pl.BlockSpec((TOKEN_TILE, LANES), lambda i: (table_block(i), 0))
    return pl.pallas_call(
        _odd_in_kernel,
        out_shape=(jax.ShapeDtypeStruct((n, Q_WIDTH), MXU_DTYPE), jax.ShapeDtypeStruct((n, 2 * KV_WIDTH), F32)),
        grid=(n // TOKEN_TILE,),
        in_specs=[pl.BlockSpec((TOKEN_TILE, D_MODEL), lambda i: (i, 0)), _full((1, D_MODEL)),
                  _full((D_MODEL, wq)), _full((1, wq)), tab, tab],
        out_specs=(pl.BlockSpec((TOKEN_TILE, Q_WIDTH), lambda i: (i, 0)),
                   pl.BlockSpec((TOKEN_TILE, 2 * KV_WIDTH), lambda i: (i, 0))),
        compiler_params=_params(), name="odd_in")(h, g, w_qkv, b_qkv, cos, sin)


def _attn_kernel(sink_ref, q_ref, kv0_ref, kv1_ref, kv2_ref, ck0_ref, ck1_ref, cv0_ref, cv1_ref, o_ref, *, sq):
    i = pl.program_id(0)
    smp = sq.is_sample(i)
    t = sq.tile_in_seq(i)
    kv0 = kv0_ref[...]
    k_old = jnp.where(smp, ck0_ref[...], kv2_ref[:, :KV_WIDTH])
    k_mid = jnp.where(smp, ck1_ref[...], kv1_ref[:, :KV_WIDTH])
    v_old = jnp.where(smp, cv0_ref[...], kv2_ref[:, KV_WIDTH:])
    v_mid = jnp.where(smp, cv1_ref[...], kv1_ref[:, KV_WIDTH:])
    pad = jnp.zeros((CHUNK, KV_WIDTH), F32)
    keys = jnp.concatenate([k_old, k_mid, kv0[:, :KV_WIDTH], pad], axis=0)
    vals = jnp.concatenate([v_old, v_mid, kv0[:, KV_WIDTH:], pad], axis=0)
    n_keys = 4 * CHUNK
    lane = lax.broadcasted_iota(jnp.int32, (n_keys, KV_WIDTH), 1)
    lo = lane < HEAD_DIM
    keys_sw = pltpu.roll(keys, HEAD_DIM, axis=1)
    vals_sw = pltpu.roll(vals, HEAD_DIM, axis=1)
    kcol = lax.broadcasted_iota(jnp.int32, (1, n_keys), 1)
    first_valid = jnp.where(smp, 0, (2 - jnp.minimum(t, 2)) * CHUNK)
    key_ok = jnp.logical_and(kcol >= first_valid, kcol < 3 * CHUNK)
    n_pairs = N_Q_HEADS // N_KV_HEADS // 2
    pair_of_row = lax.broadcasted_iota(jnp.int32, (n_pairs * CHUNK, 1), 0) // CHUNK
    for g in range(N_KV_HEADS):
        own, other = (keys, keys_sw) if g == 0 else (keys_sw, keys)
        vown, vother = (vals, vals_sw) if g == 0 else (vals_sw, vals)
        kb = jnp.concatenate([jnp.where(lo, own, 0.0), jnp.where(lo, 0.0, other)], axis=0).astype(MXU_DTYPE)
        vb = jnp.concatenate([jnp.where(lo, vown, 0.0), jnp.where(lo, 0.0, vother)], axis=0).astype(MXU_DTYPE)
        pairs = [slice((g * n_pairs + pr) * LANES, (g * n_pairs + pr + 1) * LANES) for pr in range(n_pairs)]
        qg = jnp.concatenate([q_ref[:, sl] for sl in pairs], axis=0)
        s = _dot_nt(qg, kb) * (HEAD_DIM ** -0.5)
        halves = []
        for half in range(2):
            sh = jnp.where(key_ok, s[:, half * n_keys:(half + 1) * n_keys], NEG_INF)
            sink = jnp.zeros((n_pairs * CHUNK, 1), F32)
            for pr in range(n_pairs):
                sink = jnp.where(pair_of_row == pr, sink_ref[2 * (g * n_pairs + pr) + half], sink)
            m = jnp.maximum(jnp.max(sh, axis=-1, keepdims=True), sink)
            p = jnp.exp(sh - m)
            halves.append(p * (1.0 / (jnp.sum(p, axis=-1, keepdims=True) + jnp.exp(sink - m))))
        o = _dot(jnp.concatenate(halves, axis=1), vb)
        for pr, sl in enumerate(pairs):
            o_ref[:, sl] = o[pr * CHUNK:(pr + 1) * CHUNK].astype(o_ref.dtype)


def _attention(sinks, q, kv, ck, cv, sq):
    n = q.shape[0]
    npt = sq.n_prompt_tiles
    prev = lambda d: (lambda i, s: (jnp.where(sq.is_sample(i), i, jnp.maximum(i - d, 0)), 0))
    cache = lambda d: (lambda i, s: (jnp.where(sq.is_sample(i), 2 * (i - npt) + d, 0), 0))
    kvspec = lambda f: pl.BlockSpec((CHUNK, 2 * KV_WIDTH), f)
    cspec = lambda f: pl.BlockSpec((CHUNK, KV_WIDTH), f)
    grid_spec = pltpu.PrefetchScalarGridSpec(
        num_scalar_prefetch=1, grid=(sq.n_tiles,),
        in_specs=[pl.BlockSpec((CHUNK, Q_WIDTH), lambda i, s: (i, 0)),
                  kvspec(prev(0)), kvspec(prev(1)), kvspec(prev(2)),
                  cspec(cache(0)), cspec(cache(1)), cspec(cache(0)), cspec(cache(1))],
        out_specs=pl.BlockSpec((CHUNK, Q_WIDTH), lambda i, s: (i, 0)))
    return pl.pallas_call(
        functools.partial(_attn_kernel, sq=sq),
        out_shape=jax.ShapeDtypeStruct((n, Q_WIDTH), MXU_DTYPE),
        grid_spec=grid_spec, compiler_params=_params(), name="attention")(sinks, q, kv, kv, kv, ck, ck, cv, cv)


def _odd_out_kernel(o_ref, h_ref, w_ref, b_ref, nffn_ref, wr_hi_ref, wr_lo_ref, br_ref,
                    h1_ref, xn_ref, info_ref, cnt_ref, carry):
    h1 = h_ref[...] + _dot(o_ref[...], w_ref[...]) + b_ref[...]
    h1_ref[...] = h1
    _route(h1, nffn_ref, wr_hi_ref, wr_lo_ref, br_ref, xn_ref, info_ref, cnt_ref, carry)


def _odd_out(o, h, w_out, b_out, nffn, wr_hi, wr_lo, br):
    n = h.shape[0]
    row = pl.BlockSpec((TOKEN_TILE, D_MODEL), lambda i: (i, 0))
    return pl.pallas_call(
        _odd_out_kernel,
        out_shape=_ROUTE_OUT_SHAPES(n),
        grid=(n // TOKEN_TILE,),
        in_specs=[row, row, _full((D_MODEL, D_MODEL)), _full((1, D_MODEL))] + _route_in_specs(),
        out_specs=_ROUTE_OUT_SPECS,
        scratch_shapes=[pltpu.VMEM((1, LANES), F32)],
        compiler_params=_params(), name="odd_out")(o, h, w_out, b_out, nffn, wr_hi, wr_lo, br)


def _token_copy(src, s, dst, d, sem, rows=ROW_TILES):
    return pltpu.make_async_copy(src.at[pl.ds(pl.multiple_of(s, rows), rows)],
                                 dst.at[pl.ds(pl.multiple_of(d, rows), rows)], sem)


ZERO_TOKENS = EXPERT_TILE // 2
HIDDEN_BLOCK = 512


def _zero_fill(fill_ref, zeros, xs_out, zsem):
    zeros[...] = jnp.zeros_like(zeros)

    def sweep(wait):
        def go(cp):
            cp.wait() if wait else cp.start()

        def tail(e, c):
            off, length = fill_ref[e], fill_ref[N_EXPERTS + e]
            for bit in range(EXPERT_TILE.bit_length() - 1):
                rows = (1 << bit) * TAGGED_ROWS

                @pl.when((length >> bit) & 1 == 1)
                def _():
                    o = pl.multiple_of(off + (length & ((1 << bit) - 1)) * TAGGED_ROWS, TAGGED_ROWS)
                    go(pltpu.make_async_copy(zeros.at[pl.ds(0, rows)], xs_out.at[pl.ds(o, rows)], zsem))
            return c

        lax.fori_loop(0, N_EXPERTS, tail, 0)

        def unused(t, c):
            o = pl.multiple_of(fill_ref[2 * N_EXPERTS] + t * ZERO_TOKENS * TAGGED_ROWS, TAGGED_ROWS)
            go(pltpu.make_async_copy(zeros, xs_out.at[pl.ds(o, ZERO_TOKENS * TAGGED_ROWS)], zsem))
            return c

        lax.fori_loop(0, fill_ref[2 * N_EXPERTS + 1], unused, 0)

    sweep(wait=False)
    sweep(wait=True)


def _dispatch_kernel(fill_ref, dest_hbm, x_ref, xs_out, dest_smem, zeros, sem, dsems, zsem):
    i = pl.program_id(0)
    n_slots = TOKEN_TILE * TOP_K

    @pl.when(i == 0)
    def _():
        _zero_fill(fill_ref, zeros, xs_out, zsem)

    slot = i % 2

    def dest_rows(step, s):
        return pltpu.make_async_copy(dest_hbm.at[pl.ds(step * n_slots, n_slots)], dest_smem.at[s], dsems.at[s])

    @pl.when(i == 0)
    def _():
        dest_rows(0, 0).start()

    dest_rows(i, slot).wait()

    @pl.when(i + 1 < pl.num_programs(0))
    def _():
        dest_rows(i + 1, 1 - slot).start()

    def issue(t, c):
        for k in range(TOP_K):
            _token_copy(x_ref, t * TAGGED_ROWS, xs_out, dest_smem[slot, t * TOP_K + k], sem, TAGGED_ROWS).start(priority=k % 2)
        return c

    lax.fori_loop(0, TOKEN_TILE, issue, 0, unroll=2)
    whole = xs_out.at[pl.ds(0, n_slots * TAGGED_ROWS)]
    pltpu.make_async_copy(whole, whole, sem).wait()


def _dispatch(fill, dest, xt, n_rows):
    n = xt.shape[0] // TAGGED_ROWS
    grid_spec = pltpu.PrefetchScalarGridSpec(
        num_scalar_prefetch=1, grid=(n // TOKEN_TILE,),
        in_specs=[pl.BlockSpec(memory_space=pl.ANY),
                  pl.BlockSpec((TOKEN_TILE * TAGGED_ROWS, LANES), lambda i, f: (i, 0))],
        out_specs=pl.BlockSpec(memory_space=pl.ANY),
        scratch_shapes=[pltpu.SMEM((2, TOKEN_TILE * TOP_K), jnp.int32), pltpu.VMEM((ZERO_TOKENS * TAGGED_ROWS, LANES), xt.dtype),
                        pltpu.SemaphoreType.DMA, pltpu.SemaphoreType.DMA((2,)), pltpu.SemaphoreType.DMA])
    return pl.pallas_call(
        _dispatch_kernel, out_shape=jax.ShapeDtypeStruct((n_rows * TAGGED_ROWS, LANES), xt.dtype),
        grid_spec=grid_spec, compiler_params=_params(), name="moe_dispatch")(fill, dest, xt)


def _experts_kernel(te_ref, tr_ref, x_ref, wgu_ref, bgu_ref, wd_ref, bd_ref, yt_ref,
                    wgu_mxu, wd_mxu, ybuf, place_vmem, place_smem, ysems, psem, *, n_tokens):
    i = pl.program_id(0)
    slot = i % 2
    prev = 1 - slot
    tile_rows = EXPERT_TILE * ROW_TILES
    spare = yt_ref.shape[0] - 2 * tile_rows
    col = lax.broadcasted_iota(jnp.int32, (1, EXPERT_TILE), 1)

    def rows_done(s):
        whole = yt_ref.at[pl.ds(0, tile_rows)]
        pltpu.make_async_copy(whole, whole, ysems.at[s]).wait()

    def places_to_smem(first_rows, s):
        place_vmem[...] = jnp.broadcast_to(first_rows, place_vmem.shape)
        return pltpu.make_async_copy(place_vmem, place_smem.at[s], psem)

    def start_row_copy(j, s, queue=0):
        _token_copy(ybuf.at[s], j * ROW_TILES, yt_ref, place_smem[s, 0, j], ysems.at[s]).start(priority=queue)

    @pl.when(i == 0)
    def _():
        ybuf[...] = jnp.zeros(ybuf.shape, F32)
        for s in (1, 0):
            cp = pltpu.make_async_copy(ybuf.at[0], yt_ref.at[pl.ds(spare + s * tile_rows, tile_rows)], ysems.at[s])
            cp.start()
            if s == 1:
                cp.wait()
        cp = places_to_smem(spare + tile_rows + col * ROW_TILES, 1)
        cp.start()
        cp.wait()

    @pl.when(jnp.logical_or(i == 0, te_ref[i] != te_ref[jnp.maximum(i - 1, 0)]))
    def _():
        wgu_mxu[...] = wgu_ref[0, 0].astype(MXU_DTYPE)
        wd_mxu[...] = wd_ref[0, 0].astype(MXU_DTYPE)

    n_valid = tr_ref[i]
    after_last = jnp.logical_and(n_valid == 0, jnp.logical_and(i >= 1, tr_ref[jnp.maximum(i - 1, 0)] > 0))

    @pl.when(n_valid > 0)
    def _():
        rows_done(slot)
        for j in range(EXPERT_TILE):
            start_row_copy(j, prev, j % 2)
        wide = x_ref[...].reshape(EXPERT_TILE, TAGGED_ROWS * LANES)
        tag = wide[:, D_MODEL:D_MODEL + LANES].astype(F32)
        lane = lax.broadcasted_iota(jnp.int32, (EXPERT_TILE, LANES), 1)
        is_id = jnp.logical_and(lane >= TAG_DIGITS, lane < TAG_DIGITS + TOP_K)
        mine = jnp.logical_and(is_id, tag == te_ref[i].astype(F32))
        weight = jnp.where(lane == 0, 1.0, jnp.where(lane == 1, 256.0, 65536.0))
        terms = jnp.where(lane < TAG_DIGITS, tag * weight, jnp.where(mine, ((lane - TAG_DIGITS) * n_tokens).astype(F32), 0.0))
        place = jnp.sum(terms, axis=-1, keepdims=True)
        hi = jnp.floor(place * (1.0 / 65536.0))
        mid = jnp.floor((place - hi * 65536.0) * (1.0 / 256.0))
        digits = jnp.where(lane == 0, place - hi * 65536.0 - mid * 256.0, jnp.where(lane == 1, mid, jnp.where(lane == 2, hi, 0.0)))
        pick = jnp.where(lax.broadcasted_iota(jnp.int32, (ROW_TILES, LANES), 0)
                         == lax.broadcasted_iota(jnp.int32, (ROW_TILES, LANES), 1), 1.0, 0.0)
        planes = _dot_nt(pick, digits)
        place_row = (planes[0:1] + 256.0 * planes[1:2] + 65536.0 * planes[2:3]).astype(jnp.int32) * ROW_TILES
        to_smem = places_to_smem(jnp.where(col < n_valid, place_row, spare + slot * tile_rows + col * ROW_TILES), slot)
        to_smem.start()

        x = wide[:, :D_MODEL]
        out = None
        for c in range(D_MODEL // HIDDEN_BLOCK):
            cols = slice(c * HIDDEN_BLOCK, (c + 1) * HIDDEN_BLOCK)
            ucols = slice(D_MODEL + c * HIDDEN_BLOCK, D_MODEL + (c + 1) * HIDDEN_BLOCK)
            gate = jnp.minimum(_dot(x, wgu_mxu[:, cols]) + bgu_ref[0, 0, :, cols], SWIGLU_LIMIT)
            up = jnp.clip(_dot(x, wgu_mxu[:, ucols]) + bgu_ref[0, 0, :, ucols], -SWIGLU_LIMIT, SWIGLU_LIMIT)
            act = (up + 1.0) * gate * jax.nn.sigmoid(SWIGLU_ALPHA * gate)
            part = _dot(act, wd_mxu[cols, :])
            out = part if out is None else out + part
        _store_token_tiles(ybuf.at[slot], out + bd_ref[0, 0])
        to_smem.wait()

    @pl.when(after_last)
    def _():
        rows_done(slot)
        def issue(j, c):
            start_row_copy(j, prev)
            return c

        lax.fori_loop(0, EXPERT_TILE, issue, 0, unroll=8)
        rows_done(prev)


def _experts(layer, tile_expert, tile_rows, xs, wgu, bgu, wd, bd, n_tokens):
    rows = xs.shape[0]
    w = lambda shape: pl.BlockSpec((1, 1) + shape, lambda i, te, tr: (layer, te[i], 0, 0))
    grid_spec = pltpu.PrefetchScalarGridSpec(
        num_scalar_prefetch=2, grid=(rows // EXPERT_TILE,),
        in_specs=[pl.BlockSpec((EXPERT_TILE, TAGGED_ROWS, LANES), lambda i, te, tr: (i, 0, 0)),
                  w((D_MODEL, 2 * D_MODEL)), w((1, 2 * D_MODEL)), w((D_MODEL, D_MODEL)), w((1, D_MODEL))],
        out_specs=pl.BlockSpec(memory_space=pl.ANY),
        scratch_shapes=[pltpu.VMEM((D_MODEL, 2 * D_MODEL), MXU_DTYPE), pltpu.VMEM((D_MODEL, D_MODEL), MXU_DTYPE),
                        pltpu.VMEM((2, EXPERT_TILE * ROW_TILES, LANES), F32),
                        pltpu.VMEM((ROW_TILES, EXPERT_TILE), jnp.int32), pltpu.SMEM((2, ROW_TILES, EXPERT_TILE), jnp.int32),
                        pltpu.SemaphoreType.DMA((2,)), pltpu.SemaphoreType.DMA])
    yt_rows = (n_tokens * TOP_K + 2 * EXPERT_TILE) * ROW_TILES
    return pl.pallas_call(
        functools.partial(_experts_kernel, n_tokens=n_tokens), out_shape=jax.ShapeDtypeStruct((yt_rows, LANES), F32),
        grid_spec=grid_spec, compiler_params=_params(), name="moe_experts")(
            tile_expert, tile_rows, xs, wgu, bgu, wd, bd)


def _combine_kernel(*refs, final, npt):
    y_refs, outs = refs[:TOP_K], refs[TOP_K + 7:]
    info_ref, h_ref, p_prompt_ref, p_sample_ref, pp_ref, pg_ref, nf_ref = refs[TOP_K:TOP_K + 7]
    i = pl.program_id(0)
    gates = info_ref[...]
    moe = None
    for k in range(TOP_K):
        rows = _load_token_tiles(y_refs[k], COMBINE_TILE)
        moe = gates[:, k:k + 1] * rows if moe is None else moe + gates[:, k:k + 1] * rows
    h2 = h_ref[...] + moe
    p = jnp.where(i < npt, p_prompt_ref[0], p_sample_ref[0])
    h3 = h2 + jax.nn.sigmoid(_dot(h2, pg_ref[...])) * _dot(p, pp_ref[...])
    if not final:
        outs[0][...] = h3
        return
    y = _rms(h3, nf_ref[...])

    @pl.when(i < npt)
    def _():
        outs[0][...] = y

    @pl.when(i >= npt)
    def _():
        outs[1][...] = y


def _combine(layer, yt, info, h1, p_prompt, p_sample, ple_proj, ple_gate, norm_final, final):
    n = h1.shape[0]
    n_p = p_prompt.shape[1]
    npt = n_p // COMBINE_TILE
    steps = n // COMBINE_TILE
    row = lambda w: pl.BlockSpec((COMBINE_TILE, w), lambda i: (i, 0))
    rank_rows = lambda k: pl.BlockSpec((COMBINE_TILE * ROW_TILES, LANES), lambda i: (k * steps + i, 0))
    prompt_rows = lambda i: jnp.minimum(i, npt - 1)
    sample_rows = lambda i: jnp.maximum(i - npt, 0)
    if final:
        out_shape = (jax.ShapeDtypeStruct((n_p, D_MODEL), F32), jax.ShapeDtypeStruct((n - n_p, D_MODEL), F32))
        out_specs = (pl.BlockSpec((COMBINE_TILE, D_MODEL), lambda i: (prompt_rows(i), 0)),
                     pl.BlockSpec((COMBINE_TILE, D_MODEL), lambda i: (sample_rows(i), 0)))
    else:
        out_shape, out_specs = jax.ShapeDtypeStruct((n, D_MODEL), F32), row(D_MODEL)
    return pl.pallas_call(
        functools.partial(_combine_kernel, final=final, npt=npt),
        out_shape=out_shape,
        grid=(steps,),
        in_specs=[rank_rows(k) for k in range(TOP_K)] + [
            row(LANES), row(D_MODEL),
            pl.BlockSpec((1, COMBINE_TILE, PLE_DIM), lambda i: (layer, prompt_rows(i), 0)),
            pl.BlockSpec((1, COMBINE_TILE, PLE_DIM), lambda i: (layer, sample_rows(i), 0)),
            _full((PLE_DIM, D_MODEL)), _full((D_MODEL, D_MODEL)), _full((1, D_MODEL))],
        out_specs=out_specs,
        compiler_params=_params(), name="moe_combine")(
            *([yt] * TOP_K), info, h1, p_prompt, p_sample, ple_proj, ple_gate, norm_final)


def _moe_and_embed(layer, h1, xt, info, counts, p_prompt, p_sample, wgu, bgu, wd, bd, ple_proj, ple_gate,
                   norm_final, final):
    n = h1.shape[0]
    n_tiles = (n * TOP_K + N_EXPERTS * (EXPERT_TILE - 1)) // EXPERT_TILE + 1
    ids = info[:, TOP_K:2 * TOP_K].astype(jnp.int32)
    rank = info[:, 2 * TOP_K:3 * TOP_K].astype(jnp.int32)
    cnt = counts[0, :N_EXPERTS].astype(jnp.int32)
    padded = ((cnt + EXPERT_TILE - 1) // EXPERT_TILE) * EXPERT_TILE
    ends = jnp.cumsum(padded)
    starts = ends - padded
    dest = ((starts[ids] + rank) * TAGGED_ROWS).reshape(-1)
    tile_start = jnp.arange(n_tiles, dtype=jnp.int32) * EXPERT_TILE
    tile_expert = jnp.minimum(jnp.sum((tile_start[:, None] >= ends[None, :]).astype(jnp.int32), axis=1), N_EXPERTS - 1)
    tile_rows = jnp.clip(cnt[tile_expert] - (tile_start - starts[tile_expert]), 0, EXPERT_TILE)
    tile_rows = jnp.where(tile_start < ends[-1], tile_rows, 0).astype(jnp.int32)
    fill = jnp.concatenate([(starts + cnt) * TAGGED_ROWS, padded - cnt,
                            jnp.stack([ends[-1] * TAGGED_ROWS,
                                       (n_tiles * EXPERT_TILE - ends[-1]) // ZERO_TOKENS])]).astype(jnp.int32)
    xs = _dispatch(fill, dest, xt.reshape(n * TAGGED_ROWS, LANES), n_tiles * EXPERT_TILE)
    yt = _experts(layer, tile_expert, tile_rows, xs.reshape(-1, TAGGED_ROWS, LANES), wgu, bgu, wd, bd, n)
    return _combine(layer, yt, info, h1, p_prompt, p_sample, ple_proj, ple_gate, norm_final, final)


def _rope_tables(prompt_len, n_sample_seq, sample_len):
    half = HEAD_DIM // 2
    inv = jnp.power(jnp.float32(ROPE_THETA), -jnp.arange(half, dtype=F32) / half)
    pos = jnp.concatenate([jnp.arange(prompt_len), jnp.tile(PAST_LEN + jnp.arange(sample_len), n_sample_seq)])
    ang = pos.astype(F32)[:, None] * inv[None, :]
    cos = jnp.tile(jnp.cos(ang), (1, LANES // half))
    sin = jnp.tile(jnp.concatenate([-jnp.sin(ang), jnp.sin(ang)], axis=1), (1, LANES // HEAD_DIM))
    return cos, sin


def kernel(x_prompt, x_sample, state_pool, state_gla, cache_k, cache_v, p_prompt, p_sample, norm_mix, norm_ffn, norm_final, w_in_even, pool_w, pool_scale, gla_w_gate, gla_b_gate, gla_norm, w_out_even, w_qkv_odd, b_qkv_odd, attn_sinks, w_out_odd, b_out_odd, w_router, b_router, w_gate_up, b_gate_up, w_down, b_down, ple_proj, ple_gate):
    bsz, t_len, _ = x_prompt.shape
    dec_bsz, dec_len, _ = x_sample.shape
    depth = norm_mix.shape[0]
    n_p, n_s = bsz * t_len, dec_bsz * dec_len
    n = n_p + n_s
    assert dec_len == CHUNK and n_s == TOKEN_TILE and t_len % TOKEN_TILE == 0 and cache_k.shape[2] == WINDOW
    sq = _Seq(bsz, t_len, dec_bsz)
    grp_p = _Group(bsz, t_len, min(SEQ_TILE, t_len), 0, 0)
    grp_s = _Group(dec_bsz, dec_len, dec_len, n_p, PAST_LEN)
    bf = lambda a: a.astype(MXU_DTYPE)
    row = lambda a: a.reshape(1, -1)

    h_parts = (x_prompt.reshape(n_p, D_MODEL), x_sample.reshape(n_s, D_MODEL))
    p_parts = (p_prompt.reshape(depth, n_p, PLE_DIM), p_sample.reshape(depth, n_s, PLE_DIM))
    b_gu = b_gate_up.reshape(depth, N_EXPERTS, 1, 2 * D_MODEL)
    b_dn = b_down.reshape(depth, N_EXPERTS, 1, D_MODEL)
    cos, sin = _rope_tables(t_len, dec_bsz, dec_len)
    tiles_per_seq = t_len // TOKEN_TILE
    table_block = lambda i: jnp.where(i < bsz * tiles_per_seq, i % tiles_per_seq, tiles_per_seq)

    pools, glas, new_k, new_v = [], [], [], []
    for i in range(depth):
        if i % 2 == 0:
            e = i // 2
            w_in = jnp.pad(bf(w_in_even[e]), ((0, 0), (0, (-IN_EVEN) % LANES)))
            wg = jnp.pad(bf(gla_w_gate[e]), ((0, w_in.shape[1] - IN_EVEN_MAIN - GLA_RANK), (0, 0)))
            proj = _even_in(h_parts, n, n_p, row(norm_mix[i]), w_in, wg, row(gla_b_gate[e]))
            pw, ps = bf(pool_w[e]), row(pool_scale[e])
            y_pool_p, pool_p = _pool(proj, jnp.zeros((bsz, 16, POOL_WIDTH), F32), pw, ps, grp_p)
            y_pool_s, pool_s = _pool(proj, jnp.pad(state_pool[e], ((0, 0), (1, 0), (0, 0))), pw, ps, grp_s)
            o_gla_p, gla_p = _gla(proj, jnp.zeros((bsz,) + state_gla.shape[2:], F32), grp_p)
            o_gla_s, gla_s = _gla(proj, state_gla[e], grp_s)
            pools.append((pool_p[:, 1:], pool_s[:, 1:]))
            glas.append((gla_p, gla_s))
            wr = jnp.pad(w_router[i], ((0, 0), (0, LANES - N_EXPERTS)))
            wr_hi = bf(wr)
            wr_lo = bf(wr - wr_hi.astype(F32))
            br = jnp.pad(row(b_router[i]), ((0, 0), (0, LANES - N_EXPERTS)), constant_values=NEG_INF)
            h1, xn, info, counts = _even_out((y_pool_p, y_pool_s), (o_gla_p, o_gla_s), proj, h_parts, n_p,
                                             row(gla_norm[e]), bf(w_out_even[e]), row(norm_ffn[i]), wr_hi, wr_lo, br)
        else:
            o = i // 2
            h = h_parts[0]
            q, kv = _odd_in(h, row(norm_mix[i]), bf(w_qkv_odd[o]), row(b_qkv_odd[o]), cos, sin, table_block)
            ck = cache_k[o].reshape(dec_bsz * WINDOW, KV_WIDTH)
            cv = cache_v[o].reshape(dec_bsz * WINDOW, KV_WIDTH)
            att = _attention(attn_sinks[o], q, kv, ck, cv, sq)
            kv_p = kv[:n_p].reshape(bsz, t_len, 2 * KV_WIDTH)[:, -WINDOW:]
            kv_s = kv[n_p:].reshape(dec_bsz, dec_len, 2 * KV_WIDTH)
            hd = (N_KV_HEADS, HEAD_DIM)
            new_k.append((kv_p[..., :KV_WIDTH].reshape(bsz, WINDOW, *hd),
                          jnp.concatenate([cache_k[o], kv_s[..., :KV_WIDTH].reshape(dec_bsz, dec_len, *hd)], axis=1)[:, -WINDOW:]))
            new_v.append((kv_p[..., KV_WIDTH:].reshape(bsz, WINDOW, *hd),
                          jnp.concatenate([cache_v[o], kv_s[..., KV_WIDTH:].reshape(dec_bsz, dec_len, *hd)], axis=1)[:, -WINDOW:]))
            wr = jnp.pad(w_router[i], ((0, 0), (0, LANES - N_EXPERTS)))
            wr_hi = bf(wr)
            wr_lo = bf(wr - wr_hi.astype(F32))
            br = jnp.pad(row(b_router[i]), ((0, 0), (0, LANES - N_EXPERTS)), constant_values=NEG_INF)
            h1, xn, info, counts = _odd_out(att, h, bf(w_out_odd[o]), row(b_out_odd[o]),
                                            row(norm_ffn[i]), wr_hi, wr_lo, br)
        final = i == depth - 1
        out = _moe_and_embed(i, h1, xn, info, counts, *p_parts, w_gate_up, b_gu, w_down, b_dn,
                             bf(ple_proj[i]), bf(ple_gate[i]), row(norm_final), final)
        h_parts = out if final else (out, out)

    y_prompt = h_parts[0].reshape(bsz, t_len, D_MODEL)
    y_sample = h_parts[1].reshape(dec_bsz, dec_len, D_MODEL)
    part = lambda pairs, j: jnp.stack([p[j] for p in pairs])
    return (y_prompt, y_sample, part(pools, 0), part(glas, 0), part(new_k, 0), part(new_v, 0),
            part(pools, 1), part(glas, 1), part(new_k, 1), part(new_v, 1))
```

```python
import functools

import jax
import jax.numpy as jnp
from jax import lax
from jax.experimental import pallas as pl
from jax.experimental.pallas import tpu as pltpu

F32 = jnp.float32
MXU_DTYPE = jnp.bfloat16

V7X_VMEM_BYTES = 64 * 1024 * 1024
VMEM_LIMIT = (V7X_VMEM_BYTES * 7) // 8
LANES = 128

D_MODEL = 1024
CHUNK = 64
PAST_LEN = 2048
PLE_DIM = 256
RMS_EPS = 1e-6
POOL_WINDOWS = (2, 4, 8, 16)
POOL_WIDTH = 512
POOL_GROUP_DIM = 128
POOL_STATE = 15
GLA_HEADS = 4
GLA_DK = 64
GLA_DV = 128
GLA_KEY_WIDTH = GLA_HEADS * GLA_DK
GLA_WIDTH = GLA_HEADS * GLA_DV
GLA_RANK = 16
GLA_TAU = 16.0
GLA_BLOCK = 16
IN_EVEN = POOL_WIDTH + 2 * GLA_KEY_WIDTH + 2 * GLA_WIDTH + GLA_RANK
IN_EVEN_MAIN = IN_EVEN - GLA_RANK
PROJ_WIDTH = IN_EVEN_MAIN + GLA_KEY_WIDTH
N_Q_HEADS = 16
N_KV_HEADS = 2
HEAD_DIM = 64
WINDOW = 128
ROPE_THETA = 10000.0
Q_WIDTH = N_Q_HEADS * HEAD_DIM
KV_WIDTH = N_KV_HEADS * HEAD_DIM
N_EXPERTS = 32
TOP_K = 4
SWIGLU_LIMIT = 7.0
SWIGLU_ALPHA = 1.702
NEG_INF = -1e30

TOKEN_TILE = 512
EXPERT_TILE = 512
COMBINE_TILE = 256


def _dot(a, b):
    return jnp.dot(a.astype(MXU_DTYPE), b.astype(MXU_DTYPE), preferred_element_type=F32)


def _dot_nt(a, b):
    return lax.dot_general(a.astype(MXU_DTYPE), b.astype(MXU_DTYPE), (((1,), (1,)), ((), ())),
                           preferred_element_type=F32)


def _split3(x):
    x1 = x.astype(MXU_DTYPE)
    r1 = x - x1.astype(F32)
    x2 = r1.astype(MXU_DTYPE)
    x3 = (r1 - x2.astype(F32)).astype(MXU_DTYPE)
    return x1, x2, x3


def _rms(x, g):
    return x * lax.rsqrt(jnp.mean(x * x, axis=-1, keepdims=True) + RMS_EPS) * g


ROW_TILES = D_MODEL // LANES
TAGGED_ROWS = 2 * ROW_TILES
TAG_DIGITS = 3


def _store_token_tiles(ref, x, pitch=ROW_TILES):
    for s in range(ROW_TILES):
        ref[pl.ds(s, x.shape[0], stride=pitch), :] = x[:, s * LANES:(s + 1) * LANES]


def _load_token_tiles(ref, rows, pitch=ROW_TILES):
    return jnp.concatenate([ref[pl.ds(s, rows, stride=pitch), :] for s in range(ROW_TILES)], axis=1)


def _params(n_axes=1):
    return pltpu.CompilerParams(dimension_semantics=("arbitrary",) * n_axes, vmem_limit_bytes=VMEM_LIMIT)


def _full(shape):
    return pl.BlockSpec(shape, lambda *_: (0,) * len(shape))


class _Seq:
    def __init__(self, n_prompt_seq, prompt_len, n_sample_seq):
        self.tiles_per_seq = prompt_len // CHUNK
        self.n_prompt_seq = n_prompt_seq
        self.n_prompt_tiles = n_prompt_seq * self.tiles_per_seq
        self.n_tiles = self.n_prompt_tiles + n_sample_seq
        self.n_seq = n_prompt_seq + n_sample_seq

    def is_sample(self, i):
        return i >= self.n_prompt_tiles

    def tile_in_seq(self, i):
        return jnp.where(self.is_sample(i), 0, i % self.tiles_per_seq)

    def seq(self, i):
        return jnp.where(self.is_sample(i), self.n_prompt_seq + i - self.n_prompt_tiles, i // self.tiles_per_seq)

    def last(self, i):
        return jnp.logical_or(self.is_sample(i), i % self.tiles_per_seq == self.tiles_per_seq - 1)

    def pos0(self, i):
        return jnp.where(self.is_sample(i), PAST_LEN, self.tile_in_seq(i) * CHUNK)


class _Group:
    def __init__(self, n_seq, seq_len, tile, row0, pos_base):
        assert seq_len % tile == 0 and row0 % tile == 0
        self.n_seq, self.tile, self.pos_base = n_seq, tile, pos_base
        self.tiles_per_seq = seq_len // tile
        self.n_tiles = n_seq * self.tiles_per_seq
        self.block0 = row0 // tile

    def block(self, i):
        return self.block0 + i

    def seq(self, i):
        return i // self.tiles_per_seq

    def tile_in_seq(self, i):
        return i % self.tiles_per_seq

    def last(self, i):
        return i % self.tiles_per_seq == self.tiles_per_seq - 1

    def pos0(self, i):
        return self.pos_base + self.tile_in_seq(i) * self.tile


SEQ_TILE = 256


def _split_specs(tile, width, n_prompt, same_array):
    npt = n_prompt // tile
    off = npt if same_array else 0
    return [pl.BlockSpec((tile, width), lambda i: (jnp.minimum(i, npt - 1), 0)),
            pl.BlockSpec((tile, width), lambda i: (jnp.maximum(i - npt, 0) + off, 0))]


def _split_rows(prompt_ref, sample_ref, n_prompt_tiles):
    return jnp.where(pl.program_id(0) < n_prompt_tiles, prompt_ref[...], sample_ref[...])


def _even_in_kernel(hp_ref, hs_ref, g_ref, w_ref, wg_ref, bg_ref, out_ref, *, npt):
    xn = _rms(_split_rows(hp_ref, hs_ref, npt), g_ref[...])
    proj = _dot(xn, w_ref[...])
    z = proj[:, IN_EVEN_MAIN:]
    a = _dot(z, wg_ref[...]) + bg_ref[...]
    log_alpha = (jnp.minimum(a, 0.0) - jnp.log1p(jnp.exp(-jnp.abs(a)))) * (1.0 / GLA_TAU)
    out_ref[:, :IN_EVEN_MAIN] = proj[:, :IN_EVEN_MAIN]
    out_ref[:, IN_EVEN_MAIN:] = log_alpha


def _even_in(h_parts, n, n_p, g, w_in, wg, bg):
    wp = w_in.shape[1]
    return pl.pallas_call(
        functools.partial(_even_in_kernel, npt=n_p // TOKEN_TILE),
        out_shape=jax.ShapeDtypeStruct((n, PROJ_WIDTH), F32),
        grid=(n // TOKEN_TILE,),
        in_specs=_split_specs(TOKEN_TILE, D_MODEL, n_p, h_parts[0] is h_parts[1]) + [
            _full((1, D_MODEL)), _full((D_MODEL, wp)), _full((wp - IN_EVEN_MAIN, GLA_KEY_WIDTH)),
            _full((1, GLA_KEY_WIDTH))],
        out_specs=pl.BlockSpec((TOKEN_TILE, PROJ_WIDTH), lambda i: (i, 0)),
        compiler_params=_params(), name="even_in")(*h_parts, g, w_in, wg, bg)


def _pool_kernel(u_ref, init_ref, pw_ref, ps_ref, y_ref, st_ref, buf, *, grp):
    i = pl.program_id(0)
    rows = grp.tile

    @pl.when(grp.tile_in_seq(i) == 0)
    def _():
        buf[0:16, :] = init_ref[0]

    u = u_ref[...]
    buf[16:16 + rows, :] = u
    pos = grp.pos0(i) + lax.broadcasted_iota(jnp.int32, (rows, 1), 0)
    for g, w in enumerate(POOL_WINDOWS):
        sl = slice(g * POOL_GROUP_DIM, (g + 1) * POOL_GROUP_DIM)
        acc = u[:, sl]
        for j in range(1, w):
            acc = acc + buf[16 - j:16 - j + rows, sl]
        cnt = jnp.minimum(w, pos + 1).astype(F32)
        d = acc / cnt - u[:, sl]
        y_ref[:, sl] = _dot(d, pw_ref[g]) * ps_ref[:, sl]
    tail = buf[rows:rows + 16, :]
    st_ref[0] = tail
    buf[0:16, :] = tail


def _pool(proj, init, pool_w, pool_scale, grp):
    return pl.pallas_call(
        functools.partial(_pool_kernel, grp=grp),
        out_shape=(jax.ShapeDtypeStruct((grp.n_tiles * grp.tile, POOL_WIDTH), F32),
                   jax.ShapeDtypeStruct((grp.n_seq, 16, POOL_WIDTH), F32)),
        grid=(grp.n_tiles,),
        in_specs=[pl.BlockSpec((grp.tile, POOL_WIDTH), lambda i: (grp.block(i), 0)),
                  pl.BlockSpec((1, 16, POOL_WIDTH), lambda i: (grp.seq(i), 0, 0)),
                  _full((len(POOL_WINDOWS), POOL_GROUP_DIM, POOL_GROUP_DIM)), _full((1, POOL_WIDTH))],
        out_specs=(pl.BlockSpec((grp.tile, POOL_WIDTH), lambda i: (i, 0)),
                   pl.BlockSpec((1, 16, POOL_WIDTH), lambda i: (grp.seq(i), 0, 0))),
        scratch_shapes=[pltpu.VMEM((grp.tile + 16, POOL_WIDTH), F32)],
        compiler_params=_params(), name="pool")(proj, init, pool_w, pool_scale)


def _gla_kernel(q_ref, k_ref, v_ref, g_ref, s0_ref, o_ref, sout_ref, state, before, *, grp):
    i = pl.program_id(0)
    tile = grp.tile
    n_blk = tile // GLA_BLOCK

    @pl.when(grp.tile_in_seq(i) == 0)
    def _():
        state[...] = s0_ref[0]

    g = g_ref[...]
    row = lax.broadcasted_iota(jnp.int32, (tile, tile), 0)
    col = lax.broadcasted_iota(jnp.int32, (tile, tile), 1)
    same = (row >> 4) == (col >> 4)
    causal = jnp.logical_and(same, col <= row)
    tri = jnp.where(causal, 1.0, 0.0).astype(MXU_DTYPE)
    ones = jnp.where(same, 1.0, 0.0).astype(MXU_DTYPE)
    g1, g2, g3 = _split3(g)
    b = _dot(tri, g1) + _dot(tri, g2) + _dot(tri, g3)
    b_last = _dot(ones, g1) + _dot(ones, g2) + _dot(ones, g3)
    q_t = q_ref[...] * (GLA_DK ** -0.5) * jnp.exp(b)
    k = k_ref[...]
    k_t = k * jnp.exp(-b)
    k_dec_t = (k * jnp.exp(b_last - b)).T
    sel = jnp.where((lax.broadcasted_iota(jnp.int32, (tile, LANES), 0) >> 4)
                    == lax.broadcasted_iota(jnp.int32, (tile, LANES), 1), 1.0, 0.0).astype(MXU_DTYPE)
    t1, t2, t3 = _split3(g.T)
    blk_decay = jnp.exp(_dot(t1, sel) + _dot(t2, sel) + _dot(t3, sel))
    v = v_ref[...]
    stacked = n_blk * GLA_DK
    dk_bits, blk_bits = GLA_DK.bit_length() - 1, GLA_BLOCK.bit_length() - 1
    upd_live = (lax.broadcasted_iota(jnp.int32, (stacked, tile), 0) >> dk_bits
                == lax.broadcasted_iota(jnp.int32, (stacked, tile), 1) >> blk_bits)
    qry_live = (lax.broadcasted_iota(jnp.int32, (tile, stacked), 0) >> blk_bits
                == lax.broadcasted_iota(jnp.int32, (tile, stacked), 1) >> dk_bits)
    low_half = lax.broadcasted_iota(jnp.int32, (tile, LANES), 1) < GLA_DK
    for h in range(GLA_HEADS):
        ks = slice(h * GLA_DK, (h + 1) * GLA_DK)
        vs = slice(h * GLA_DV, (h + 1) * GLA_DV)
        vh = v[:, vs]
        scores = jnp.where(causal, _dot_nt(q_t[:, ks], k_t[:, ks]), 0.0)
        o = _dot(scores, vh)
        upd = _dot(jnp.where(upd_live, jnp.concatenate([k_dec_t[ks, :]] * n_blk, axis=0), 0.0), vh)
        s = state[h]
        for j in range(n_blk):
            rows = slice(j * GLA_DK, (j + 1) * GLA_DK)
            before[rows, :] = s
            s = blk_decay[ks, j:j + 1] * s + upd[rows, :]
        state[h] = s
        pair = q_t[:, (h // 2) * LANES:(h // 2 + 1) * LANES]
        swapped = pltpu.roll(pair, GLA_DK, axis=1)
        both = jnp.where(low_half, pair, swapped) if h % 2 == 0 else jnp.where(low_half, swapped, pair)
        q_exp = jnp.where(qry_live, jnp.concatenate([both] * (stacked // LANES), axis=1), 0.0)
        o_ref[:, vs] = o + _dot(q_exp, before[...])

    @pl.when(grp.last(i))
    def _():
        sout_ref[0] = state[...]


def _gla(proj, s0, grp):
    st_shape = (grp.n_seq, GLA_HEADS, GLA_DK, GLA_DV)
    st_spec = pl.BlockSpec((1, GLA_HEADS, GLA_DK, GLA_DV), lambda i: (grp.seq(i), 0, 0, 0))
    kw = GLA_KEY_WIDTH
    cols = lambda width, c: pl.BlockSpec((grp.tile, width), lambda i: (grp.block(i), c))
    return pl.pallas_call(
        functools.partial(_gla_kernel, grp=grp),
        out_shape=(jax.ShapeDtypeStruct((grp.n_tiles * grp.tile, GLA_WIDTH), F32), jax.ShapeDtypeStruct(st_shape, F32)),
        grid=(grp.n_tiles,),
        in_specs=[cols(kw, POOL_WIDTH // kw), cols(kw, POOL_WIDTH // kw + 1),
                  cols(GLA_WIDTH, (POOL_WIDTH + 2 * kw) // GLA_WIDTH), cols(kw, IN_EVEN_MAIN // kw), st_spec],
        out_specs=(pl.BlockSpec((grp.tile, GLA_WIDTH), lambda i: (i, 0)), st_spec),
        scratch_shapes=[pltpu.VMEM((GLA_HEADS, GLA_DK, GLA_DV), F32),
                        pltpu.VMEM((grp.tile // GLA_BLOCK * GLA_DK, GLA_DV), F32)],
        compiler_params=_params(), name="gla")(proj, proj, proj, proj, s0)


def _route(h1, nffn_ref, wr_hi_ref, wr_lo_ref, br_ref, xn_ref, info_ref, cnt_ref, carry):
    tm = h1.shape[0]

    @pl.when(pl.program_id(0) == 0)
    def _():
        carry[...] = jnp.zeros_like(carry)

    xn = _rms(h1, nffn_ref[...])
    x_hi = xn.astype(MXU_DTYPE)
    xn_ref[:, :ROW_TILES, :] = x_hi.reshape(tm, ROW_TILES, LANES)
    x_lo = (xn - x_hi.astype(F32)).astype(MXU_DTYPE)
    logits = (_dot(x_hi, wr_hi_ref[...]) + _dot(x_lo, wr_hi_ref[...]) + _dot(x_hi, wr_lo_ref[...])
              + br_ref[...])
    lane = lax.broadcasted_iota(jnp.int32, (tm, LANES), 1)
    lane_f = lane.astype(F32)
    vals, ids, hots = [], [], []
    for _ in range(TOP_K):
        m = jnp.max(logits, axis=-1, keepdims=True)
        ix = jnp.min(jnp.where(logits == m, lane_f, float(LANES)), axis=-1, keepdims=True)
        hot = lane_f == ix
        vals.append(m)
        ids.append(ix)
        hots.append(hot)
        logits = jnp.where(hot, -jnp.inf, logits)
    es = [jnp.exp(v - vals[0]) for v in vals]
    den = es[0] + es[1] + es[2] + es[3]
    chosen = jnp.zeros((tm, LANES), F32)
    for hot in hots:
        chosen = chosen + jnp.where(hot, 1.0, 0.0)
    before = (lax.broadcasted_iota(jnp.int32, (tm, tm), 1) < lax.broadcasted_iota(jnp.int32, (tm, tm), 0))
    rank = _dot(jnp.where(before, 1.0, 0.0), chosen) + carry[...]
    info = jnp.zeros((tm, LANES), F32)
    for k in range(TOP_K):
        pos = jnp.sum(jnp.where(hots[k], rank, 0.0), axis=-1, keepdims=True)
        info = jnp.where(lane == k, es[k] / den, info)
        info = jnp.where(lane == TOP_K + k, ids[k].astype(F32), info)
        info = jnp.where(lane == 2 * TOP_K + k, pos, info)
    info_ref[...] = info
    token = pl.program_id(0) * tm + lax.broadcasted_iota(jnp.int32, (tm, 1), 0)
    tag = jnp.zeros((tm, LANES), F32)
    for d in range(TAG_DIGITS):
        tag = jnp.where(lane == d, ((token >> (8 * d)) & 255).astype(F32), tag)
    for k in range(TOP_K):
        tag = jnp.where(lane == TAG_DIGITS + k, ids[k].astype(F32), tag)
    tag_row = jnp.concatenate([tag, jnp.zeros((tm, D_MODEL - LANES), F32)], axis=1).astype(xn_ref.dtype)
    xn_ref[:, ROW_TILES:, :] = tag_row.reshape(tm, ROW_TILES, LANES)
    carry[...] = carry[...] + jnp.sum(chosen, axis=0, keepdims=True)
    cnt_ref[...] = carry[...]


_ROUTE_OUT_SHAPES = lambda n: (jax.ShapeDtypeStruct((n, D_MODEL), F32), jax.ShapeDtypeStruct((n, TAGGED_ROWS, LANES), MXU_DTYPE),
                               jax.ShapeDtypeStruct((n, LANES), F32), jax.ShapeDtypeStruct((1, LANES), F32))
_ROUTE_OUT_SPECS = (pl.BlockSpec((TOKEN_TILE, D_MODEL), lambda i: (i, 0)),
                    pl.BlockSpec((TOKEN_TILE, TAGGED_ROWS, LANES), lambda i: (i, 0, 0)),
                    pl.BlockSpec((TOKEN_TILE, LANES), lambda i: (i, 0)),
                    pl.BlockSpec((1, LANES), lambda i: (0, 0)))


def _route_in_specs():
    return [_full((1, D_MODEL)), _full((D_MODEL, LANES)), _full((D_MODEL, LANES)), _full((1, LANES))]


def _even_out_kernel(ypp_ref, yps_ref, op_ref, os_ref, r_ref, hp_ref, hs_ref, gn_ref, w_ref, nffn_ref, wr_hi_ref,
                     wr_lo_ref, br_ref, h1_ref, xn_ref, info_ref, cnt_ref, carry, *, npt):
    o = _split_rows(op_ref, os_ref, npt)
    r = r_ref[...]
    parts = []
    for hd in range(GLA_HEADS):
        sl = slice(hd * GLA_DV, (hd + 1) * GLA_DV)
        oh = o[:, sl]
        oh = oh * lax.rsqrt(jnp.mean(oh * oh, axis=-1, keepdims=True) + RMS_EPS) * gn_ref[...]
        rh = r[:, sl]
        parts.append(oh * (rh * jax.nn.sigmoid(rh)))
    gla = jnp.concatenate(parts, axis=1)
    mix = _dot(_split_rows(ypp_ref, yps_ref, npt), w_ref[:POOL_WIDTH, :]) + _dot(gla, w_ref[POOL_WIDTH:, :])
    h1 = _split_rows(hp_ref, hs_ref, npt) + mix
    h1_ref[...] = h1
    _route(h1, nffn_ref, wr_hi_ref, wr_lo_ref, br_ref, xn_ref, info_ref, cnt_ref, carry)


def _even_out(y_pool_parts, o_gla_parts, proj, h_parts, n_p, gla_norm, w_out, nffn, wr_hi, wr_lo, br):
    n = proj.shape[0]
    return pl.pallas_call(
        functools.partial(_even_out_kernel, npt=n_p // TOKEN_TILE),
        out_shape=_ROUTE_OUT_SHAPES(n),
        grid=(n // TOKEN_TILE,),
        in_specs=_split_specs(TOKEN_TILE, POOL_WIDTH, n_p, False) + _split_specs(TOKEN_TILE, GLA_WIDTH, n_p, False)
        + [pl.BlockSpec((TOKEN_TILE, GLA_WIDTH), lambda i: (i, (IN_EVEN_MAIN - GLA_WIDTH) // GLA_WIDTH))]
        + _split_specs(TOKEN_TILE, D_MODEL, n_p, h_parts[0] is h_parts[1])
        + [_full((1, GLA_DV)), _full((D_MODEL, D_MODEL))] + _route_in_specs(),
        out_specs=_ROUTE_OUT_SPECS,
        scratch_shapes=[pltpu.VMEM((1, LANES), F32)],
        compiler_params=_params(), name="even_out")(
            *y_pool_parts, *o_gla_parts, proj, *h_parts, gla_norm, w_out, nffn, wr_hi, wr_lo, br)


def _rope_tile(x, cos, sin, lo_half):
    swapped = jnp.where(lo_half, pltpu.roll(x, LANES - HEAD_DIM // 2, axis=1), pltpu.roll(x, HEAD_DIM // 2, axis=1))
    return x * cos + swapped * sin


def _odd_in_kernel(h_ref, g_ref, w_ref, b_ref, cos_ref, sin_ref, q_ref, kv_ref):
    xn = _rms(h_ref[...], g_ref[...])
    qkv = _dot(xn, w_ref[...]) + b_ref[...]
    cos = cos_ref[...]
    sin = sin_ref[...]
    lo_half = (lax.broadcasted_iota(jnp.int32, cos.shape, 1) % HEAD_DIM) < HEAD_DIM // 2
    for j in range(Q_WIDTH // LANES):
        sl = slice(j * LANES, (j + 1) * LANES)
        q_ref[:, sl] = _rope_tile(qkv[:, sl], cos, sin, lo_half).astype(q_ref.dtype)
    kv_ref[:, :KV_WIDTH] = _rope_tile(qkv[:, Q_WIDTH:Q_WIDTH + KV_WIDTH], cos, sin, lo_half)
    kv_ref[:, KV_WIDTH:] = qkv[:, Q_WIDTH + KV_WIDTH:]


def _odd_in(h, g, w_qkv, b_qkv, cos, sin, table_block):
    n = h.shape[0]
    wq = w_qkv.shape[1]
    tab = pl.BlockSpec((TOKEN_TILE, LANES), lambda i: (table_block(i), 0))
    return pl.pallas_call(
        _odd_in_kernel,
        out_shape=(jax.ShapeDtypeStruct((n, Q_WIDTH), MXU_DTYPE), jax.ShapeDtypeStruct((n, 2 * KV_WIDTH), F32)),
        grid=(n // TOKEN_TILE,),
        in_specs=[pl.BlockSpec((TOKEN_TILE, D_MODEL), lambda i: (i, 0)), _full((1, D_MODEL)),
                  _full((D_MODEL, wq)), _full((1, wq)), tab, tab],
        out_specs=(pl.BlockSpec((TOKEN_TILE, Q_WIDTH), lambda i: (i, 0)),
                   pl.BlockSpec((TOKEN_TILE, 2 * KV_WIDTH), lambda i: (i, 0))),
        compiler_params=_params(), name="odd_in")(h, g, w_qkv, b_qkv, cos, sin)


def _attn_kernel(sink_ref, q_ref, kv0_ref, kv1_ref, kv2_ref, ck0_ref, ck1_ref, cv0_ref, cv1_ref, o_ref, *, sq):
    i = pl.program_id(0)
    smp = sq.is_sample(i)
    t = sq.tile_in_seq(i)
    kv0 = kv0_ref[...]
    k_old = jnp.where(smp, ck0_ref[...], kv2_ref[:, :KV_WIDTH])
    k_mid = jnp.where(smp, ck1_ref[...], kv1_ref[:, :KV_WIDTH])
    v_old = jnp.where(smp, cv0_ref[...], kv2_ref[:, KV_WIDTH:])
    v_mid = jnp.where(smp, cv1_ref[...], kv1_ref[:, KV_WIDTH:])
    pad = jnp.zeros((CHUNK, KV_WIDTH), F32)
    keys = jnp.concatenate([k_old, k_mid, kv0[:, :KV_WIDTH], pad], axis=0)
    vals = jnp.concatenate([v_old, v_mid, kv0[:, KV_WIDTH:], pad], axis=0)
    n_keys = 4 * CHUNK
    lane = lax.broadcasted_iota(jnp.int32, (n_keys, KV_WIDTH), 1)
    lo = lane < HEAD_DIM
    keys_sw = pltpu.roll(keys, HEAD_DIM, axis=1)
    vals_sw = pltpu.roll(vals, HEAD_DIM, axis=1)
    kcol = lax.broadcasted_iota(jnp.int32, (1, n_keys), 1)
    first_valid = jnp.where(smp, 0, (2 - jnp.minimum(t, 2)) * CHUNK)
    key_ok = jnp.logical_and(kcol >= first_valid, kcol < 3 * CHUNK)
    n_pairs = N_Q_HEADS // N_KV_HEADS // 2
    pair_of_row = lax.broadcasted_iota(jnp.int32, (n_pairs * CHUNK, 1), 0) // CHUNK
    for g in range(N_KV_HEADS):
        own, other = (keys, keys_sw) if g == 0 else (keys_sw, keys)
        vown, vother = (vals, vals_sw) if g == 0 else (vals_sw, vals)
        kb = jnp.concatenate([jnp.where(lo, own, 0.0), jnp.where(lo, 0.0, other)], axis=0).astype(MXU_DTYPE)
        vb = jnp.concatenate([jnp.where(lo, vown, 0.0), jnp.where(lo, 0.0, vother)], axis=0).astype(MXU_DTYPE)
        pairs = [slice((g * n_pairs + pr) * LANES, (g * n_pairs + pr + 1) * LANES) for pr in range(n_pairs)]
        qg = jnp.concatenate([q_ref[:, sl] for sl in pairs], axis=0)
        s = _dot_nt(qg, kb) * (HEAD_DIM ** -0.5)
        halves = []
        for half in range(2):
            sh = jnp.where(key_ok, s[:, half * n_keys:(half + 1) * n_keys], NEG_INF)
            sink = jnp.zeros((n_pairs * CHUNK, 1), F32)
            for pr in range(n_pairs):
                sink = jnp.where(pair_of_row == pr, sink_ref[2 * (g * n_pairs + pr) + half], sink)
            m = jnp.maximum(jnp.max(sh, axis=-1, keepdims=True), sink)
            p = jnp.exp(sh - m)
            halves.append(p * (1.0 / (jnp.sum(p, axis=-1, keepdims=True) + jnp.exp(sink - m))))
        o = _dot(jnp.concatenate(halves, axis=1), vb)
        for pr, sl in enumerate(pairs):
            o_ref[:, sl] = o[pr * CHUNK:(pr + 1) * CHUNK].astype(o_ref.dtype)


def _attention(sinks, q, kv, ck, cv, sq):
    n = q.shape[0]
    npt = sq.n_prompt_tiles
    prev = lambda d: (lambda i, s: (jnp.where(sq.is_sample(i), i, jnp.maximum(i - d, 0)), 0))
    cache = lambda d: (lambda i, s: (jnp.where(sq.is_sample(i), 2 * (i - npt) + d, 0), 0))
    kvspec = lambda f: pl.BlockSpec((CHUNK, 2 * KV_WIDTH), f)
    cspec = lambda f: pl.BlockSpec((CHUNK, KV_WIDTH), f)
    grid_spec = pltpu.PrefetchScalarGridSpec(
        num_scalar_prefetch=1, grid=(sq.n_tiles,),
        in_specs=[pl.BlockSpec((CHUNK, Q_WIDTH), lambda i, s: (i, 0)),
                  kvspec(prev(0)), kvspec(prev(1)), kvspec(prev(2)),
                  cspec(cache(0)), cspec(cache(1)), cspec(cache(0)), cspec(cache(1))],
        out_specs=pl.BlockSpec((CHUNK, Q_WIDTH), lambda i, s: (i, 0)))
    return pl.pallas_call(
        functools.partial(_attn_kernel, sq=sq),
        out_shape=jax.ShapeDtypeStruct((n, Q_WIDTH), MXU_DTYPE),
        grid_spec=grid_spec, compiler_params=_params(), name="attention")(sinks, q, kv, kv, kv, ck, ck, cv, cv)


def _odd_out_kernel(o_ref, h_ref, w_ref, b_ref, nffn_ref, wr_hi_ref, wr_lo_ref, br_ref,
                    h1_ref, xn_ref, info_ref, cnt_ref, carry):
    h1 = h_ref[...] + _dot(o_ref[...], w_ref[...]) + b_ref[...]
    h1_ref[...] = h1
    _route(h1, nffn_ref, wr_hi_ref, wr_lo_ref, br_ref, xn_ref, info_ref, cnt_ref, carry)


def _odd_out(o, h, w_out, b_out, nffn, wr_hi, wr_lo, br):
    n = h.shape[0]
    row = pl.BlockSpec((TOKEN_TILE, D_MODEL), lambda i: (i, 0))
    return pl.pallas_call(
        _odd_out_kernel,
        out_shape=_ROUTE_OUT_SHAPES(n),
        grid=(n // TOKEN_TILE,),
        in_specs=[row, row, _full((D_MODEL, D_MODEL)), _full((1, D_MODEL))] + _route_in_specs(),
        out_specs=_ROUTE_OUT_SPECS,
        scratch_shapes=[pltpu.VMEM((1, LANES), F32)],
        compiler_params=_params(), name="odd_out")(o, h, w_out, b_out, nffn, wr_hi, wr_lo, br)


def _token_copy(src, s, dst, d, sem, rows=ROW_TILES):
    return pltpu.make_async_copy(src.at[pl.ds(pl.multiple_of(s, rows), rows)],
                                 dst.at[pl.ds(pl.multiple_of(d, rows), rows)], sem)


ZERO_TOKENS = EXPERT_TILE // 2
HIDDEN_BLOCK = 512


def _zero_fill(fill_ref, zeros, xs_out, zsem):
    zeros[...] = jnp.zeros_like(zeros)

    def sweep(wait):
        def go(cp):
            cp.wait() if wait else cp.start()

        def tail(e, c):
            off, length = fill_ref[e], fill_ref[N_EXPERTS + e]
            for bit in range(EXPERT_TILE.bit_length() - 1):
                rows = (1 << bit) * TAGGED_ROWS

                @pl.when((length >> bit) & 1 == 1)
                def _():
                    o = pl.multiple_of(off + (length & ((1 << bit) - 1)) * TAGGED_ROWS, TAGGED_ROWS)
                    go(pltpu.make_async_copy(zeros.at[pl.ds(0, rows)], xs_out.at[pl.ds(o, rows)], zsem))
            return c

        lax.fori_loop(0, N_EXPERTS, tail, 0)

        def unused(t, c):
            o = pl.multiple_of(fill_ref[2 * N_EXPERTS] + t * ZERO_TOKENS * TAGGED_ROWS, TAGGED_ROWS)
            go(pltpu.make_async_copy(zeros, xs_out.at[pl.ds(o, ZERO_TOKENS * TAGGED_ROWS)], zsem))
            return c

        lax.fori_loop(0, fill_ref[2 * N_EXPERTS + 1], unused, 0)

    sweep(wait=False)
    sweep(wait=True)


def _dispatch_kernel(fill_ref, dest_hbm, x_ref, xs_out, dest_even, dest_odd, zeros, sem, dsems, zsem):
    i = pl.program_id(0)
    n_slots = TOKEN_TILE * TOP_K

    @pl.when(i == 0)
    def _():
        _zero_fill(fill_ref, zeros, xs_out, zsem)

    def run(parity, mine, other):
        def dest_rows(step, buf, s):
            return pltpu.make_async_copy(dest_hbm.at[pl.ds(step * n_slots, n_slots)], buf, dsems.at[s])

        if parity == 0:
            @pl.when(i == 0)
            def _():
                dest_rows(0, mine, parity).start()

        dest_rows(i, mine, parity).wait()

        @pl.when(i + 1 < pl.num_programs(0))
        def _():
            dest_rows(i + 1, other, 1 - parity).start()

        def issue(t, c):
            for k in range(TOP_K):
                _token_copy(x_ref, t * TAGGED_ROWS, xs_out, mine[t * TOP_K + k], sem, TAGGED_ROWS).start(priority=k % 2)
            return c

        lax.fori_loop(0, TOKEN_TILE, issue, 0, unroll=2)

    @pl.when(i % 2 == 0)
    def _():
        run(0, dest_even, dest_odd)

    @pl.when(i % 2 == 1)
    def _():
        run(1, dest_odd, dest_even)

    whole = xs_out.at[pl.ds(0, n_slots * TAGGED_ROWS)]
    pltpu.make_async_copy(whole, whole, sem).wait()


def _dispatch(fill, dest, xt, n_rows):
    n = xt.shape[0] // TAGGED_ROWS
    grid_spec = pltpu.PrefetchScalarGridSpec(
        num_scalar_prefetch=1, grid=(n // TOKEN_TILE,),
        in_specs=[pl.BlockSpec(memory_space=pl.ANY),
                  pl.BlockSpec((TOKEN_TILE * TAGGED_ROWS, LANES), lambda i, f: (i, 0))],
        out_specs=pl.BlockSpec(memory_space=pl.ANY),
        scratch_shapes=[pltpu.SMEM((TOKEN_TILE * TOP_K,), jnp.int32), pltpu.SMEM((TOKEN_TILE * TOP_K,), jnp.int32),
                        pltpu.VMEM((ZERO_TOKENS * TAGGED_ROWS, LANES), xt.dtype),
                        pltpu.SemaphoreType.DMA, pltpu.SemaphoreType.DMA((2,)), pltpu.SemaphoreType.DMA])
    return pl.pallas_call(
        _dispatch_kernel, out_shape=jax.ShapeDtypeStruct((n_rows * TAGGED_ROWS, LANES), xt.dtype),
        grid_spec=grid_spec, compiler_params=_params(), name="moe_dispatch")(fill, dest, xt)


def _experts_kernel(te_ref, tr_ref, x_ref, wgu_ref, bgu_ref, wd_ref, bd_ref, yt_ref,
                    wgu_mxu, wd_mxu, ybuf, place_vmem, place_smem, ysems, psem, *, n_tokens):
    i = pl.program_id(0)
    slot = i % 2
    prev = 1 - slot
    tile_rows = EXPERT_TILE * ROW_TILES
    spare = yt_ref.shape[0] - 2 * tile_rows
    col = lax.broadcasted_iota(jnp.int32, (1, EXPERT_TILE), 1)

    def rows_done(s):
        whole = yt_ref.at[pl.ds(0, tile_rows)]
        pltpu.make_async_copy(whole, whole, ysems.at[s]).wait()

    def places_to_smem(first_rows, s):
        place_vmem[...] = jnp.broadcast_to(first_rows, place_vmem.shape)
        return pltpu.make_async_copy(place_vmem, place_smem.at[s], psem)

    def start_row_copy(j, s, queue=0):
        _token_copy(ybuf.at[s], j * ROW_TILES, yt_ref, place_smem[s, 0, j], ysems.at[s]).start(priority=queue)

    @pl.when(i == 0)
    def _():
        ybuf[...] = jnp.zeros(ybuf.shape, F32)
        for s in (1, 0):
            cp = pltpu.make_async_copy(ybuf.at[0], yt_ref.at[pl.ds(spare + s * tile_rows, tile_rows)], ysems.at[s])
            cp.start()
            if s == 1:
                cp.wait()
        cp = places_to_smem(spare + tile_rows + col * ROW_TILES, 1)
        cp.start()
        cp.wait()

    @pl.when(jnp.logical_or(i == 0, te_ref[i] != te_ref[jnp.maximum(i - 1, 0)]))
    def _():
        wgu_mxu[...] = wgu_ref[0, 0].astype(MXU_DTYPE)
        wd_mxu[...] = wd_ref[0, 0].astype(MXU_DTYPE)

    n_valid = tr_ref[i]
    after_last = jnp.logical_and(n_valid == 0, jnp.logical_and(i >= 1, tr_ref[jnp.maximum(i - 1, 0)] > 0))

    @pl.when(n_valid > 0)
    def _():
        rows_done(slot)
        for j in range(EXPERT_TILE):
            start_row_copy(j, prev, j % 2)
        wide = x_ref[...].reshape(EXPERT_TILE, TAGGED_ROWS * LANES)
        tag = wide[:, D_MODEL:D_MODEL + LANES].astype(F32)
        lane = lax.broadcasted_iota(jnp.int32, (EXPERT_TILE, LANES), 1)
        is_id = jnp.logical_and(lane >= TAG_DIGITS, lane < TAG_DIGITS + TOP_K)
        mine = jnp.logical_and(is_id, tag == te_ref[i].astype(F32))
        weight = jnp.where(lane == 0, 1.0, jnp.where(lane == 1, 256.0, 65536.0))
        terms = jnp.where(lane < TAG_DIGITS, tag * weight, jnp.where(mine, ((lane - TAG_DIGITS) * n_tokens).astype(F32), 0.0))
        place = jnp.sum(terms, axis=-1, keepdims=True)
        hi = jnp.floor(place * (1.0 / 65536.0))
        mid = jnp.floor((place - hi * 65536.0) * (1.0 / 256.0))
        digits = jnp.where(lane == 0, place - hi * 65536.0 - mid * 256.0, jnp.where(lane == 1, mid, jnp.where(lane == 2, hi, 0.0)))
        pick = jnp.where(lax.broadcasted_iota(jnp.int32, (ROW_TILES, LANES), 0)
                         == lax.broadcasted_iota(jnp.int32, (ROW_TILES, LANES), 1), 1.0, 0.0)
        planes = _dot_nt(pick, digits)
        place_row = (planes[0:1] + 256.0 * planes[1:2] + 65536.0 * planes[2:3]).astype(jnp.int32) * ROW_TILES
        to_smem = places_to_smem(jnp.where(col < n_valid, place_row, spare + slot * tile_rows + col * ROW_TILES), slot)
        to_smem.start()

        x = wide[:, :D_MODEL]
        out = None
        for c in range(D_MODEL // HIDDEN_BLOCK):
            cols = slice(c * HIDDEN_BLOCK, (c + 1) * HIDDEN_BLOCK)
            ucols = slice(D_MODEL + c * HIDDEN_BLOCK, D_MODEL + (c + 1) * HIDDEN_BLOCK)
            gate = jnp.minimum(_dot(x, wgu_mxu[:, cols]) + bgu_ref[0, 0, :, cols], SWIGLU_LIMIT)
            up = jnp.clip(_dot(x, wgu_mxu[:, ucols]) + bgu_ref[0, 0, :, ucols], -SWIGLU_LIMIT, SWIGLU_LIMIT)
            act = (up + 1.0) * gate * jax.nn.sigmoid(SWIGLU_ALPHA * gate)
            part = _dot(act, wd_mxu[cols, :])
            out = part if out is None else out + part
        _store_token_tiles(ybuf.at[slot], out + bd_ref[0, 0])
        to_smem.wait()

    @pl.when(after_last)
    def _():
        rows_done(slot)
        def issue(j, c):
            start_row_copy(j, prev)
            return c

        lax.fori_loop(0, EXPERT_TILE, issue, 0, unroll=8)
        rows_done(prev)


def _experts(layer, tile_expert, tile_rows, xs, wgu, bgu, wd, bd, n_tokens):
    rows = xs.shape[0]
    w = lambda shape: pl.BlockSpec((1, 1) + shape, lambda i, te, tr: (layer, te[i], 0, 0))
    grid_spec = pltpu.PrefetchScalarGridSpec(
        num_scalar_prefetch=2, grid=(rows // EXPERT_TILE,),
        in_specs=[pl.BlockSpec((EXPERT_TILE, TAGGED_ROWS, LANES), lambda i, te, tr: (i, 0, 0)),
                  w((D_MODEL, 2 * D_MODEL)), w((1, 2 * D_MODEL)), w((D_MODEL, D_MODEL)), w((1, D_MODEL))],
        out_specs=pl.BlockSpec(memory_space=pl.ANY),
        scratch_shapes=[pltpu.VMEM((D_MODEL, 2 * D_MODEL), MXU_DTYPE), pltpu.VMEM((D_MODEL, D_MODEL), MXU_DTYPE),
                        pltpu.VMEM((2, EXPERT_TILE * ROW_TILES, LANES), F32),
                        pltpu.VMEM((ROW_TILES, EXPERT_TILE), jnp.int32), pltpu.SMEM((2, ROW_TILES, EXPERT_TILE), jnp.int32),
                        pltpu.SemaphoreType.DMA((2,)), pltpu.SemaphoreType.DMA])
    yt_rows = (n_tokens * TOP_K + 2 * EXPERT_TILE) * ROW_TILES
    return pl.pallas_call(
        functools.partial(_experts_kernel, n_tokens=n_tokens), out_shape=jax.ShapeDtypeStruct((yt_rows, LANES), F32),
        grid_spec=grid_spec, compiler_params=_params(), name="moe_experts")(
            tile_expert, tile_rows, xs, wgu, bgu, wd, bd)


def _combine_kernel(*refs, final, npt):
    y_refs, outs = refs[:TOP_K], refs[TOP_K + 7:]
    info_ref, h_ref, p_prompt_ref, p_sample_ref, pp_ref, pg_ref, nf_ref = refs[TOP_K:TOP_K + 7]
    i = pl.program_id(0)
    gates = info_ref[...]
    moe = None
    for k in range(TOP_K):
        rows = _load_token_tiles(y_refs[k], COMBINE_TILE)
        moe = gates[:, k:k + 1] * rows if moe is None else moe + gates[:, k:k + 1] * rows
    h2 = h_ref[...] + moe
    p = jnp.where(i < npt, p_prompt_ref[0], p_sample_ref[0])
    h3 = h2 + jax.nn.sigmoid(_dot(h2, pg_ref[...])) * _dot(p, pp_ref[...])
    if not final:
        outs[0][...] = h3
        return
    y = _rms(h3, nf_ref[...])

    @pl.when(i < npt)
    def _():
        outs[0][...] = y

    @pl.when(i >= npt)
    def _():
        outs[1][...] = y


def _combine(layer, yt, info, h1, p_prompt, p_sample, ple_proj, ple_gate, norm_final, final):
    n = h1.shape[0]
    n_p = p_prompt.shape[1]
    npt = n_p // COMBINE_TILE
    steps = n // COMBINE_TILE
    row = lambda w: pl.BlockSpec((COMBINE_TILE, w), lambda i: (i, 0))
    rank_rows = lambda k: pl.BlockSpec((COMBINE_TILE * ROW_TILES, LANES), lambda i: (k * steps + i, 0))
    prompt_rows = lambda i: jnp.minimum(i, npt - 1)
    sample_rows = lambda i: jnp.maximum(i - npt, 0)
    if final:
        out_shape = (jax.ShapeDtypeStruct((n_p, D_MODEL), F32), jax.ShapeDtypeStruct((n - n_p, D_MODEL), F32))
        out_specs = (pl.BlockSpec((COMBINE_TILE, D_MODEL), lambda i: (prompt_rows(i), 0)),
                     pl.BlockSpec((COMBINE_TILE, D_MODEL), lambda i: (sample_rows(i), 0)))
    else:
        out_shape, out_specs = jax.ShapeDtypeStruct((n, D_MODEL), F32), row(D_MODEL)
    return pl.pallas_call(
        functools.partial(_combine_kernel, final=final, npt=npt),
        out_shape=out_shape,
        grid=(steps,),
        in_specs=[rank_rows(k) for k in range(TOP_K)] + [
            row(LANES), row(D_MODEL),
            pl.BlockSpec((1, COMBINE_TILE, PLE_DIM), lambda i: (layer, prompt_rows(i), 0)),
            pl.BlockSpec((1, COMBINE_TILE, PLE_DIM), lambda i: (layer, sample_rows(i), 0)),
            _full((PLE_DIM, D_MODEL)), _full((D_MODEL, D_MODEL)), _full((1, D_MODEL))],
        out_specs=out_specs,
        compiler_params=_params(), name="moe_combine")(
            *([yt] * TOP_K), info, h1, p_prompt, p_sample, ple_proj, ple_gate, norm_final)


def _moe_and_embed(layer, h1, xt, info, counts, p_prompt, p_sample, wgu, bgu, wd, bd, ple_proj, ple_gate,
                   norm_final, final):
    n = h1.shape[0]
    n_tiles = (n * TOP_K + N_EXPERTS * (EXPERT_TILE - 1)) // EXPERT_TILE + 1
    ids = info[:, TOP_K:2 * TOP_K].astype(jnp.int32)
    rank = info[:, 2 * TOP_K:3 * TOP_K].astype(jnp.int32)
    cnt = counts[0, :N_EXPERTS].astype(jnp.int32)
    padded = ((cnt + EXPERT_TILE - 1) // EXPERT_TILE) * EXPERT_TILE
    ends = jnp.cumsum(padded)
    starts = ends - padded
    dest = ((starts[ids] + rank) * TAGGED_ROWS).reshape(-1)
    tile_start = jnp.arange(n_tiles, dtype=jnp.int32) * EXPERT_TILE
    tile_expert = jnp.minimum(jnp.sum((tile_start[:, None] >= ends[None, :]).astype(jnp.int32), axis=1), N_EXPERTS - 1)
    tile_rows = jnp.clip(cnt[tile_expert] - (tile_start - starts[tile_expert]), 0, EXPERT_TILE)
    tile_rows = jnp.where(tile_start < ends[-1], tile_rows, 0).astype(jnp.int32)
    fill = jnp.concatenate([(starts + cnt) * TAGGED_ROWS, padded - cnt,
                            jnp.stack([ends[-1] * TAGGED_ROWS,
                                       (n_tiles * EXPERT_TILE - ends[-1]) // ZERO_TOKENS])]).astype(jnp.int32)
    xs = _dispatch(fill, dest, xt.reshape(n * TAGGED_ROWS, LANES), n_tiles * EXPERT_TILE)
    yt = _experts(layer, tile_expert, tile_rows, xs.reshape(-1, TAGGED_ROWS, LANES), wgu, bgu, wd, bd, n)
    return _combine(layer, yt, info, h1, p_prompt, p_sample, ple_proj, ple_gate, norm_final, final)


def _rope_tables(prompt_len, n_sample_seq, sample_len):
    half = HEAD_DIM // 2
    inv = jnp.power(jnp.float32(ROPE_THETA), -jnp.arange(half, dtype=F32) / half)
    pos = jnp.concatenate([jnp.arange(prompt_len), jnp.tile(PAST_LEN + jnp.arange(sample_len), n_sample_seq)])
    ang = pos.astype(F32)[:, None] * inv[None, :]
    cos = jnp.tile(jnp.cos(ang), (1, LANES // half))
    sin = jnp.tile(jnp.concatenate([-jnp.sin(ang), jnp.sin(ang)], axis=1), (1, LANES // HEAD_DIM))
    return cos, sin


def kernel(x_prompt, x_sample, state_pool, state_gla, cache_k, cache_v, p_prompt, p_sample, norm_mix, norm_ffn, norm_final, w_in_even, pool_w, pool_scale, gla_w_gate, gla_b_gate, gla_norm, w_out_even, w_qkv_odd, b_qkv_odd, attn_sinks, w_out_odd, b_out_odd, w_router, b_router, w_gate_up, b_gate_up, w_down, b_down, ple_proj, ple_gate):
    bsz, t_len, _ = x_prompt.shape
    dec_bsz, dec_len, _ = x_sample.shape
    depth = norm_mix.shape[0]
    n_p, n_s = bsz * t_len, dec_bsz * dec_len
    n = n_p + n_s
    assert dec_len == CHUNK and n_s == TOKEN_TILE and t_len % TOKEN_TILE == 0 and cache_k.shape[2] == WINDOW
    sq = _Seq(bsz, t_len, dec_bsz)
    grp_p = _Group(bsz, t_len, min(SEQ_TILE, t_len), 0, 0)
    grp_s = _Group(dec_bsz, dec_len, dec_len, n_p, PAST_LEN)
    bf = lambda a: a.astype(MXU_DTYPE)
    row = lambda a: a.reshape(1, -1)

    h_parts = (x_prompt.reshape(n_p, D_MODEL), x_sample.reshape(n_s, D_MODEL))
    p_parts = (p_prompt.reshape(depth, n_p, PLE_DIM), p_sample.reshape(depth, n_s, PLE_DIM))
    b_gu = b_gate_up.reshape(depth, N_EXPERTS, 1, 2 * D_MODEL)
    b_dn = b_down.reshape(depth, N_EXPERTS, 1, D_MODEL)
    cos, sin = _rope_tables(t_len, dec_bsz, dec_len)
    tiles_per_seq = t_len // TOKEN_TILE
    table_block = lambda i: jnp.where(i < bsz * tiles_per_seq, i % tiles_per_seq, tiles_per_seq)

    pools, glas, new_k, new_v = [], [], [], []
    for i in range(depth):
        if i % 2 == 0:
            e = i // 2
            w_in = jnp.pad(bf(w_in_even[e]), ((0, 0), (0, (-IN_EVEN) % LANES)))
            wg = jnp.pad(bf(gla_w_gate[e]), ((0, w_in.shape[1] - IN_EVEN_MAIN - GLA_RANK), (0, 0)))
            proj = _even_in(h_parts, n, n_p, row(norm_mix[i]), w_in, wg, row(gla_b_gate[e]))
            pw, ps = bf(pool_w[e]), row(pool_scale[e])
            y_pool_p, pool_p = _pool(proj, jnp.zeros((bsz, 16, POOL_WIDTH), F32), pw, ps, grp_p)
            y_pool_s, pool_s = _pool(proj, jnp.pad(state_pool[e], ((0, 0), (1, 0), (0, 0))), pw, ps, grp_s)
            o_gla_p, gla_p = _gla(proj, jnp.zeros((bsz,) + state_gla.shape[2:], F32), grp_p)
            o_gla_s, gla_s = _gla(proj, state_gla[e], grp_s)
            pools.append((pool_p[:, 1:], pool_s[:, 1:]))
            glas.append((gla_p, gla_s))
            wr = jnp.pad(w_router[i], ((0, 0), (0, LANES - N_EXPERTS)))
            wr_hi = bf(wr)
            wr_lo = bf(wr - wr_hi.astype(F32))
            br = jnp.pad(row(b_router[i]), ((0, 0), (0, LANES - N_EXPERTS)), constant_values=NEG_INF)
            h1, xn, info, counts = _even_out((y_pool_p, y_pool_s), (o_gla_p, o_gla_s), proj, h_parts, n_p,
                                             row(gla_norm[e]), bf(w_out_even[e]), row(norm_ffn[i]), wr_hi, wr_lo, br)
        else:
            o = i // 2
            h = h_parts[0]
            q, kv = _odd_in(h, row(norm_mix[i]), bf(w_qkv_odd[o]), row(b_qkv_odd[o]), cos, sin, table_block)
            ck = cache_k[o].reshape(dec_bsz * WINDOW, KV_WIDTH)
            cv = cache_v[o].reshape(dec_bsz * WINDOW, KV_WIDTH)
            att = _attention(attn_sinks[o], q, kv, ck, cv, sq)
            kv_p = kv[:n_p].reshape(bsz, t_len, 2 * KV_WIDTH)[:, -WINDOW:]
            kv_s = kv[n_p:].reshape(dec_bsz, dec_len, 2 * KV_WIDTH)
            hd = (N_KV_HEADS, HEAD_DIM)
            new_k.append((kv_p[..., :KV_WIDTH].reshape(bsz, WINDOW, *hd),
                          jnp.concatenate([cache_k[o], kv_s[..., :KV_WIDTH].reshape(dec_bsz, dec_len, *hd)], axis=1)[:, -WINDOW:]))
            new_v.append((kv_p[..., KV_WIDTH:].reshape(bsz, WINDOW, *hd),
                          jnp.concatenate([cache_v[o], kv_s[..., KV_WIDTH:].reshape(dec_bsz, dec_len, *hd)], axis=1)[:, -WINDOW:]))
            wr = jnp.pad(w_router[i], ((0, 0), (0, LANES - N_EXPERTS)))
            wr_hi = bf(wr)
            wr_lo = bf(wr - wr_hi.astype(F32))
            br = jnp.pad(row(b_router[i]), ((0, 0), (0, LANES - N_EXPERTS)), constant_values=NEG_INF)
            h1, xn, info, counts = _odd_out(att, h, bf(w_out_odd[o]), row(b_out_odd[o]),
                                            row(norm_ffn[i]), wr_hi, wr_lo, br)
        final = i == depth - 1
        out = _moe_and_embed(i, h1, xn, info, counts, *p_parts, w_gate_up, b_gu, w_down, b_dn,
                             bf(ple_proj[i]), bf(ple_gate[i]), row(norm_final), final)
        h_parts = out if final else (out, out)

    y_prompt = h_parts[0].reshape(bsz, t_len, D_MODEL)
    y_sample = h_parts[1].reshape(dec_bsz, dec_len, D_MODEL)
    part = lambda pairs, j: jnp.stack([p[j] for p in pairs])
    return (y_prompt, y_sample, part(pools, 0), part(glas, 0), part(new_k, 0), part(new_v, 0),
            part(pools, 1), part(glas, 1), part(new_k, 1), part(new_v, 1))
```

```python
import functools

import jax
import jax.numpy as jnp
from jax import lax
from jax.experimental import pallas as pl
from jax.experimental.pallas import tpu as pltpu

F32 = jnp.float32
MXU_DTYPE = jnp.bfloat16

V7X_VMEM_BYTES = 64 * 1024 * 1024
VMEM_LIMIT = (V7X_VMEM_BYTES * 7) // 8
LANES = 128

D_MODEL = 1024
CHUNK = 64
PAST_LEN = 2048
PLE_DIM = 256
RMS_EPS = 1e-6
POOL_WINDOWS = (2, 4, 8, 16)
POOL_WIDTH = 512
POOL_GROUP_DIM = 128
POOL_STATE = 15
GLA_HEADS = 4
GLA_DK = 64
GLA_DV = 128
GLA_KEY_WIDTH = GLA_HEADS * GLA_DK
GLA_WIDTH = GLA_HEADS * GLA_DV
GLA_RANK = 16
GLA_TAU = 16.0
GLA_BLOCK = 16
IN_EVEN = POOL_WIDTH + 2 * GLA_KEY_WIDTH + 2 * GLA_WIDTH + GLA_RANK
IN_EVEN_MAIN = IN_EVEN - GLA_RANK
PROJ_WIDTH = IN_EVEN_MAIN + GLA_KEY_WIDTH
N_Q_HEADS = 16
N_KV_HEADS = 2
HEAD_DIM = 64
WINDOW = 128
ROPE_THETA = 10000.0
Q_WIDTH = N_Q_HEADS * HEAD_DIM
KV_WIDTH = N_KV_HEADS * HEAD_DIM
N_EXPERTS = 32
TOP_K = 4
SWIGLU_LIMIT = 7.0
SWIGLU_ALPHA = 1.702
NEG_INF = -1e30

TOKEN_TILE = 512
EXPERT_TILE = 512
COMBINE_TILE = 512
DISPATCH_TILES = (1280, TOKEN_TILE)


def _dot(a, b):
    return jnp.dot(a.astype(MXU_DTYPE), b.astype(MXU_DTYPE), preferred_element_type=F32)


def _dot_nt(a, b):
    return lax.dot_general(a.astype(MXU_DTYPE), b.astype(MXU_DTYPE), (((1,), (1,)), ((), ())),
                           preferred_element_type=F32)


def _split3(x):
    x1 = x.astype(MXU_DTYPE)
    r1 = x - x1.astype(F32)
    x2 = r1.astype(MXU_DTYPE)
    x3 = (r1 - x2.astype(F32)).astype(MXU_DTYPE)
    return x1, x2, x3


def _rms(x, g):
    return x * lax.rsqrt(jnp.mean(x * x, axis=-1, keepdims=True) + RMS_EPS) * g


ROW_TILES = D_MODEL // LANES
TAGGED_ROWS = 2 * ROW_TILES
TAG_DIGITS = 3


def _store_token_tiles(ref, x, pitch=ROW_TILES):
    for s in range(ROW_TILES):
        ref[pl.ds(s, x.shape[0], stride=pitch), :] = x[:, s * LANES:(s + 1) * LANES]


def _load_token_tiles(ref, rows, pitch=ROW_TILES):
    return jnp.concatenate([ref[pl.ds(s, rows, stride=pitch), :] for s in range(ROW_TILES)], axis=1)


def _params(n_axes=1):
    return pltpu.CompilerParams(dimension_semantics=("arbitrary",) * n_axes, vmem_limit_bytes=VMEM_LIMIT)


def _full(shape):
    return pl.BlockSpec(shape, lambda *_: (0,) * len(shape))


class _Seq:
    def __init__(self, n_prompt_seq, prompt_len, n_sample_seq):
        self.tiles_per_seq = prompt_len // CHUNK
        self.n_prompt_seq = n_prompt_seq
        self.n_prompt_tiles = n_prompt_seq * self.tiles_per_seq
        self.n_tiles = self.n_prompt_tiles + n_sample_seq
        self.n_seq = n_prompt_seq + n_sample_seq

    def is_sample(self, i):
        return i >= self.n_prompt_tiles

    def tile_in_seq(self, i):
        return jnp.where(self.is_sample(i), 0, i % self.tiles_per_seq)

    def seq(self, i):
        return jnp.where(self.is_sample(i), self.n_prompt_seq + i - self.n_prompt_tiles, i // self.tiles_per_seq)

    def last(self, i):
        return jnp.logical_or(self.is_sample(i), i % self.tiles_per_seq == self.tiles_per_seq - 1)

    def pos0(self, i):
        return jnp.where(self.is_sample(i), PAST_LEN, self.tile_in_seq(i) * CHUNK)


class _Group:
    def __init__(self, n_seq, seq_len, tile, row0, pos_base):
        assert seq_len % tile == 0 and row0 % tile == 0
        self.n_seq, self.tile, self.pos_base = n_seq, tile, pos_base
        self.tiles_per_seq = seq_len // tile
        self.n_tiles = n_seq * self.tiles_per_seq
        self.block0 = row0 // tile

    def block(self, i):
        return self.block0 + i

    def seq(self, i):
        return i // self.tiles_per_seq

    def tile_in_seq(self, i):
        return i % self.tiles_per_seq

    def last(self, i):
        return i % self.tiles_per_seq == self.tiles_per_seq - 1

    def pos0(self, i):
        return self.pos_base + self.tile_in_seq(i) * self.tile


SEQ_TILE = 256


def _split_specs(tile, width, n_prompt, same_array):
    npt = n_prompt // tile
    off = npt if same_array else 0
    return [pl.BlockSpec((tile, width), lambda i: (jnp.minimum(i, npt - 1), 0)),
            pl.BlockSpec((tile, width), lambda i: (jnp.maximum(i - npt, 0) + off, 0))]


def _split_rows(prompt_ref, sample_ref, n_prompt_tiles):
    return jnp.where(pl.program_id(0) < n_prompt_tiles, prompt_ref[...], sample_ref[...])


def _even_in_kernel(hp_ref, hs_ref, g_ref, w_ref, wg_ref, bg_ref, out_ref, *, npt):
    xn = _rms(_split_rows(hp_ref, hs_ref, npt), g_ref[...])
    proj = _dot(xn, w_ref[...])
    z = proj[:, IN_EVEN_MAIN:]
    a = _dot(z, wg_ref[...]) + bg_ref[...]
    log_alpha = (jnp.minimum(a, 0.0) - jnp.log1p(jnp.exp(-jnp.abs(a)))) * (1.0 / GLA_TAU)
    out_ref[:, :IN_EVEN_MAIN] = proj[:, :IN_EVEN_MAIN]
    out_ref[:, IN_EVEN_MAIN:] = log_alpha


def _even_in(h_parts, n, n_p, g, w_in, wg, bg):
    wp = w_in.shape[1]
    return pl.pallas_call(
        functools.partial(_even_in_kernel, npt=n_p // TOKEN_TILE),
        out_shape=jax.ShapeDtypeStruct((n, PROJ_WIDTH), F32),
        grid=(n // TOKEN_TILE,),
        in_specs=_split_specs(TOKEN_TILE, D_MODEL, n_p, h_parts[0] is h_parts[1]) + [
            _full((1, D_MODEL)), _full((D_MODEL, wp)), _full((wp - IN_EVEN_MAIN, GLA_KEY_WIDTH)),
            _full((1, GLA_KEY_WIDTH))],
        out_specs=pl.BlockSpec((TOKEN_TILE, PROJ_WIDTH), lambda i: (i, 0)),
        compiler_params=_params(), name="even_in")(*h_parts, g, w_in, wg, bg)


def _pool_kernel(u_ref, init_ref, pw_ref, ps_ref, y_ref, st_ref, buf, *, grp):
    i = pl.program_id(0)
    rows = grp.tile

    @pl.when(grp.tile_in_seq(i) == 0)
    def _():
        buf[0:16, :] = init_ref[0]

    u = u_ref[...]
    buf[16:16 + rows, :] = u
    pos = grp.pos0(i) + lax.broadcasted_iota(jnp.int32, (rows, 1), 0)
    for g, w in enumerate(POOL_WINDOWS):
        sl = slice(g * POOL_GROUP_DIM, (g + 1) * POOL_GROUP_DIM)
        acc = u[:, sl]
        for j in range(1, w):
            acc = acc + buf[16 - j:16 - j + rows, sl]
        cnt = jnp.minimum(w, pos + 1).astype(F32)
        d = acc / cnt - u[:, sl]
        y_ref[:, sl] = _dot(d, pw_ref[g]) * ps_ref[:, sl]
    tail = buf[rows:rows + 16, :]
    st_ref[0] = tail
    buf[0:16, :] = tail


def _pool(proj, init, pool_w, pool_scale, grp):
    return pl.pallas_call(
        functools.partial(_pool_kernel, grp=grp),
        out_shape=(jax.ShapeDtypeStruct((grp.n_tiles * grp.tile, POOL_WIDTH), F32),
                   jax.ShapeDtypeStruct((grp.n_seq, 16, POOL_WIDTH), F32)),
        grid=(grp.n_tiles,),
        in_specs=[pl.BlockSpec((grp.tile, POOL_WIDTH), lambda i: (grp.block(i), 0)),
                  pl.BlockSpec((1, 16, POOL_WIDTH), lambda i: (grp.seq(i), 0, 0)),
                  _full((len(POOL_WINDOWS), POOL_GROUP_DIM, POOL_GROUP_DIM)), _full((1, POOL_WIDTH))],
        out_specs=(pl.BlockSpec((grp.tile, POOL_WIDTH), lambda i: (i, 0)),
                   pl.BlockSpec((1, 16, POOL_WIDTH), lambda i: (grp.seq(i), 0, 0))),
        scratch_shapes=[pltpu.VMEM((grp.tile + 16, POOL_WIDTH), F32)],
        compiler_params=_params(), name="pool")(proj, init, pool_w, pool_scale)


def _gla_kernel(q_ref, k_ref, v_ref, g_ref, s0_ref, o_ref, sout_ref, state, before, *, grp):
    i = pl.program_id(0)
    tile = grp.tile
    n_blk = tile // GLA_BLOCK

    @pl.when(grp.tile_in_seq(i) == 0)
    def _():
        state[...] = s0_ref[0]

    g = g_ref[...]
    row = lax.broadcasted_iota(jnp.int32, (tile, tile), 0)
    col = lax.broadcasted_iota(jnp.int32, (tile, tile), 1)
    same = (row >> 4) == (col >> 4)
    causal = jnp.logical_and(same, col <= row)
    tri = jnp.where(causal, 1.0, 0.0).astype(MXU_DTYPE)
    ones = jnp.where(same, 1.0, 0.0).astype(MXU_DTYPE)
    g1, g2, g3 = _split3(g)
    b = _dot(tri, g1) + _dot(tri, g2) + _dot(tri, g3)
    b_last = _dot(ones, g1) + _dot(ones, g2) + _dot(ones, g3)
    q_t = q_ref[...] * (GLA_DK ** -0.5) * jnp.exp(b)
    k = k_ref[...]
    k_t = k * jnp.exp(-b)
    k_dec_t = (k * jnp.exp(b_last - b)).T
    sel = jnp.where((lax.broadcasted_iota(jnp.int32, (tile, LANES), 0) >> 4)
                    == lax.broadcasted_iota(jnp.int32, (tile, LANES), 1), 1.0, 0.0).astype(MXU_DTYPE)
    t1, t2, t3 = _split3(g.T)
    blk_decay = jnp.exp(_dot(t1, sel) + _dot(t2, sel) + _dot(t3, sel))
    v = v_ref[...]
    stacked = n_blk * GLA_DK
    dk_bits, blk_bits = GLA_DK.bit_length() - 1, GLA_BLOCK.bit_length() - 1
    upd_live = (lax.broadcasted_iota(jnp.int32, (stacked, tile), 0) >> dk_bits
                == lax.broadcasted_iota(jnp.int32, (stacked, tile), 1) >> blk_bits)
    qry_live = (lax.broadcasted_iota(jnp.int32, (tile, stacked), 0) >> blk_bits
                == lax.broadcasted_iota(jnp.int32, (tile, stacked), 1) >> dk_bits)
    low_half = lax.broadcasted_iota(jnp.int32, (tile, LANES), 1) < GLA_DK
    for h in range(GLA_HEADS):
        ks = slice(h * GLA_DK, (h + 1) * GLA_DK)
        vs = slice(h * GLA_DV, (h + 1) * GLA_DV)
        vh = v[:, vs]
        scores = jnp.where(causal, _dot_nt(q_t[:, ks], k_t[:, ks]), 0.0)
        o = _dot(scores, vh)
        upd = _dot(jnp.where(upd_live, jnp.concatenate([k_dec_t[ks, :]] * n_blk, axis=0), 0.0), vh)
        s = state[h]
        for j in range(n_blk):
            rows = slice(j * GLA_DK, (j + 1) * GLA_DK)
            before[rows, :] = s
            s = blk_decay[ks, j:j + 1] * s + upd[rows, :]
        state[h] = s
        pair = q_t[:, (h // 2) * LANES:(h // 2 + 1) * LANES]
        swapped = pltpu.roll(pair, GLA_DK, axis=1)
        both = jnp.where(low_half, pair, swapped) if h % 2 == 0 else jnp.where(low_half, swapped, pair)
        q_exp = jnp.where(qry_live, jnp.concatenate([both] * (stacked // LANES), axis=1), 0.0)
        o_ref[:, vs] = o + _dot(q_exp, before[...])

    @pl.when(grp.last(i))
    def _():
        sout_ref[0] = state[...]


def _gla(proj, s0, grp):
    st_shape = (grp.n_seq, GLA_HEADS, GLA_DK, GLA_DV)
    st_spec = pl.BlockSpec((1, GLA_HEADS, GLA_DK, GLA_DV), lambda i: (grp.seq(i), 0, 0, 0))
    kw = GLA_KEY_WIDTH
    cols = lambda width, c: pl.BlockSpec((grp.tile, width), lambda i: (grp.block(i), c))
    return pl.pallas_call(
        functools.partial(_gla_kernel, grp=grp),
        out_shape=(jax.ShapeDtypeStruct((grp.n_tiles * grp.tile, GLA_WIDTH), F32), jax.ShapeDtypeStruct(st_shape, F32)),
        grid=(grp.n_tiles,),
        in_specs=[cols(kw, POOL_WIDTH // kw), cols(kw, POOL_WIDTH // kw + 1),
                  cols(GLA_WIDTH, (POOL_WIDTH + 2 * kw) // GLA_WIDTH), cols(kw, IN_EVEN_MAIN // kw), st_spec],
        out_specs=(pl.BlockSpec((grp.tile, GLA_WIDTH), lambda i: (i, 0)), st_spec),
        scratch_shapes=[pltpu.VMEM((GLA_HEADS, GLA_DK, GLA_DV), F32),
                        pltpu.VMEM((grp.tile // GLA_BLOCK * GLA_DK, GLA_DV), F32)],
        compiler_params=_params(), name="gla")(proj, proj, proj, proj, s0)


def _route(h1, nffn_ref, wr_hi_ref, wr_lo_ref, br_ref, xn_ref, info_ref, cnt_ref, carry):
    tm = h1.shape[0]

    @pl.when(pl.program_id(0) == 0)
    def _():
        carry[...] = jnp.zeros_like(carry)

    xn = _rms(h1, nffn_ref[...])
    x_hi = xn.astype(MXU_DTYPE)
    xn_ref[:, :ROW_TILES, :] = x_hi.reshape(tm, ROW_TILES, LANES)
    x_lo = (xn - x_hi.astype(F32)).astype(MXU_DTYPE)
    logits = (_dot(x_hi, wr_hi_ref[...]) + _dot(x_lo, wr_hi_ref[...]) + _dot(x_hi, wr_lo_ref[...])
              + br_ref[...])
    lane = lax.broadcasted_iota(jnp.int32, (tm, LANES), 1)
    lane_f = lane.astype(F32)
    vals, ids, hots = [], [], []
    for _ in range(TOP_K):
        m = jnp.max(logits, axis=-1, keepdims=True)
        ix = jnp.min(jnp.where(logits == m, lane_f, float(LANES)), axis=-1, keepdims=True)
        hot = lane_f == ix
        vals.append(m)
        ids.append(ix)
        hots.append(hot)
        logits = jnp.where(hot, -jnp.inf, logits)
    es = [jnp.exp(v - vals[0]) for v in vals]
    den = es[0] + es[1] + es[2] + es[3]
    chosen = jnp.zeros((tm, LANES), F32)
    for hot in hots:
        chosen = chosen + jnp.where(hot, 1.0, 0.0)
    before = (lax.broadcasted_iota(jnp.int32, (tm, tm), 1) < lax.broadcasted_iota(jnp.int32, (tm, tm), 0))
    rank = _dot(jnp.where(before, 1.0, 0.0), chosen) + carry[...]
    info = jnp.zeros((tm, LANES), F32)
    for k in range(TOP_K):
        pos = jnp.sum(jnp.where(hots[k], rank, 0.0), axis=-1, keepdims=True)
        info = jnp.where(lane == k, es[k] / den, info)
        info = jnp.where(lane == TOP_K + k, ids[k].astype(F32), info)
        info = jnp.where(lane == 2 * TOP_K + k, pos, info)
    info_ref[...] = info
    token = pl.program_id(0) * tm + lax.broadcasted_iota(jnp.int32, (tm, 1), 0)
    tag = jnp.zeros((tm, LANES), F32)
    for d in range(TAG_DIGITS):
        tag = jnp.where(lane == d, ((token >> (8 * d)) & 255).astype(F32), tag)
    for k in range(TOP_K):
        tag = jnp.where(lane == TAG_DIGITS + k, ids[k].astype(F32), tag)
    tag_row = jnp.concatenate([tag, jnp.zeros((tm, D_MODEL - LANES), F32)], axis=1).astype(xn_ref.dtype)
    xn_ref[:, ROW_TILES:, :] = tag_row.reshape(tm, ROW_TILES, LANES)
    carry[...] = carry[...] + jnp.sum(chosen, axis=0, keepdims=True)
    cnt_ref[...] = carry[...]


_ROUTE_OUT_SHAPES = lambda n: (jax.ShapeDtypeStruct((n, D_MODEL), F32), jax.ShapeDtypeStruct((n, TAGGED_ROWS, LANES), MXU_DTYPE),
                               jax.ShapeDtypeStruct((n, LANES), F32), jax.ShapeDtypeStruct((1, LANES), F32))
_ROUTE_OUT_SPECS = (pl.BlockSpec((TOKEN_TILE, D_MODEL), lambda i: (i, 0)),
                    pl.BlockSpec((TOKEN_TILE, TAGGED_ROWS, LANES), lambda i: (i, 0, 0)),
                    pl.BlockSpec((TOKEN_TILE, LANES), lambda i: (i, 0)),
                    pl.BlockSpec((1, LANES), lambda i: (0, 0)))


def _route_in_specs():
    return [_full((1, D_MODEL)), _full((D_MODEL, LANES)), _full((D_MODEL, LANES)), _full((1, LANES))]


def _even_out_kernel(ypp_ref, yps_ref, op_ref, os_ref, r_ref, hp_ref, hs_ref, gn_ref, w_ref, nffn_ref, wr_hi_ref,
                     wr_lo_ref, br_ref, h1_ref, xn_ref, info_ref, cnt_ref, carry, *, npt):
    o = _split_rows(op_ref, os_ref, npt)
    r = r_ref[...]
    parts = []
    for hd in range(GLA_HEADS):
        sl = slice(hd * GLA_DV, (hd + 1) * GLA_DV)
        oh = o[:, sl]
        oh = oh * lax.rsqrt(jnp.mean(oh * oh, axis=-1, keepdims=True) + RMS_EPS) * gn_ref[...]
        rh = r[:, sl]
        parts.append(oh * (rh * jax.nn.sigmoid(rh)))
    gla = jnp.concatenate(parts, axis=1)
    mix = _dot(_split_rows(ypp_ref, yps_ref, npt), w_ref[:POOL_WIDTH, :]) + _dot(gla, w_ref[POOL_WIDTH:, :])
    h1 = _split_rows(hp_ref, hs_ref, npt) + mix
    h1_ref[...] = h1
    _route(h1, nffn_ref, wr_hi_ref, wr_lo_ref, br_ref, xn_ref, info_ref, cnt_ref, carry)


def _even_out(y_pool_parts, o_gla_parts, proj, h_parts, n_p, gla_norm, w_out, nffn, wr_hi, wr_lo, br):
    n = proj.shape[0]
    return pl.pallas_call(
        functools.partial(_even_out_kernel, npt=n_p // TOKEN_TILE),
        out_shape=_ROUTE_OUT_SHAPES(n),
        grid=(n // TOKEN_TILE,),
        in_specs=_split_specs(TOKEN_TILE, POOL_WIDTH, n_p, False) + _split_specs(TOKEN_TILE, GLA_WIDTH, n_p, False)
        + [pl.BlockSpec((TOKEN_TILE, GLA_WIDTH), lambda i: (i, (IN_EVEN_MAIN - GLA_WIDTH) // GLA_WIDTH))]
        + _split_specs(TOKEN_TILE, D_MODEL, n_p, h_parts[0] is h_parts[1])
        + [_full((1, GLA_DV)), _full((D_MODEL, D_MODEL))] + _route_in_specs(),
        out_specs=_ROUTE_OUT_SPECS,
        scratch_shapes=[pltpu.VMEM((1, LANES), F32)],
        compiler_params=_params(), name="even_out")(
            *y_pool_parts, *o_gla_parts, proj, *h_parts, gla_norm, w_out, nffn, wr_hi, wr_lo, br)


def _rope_tile(x, cos, sin, lo_half):
    swapped = jnp.where(lo_half, pltpu.roll(x, LANES - HEAD_DIM // 2, axis=1), pltpu.roll(x, HEAD_DIM // 2, axis=1))
    return x * cos + swapped * sin


def _odd_in_kernel(h_ref, g_ref, w_ref, b_ref, cos_ref, sin_ref, q_ref, kv_ref):
    xn = _rms(h_ref[...], g_ref[...])
    qkv = _dot(xn, w_ref[...]) + b_ref[...]
    cos = cos_ref[...]
    sin = sin_ref[...]
    lo_half = (lax.broadcasted_iota(jnp.int32, cos.shape, 1) % HEAD_DIM) < HEAD_DIM // 2
    for j in range(Q_WIDTH // LANES):
        sl = slice(j * LANES, (j + 1) * LANES)
        q_ref[:, sl] = _rope_tile(qkv[:, sl], cos, sin, lo_half).astype(q_ref.dtype)
    kv_ref[:, :KV_WIDTH] = _rope_tile(qkv[:, Q_WIDTH:Q_WIDTH + KV_WIDTH], cos, sin, lo_half)
    kv_ref[:, KV_WIDTH:] = qkv[:, Q_WIDTH + KV_WIDTH:]


def _odd_in(h, g, w_qkv, b_qkv, cos, sin, table_block):
    n = h.shape[0]
    wq = w_qkv.shape[1]
    tab = pl.BlockSpec((TOKEN_TILE, LANES), lambda i: (table_block(i), 0))
    return pl.pallas_call(
        _odd_in_kernel,
        out_shape=(jax.ShapeDtypeStruct((n, Q_WIDTH), MXU_DTYPE), jax.ShapeDtypeStruct((n, 2 * KV_WIDTH), F32)),
        grid=(n // TOKEN_TILE,),
        in_specs=[pl.BlockSpec((TOKEN_TILE, D_MODEL), lambda i: (i, 0)), _full((1, D_MODEL)),
                  _full((D_MODEL, wq)), _full((1, wq)), tab, tab],
        out_specs=(pl.BlockSpec((TOKEN_TILE, Q_WIDTH), lambda i: (i, 0)),
                   pl.BlockSpec((TOKEN_TILE, 2 * KV_WIDTH), lambda i: (i, 0))),
        compiler_params=_params(), name="odd_in")(h, g, w_qkv, b_qkv, cos, sin)


def _attn_kernel(sink_ref, q_ref, kv0_ref, kv1_ref, kv2_ref, ck0_ref, ck1_ref, cv0_ref, cv1_ref, o_ref, *, sq):
    i = pl.program_id(0)
    smp = sq.is_sample(i)
    t = sq.tile_in_seq(i)
    kv0 = kv0_ref[...]
    k_old = jnp.where(smp, ck0_ref[...], kv2_ref[:, :KV_WIDTH])
    k_mid = jnp.where(smp, ck1_ref[...], kv1_ref[:, :KV_WIDTH])
    v_old = jnp.where(smp, cv0_ref[...], kv2_ref[:, KV_WIDTH:])
    v_mid = jnp.where(smp, cv1_ref[...], kv1_ref[:, KV_WIDTH:])
    pad = jnp.zeros((CHUNK, KV_WIDTH), F32)
    keys = jnp.concatenate([k_old, k_mid, kv0[:, :KV_WIDTH], pad], axis=0)
    vals = jnp.concatenate([v_old, v_mid, kv0[:, KV_WIDTH:], pad], axis=0)
    n_keys = 4 * CHUNK
    lane = lax.broadcasted_iota(jnp.int32, (n_keys, KV_WIDTH), 1)
    lo = lane < HEAD_DIM
    keys_sw = pltpu.roll(keys, HEAD_DIM, axis=1)
    vals_sw = pltpu.roll(vals, HEAD_DIM, axis=1)
    kcol = lax.broadcasted_iota(jnp.int32, (1, n_keys), 1)
    first_valid = jnp.where(smp, 0, (2 - jnp.minimum(t, 2)) * CHUNK)
    key_ok = jnp.logical_and(kcol >= first_valid, kcol < 3 * CHUNK)
    n_pairs = N_Q_HEADS // N_KV_HEADS // 2
    pair_of_row = lax.broadcasted_iota(jnp.int32, (n_pairs * CHUNK, 1), 0) // CHUNK
    for g in range(N_KV_HEADS):
        own, other = (keys, keys_sw) if g == 0 else (keys_sw, keys)
        vown, vother = (vals, vals_sw) if g == 0 else (vals_sw, vals)
        kb = jnp.concatenate([jnp.where(lo, own, 0.0), jnp.where(lo, 0.0, other)], axis=0).astype(MXU_DTYPE)
        vb = jnp.concatenate([jnp.where(lo, vown, 0.0), jnp.where(lo, 0.0, vother)], axis=0).astype(MXU_DTYPE)
        pairs = [slice((g * n_pairs + pr) * LANES, (g * n_pairs + pr + 1) * LANES) for pr in range(n_pairs)]
        qg = jnp.concatenate([q_ref[:, sl] for sl in pairs], axis=0)
        s = _dot_nt(qg, kb) * (HEAD_DIM ** -0.5)
        halves = []
        for half in range(2):
            sh = jnp.where(key_ok, s[:, half * n_keys:(half + 1) * n_keys], NEG_INF)
            sink = jnp.zeros((n_pairs * CHUNK, 1), F32)
            for pr in range(n_pairs):
                sink = jnp.where(pair_of_row == pr, sink_ref[2 * (g * n_pairs + pr) + half], sink)
            m = jnp.maximum(jnp.max(sh, axis=-1, keepdims=True), sink)
            p = jnp.exp(sh - m)
            halves.append(p * (1.0 / (jnp.sum(p, axis=-1, keepdims=True) + jnp.exp(sink - m))))
        o = _dot(jnp.concatenate(halves, axis=1), vb)
        for pr, sl in enumerate(pairs):
            o_ref[:, sl] = o[pr * CHUNK:(pr + 1) * CHUNK].astype(o_ref.dtype)


def _attention(sinks, q, kv, ck, cv, sq):
    n = q.shape[0]
    npt = sq.n_prompt_tiles
    prev = lambda d: (lambda i, s: (jnp.where(sq.is_sample(i), i, jnp.maximum(i - d, 0)), 0))
    cache = lambda d: (lambda i, s: (jnp.where(sq.is_sample(i), 2 * (i - npt) + d, 0), 0))
    kvspec = lambda f: pl.BlockSpec((CHUNK, 2 * KV_WIDTH), f)
    cspec = lambda f: pl.BlockSpec((CHUNK, KV_WIDTH), f)
    grid_spec = pltpu.PrefetchScalarGridSpec(
        num_scalar_prefetch=1, grid=(sq.n_tiles,),
        in_specs=[pl.BlockSpec((CHUNK, Q_WIDTH), lambda i, s: (i, 0)),
                  kvspec(prev(0)), kvspec(prev(1)), kvspec(prev(2)),
                  cspec(cache(0)), cspec(cache(1)), cspec(cache(0)), cspec(cache(1))],
        out_specs=pl.BlockSpec((CHUNK, Q_WIDTH), lambda i, s: (i, 0)))
    return pl.pallas_call(
        functools.partial(_attn_kernel, sq=sq),
        out_shape=jax.ShapeDtypeStruct((n, Q_WIDTH), MXU_DTYPE),
        grid_spec=grid_spec, compiler_params=_params(), name="attention")(sinks, q, kv, kv, kv, ck, ck, cv, cv)


def _odd_out_kernel(o_ref, h_ref, w_ref, b_ref, nffn_ref, wr_hi_ref, wr_lo_ref, br_ref,
                    h1_ref, xn_ref, info_ref, cnt_ref, carry):
    h1 = h_ref[...] + _dot(o_ref[...], w_ref[...]) + b_ref[...]
    h1_ref[...] = h1
    _route(h1, nffn_ref, wr_hi_ref, wr_lo_ref, br_ref, xn_ref, info_ref, cnt_ref, carry)


def _odd_out(o, h, w_out, b_out, nffn, wr_hi, wr_lo, br):
    n = h.shape[0]
    row = pl.BlockSpec((TOKEN_TILE, D_MODEL), lambda i: (i, 0))
    return pl.pallas_call(
        _odd_out_kernel,
        out_shape=_ROUTE_OUT_SHAPES(n),
        grid=(n // TOKEN_TILE,),
        in_specs=[row, row, _full((D_MODEL, D_MODEL)), _full((1, D_MODEL))] + _route_in_specs(),
        out_specs=_ROUTE_OUT_SPECS,
        scratch_shapes=[pltpu.VMEM((1, LANES), F32)],
        compiler_params=_params(), name="odd_out")(o, h, w_out, b_out, nffn, wr_hi, wr_lo, br)


def _token_copy(src, s, dst, d, sem, rows=ROW_TILES):
    return pltpu.make_async_copy(src.at[pl.ds(pl.multiple_of(s, rows), rows)],
                                 dst.at[pl.ds(pl.multiple_of(d, rows), rows)], sem)


ZERO_TOKENS = EXPERT_TILE // 2
HIDDEN_BLOCK = 512


def _zero_fill(fill_ref, zeros, xs_out, zsem):
    zeros[...] = jnp.zeros_like(zeros)

    def sweep(wait):
        def go(cp):
            cp.wait() if wait else cp.start()

        def tail(e, c):
            off, length = fill_ref[e], fill_ref[N_EXPERTS + e]
            for bit in range(EXPERT_TILE.bit_length() - 1):
                rows = (1 << bit) * TAGGED_ROWS

                @pl.when((length >> bit) & 1 == 1)
                def _():
                    o = pl.multiple_of(off + (length & ((1 << bit) - 1)) * TAGGED_ROWS, TAGGED_ROWS)
                    go(pltpu.make_async_copy(zeros.at[pl.ds(0, rows)], xs_out.at[pl.ds(o, rows)], zsem))
            return c

        lax.fori_loop(0, N_EXPERTS, tail, 0)

        def unused(t, c):
            o = pl.multiple_of(fill_ref[2 * N_EXPERTS] + t * ZERO_TOKENS * TAGGED_ROWS, TAGGED_ROWS)
            go(pltpu.make_async_copy(zeros, xs_out.at[pl.ds(o, ZERO_TOKENS * TAGGED_ROWS)], zsem))
            return c

        lax.fori_loop(0, fill_ref[2 * N_EXPERTS + 1], unused, 0)

    sweep(wait=False)
    sweep(wait=True)


def _dispatch_kernel(fill_ref, dest_hbm, x_ref, xs_out, dest_even, dest_odd, zeros, sem, dsems, zsem, *, tile):
    i = pl.program_id(0)
    n_slots = tile * TOP_K

    @pl.when(i == 0)
    def _():
        _zero_fill(fill_ref, zeros, xs_out, zsem)

    def run(parity, mine, other):
        def dest_rows(step, buf, s):
            return pltpu.make_async_copy(dest_hbm.at[pl.ds(step * n_slots, n_slots)], buf, dsems.at[s])

        if parity == 0:
            @pl.when(i == 0)
            def _():
                dest_rows(0, mine, parity).start()

        dest_rows(i, mine, parity).wait()

        @pl.when(i + 1 < pl.num_programs(0))
        def _():
            dest_rows(i + 1, other, 1 - parity).start()

        def issue(t, c):
            for k in range(TOP_K):
                _token_copy(x_ref, t * TAGGED_ROWS, xs_out, mine[t * TOP_K + k], sem, TAGGED_ROWS).start(priority=k % 2)
            return c

        lax.fori_loop(0, tile, issue, 0, unroll=2)

    @pl.when(i % 2 == 0)
    def _():
        run(0, dest_even, dest_odd)

    @pl.when(i % 2 == 1)
    def _():
        run(1, dest_odd, dest_even)

    whole = xs_out.at[pl.ds(0, n_slots * TAGGED_ROWS)]
    pltpu.make_async_copy(whole, whole, sem).wait()


def _dispatch(fill, dest, xt, n_rows):
    n = xt.shape[0] // TAGGED_ROWS
    tile = next(t for t in DISPATCH_TILES if n % t == 0)
    grid_spec = pltpu.PrefetchScalarGridSpec(
        num_scalar_prefetch=1, grid=(n // tile,),
        in_specs=[pl.BlockSpec(memory_space=pl.ANY),
                  pl.BlockSpec((tile * TAGGED_ROWS, LANES), lambda i, f: (i, 0))],
        out_specs=pl.BlockSpec(memory_space=pl.ANY),
        scratch_shapes=[pltpu.SMEM((tile * TOP_K,), jnp.int32), pltpu.SMEM((tile * TOP_K,), jnp.int32),
                        pltpu.VMEM((ZERO_TOKENS * TAGGED_ROWS, LANES), xt.dtype),
                        pltpu.SemaphoreType.DMA, pltpu.SemaphoreType.DMA((2,)), pltpu.SemaphoreType.DMA])
    return pl.pallas_call(
        functools.partial(_dispatch_kernel, tile=tile),
        out_shape=jax.ShapeDtypeStruct((n_rows * TAGGED_ROWS, LANES), xt.dtype),
        grid_spec=grid_spec, compiler_params=_params(), name="moe_dispatch")(fill, dest, xt)


def _experts_kernel(te_ref, tr_ref, x_ref, wgu_ref, bgu_ref, wd_ref, bd_ref, yt_ref,
                    wgu_mxu, wd_mxu, ybuf, place_vmem, place_smem, ysems, psem, *, n_tokens):
    i = pl.program_id(0)
    slot = i % 2
    prev = 1 - slot
    tile_rows = EXPERT_TILE * ROW_TILES
    spare = yt_ref.shape[0] - 2 * tile_rows
    col = lax.broadcasted_iota(jnp.int32, (1, EXPERT_TILE), 1)

    def rows_done(s):
        whole = yt_ref.at[pl.ds(0, tile_rows)]
        pltpu.make_async_copy(whole, whole, ysems.at[s]).wait()

    def places_to_smem(first_rows, s):
        place_vmem[...] = jnp.broadcast_to(first_rows, place_vmem.shape)
        return pltpu.make_async_copy(place_vmem, place_smem.at[s], psem)

    def start_row_copy(j, s, queue=0):
        _token_copy(ybuf.at[s], j * ROW_TILES, yt_ref, place_smem[s, 0, j], ysems.at[s]).start(priority=queue)

    @pl.when(i == 0)
    def _():
        ybuf[...] = jnp.zeros(ybuf.shape, F32)
        for s in (1, 0):
            cp = pltpu.make_async_copy(ybuf.at[0], yt_ref.at[pl.ds(spare + s * tile_rows, tile_rows)], ysems.at[s])
            cp.start()
            if s == 1:
                cp.wait()
        cp = places_to_smem(spare + tile_rows + col * ROW_TILES, 1)
        cp.start()
        cp.wait()

    @pl.when(jnp.logical_or(i == 0, te_ref[i] != te_ref[jnp.maximum(i - 1, 0)]))
    def _():
        wgu_mxu[...] = wgu_ref[0, 0].astype(MXU_DTYPE)
        wd_mxu[...] = wd_ref[0, 0].astype(MXU_DTYPE)

    n_valid = tr_ref[i]
    after_last = jnp.logical_and(n_valid == 0, jnp.logical_and(i >= 1, tr_ref[jnp.maximum(i - 1, 0)] > 0))

    @pl.when(n_valid > 0)
    def _():
        rows_done(slot)
        for j in range(EXPERT_TILE):
            start_row_copy(j, prev, j % 2)
        wide = x_ref[...].reshape(EXPERT_TILE, TAGGED_ROWS * LANES)
        tag = wide[:, D_MODEL:D_MODEL + LANES].astype(F32)
        lane = lax.broadcasted_iota(jnp.int32, (EXPERT_TILE, LANES), 1)
        is_id = jnp.logical_and(lane >= TAG_DIGITS, lane < TAG_DIGITS + TOP_K)
        mine = jnp.logical_and(is_id, tag == te_ref[i].astype(F32))
        weight = jnp.where(lane == 0, 1.0, jnp.where(lane == 1, 256.0, 65536.0))
        terms = jnp.where(lane < TAG_DIGITS, tag * weight, jnp.where(mine, ((lane - TAG_DIGITS) * n_tokens).astype(F32), 0.0))
        place = jnp.sum(terms, axis=-1, keepdims=True)
        hi = jnp.floor(place * (1.0 / 65536.0))
        mid = jnp.floor((place - hi * 65536.0) * (1.0 / 256.0))
        digits = jnp.where(lane == 0, place - hi * 65536.0 - mid * 256.0, jnp.where(lane == 1, mid, jnp.where(lane == 2, hi, 0.0)))
        pick = jnp.where(lax.broadcasted_iota(jnp.int32, (ROW_TILES, LANES), 0)
                         == lax.broadcasted_iota(jnp.int32, (ROW_TILES, LANES), 1), 1.0, 0.0)
        planes = _dot_nt(pick, digits)
        place_row = (planes[0:1] + 256.0 * planes[1:2] + 65536.0 * planes[2:3]).astype(jnp.int32) * ROW_TILES
        to_smem = places_to_smem(jnp.where(col < n_valid, place_row, spare + slot * tile_rows + col * ROW_TILES), slot)
        to_smem.start()

        x = wide[:, :D_MODEL]
        out = None
        for c in range(D_MODEL // HIDDEN_BLOCK):
            cols = slice(c * HIDDEN_BLOCK, (c + 1) * HIDDEN_BLOCK)
            ucols = slice(D_MODEL + c * HIDDEN_BLOCK, D_MODEL + (c + 1) * HIDDEN_BLOCK)
            gate = jnp.minimum(_dot(x, wgu_mxu[:, cols]) + bgu_ref[0, 0, :, cols], SWIGLU_LIMIT)
            up = jnp.clip(_dot(x, wgu_mxu[:, ucols]) + bgu_ref[0, 0, :, ucols], -SWIGLU_LIMIT, SWIGLU_LIMIT)
            act = (up + 1.0) * gate * jax.nn.sigmoid(SWIGLU_ALPHA * gate)
            part = _dot(act, wd_mxu[cols, :])
            out = part if out is None else out + part
        _store_token_tiles(ybuf.at[slot], out + bd_ref[0, 0])
        to_smem.wait()

    @pl.when(after_last)
    def _():
        rows_done(slot)
        def issue(j, c):
            start_row_copy(j, prev)
            return c

        lax.fori_loop(0, EXPERT_TILE, issue, 0, unroll=8)
        rows_done(prev)


def _experts(layer, tile_expert, tile_rows, xs, wgu, bgu, wd, bd, n_tokens):
    rows = xs.shape[0]
    w = lambda shape: pl.BlockSpec((1, 1) + shape, lambda i, te, tr: (layer, te[i], 0, 0))
    grid_spec = pltpu.PrefetchScalarGridSpec(
        num_scalar_prefetch=2, grid=(rows // EXPERT_TILE,),
        in_specs=[pl.BlockSpec((EXPERT_TILE, TAGGED_ROWS, LANES), lambda i, te, tr: (i, 0, 0)),
                  w((D_MODEL, 2 * D_MODEL)), w((1, 2 * D_MODEL)), w((D_MODEL, D_MODEL)), w((1, D_MODEL))],
        out_specs=pl.BlockSpec(memory_space=pl.ANY),
        scratch_shapes=[pltpu.VMEM((D_MODEL, 2 * D_MODEL), MXU_DTYPE), pltpu.VMEM((D_MODEL, D_MODEL), MXU_DTYPE),
                        pltpu.VMEM((2, EXPERT_TILE * ROW_TILES, LANES), F32),
                        pltpu.VMEM((ROW_TILES, EXPERT_TILE), jnp.int32), pltpu.SMEM((2, ROW_TILES, EXPERT_TILE), jnp.int32),
                        pltpu.SemaphoreType.DMA((2,)), pltpu.SemaphoreType.DMA])
    yt_rows = (n_tokens * TOP_K + 2 * EXPERT_TILE) * ROW_TILES
    return pl.pallas_call(
        functools.partial(_experts_kernel, n_tokens=n_tokens), out_shape=jax.ShapeDtypeStruct((yt_rows, LANES), F32),
        grid_spec=grid_spec, compiler_params=_params(), name="moe_experts")(
            tile_expert, tile_rows, xs, wgu, bgu, wd, bd)


def _combine_kernel(*refs, final, npt):
    y_refs, outs = refs[:TOP_K], refs[TOP_K + 7:]
    info_ref, h_ref, p_prompt_ref, p_sample_ref, pp_ref, pg_ref, nf_ref = refs[TOP_K:TOP_K + 7]
    i = pl.program_id(0)
    gates = info_ref[...]
    moe = None
    for k in range(TOP_K):
        rows = _load_token_tiles(y_refs[k], COMBINE_TILE)
        moe = gates[:, k:k + 1] * rows if moe is None else moe + gates[:, k:k + 1] * rows
    h2 = h_ref[...] + moe
    p = jnp.where(i < npt, p_prompt_ref[0], p_sample_ref[0])
    h3 = h2 + jax.nn.sigmoid(_dot(h2, pg_ref[...])) * _dot(p, pp_ref[...])
    if not final:
        outs[0][...] = h3
        return
    y = _rms(h3, nf_ref[...])

    @pl.when(i < npt)
    def _():
        outs[0][...] = y

    @pl.when(i >= npt)
    def _():
        outs[1][...] = y


def _combine(layer, yt, info, h1, p_prompt, p_sample, ple_proj, ple_gate, norm_final, final):
    n = h1.shape[0]
    n_p = p_prompt.shape[1]
    npt = n_p // COMBINE_TILE
    steps = n // COMBINE_TILE
    row = lambda w: pl.BlockSpec((COMBINE_TILE, w), lambda i: (i, 0))
    rank_rows = lambda k: pl.BlockSpec((COMBINE_TILE * ROW_TILES, LANES), lambda i: (k * steps + i, 0))
    prompt_rows = lambda i: jnp.minimum(i, npt - 1)
    sample_rows = lambda i: jnp.maximum(i - npt, 0)
    if final:
        out_shape = (jax.ShapeDtypeStruct((n_p, D_MODEL), F32), jax.ShapeDtypeStruct((n - n_p, D_MODEL), F32))
        out_specs = (pl.BlockSpec((COMBINE_TILE, D_MODEL), lambda i: (prompt_rows(i), 0)),
                     pl.BlockSpec((COMBINE_TILE, D_MODEL), lambda i: (sample_rows(i), 0)))
    else:
        out_shape, out_specs = jax.ShapeDtypeStruct((n, D_MODEL), F32), row(D_MODEL)
    return pl.pallas_call(
        functools.partial(_combine_kernel, final=final, npt=npt),
        out_shape=out_shape,
        grid=(steps,),
        in_specs=[rank_rows(k) for k in range(TOP_K)] + [
            row(LANES), row(D_MODEL),
            pl.BlockSpec((1, COMBINE_TILE, PLE_DIM), lambda i: (layer, prompt_rows(i), 0)),
            pl.BlockSpec((1, COMBINE_TILE, PLE_DIM), lambda i: (layer, sample_rows(i), 0)),
            _full((PLE_DIM, D_MODEL)), _full((D_MODEL, D_MODEL)), _full((1, D_MODEL))],
        out_specs=out_specs,
        compiler_params=_params(), name="moe_combine")(
            *([yt] * TOP_K), info, h1, p_prompt, p_sample, ple_proj, ple_gate, norm_final)


def _moe_and_embed(layer, h1, xt, info, counts, p_prompt, p_sample, wgu, bgu, wd, bd, ple_proj, ple_gate,
                   norm_final, final):
    n = h1.shape[0]
    n_tiles = (n * TOP_K + N_EXPERTS * (EXPERT_TILE - 1)) // EXPERT_TILE + 1
    ids = info[:, TOP_K:2 * TOP_K].astype(jnp.int32)
    rank = info[:, 2 * TOP_K:3 * TOP_K].astype(jnp.int32)
    cnt = counts[0, :N_EXPERTS].astype(jnp.int32)
    padded = ((cnt + EXPERT_TILE - 1) // EXPERT_TILE) * EXPERT_TILE
    ends = jnp.cumsum(padded)
    starts = ends - padded
    dest = ((starts[ids] + rank) * TAGGED_ROWS).reshape(-1)
    tile_start = jnp.arange(n_tiles, dtype=jnp.int32) * EXPERT_TILE
    tile_expert = jnp.minimum(jnp.sum((tile_start[:, None] >= ends[None, :]).astype(jnp.int32), axis=1), N_EXPERTS - 1)
    tile_rows = jnp.clip(cnt[tile_expert] - (tile_start - starts[tile_expert]), 0, EXPERT_TILE)
    tile_rows = jnp.where(tile_start < ends[-1], tile_rows, 0).astype(jnp.int32)
    fill = jnp.concatenate([(starts + cnt) * TAGGED_ROWS, padded - cnt,
                            jnp.stack([ends[-1] * TAGGED_ROWS,
                                       (n_tiles * EXPERT_TILE - ends[-1]) // ZERO_TOKENS])]).astype(jnp.int32)
    xs = _dispatch(fill, dest, xt.reshape(n * TAGGED_ROWS, LANES), n_tiles * EXPERT_TILE)
    yt = _experts(layer, tile_expert, tile_rows, xs.reshape(-1, TAGGED_ROWS, LANES), wgu, bgu, wd, bd, n)
    return _combine(layer, yt, info, h1, p_prompt, p_sample, ple_proj, ple_gate, norm_final, final)


def _rope_tables(prompt_len, n_sample_seq, sample_len):
    half = HEAD_DIM // 2
    inv = jnp.power(jnp.float32(ROPE_THETA), -jnp.arange(half, dtype=F32) / half)
    pos = jnp.concatenate([jnp.arange(prompt_len), jnp.tile(PAST_LEN + jnp.arange(sample_len), n_sample_seq)])
    ang = pos.astype(F32)[:, None] * inv[None, :]
    cos = jnp.tile(jnp.cos(ang), (1, LANES // half))
    sin = jnp.tile(jnp.concatenate([-jnp.sin(ang), jnp.sin(ang)], axis=1), (1, LANES // HEAD_DIM))
    return cos, sin


def kernel(x_prompt, x_sample, state_pool, state_gla, cache_k, cache_v, p_prompt, p_sample, norm_mix, norm_ffn, norm_final, w_in_even, pool_w, pool_scale, gla_w_gate, gla_b_gate, gla_norm, w_out_even, w_qkv_odd, b_qkv_odd, attn_sinks, w_out_odd, b_out_odd, w_router, b_router, w_gate_up, b_gate_up, w_down, b_down, ple_proj, ple_gate):
    bsz, t_len, _ = x_prompt.shape
    dec_bsz, dec_len, _ = x_sample.shape
    depth = norm_mix.shape[0]
    n_p, n_s = bsz * t_len, dec_bsz * dec_len
    n = n_p + n_s
    assert dec_len == CHUNK and n_s == TOKEN_TILE and t_len % TOKEN_TILE == 0 and cache_k.shape[2] == WINDOW
    sq = _Seq(bsz, t_len, dec_bsz)
    grp_p = _Group(bsz, t_len, min(SEQ_TILE, t_len), 0, 0)
    grp_s = _Group(dec_bsz, dec_len, dec_len, n_p, PAST_LEN)
    bf = lambda a: a.astype(MXU_DTYPE)
    row = lambda a: a.reshape(1, -1)

    h_parts = (x_prompt.reshape(n_p, D_MODEL), x_sample.reshape(n_s, D_MODEL))
    p_parts = (p_prompt.reshape(depth, n_p, PLE_DIM), p_sample.reshape(depth, n_s, PLE_DIM))
    b_gu = b_gate_up.reshape(depth, N_EXPERTS, 1, 2 * D_MODEL)
    b_dn = b_down.reshape(depth, N_EXPERTS, 1, D_MODEL)
    cos, sin = _rope_tables(t_len, dec_bsz, dec_len)
    tiles_per_seq = t_len // TOKEN_TILE
    table_block = lambda i: jnp.where(i < bsz * tiles_per_seq, i % tiles_per_seq, tiles_per_seq)

    pools, glas, new_k, new_v = [], [], [], []
    for i in range(depth):
        if i % 2 == 0:
            e = i // 2
            w_in = jnp.pad(bf(w_in_even[e]), ((0, 0), (0, (-IN_EVEN) % LANES)))
            wg = jnp.pad(bf(gla_w_gate[e]), ((0, w_in.shape[1] - IN_EVEN_MAIN - GLA_RANK), (0, 0)))
            proj = _even_in(h_parts, n, n_p, row(norm_mix[i]), w_in, wg, row(gla_b_gate[e]))
            pw, ps = bf(pool_w[e]), row(pool_scale[e])
            y_pool_p, pool_p = _pool(proj, jnp.zeros((bsz, 16, POOL_WIDTH), F32), pw, ps, grp_p)
            y_pool_s, pool_s = _pool(proj, jnp.pad(state_pool[e], ((0, 0), (1, 0), (0, 0))), pw, ps, grp_s)
            o_gla_p, gla_p = _gla(proj, jnp.zeros((bsz,) + state_gla.shape[2:], F32), grp_p)
            o_gla_s, gla_s = _gla(proj, state_gla[e], grp_s)
            pools.append((pool_p[:, 1:], pool_s[:, 1:]))
            glas.append((gla_p, gla_s))
            wr = jnp.pad(w_router[i], ((0, 0), (0, LANES - N_EXPERTS)))
            wr_hi = bf(wr)
            wr_lo = bf(wr - wr_hi.astype(F32))
            br = jnp.pad(row(b_router[i]), ((0, 0), (0, LANES - N_EXPERTS)), constant_values=NEG_INF)
            h1, xn, info, counts = _even_out((y_pool_p, y_pool_s), (o_gla_p, o_gla_s), proj, h_parts, n_p,
                                             row(gla_norm[e]), bf(w_out_even[e]), row(norm_ffn[i]), wr_hi, wr_lo, br)
        else:
            o = i // 2
            h = h_parts[0]
            q, kv = _odd_in(h, row(norm_mix[i]), bf(w_qkv_odd[o]), row(b_qkv_odd[o]), cos, sin, table_block)
            ck = cache_k[o].reshape(dec_bsz * WINDOW, KV_WIDTH)
            cv = cache_v[o].reshape(dec_bsz * WINDOW, KV_WIDTH)
            att = _attention(attn_sinks[o], q, kv, ck, cv, sq)
            kv_p = kv[:n_p].reshape(bsz, t_len, 2 * KV_WIDTH)[:, -WINDOW:]
            kv_s = kv[n_p:].reshape(dec_bsz, dec_len, 2 * KV_WIDTH)
            hd = (N_KV_HEADS, HEAD_DIM)
            new_k.append((kv_p[..., :KV_WIDTH].reshape(bsz, WINDOW, *hd),
                          jnp.concatenate([cache_k[o], kv_s[..., :KV_WIDTH].reshape(dec_bsz, dec_len, *hd)], axis=1)[:, -WINDOW:]))
            new_v.append((kv_p[..., KV_WIDTH:].reshape(bsz, WINDOW, *hd),
                          jnp.concatenate([cache_v[o], kv_s[..., KV_WIDTH:].reshape(dec_bsz, dec_len, *hd)], axis=1)[:, -WINDOW:]))
            wr = jnp.pad(w_router[i], ((0, 0), (0, LANES - N_EXPERTS)))
            wr_hi = bf(wr)
            wr_lo = bf(wr - wr_hi.astype(F32))
            br = jnp.pad(row(b_router[i]), ((0, 0), (0, LANES - N_EXPERTS)), constant_values=NEG_INF)
            h1, xn, info, counts = _odd_out(att, h, bf(w_out_odd[o]), row(b_out_odd[o]),
                                            row(norm_ffn[i]), wr_hi, wr_lo, br)
        final = i == depth - 1
        out = _moe_and_embed(i, h1, xn, info, counts, *p_parts, w_gate_up, b_gu, w_down, b_dn,
                             bf(ple_proj[i]), bf(ple_gate[i]), row(norm_final), final)
        h_parts = out if final else (out, out)

    y_prompt = h_parts[0].reshape(bsz, t_len, D_MODEL)
    y_sample = h_parts[1].reshape(dec_bsz, dec_len, D_MODEL)
    part = lambda pairs, j: jnp.stack([p[j] for p in pairs])
    return (y_prompt, y_sample, part(pools, 0), part(glas, 0), part(new_k, 0), part(new_v, 0),
            part(pools, 1), part(glas, 1), part(new_k, 1), part(new_v, 1))
```

```python
import functools

import jax
import jax.numpy as jnp
from jax import lax
from jax.experimental import pallas as pl
from jax.experimental.pallas import tpu as pltpu

F32 = jnp.float32
MXU_DTYPE = jnp.bfloat16

V7X_VMEM_BYTES = 64 * 1024 * 1024
VMEM_LIMIT = (V7X_VMEM_BYTES * 7) // 8
LANES = 128

D_MODEL = 1024
CHUNK = 64
PAST_LEN = 2048
PLE_DIM = 256
RMS_EPS = 1e-6
POOL_WINDOWS = (2, 4, 8, 16)
POOL_WIDTH = 512
POOL_GROUP_DIM = 128
POOL_STATE = 15
GLA_HEADS = 4
GLA_DK = 64
GLA_DV = 128
GLA_KEY_WIDTH = GLA_HEADS * GLA_DK
GLA_WIDTH = GLA_HEADS * GLA_DV
GLA_RANK = 16
GLA_TAU = 16.0
GLA_BLOCK = 16
IN_EVEN = POOL_WIDTH + 2 * GLA_KEY_WIDTH + 2 * GLA_WIDTH + GLA_RANK
IN_EVEN_MAIN = IN_EVEN - GLA_RANK
PROJ_WIDTH = IN_EVEN_MAIN + GLA_KEY_WIDTH
N_Q_HEADS = 16
N_KV_HEADS = 2
HEAD_DIM = 64
WINDOW = 128
ROPE_THETA = 10000.0
Q_WIDTH = N_Q_HEADS * HEAD_DIM
KV_WIDTH = N_KV_HEADS * HEAD_DIM
N_EXPERTS = 32
TOP_K = 4
SWIGLU_LIMIT = 7.0
SWIGLU_ALPHA = 1.702
NEG_INF = -1e30

TOKEN_TILE = 512
EXPERT_TILE = 512
COMBINE_TILE = 512
DISPATCH_TILES = (1280, TOKEN_TILE)


def _dot(a, b):
    return jnp.dot(a.astype(MXU_DTYPE), b.astype(MXU_DTYPE), preferred_element_type=F32)


def _dot_nt(a, b):
    return lax.dot_general(a.astype(MXU_DTYPE), b.astype(MXU_DTYPE), (((1,), (1,)), ((), ())),
                           preferred_element_type=F32)


def _split3(x):
    x1 = x.astype(MXU_DTYPE)
    r1 = x - x1.astype(F32)
    x2 = r1.astype(MXU_DTYPE)
    x3 = (r1 - x2.astype(F32)).astype(MXU_DTYPE)
    return x1, x2, x3


def _rms(x, g):
    return x * lax.rsqrt(jnp.mean(x * x, axis=-1, keepdims=True) + RMS_EPS) * g


ROW_TILES = D_MODEL // LANES
TAGGED_ROWS = 2 * ROW_TILES
TAG_DIGITS = 3


def _store_token_tiles(ref, x, pitch=ROW_TILES):
    for s in range(ROW_TILES):
        ref[pl.ds(s, x.shape[0], stride=pitch), :] = x[:, s * LANES:(s + 1) * LANES]


def _load_token_tiles(ref, rows, pitch=ROW_TILES):
    return jnp.concatenate([ref[pl.ds(s, rows, stride=pitch), :] for s in range(ROW_TILES)], axis=1)


def _params(n_axes=1):
    return pltpu.CompilerParams(dimension_semantics=("arbitrary",) * n_axes, vmem_limit_bytes=VMEM_LIMIT)


def _full(shape):
    return pl.BlockSpec(shape, lambda *_: (0,) * len(shape))


class _Seq:
    def __init__(self, n_prompt_seq, prompt_len, n_sample_seq):
        self.tiles_per_seq = prompt_len // CHUNK
        self.n_prompt_seq = n_prompt_seq
        self.n_prompt_tiles = n_prompt_seq * self.tiles_per_seq
        self.n_tiles = self.n_prompt_tiles + n_sample_seq
        self.n_seq = n_prompt_seq + n_sample_seq

    def is_sample(self, i):
        return i >= self.n_prompt_tiles

    def tile_in_seq(self, i):
        return jnp.where(self.is_sample(i), 0, i % self.tiles_per_seq)

    def seq(self, i):
        return jnp.where(self.is_sample(i), self.n_prompt_seq + i - self.n_prompt_tiles, i // self.tiles_per_seq)

    def last(self, i):
        return jnp.logical_or(self.is_sample(i), i % self.tiles_per_seq == self.tiles_per_seq - 1)

    def pos0(self, i):
        return jnp.where(self.is_sample(i), PAST_LEN, self.tile_in_seq(i) * CHUNK)


class _Group:
    def __init__(self, n_seq, seq_len, tile, row0, pos_base):
        assert seq_len % tile == 0 and row0 % tile == 0
        self.n_seq, self.tile, self.pos_base = n_seq, tile, pos_base
        self.tiles_per_seq = seq_len // tile
        self.n_tiles = n_seq * self.tiles_per_seq
        self.block0 = row0 // tile

    def block(self, i):
        return self.block0 + i

    def seq(self, i):
        return i // self.tiles_per_seq

    def tile_in_seq(self, i):
        return i % self.tiles_per_seq

    def last(self, i):
        return i % self.tiles_per_seq == self.tiles_per_seq - 1

    def pos0(self, i):
        return self.pos_base + self.tile_in_seq(i) * self.tile


SEQ_TILE = 256


def _split_specs(tile, width, n_prompt, same_array):
    npt = n_prompt // tile
    off = npt if same_array else 0
    return [pl.BlockSpec((tile, width), lambda i: (jnp.minimum(i, npt - 1), 0)),
            pl.BlockSpec((tile, width), lambda i: (jnp.maximum(i - npt, 0) + off, 0))]


def _split_rows(prompt_ref, sample_ref, n_prompt_tiles):
    return jnp.where(pl.program_id(0) < n_prompt_tiles, prompt_ref[...], sample_ref[...])


def _even_in_kernel(hp_ref, hs_ref, g_ref, w_ref, wg_ref, bg_ref, out_ref, *, npt):
    xn = _rms(_split_rows(hp_ref, hs_ref, npt), g_ref[...])
    proj = _dot(xn, w_ref[...])
    z = proj[:, IN_EVEN_MAIN:]
    a = _dot(z, wg_ref[...]) + bg_ref[...]
    log_alpha = (jnp.minimum(a, 0.0) - jnp.log1p(jnp.exp(-jnp.abs(a)))) * (1.0 / GLA_TAU)
    out_ref[:, :IN_EVEN_MAIN] = proj[:, :IN_EVEN_MAIN]
    out_ref[:, IN_EVEN_MAIN:] = log_alpha


def _even_in(h_parts, n, n_p, g, w_in, wg, bg):
    wp = w_in.shape[1]
    return pl.pallas_call(
        functools.partial(_even_in_kernel, npt=n_p // TOKEN_TILE),
        out_shape=jax.ShapeDtypeStruct((n, PROJ_WIDTH), F32),
        grid=(n // TOKEN_TILE,),
        in_specs=_split_specs(TOKEN_TILE, D_MODEL, n_p, h_parts[0] is h_parts[1]) + [
            _full((1, D_MODEL)), _full((D_MODEL, wp)), _full((wp - IN_EVEN_MAIN, GLA_KEY_WIDTH)),
            _full((1, GLA_KEY_WIDTH))],
        out_specs=pl.BlockSpec((TOKEN_TILE, PROJ_WIDTH), lambda i: (i, 0)),
        compiler_params=_params(), name="even_in")(*h_parts, g, w_in, wg, bg)


def _pool_kernel(u_ref, init_ref, pw_ref, ps_ref, y_ref, st_ref, buf, *, grp):
    i = pl.program_id(0)
    rows = grp.tile

    @pl.when(grp.tile_in_seq(i) == 0)
    def _():
        buf[0:16, :] = init_ref[0]

    u = u_ref[...]
    buf[16:16 + rows, :] = u
    pos = grp.pos0(i) + lax.broadcasted_iota(jnp.int32, (rows, 1), 0)
    for g, w in enumerate(POOL_WINDOWS):
        sl = slice(g * POOL_GROUP_DIM, (g + 1) * POOL_GROUP_DIM)
        acc = u[:, sl]
        for j in range(1, w):
            acc = acc + buf[16 - j:16 - j + rows, sl]
        cnt = jnp.minimum(w, pos + 1).astype(F32)
        d = acc / cnt - u[:, sl]
        y_ref[:, sl] = _dot(d, pw_ref[g]) * ps_ref[:, sl]
    tail = buf[rows:rows + 16, :]
    st_ref[0] = tail
    buf[0:16, :] = tail


def _pool(proj, init, pool_w, pool_scale, grp):
    return pl.pallas_call(
        functools.partial(_pool_kernel, grp=grp),
        out_shape=(jax.ShapeDtypeStruct((grp.n_tiles * grp.tile, POOL_WIDTH), F32),
                   jax.ShapeDtypeStruct((grp.n_seq, 16, POOL_WIDTH), F32)),
        grid=(grp.n_tiles,),
        in_specs=[pl.BlockSpec((grp.tile, POOL_WIDTH), lambda i: (grp.block(i), 0)),
                  pl.BlockSpec((1, 16, POOL_WIDTH), lambda i: (grp.seq(i), 0, 0)),
                  _full((len(POOL_WINDOWS), POOL_GROUP_DIM, POOL_GROUP_DIM)), _full((1, POOL_WIDTH))],
        out_specs=(pl.BlockSpec((grp.tile, POOL_WIDTH), lambda i: (i, 0)),
                   pl.BlockSpec((1, 16, POOL_WIDTH), lambda i: (grp.seq(i), 0, 0))),
        scratch_shapes=[pltpu.VMEM((grp.tile + 16, POOL_WIDTH), F32)],
        compiler_params=_params(), name="pool")(proj, init, pool_w, pool_scale)


def _gla_kernel(q_ref, k_ref, v_ref, g_ref, s0_ref, o_ref, sout_ref, state, before, *, grp):
    i = pl.program_id(0)
    tile = grp.tile
    n_blk = tile // GLA_BLOCK

    @pl.when(grp.tile_in_seq(i) == 0)
    def _():
        state[...] = s0_ref[0]

    g = g_ref[...]
    row = lax.broadcasted_iota(jnp.int32, (tile, tile), 0)
    col = lax.broadcasted_iota(jnp.int32, (tile, tile), 1)
    same = (row >> 4) == (col >> 4)
    causal = jnp.logical_and(same, col <= row)
    tri = jnp.where(causal, 1.0, 0.0).astype(MXU_DTYPE)
    ones = jnp.where(same, 1.0, 0.0).astype(MXU_DTYPE)
    g1, g2, g3 = _split3(g)
    b = _dot(tri, g1) + _dot(tri, g2) + _dot(tri, g3)
    b_last = _dot(ones, g1) + _dot(ones, g2) + _dot(ones, g3)
    q_t = q_ref[...] * (GLA_DK ** -0.5) * jnp.exp(b)
    k = k_ref[...]
    k_t = k * jnp.exp(-b)
    k_dec_t = (k * jnp.exp(b_last - b)).T
    sel = jnp.where((lax.broadcasted_iota(jnp.int32, (tile, LANES), 0) >> 4)
                    == lax.broadcasted_iota(jnp.int32, (tile, LANES), 1), 1.0, 0.0).astype(MXU_DTYPE)
    t1, t2, t3 = _split3(g.T)
    blk_decay = jnp.exp(_dot(t1, sel) + _dot(t2, sel) + _dot(t3, sel))
    v = v_ref[...]
    stacked = n_blk * GLA_DK
    dk_bits, blk_bits = GLA_DK.bit_length() - 1, GLA_BLOCK.bit_length() - 1
    upd_live = (lax.broadcasted_iota(jnp.int32, (stacked, tile), 0) >> dk_bits
                == lax.broadcasted_iota(jnp.int32, (stacked, tile), 1) >> blk_bits)
    qry_live = (lax.broadcasted_iota(jnp.int32, (tile, stacked), 0) >> blk_bits
                == lax.broadcasted_iota(jnp.int32, (tile, stacked), 1) >> dk_bits)
    low_half = lax.broadcasted_iota(jnp.int32, (tile, LANES), 1) < GLA_DK
    for h in range(GLA_HEADS):
        ks = slice(h * GLA_DK, (h + 1) * GLA_DK)
        vs = slice(h * GLA_DV, (h + 1) * GLA_DV)
        vh = v[:, vs]
        scores = jnp.where(causal, _dot_nt(q_t[:, ks], k_t[:, ks]), 0.0)
        o = _dot(scores, vh)
        upd = _dot(jnp.where(upd_live, jnp.concatenate([k_dec_t[ks, :]] * n_blk, axis=0), 0.0), vh)
        s = state[h]
        for j in range(n_blk):
            rows = slice(j * GLA_DK, (j + 1) * GLA_DK)
            before[rows, :] = s
            s = blk_decay[ks, j:j + 1] * s + upd[rows, :]
        state[h] = s
        pair = q_t[:, (h // 2) * LANES:(h // 2 + 1) * LANES]
        swapped = pltpu.roll(pair, GLA_DK, axis=1)
        both = jnp.where(low_half, pair, swapped) if h % 2 == 0 else jnp.where(low_half, swapped, pair)
        q_exp = jnp.where(qry_live, jnp.concatenate([both] * (stacked // LANES), axis=1), 0.0)
        o_ref[:, vs] = o + _dot(q_exp, before[...])

    @pl.when(grp.last(i))
    def _():
        sout_ref[0] = state[...]


def _gla(proj, s0, grp):
    st_shape = (grp.n_seq, GLA_HEADS, GLA_DK, GLA_DV)
    st_spec = pl.BlockSpec((1, GLA_HEADS, GLA_DK, GLA_DV), lambda i: (grp.seq(i), 0, 0, 0))
    kw = GLA_KEY_WIDTH
    cols = lambda width, c: pl.BlockSpec((grp.tile, width), lambda i: (grp.block(i), c))
    return pl.pallas_call(
        functools.partial(_gla_kernel, grp=grp),
        out_shape=(jax.ShapeDtypeStruct((grp.n_tiles * grp.tile, GLA_WIDTH), F32), jax.ShapeDtypeStruct(st_shape, F32)),
        grid=(grp.n_tiles,),
        in_specs=[cols(kw, POOL_WIDTH // kw), cols(kw, POOL_WIDTH // kw + 1),
                  cols(GLA_WIDTH, (POOL_WIDTH + 2 * kw) // GLA_WIDTH), cols(kw, IN_EVEN_MAIN // kw), st_spec],
        out_specs=(pl.BlockSpec((grp.tile, GLA_WIDTH), lambda i: (i, 0)), st_spec),
        scratch_shapes=[pltpu.VMEM((GLA_HEADS, GLA_DK, GLA_DV), F32),
                        pltpu.VMEM((grp.tile // GLA_BLOCK * GLA_DK, GLA_DV), F32)],
        compiler_params=_params(), name="gla")(proj, proj, proj, proj, s0)


def _route(h1, nffn_ref, wr_hi_ref, wr_lo_ref, br_ref, xn_ref, info_ref, cnt_ref, carry):
    tm = h1.shape[0]

    @pl.when(pl.program_id(0) == 0)
    def _():
        carry[...] = jnp.zeros_like(carry)

    xn = _rms(h1, nffn_ref[...])
    x_hi = xn.astype(MXU_DTYPE)
    xn_ref[:, :ROW_TILES, :] = x_hi.reshape(tm, ROW_TILES, LANES)
    x_lo = (xn - x_hi.astype(F32)).astype(MXU_DTYPE)
    logits = (_dot(x_hi, wr_hi_ref[...]) + _dot(x_lo, wr_hi_ref[...]) + _dot(x_hi, wr_lo_ref[...])
              + br_ref[...])
    lane = lax.broadcasted_iota(jnp.int32, (tm, LANES), 1)
    lane_f = lane.astype(F32)
    vals, ids, hots = [], [], []
    for _ in range(TOP_K):
        m = jnp.max(logits, axis=-1, keepdims=True)
        ix = jnp.min(jnp.where(logits == m, lane_f, float(LANES)), axis=-1, keepdims=True)
        hot = lane_f == ix
        vals.append(m)
        ids.append(ix)
        hots.append(hot)
        logits = jnp.where(hot, -jnp.inf, logits)
    es = [jnp.exp(v - vals[0]) for v in vals]
    den = es[0] + es[1] + es[2] + es[3]
    chosen = jnp.zeros((tm, LANES), F32)
    for hot in hots:
        chosen = chosen + jnp.where(hot, 1.0, 0.0)
    before = (lax.broadcasted_iota(jnp.int32, (tm, tm), 1) < lax.broadcasted_iota(jnp.int32, (tm, tm), 0))
    rank = _dot(jnp.where(before, 1.0, 0.0), chosen) + carry[...]
    info = jnp.zeros((tm, LANES), F32)
    for k in range(TOP_K):
        pos = jnp.sum(jnp.where(hots[k], rank, 0.0), axis=-1, keepdims=True)
        info = jnp.where(lane == k, es[k] / den, info)
        info = jnp.where(lane == TOP_K + k, ids[k].astype(F32), info)
        info = jnp.where(lane == 2 * TOP_K + k, pos, info)
    info_ref[...] = info
    token = pl.program_id(0) * tm + lax.broadcasted_iota(jnp.int32, (tm, 1), 0)
    tag = jnp.zeros((tm, LANES), F32)
    for d in range(TAG_DIGITS):
        tag = jnp.where(lane == d, ((token >> (8 * d)) & 255).astype(F32), tag)
    for k in range(TOP_K):
        tag = jnp.where(lane == TAG_DIGITS + k, ids[k].astype(F32), tag)
    tag_row = jnp.concatenate([tag, jnp.zeros((tm, D_MODEL - LANES), F32)], axis=1).astype(xn_ref.dtype)
    xn_ref[:, ROW_TILES:, :] = tag_row.reshape(tm, ROW_TILES, LANES)
    carry[...] = carry[...] + jnp.sum(chosen, axis=0, keepdims=True)
    cnt_ref[...] = carry[...]


_ROUTE_OUT_SHAPES = lambda n: (jax.ShapeDtypeStruct((n, D_MODEL), F32), jax.ShapeDtypeStruct((n, TAGGED_ROWS, LANES), MXU_DTYPE),
                               jax.ShapeDtypeStruct((n, LANES), F32), jax.ShapeDtypeStruct((1, LANES), F32))
_ROUTE_OUT_SPECS = (pl.BlockSpec((TOKEN_TILE, D_MODEL), lambda i: (i, 0)),
                    pl.BlockSpec((TOKEN_TILE, TAGGED_ROWS, LANES), lambda i: (i, 0, 0)),
                    pl.BlockSpec((TOKEN_TILE, LANES), lambda i: (i, 0)),
                    pl.BlockSpec((1, LANES), lambda i: (0, 0)))


def _route_in_specs():
    return [_full((1, D_MODEL)), _full((D_MODEL, LANES)), _full((D_MODEL, LANES)), _full((1, LANES))]


def _even_out_kernel(ypp_ref, yps_ref, op_ref, os_ref, r_ref, hp_ref, hs_ref, gn_ref, w_ref, nffn_ref, wr_hi_ref,
                     wr_lo_ref, br_ref, h1_ref, xn_ref, info_ref, cnt_ref, carry, *, npt):
    o = _split_rows(op_ref, os_ref, npt)
    r = r_ref[...]
    parts = []
    for hd in range(GLA_HEADS):
        sl = slice(hd * GLA_DV, (hd + 1) * GLA_DV)
        oh = o[:, sl]
        oh = oh * lax.rsqrt(jnp.mean(oh * oh, axis=-1, keepdims=True) + RMS_EPS) * gn_ref[...]
        rh = r[:, sl]
        parts.append(oh * (rh * jax.nn.sigmoid(rh)))
    gla = jnp.concatenate(parts, axis=1)
    mix = _dot(_split_rows(ypp_ref, yps_ref, npt), w_ref[:POOL_WIDTH, :]) + _dot(gla, w_ref[POOL_WIDTH:, :])
    h1 = _split_rows(hp_ref, hs_ref, npt) + mix
    h1_ref[...] = h1
    _route(h1, nffn_ref, wr_hi_ref, wr_lo_ref, br_ref, xn_ref, info_ref, cnt_ref, carry)


def _even_out(y_pool_parts, o_gla_parts, proj, h_parts, n_p, gla_norm, w_out, nffn, wr_hi, wr_lo, br):
    n = proj.shape[0]
    return pl.pallas_call(
        functools.partial(_even_out_kernel, npt=n_p // TOKEN_TILE),
        out_shape=_ROUTE_OUT_SHAPES(n),
        grid=(n // TOKEN_TILE,),
        in_specs=_split_specs(TOKEN_TILE, POOL_WIDTH, n_p, False) + _split_specs(TOKEN_TILE, GLA_WIDTH, n_p, False)
        + [pl.BlockSpec((TOKEN_TILE, GLA_WIDTH), lambda i: (i, (IN_EVEN_MAIN - GLA_WIDTH) // GLA_WIDTH))]
        + _split_specs(TOKEN_TILE, D_MODEL, n_p, h_parts[0] is h_parts[1])
        + [_full((1, GLA_DV)), _full((D_MODEL, D_MODEL))] + _route_in_specs(),
        out_specs=_ROUTE_OUT_SPECS,
        scratch_shapes=[pltpu.VMEM((1, LANES), F32)],
        compiler_params=_params(), name="even_out")(
            *y_pool_parts, *o_gla_parts, proj, *h_parts, gla_norm, w_out, nffn, wr_hi, wr_lo, br)


def _rope_tile(x, cos, sin, lo_half):
    swapped = jnp.where(lo_half, pltpu.roll(x, LANES - HEAD_DIM // 2, axis=1), pltpu.roll(x, HEAD_DIM // 2, axis=1))
    return x * cos + swapped * sin


def _odd_in_kernel(h_ref, g_ref, w_ref, b_ref, cos_ref, sin_ref, q_ref, kv_ref):
    xn = _rms(h_ref[...], g_ref[...])
    qkv = _dot(xn, w_ref[...]) + b_ref[...]
    cos = cos_ref[...]
    sin = sin_ref[...]
    lo_half = (lax.broadcasted_iota(jnp.int32, cos.shape, 1) % HEAD_DIM) < HEAD_DIM // 2
    for j in range(Q_WIDTH // LANES):
        sl = slice(j * LANES, (j + 1) * LANES)
        q_ref[:, sl] = _rope_tile(qkv[:, sl], cos, sin, lo_half).astype(q_ref.dtype)
    kv_ref[:, :KV_WIDTH] = _rope_tile(qkv[:, Q_WIDTH:Q_WIDTH + KV_WIDTH], cos, sin, lo_half)
    kv_ref[:, KV_WIDTH:] = qkv[:, Q_WIDTH + KV_WIDTH:]


def _odd_in(h, g, w_qkv, b_qkv, cos, sin, table_block):
    n = h.shape[0]
    wq = w_qkv.shape[1]
    tab = pl.BlockSpec((TOKEN_TILE, LANES), lambda i: (table_block(i), 0))
    return pl.pallas_call(
        _odd_in_kernel,
        out_shape=(jax.ShapeDtypeStruct((n, Q_WIDTH), MXU_DTYPE), jax.ShapeDtypeStruct((n, 2 * KV_WIDTH), F32)),
        grid=(n // TOKEN_TILE,),
        in_specs=[pl.BlockSpec((TOKEN_TILE, D_MODEL), lambda i: (i, 0)), _full((1, D_MODEL)),
                  _full((D_MODEL, wq)), _full((1, wq)), tab, tab],
        out_specs=(pl.BlockSpec((TOKEN_TILE, Q_WIDTH), lambda i: (i, 0)),
                   pl.BlockSpec((TOKEN_TILE, 2 * KV_WIDTH), lambda i: (i, 0))),
        compiler_params=_params(), name="odd_in")(h, g, w_qkv, b_qkv, cos, sin)


def _attn_kernel(sink_ref, q_ref, kv0_ref, kv1_ref, kv2_ref, ck0_ref, ck1_ref, cv0_ref, cv1_ref, o_ref, *, sq):
    i = pl.program_id(0)
    smp = sq.is_sample(i)
    t = sq.tile_in_seq(i)
    kv0 = kv0_ref[...]
    k_old = jnp.where(smp, ck0_ref[...], kv2_ref[:, :KV_WIDTH])
    k_mid = jnp.where(smp, ck1_ref[...], kv1_ref[:, :KV_WIDTH])
    v_old = jnp.where(smp, cv0_ref[...], kv2_ref[:, KV_WIDTH:])
    v_mid = jnp.where(smp, cv1_ref[...], kv1_ref[:, KV_WIDTH:])
    pad = jnp.zeros((CHUNK, KV_WIDTH), F32)
    keys = jnp.concatenate([k_old, k_mid, kv0[:, :KV_WIDTH], pad], axis=0)
    vals = jnp.concatenate([v_old, v_mid, kv0[:, KV_WIDTH:], pad], axis=0)
    n_keys = 4 * CHUNK
    lane = lax.broadcasted_iota(jnp.int32, (n_keys, KV_WIDTH), 1)
    lo = lane < HEAD_DIM
    keys_sw = pltpu.roll(keys, HEAD_DIM, axis=1)
    vals_sw = pltpu.roll(vals, HEAD_DIM, axis=1)
    kcol = lax.broadcasted_iota(jnp.int32, (1, n_keys), 1)
    first_valid = jnp.where(smp, 0, (2 - jnp.minimum(t, 2)) * CHUNK)
    key_ok = jnp.logical_and(kcol >= first_valid, kcol < 3 * CHUNK)
    n_pairs = N_Q_HEADS // N_KV_HEADS // 2
    pair_of_row = lax.broadcasted_iota(jnp.int32, (n_pairs * CHUNK, 1), 0) // CHUNK
    for g in range(N_KV_HEADS):
        own, other = (keys, keys_sw) if g == 0 else (keys_sw, keys)
        vown, vother = (vals, vals_sw) if g == 0 else (vals_sw, vals)
        kb = jnp.concatenate([jnp.where(lo, own, 0.0), jnp.where(lo, 0.0, other)], axis=0).astype(MXU_DTYPE)
        vb = jnp.concatenate([jnp.where(lo, vown, 0.0), jnp.where(lo, 0.0, vother)], axis=0).astype(MXU_DTYPE)
        pairs = [slice((g * n_pairs + pr) * LANES, (g * n_pairs + pr + 1) * LANES) for pr in range(n_pairs)]
        qg = jnp.concatenate([q_ref[:, sl] for sl in pairs], axis=0)
        s = _dot_nt(qg, kb) * (HEAD_DIM ** -0.5)
        halves = []
        for half in range(2):
            sh = jnp.where(key_ok, s[:, half * n_keys:(half + 1) * n_keys], NEG_INF)
            sink = jnp.zeros((n_pairs * CHUNK, 1), F32)
            for pr in range(n_pairs):
                sink = jnp.where(pair_of_row == pr, sink_ref[2 * (g * n_pairs + pr) + half], sink)
            m = jnp.maximum(jnp.max(sh, axis=-1, keepdims=True), sink)
            p = jnp.exp(sh - m)
            halves.append(p * (1.0 / (jnp.sum(p, axis=-1, keepdims=True) + jnp.exp(sink - m))))
        o = _dot(jnp.concatenate(halves, axis=1), vb)
        for pr, sl in enumerate(pairs):
            o_ref[:, sl] = o[pr * CHUNK:(pr + 1) * CHUNK].astype(o_ref.dtype)


def _attention(sinks, q, kv, ck, cv, sq):
    n = q.shape[0]
    npt = sq.n_prompt_tiles
    prev = lambda d: (lambda i, s: (jnp.where(sq.is_sample(i), i, jnp.maximum(i - d, 0)), 0))
    cache = lambda d: (lambda i, s: (jnp.where(sq.is_sample(i), 2 * (i - npt) + d, 0), 0))
    kvspec = lambda f: pl.BlockSpec((CHUNK, 2 * KV_WIDTH), f)
    cspec = lambda f: pl.BlockSpec((CHUNK, KV_WIDTH), f)
    grid_spec = pltpu.PrefetchScalarGridSpec(
        num_scalar_prefetch=1, grid=(sq.n_tiles,),
        in_specs=[pl.BlockSpec((CHUNK, Q_WIDTH), lambda i, s: (i, 0)),
                  kvspec(prev(0)), kvspec(prev(1)), kvspec(prev(2)),
                  cspec(cache(0)), cspec(cache(1)), cspec(cache(0)), cspec(cache(1))],
        out_specs=pl.BlockSpec((CHUNK, Q_WIDTH), lambda i, s: (i, 0)))
    return pl.pallas_call(
        functools.partial(_attn_kernel, sq=sq),
        out_shape=jax.ShapeDtypeStruct((n, Q_WIDTH), MXU_DTYPE),
        grid_spec=grid_spec, compiler_params=_params(), name="attention")(sinks, q, kv, kv, kv, ck, ck, cv, cv)


def _odd_out_kernel(o_ref, h_ref, w_ref, b_ref, nffn_ref, wr_hi_ref, wr_lo_ref, br_ref,
                    h1_ref, xn_ref, info_ref, cnt_ref, carry):
    h1 = h_ref[...] + _dot(o_ref[...], w_ref[...]) + b_ref[...]
    h1_ref[...] = h1
    _route(h1, nffn_ref, wr_hi_ref, wr_lo_ref, br_ref, xn_ref, info_ref, cnt_ref, carry)


def _odd_out(o, h, w_out, b_out, nffn, wr_hi, wr_lo, br):
    n = h.shape[0]
    row = pl.BlockSpec((TOKEN_TILE, D_MODEL), lambda i: (i, 0))
    return pl.pallas_call(
        _odd_out_kernel,
        out_shape=_ROUTE_OUT_SHAPES(n),
        grid=(n // TOKEN_TILE,),
        in_specs=[row, row, _full((D_MODEL, D_MODEL)), _full((1, D_MODEL))] + _route_in_specs(),
        out_specs=_ROUTE_OUT_SPECS,
        scratch_shapes=[pltpu.VMEM((1, LANES), F32)],
        compiler_params=_params(), name="odd_out")(o, h, w_out, b_out, nffn, wr_hi, wr_lo, br)


def _token_copy(src, s, dst, d, sem, rows=ROW_TILES):
    return pltpu.make_async_copy(src.at[pl.ds(pl.multiple_of(s, rows), rows)],
                                 dst.at[pl.ds(pl.multiple_of(d, rows), rows)], sem)


ZERO_TOKENS = EXPERT_TILE // 2
HIDDEN_BLOCK = 512


def _zero_fill(fill_ref, zeros, xs_out, zsem):
    zeros[...] = jnp.zeros_like(zeros)

    def sweep(wait):
        def go(cp):
            cp.wait() if wait else cp.start()

        def tail(e, c):
            off, length = fill_ref[e], fill_ref[N_EXPERTS + e]
            for bit in range(EXPERT_TILE.bit_length() - 1):
                rows = (1 << bit) * TAGGED_ROWS

                @pl.when((length >> bit) & 1 == 1)
                def _():
                    o = pl.multiple_of(off + (length & ((1 << bit) - 1)) * TAGGED_ROWS, TAGGED_ROWS)
                    go(pltpu.make_async_copy(zeros.at[pl.ds(0, rows)], xs_out.at[pl.ds(o, rows)], zsem))
            return c

        lax.fori_loop(0, N_EXPERTS, tail, 0)

        def unused(t, c):
            o = pl.multiple_of(fill_ref[2 * N_EXPERTS] + t * ZERO_TOKENS * TAGGED_ROWS, TAGGED_ROWS)
            go(pltpu.make_async_copy(zeros, xs_out.at[pl.ds(o, ZERO_TOKENS * TAGGED_ROWS)], zsem))
            return c

        lax.fori_loop(0, fill_ref[2 * N_EXPERTS + 1], unused, 0)

    sweep(wait=False)
    sweep(wait=True)


def _dispatch_kernel(fill_ref, dest_hbm, x_ref, xs_out, dest_even, dest_odd, zeros, sem, dsems, zsem, *, tile):
    i = pl.program_id(0)
    n_slots = tile * TOP_K

    @pl.when(i == 0)
    def _():
        _zero_fill(fill_ref, zeros, xs_out, zsem)

    def run(parity, mine, other):
        def dest_rows(step, buf, s):
            return pltpu.make_async_copy(dest_hbm.at[pl.ds(step * n_slots, n_slots)], buf, dsems.at[s])

        if parity == 0:
            @pl.when(i == 0)
            def _():
                dest_rows(0, mine, parity).start()

        dest_rows(i, mine, parity).wait()

        @pl.when(i + 1 < pl.num_programs(0))
        def _():
            dest_rows(i + 1, other, 1 - parity).start()

        def issue(t, c):
            for k in range(TOP_K):
                _token_copy(x_ref, t * TAGGED_ROWS, xs_out, mine[t * TOP_K + k], sem, TAGGED_ROWS).start(priority=k % 2)
            return c

        lax.fori_loop(0, tile, issue, 0, unroll=2)

    @pl.when(i % 2 == 0)
    def _():
        run(0, dest_even, dest_odd)

    @pl.when(i % 2 == 1)
    def _():
        run(1, dest_odd, dest_even)

    whole = xs_out.at[pl.ds(0, n_slots * TAGGED_ROWS)]
    pltpu.make_async_copy(whole, whole, sem).wait()


def _dispatch(fill, dest, xt, n_rows):
    n = xt.shape[0] // TAGGED_ROWS
    tile = next(t for t in DISPATCH_TILES if n % t == 0)
    grid_spec = pltpu.PrefetchScalarGridSpec(
        num_scalar_prefetch=1, grid=(n // tile,),
        in_specs=[pl.BlockSpec(memory_space=pl.ANY),
                  pl.BlockSpec((tile * TAGGED_ROWS, LANES), lambda i, f: (i, 0))],
        out_specs=pl.BlockSpec(memory_space=pl.ANY),
        scratch_shapes=[pltpu.SMEM((tile * TOP_K,), jnp.int32), pltpu.SMEM((tile * TOP_K,), jnp.int32),
                        pltpu.VMEM((ZERO_TOKENS * TAGGED_ROWS, LANES), xt.dtype),
                        pltpu.SemaphoreType.DMA, pltpu.SemaphoreType.DMA((2,)), pltpu.SemaphoreType.DMA])
    return pl.pallas_call(
        functools.partial(_dispatch_kernel, tile=tile),
        out_shape=jax.ShapeDtypeStruct((n_rows * TAGGED_ROWS, LANES), xt.dtype),
        grid_spec=grid_spec, compiler_params=_params(), name="moe_dispatch")(fill, dest, xt)


def _experts_kernel(te_ref, tr_ref, x_ref, wgu_ref, bgu_ref, wd_ref, bd_ref, yt_ref,
                    wgu_mxu, wd_mxu, ybuf, place_vmem, place_smem, ysems, psem, *, n_tokens):
    i = pl.program_id(0)
    slot = i % 2
    prev = 1 - slot
    tile_rows = EXPERT_TILE * ROW_TILES
    spare = yt_ref.shape[0] - 2 * tile_rows
    col = lax.broadcasted_iota(jnp.int32, (1, EXPERT_TILE), 1)

    def rows_done(s):
        whole = yt_ref.at[pl.ds(0, tile_rows)]
        pltpu.make_async_copy(whole, whole, ysems.at[s]).wait()

    def places_to_smem(first_rows, s):
        place_vmem[...] = jnp.broadcast_to(first_rows, place_vmem.shape)
        return pltpu.make_async_copy(place_vmem, place_smem.at[s], psem)

    def start_row_copy(j, s, queue=0):
        _token_copy(ybuf.at[s], j * ROW_TILES, yt_ref, place_smem[s, 0, j], ysems.at[s]).start(priority=queue)

    @pl.when(i == 0)
    def _():
        ybuf[...] = jnp.zeros(ybuf.shape, F32)
        for s in (1, 0):
            cp = pltpu.make_async_copy(ybuf.at[0], yt_ref.at[pl.ds(spare + s * tile_rows, tile_rows)], ysems.at[s])
            cp.start()
            if s == 1:
                cp.wait()
        cp = places_to_smem(spare + tile_rows + col * ROW_TILES, 1)
        cp.start()
        cp.wait()

    @pl.when(jnp.logical_or(i == 0, te_ref[i] != te_ref[jnp.maximum(i - 1, 0)]))
    def _():
        wgu_mxu[...] = wgu_ref[0, 0].astype(MXU_DTYPE)
        wd_mxu[...] = wd_ref[0, 0].astype(MXU_DTYPE)

    n_valid = tr_ref[i]
    after_last = jnp.logical_and(n_valid == 0, jnp.logical_and(i >= 1, tr_ref[jnp.maximum(i - 1, 0)] > 0))

    def compute(slot, prev):
        rows_done(slot)
        for j in range(EXPERT_TILE):
            start_row_copy(j, prev, j % 2)
        wide = x_ref[...].reshape(EXPERT_TILE, TAGGED_ROWS * LANES)
        tag = wide[:, D_MODEL:D_MODEL + LANES].astype(F32)
        lane = lax.broadcasted_iota(jnp.int32, (EXPERT_TILE, LANES), 1)
        is_id = jnp.logical_and(lane >= TAG_DIGITS, lane < TAG_DIGITS + TOP_K)
        mine = jnp.logical_and(is_id, tag == te_ref[i].astype(F32))
        weight = jnp.where(lane == 0, 1.0, jnp.where(lane == 1, 256.0, 65536.0))
        terms = jnp.where(lane < TAG_DIGITS, tag * weight, jnp.where(mine, ((lane - TAG_DIGITS) * n_tokens).astype(F32), 0.0))
        place = jnp.sum(terms, axis=-1, keepdims=True)
        hi = jnp.floor(place * (1.0 / 65536.0))
        mid = jnp.floor((place - hi * 65536.0) * (1.0 / 256.0))
        digits = jnp.where(lane == 0, place - hi * 65536.0 - mid * 256.0, jnp.where(lane == 1, mid, jnp.where(lane == 2, hi, 0.0)))
        pick = jnp.where(lax.broadcasted_iota(jnp.int32, (ROW_TILES, LANES), 0)
                         == lax.broadcasted_iota(jnp.int32, (ROW_TILES, LANES), 1), 1.0, 0.0)
        planes = _dot_nt(pick, digits)
        place_row = (planes[0:1] + 256.0 * planes[1:2] + 65536.0 * planes[2:3]).astype(jnp.int32) * ROW_TILES
        to_smem = places_to_smem(jnp.where(col < n_valid, place_row, spare + slot * tile_rows + col * ROW_TILES), slot)
        to_smem.start()

        x = wide[:, :D_MODEL]
        out = None
        for c in range(D_MODEL // HIDDEN_BLOCK):
            cols = slice(c * HIDDEN_BLOCK, (c + 1) * HIDDEN_BLOCK)
            ucols = slice(D_MODEL + c * HIDDEN_BLOCK, D_MODEL + (c + 1) * HIDDEN_BLOCK)
            gate = jnp.minimum(_dot(x, wgu_mxu[:, cols]) + bgu_ref[0, 0, :, cols], SWIGLU_LIMIT)
            up = jnp.clip(_dot(x, wgu_mxu[:, ucols]) + bgu_ref[0, 0, :, ucols], -SWIGLU_LIMIT, SWIGLU_LIMIT)
            act = (up + 1.0) * gate * jax.nn.sigmoid(SWIGLU_ALPHA * gate)
            part = _dot(act, wd_mxu[cols, :])
            out = part if out is None else out + part
        _store_token_tiles(ybuf.at[slot], out + bd_ref[0, 0])
        to_smem.wait()

    for parity in range(2):
        pl.when(jnp.logical_and(n_valid > 0, slot == parity))(functools.partial(compute, parity, 1 - parity))

    @pl.when(after_last)
    def _():
        rows_done(slot)
        def issue(j, c):
            start_row_copy(j, prev)
            return c

        lax.fori_loop(0, EXPERT_TILE, issue, 0, unroll=8)
        rows_done(prev)


def _experts(layer, tile_expert, tile_rows, xs, wgu, bgu, wd, bd, n_tokens):
    rows = xs.shape[0]
    w = lambda shape: pl.BlockSpec((1, 1) + shape, lambda i, te, tr: (layer, te[i], 0, 0))
    grid_spec = pltpu.PrefetchScalarGridSpec(
        num_scalar_prefetch=2, grid=(rows // EXPERT_TILE,),
        in_specs=[pl.BlockSpec((EXPERT_TILE, TAGGED_ROWS, LANES), lambda i, te, tr: (i, 0, 0)),
                  w((D_MODEL, 2 * D_MODEL)), w((1, 2 * D_MODEL)), w((D_MODEL, D_MODEL)), w((1, D_MODEL))],
        out_specs=pl.BlockSpec(memory_space=pl.ANY),
        scratch_shapes=[pltpu.VMEM((D_MODEL, 2 * D_MODEL), MXU_DTYPE), pltpu.VMEM((D_MODEL, D_MODEL), MXU_DTYPE),
                        pltpu.VMEM((2, EXPERT_TILE * ROW_TILES, LANES), F32),
                        pltpu.VMEM((ROW_TILES, EXPERT_TILE), jnp.int32), pltpu.SMEM((2, ROW_TILES, EXPERT_TILE), jnp.int32),
                        pltpu.SemaphoreType.DMA((2,)), pltpu.SemaphoreType.DMA])
    yt_rows = (n_tokens * TOP_K + 2 * EXPERT_TILE) * ROW_TILES
    return pl.pallas_call(
        functools.partial(_experts_kernel, n_tokens=n_tokens), out_shape=jax.ShapeDtypeStruct((yt_rows, LANES), F32),
        grid_spec=grid_spec, compiler_params=_params(), name="moe_experts")(
            tile_expert, tile_rows, xs, wgu, bgu, wd, bd)


def _combine_kernel(*refs, final, npt):
    y_refs, outs = refs[:TOP_K], refs[TOP_K + 7:]
    info_ref, h_ref, p_prompt_ref, p_sample_ref, pp_ref, pg_ref, nf_ref = refs[TOP_K:TOP_K + 7]
    i = pl.program_id(0)
    gates = info_ref[...]
    moe = None
    for k in range(TOP_K):
        rows = _load_token_tiles(y_refs[k], COMBINE_TILE)
        moe = gates[:, k:k + 1] * rows if moe is None else moe + gates[:, k:k + 1] * rows
    h2 = h_ref[...] + moe
    p = jnp.where(i < npt, p_prompt_ref[0], p_sample_ref[0])
    h3 = h2 + jax.nn.sigmoid(_dot(h2, pg_ref[...])) * _dot(p, pp_ref[...])
    if not final:
        outs[0][...] = h3
        return
    y = _rms(h3, nf_ref[...])

    @pl.when(i < npt)
    def _():
        outs[0][...] = y

    @pl.when(i >= npt)
    def _():
        outs[1][...] = y


def _combine(layer, yt, info, h1, p_prompt, p_sample, ple_proj, ple_gate, norm_final, final):
    n = h1.shape[0]
    n_p = p_prompt.shape[1]
    npt = n_p // COMBINE_TILE
    steps = n // COMBINE_TILE
    row = lambda w: pl.BlockSpec((COMBINE_TILE, w), lambda i: (i, 0))
    rank_rows = lambda k: pl.BlockSpec((COMBINE_TILE * ROW_TILES, LANES), lambda i: (k * steps + i, 0))
    prompt_rows = lambda i: jnp.minimum(i, npt - 1)
    sample_rows = lambda i: jnp.maximum(i - npt, 0)
    if final:
        out_shape = (jax.ShapeDtypeStruct((n_p, D_MODEL), F32), jax.ShapeDtypeStruct((n - n_p, D_MODEL), F32))
        out_specs = (pl.BlockSpec((COMBINE_TILE, D_MODEL), lambda i: (prompt_rows(i), 0)),
                     pl.BlockSpec((COMBINE_TILE, D_MODEL), lambda i: (sample_rows(i), 0)))
    else:
        out_shape, out_specs = jax.ShapeDtypeStruct((n, D_MODEL), F32), row(D_MODEL)
    return pl.pallas_call(
        functools.partial(_combine_kernel, final=final, npt=npt),
        out_shape=out_shape,
        grid=(steps,),
        in_specs=[rank_rows(k) for k in range(TOP_K)] + [
            row(LANES), row(D_MODEL),
            pl.BlockSpec((1, COMBINE_TILE, PLE_DIM), lambda i: (layer, prompt_rows(i), 0)),
            pl.BlockSpec((1, COMBINE_TILE, PLE_DIM), lambda i: (layer, sample_rows(i), 0)),
            _full((PLE_DIM, D_MODEL)), _full((D_MODEL, D_MODEL)), _full((1, D_MODEL))],
        out_specs=out_specs,
        compiler_params=_params(), name="moe_combine")(
            *([yt] * TOP_K), info, h1, p_prompt, p_sample, ple_proj, ple_gate, norm_final)


def _moe_and_embed(layer, h1, xt, info, counts, p_prompt, p_sample, wgu, bgu, wd, bd, ple_proj, ple_gate,
                   norm_final, final):
    n = h1.shape[0]
    n_tiles = (n * TOP_K + N_EXPERTS * (EXPERT_TILE - 1)) // EXPERT_TILE + 1
    ids = info[:, TOP_K:2 * TOP_K].astype(jnp.int32)
    rank = info[:, 2 * TOP_K:3 * TOP_K].astype(jnp.int32)
    cnt = counts[0, :N_EXPERTS].astype(jnp.int32)
    padded = ((cnt + EXPERT_TILE - 1) // EXPERT_TILE) * EXPERT_TILE
    ends = jnp.cumsum(padded)
    starts = ends - padded
    dest = ((starts[ids] + rank) * TAGGED_ROWS).reshape(-1)
    tile_start = jnp.arange(n_tiles, dtype=jnp.int32) * EXPERT_TILE
    tile_expert = jnp.minimum(jnp.sum((tile_start[:, None] >= ends[None, :]).astype(jnp.int32), axis=1), N_EXPERTS - 1)
    tile_rows = jnp.clip(cnt[tile_expert] - (tile_start - starts[tile_expert]), 0, EXPERT_TILE)
    tile_rows = jnp.where(tile_start < ends[-1], tile_rows, 0).astype(jnp.int32)
    fill = jnp.concatenate([(starts + cnt) * TAGGED_ROWS, padded - cnt,
                            jnp.stack([ends[-1] * TAGGED_ROWS,
                                       (n_tiles * EXPERT_TILE - ends[-1]) // ZERO_TOKENS])]).astype(jnp.int32)
    xs = _dispatch(fill, dest, xt.reshape(n * TAGGED_ROWS, LANES), n_tiles * EXPERT_TILE)
    yt = _experts(layer, tile_expert, tile_rows, xs.reshape(-1, TAGGED_ROWS, LANES), wgu, bgu, wd, bd, n)
    return _combine(layer, yt, info, h1, p_prompt, p_sample, ple_proj, ple_gate, norm_final, final)


def _rope_tables(prompt_len, n_sample_seq, sample_len):
    half = HEAD_DIM // 2
    inv = jnp.power(jnp.float32(ROPE_THETA), -jnp.arange(half, dtype=F32) / half)
    pos = jnp.concatenate([jnp.arange(prompt_len), jnp.tile(PAST_LEN + jnp.arange(sample_len), n_sample_seq)])
    ang = pos.astype(F32)[:, None] * inv[None, :]
    cos = jnp.tile(jnp.cos(ang), (1, LANES // half))
    sin = jnp.tile(jnp.concatenate([-jnp.sin(ang), jnp.sin(ang)], axis=1), (1, LANES // HEAD_DIM))
    return cos, sin


def kernel(x_prompt, x_sample, state_pool, state_gla, cache_k, cache_v, p_prompt, p_sample, norm_mix, norm_ffn, norm_final, w_in_even, pool_w, pool_scale, gla_w_gate, gla_b_gate, gla_norm, w_out_even, w_qkv_odd, b_qkv_odd, attn_sinks, w_out_odd, b_out_odd, w_router, b_router, w_gate_up, b_gate_up, w_down, b_down, ple_proj, ple_gate):
    bsz, t_len, _ = x_prompt.shape
    dec_bsz, dec_len, _ = x_sample.shape
    depth = norm_mix.shape[0]
    n_p, n_s = bsz * t_len, dec_bsz * dec_len
    n = n_p + n_s
    assert dec_len == CHUNK and n_s == TOKEN_TILE and t_len % TOKEN_TILE == 0 and cache_k.shape[2] == WINDOW
    sq = _Seq(bsz, t_len, dec_bsz)
    grp_p = _Group(bsz, t_len, min(SEQ_TILE, t_len), 0, 0)
    grp_s = _Group(dec_bsz, dec_len, dec_len, n_p, PAST_LEN)
    bf = lambda a: a.astype(MXU_DTYPE)
    row = lambda a: a.reshape(1, -1)

    h_parts = (x_prompt.reshape(n_p, D_MODEL), x_sample.reshape(n_s, D_MODEL))
    p_parts = (p_prompt.reshape(depth, n_p, PLE_DIM), p_sample.reshape(depth, n_s, PLE_DIM))
    b_gu = b_gate_up.reshape(depth, N_EXPERTS, 1, 2 * D_MODEL)
    b_dn = b_down.reshape(depth, N_EXPERTS, 1, D_MODEL)
    cos, sin = _rope_tables(t_len, dec_bsz, dec_len)
    tiles_per_seq = t_len // TOKEN_TILE
    table_block = lambda i: jnp.where(i < bsz * tiles_per_seq, i % tiles_per_seq, tiles_per_seq)

    pools, glas, new_k, new_v = [], [], [], []
    for i in range(depth):
        if i % 2 == 0:
            e = i // 2
            w_in = jnp.pad(bf(w_in_even[e]), ((0, 0), (0, (-IN_EVEN) % LANES)))
            wg = jnp.pad(bf(gla_w_gate[e]), ((0, w_in.shape[1] - IN_EVEN_MAIN - GLA_RANK), (0, 0)))
            proj = _even_in(h_parts, n, n_p, row(norm_mix[i]), w_in, wg, row(gla_b_gate[e]))
            pw, ps = bf(pool_w[e]), row(pool_scale[e])
            y_pool_p, pool_p = _pool(proj, jnp.zeros((bsz, 16, POOL_WIDTH), F32), pw, ps, grp_p)
            y_pool_s, pool_s = _pool(proj, jnp.pad(state_pool[e], ((0, 0), (1, 0), (0, 0))), pw, ps, grp_s)
            o_gla_p, gla_p = _gla(proj, jnp.zeros((bsz,) + state_gla.shape[2:], F32), grp_p)
            o_gla_s, gla_s = _gla(proj, state_gla[e], grp_s)
            pools.append((pool_p[:, 1:], pool_s[:, 1:]))
            glas.append((gla_p, gla_s))
            wr = jnp.pad(w_router[i], ((0, 0), (0, LANES - N_EXPERTS)))
            wr_hi = bf(wr)
            wr_lo = bf(wr - wr_hi.astype(F32))
            br = jnp.pad(row(b_router[i]), ((0, 0), (0, LANES - N_EXPERTS)), constant_values=NEG_INF)
            h1, xn, info, counts = _even_out((y_pool_p, y_pool_s), (o_gla_p, o_gla_s), proj, h_parts, n_p,
                                             row(gla_norm[e]), bf(w_out_even[e]), row(norm_ffn[i]), wr_hi, wr_lo, br)
        else:
            o = i // 2
            h = h_parts[0]
            q, kv = _odd_in(h, row(norm_mix[i]), bf(w_qkv_odd[o]), row(b_qkv_odd[o]), cos, sin, table_block)
            ck = cache_k[o].reshape(dec_bsz * WINDOW, KV_WIDTH)
            cv = cache_v[o].reshape(dec_bsz * WINDOW, KV_WIDTH)
            att = _attention(attn_sinks[o], q, kv, ck, cv, sq)
            kv_p = kv[:n_p].reshape(bsz, t_len, 2 * KV_WIDTH)[:, -WINDOW:]
            kv_s = kv[n_p:].reshape(dec_bsz, dec_len, 2 * KV_WIDTH)
            hd = (N_KV_HEADS, HEAD_DIM)
            new_k.append((kv_p[..., :KV_WIDTH].reshape(bsz, WINDOW, *hd),
                          jnp.concatenate([cache_k[o], kv_s[..., :KV_WIDTH].reshape(dec_bsz, dec_len, *hd)], axis=1)[:, -WINDOW:]))
            new_v.append((kv_p[..., KV_WIDTH:].reshape(bsz, WINDOW, *hd),
                          jnp.concatenate([cache_v[o], kv_s[..., KV_WIDTH:].reshape(dec_bsz, dec_len, *hd)], axis=1)[:, -WINDOW:]))
            wr = jnp.pad(w_router[i], ((0, 0), (0, LANES - N_EXPERTS)))
            wr_hi = bf(wr)
            wr_lo = bf(wr - wr_hi.astype(F32))
            br = jnp.pad(row(b_router[i]), ((0, 0), (0, LANES - N_EXPERTS)), constant_values=NEG_INF)
            h1, xn, info, counts = _odd_out(att, h, bf(w_out_odd[o]), row(b_out_odd[o]),
                                            row(norm_ffn[i]), wr_hi, wr_lo, br)
        final = i == depth - 1
        out = _moe_and_embed(i, h1, xn, info, counts, *p_parts, w_gate_up, b_gu, w_down, b_dn,
                             bf(ple_proj[i]), bf(ple_gate[i]), row(norm_final), final)
        h_parts = out if final else (out, out)

    y_prompt = h_parts[0].reshape(bsz, t_len, D_MODEL)
    y_sample = h_parts[1].reshape(dec_bsz, dec_len, D_MODEL)
    part = lambda pairs, j: jnp.stack([p[j] for p in pairs])
    return (y_prompt, y_sample, part(pools, 0), part(glas, 0), part(new_k, 0), part(new_v, 0),
            part(pools, 1), part(glas, 1), part(new_k, 1), part(new_v, 1))
```

```python
import functools

import jax
import jax.numpy as jnp
from jax import lax
from jax.experimental import pallas as pl
from jax.experimental.pallas import tpu as pltpu

F32 = jnp.float32
MXU_DTYPE = jnp.bfloat16

V7X_VMEM_BYTES = 64 * 1024 * 1024
VMEM_LIMIT = (V7X_VMEM_BYTES * 7) // 8
LANES = 128

D_MODEL = 1024
CHUNK = 64
PAST_LEN = 2048
PLE_DIM = 256
RMS_EPS = 1e-6
POOL_WINDOWS = (2, 4, 8, 16)
POOL_WIDTH = 512
POOL_GROUP_DIM = 128
POOL_STATE = 15
GLA_HEADS = 4
GLA_DK = 64
GLA_DV = 128
GLA_KEY_WIDTH = GLA_HEADS * GLA_DK
GLA_WIDTH = GLA_HEADS * GLA_DV
GLA_RANK = 16
GLA_TAU = 16.0
GLA_BLOCK = 16
IN_EVEN = POOL_WIDTH + 2 * GLA_KEY_WIDTH + 2 * GLA_WIDTH + GLA_RANK
IN_EVEN_MAIN = IN_EVEN - GLA_RANK
PROJ_WIDTH = IN_EVEN_MAIN + GLA_KEY_WIDTH
N_Q_HEADS = 16
N_KV_HEADS = 2
HEAD_DIM = 64
WINDOW = 128
ROPE_THETA = 10000.0
Q_WIDTH = N_Q_HEADS * HEAD_DIM
KV_WIDTH = N_KV_HEADS * HEAD_DIM
N_EXPERTS = 32
TOP_K = 4
SWIGLU_LIMIT = 7.0
SWIGLU_ALPHA = 1.702
NEG_INF = -1e30

TOKEN_TILE = 512
EXPERT_TILE = 512
COMBINE_TILE = 512
DISPATCH_TILES = (1280, TOKEN_TILE)


def _dot(a, b):
    return jnp.dot(a.astype(MXU_DTYPE), b.astype(MXU_DTYPE), preferred_element_type=F32)


def _dot_nt(a, b):
    return lax.dot_general(a.astype(MXU_DTYPE), b.astype(MXU_DTYPE), (((1,), (1,)), ((), ())),
                           preferred_element_type=F32)


def _split3(x):
    x1 = x.astype(MXU_DTYPE)
    r1 = x - x1.astype(F32)
    x2 = r1.astype(MXU_DTYPE)
    x3 = (r1 - x2.astype(F32)).astype(MXU_DTYPE)
    return x1, x2, x3


def _rms(x, g):
    return x * lax.rsqrt(jnp.mean(x * x, axis=-1, keepdims=True) + RMS_EPS) * g


ROW_TILES = D_MODEL // LANES
TAGGED_ROWS = 2 * ROW_TILES
TAG_DIGITS = 3


def _store_token_tiles(ref, x, pitch=ROW_TILES):
    for s in range(ROW_TILES):
        ref[pl.ds(s, x.shape[0], stride=pitch), :] = x[:, s * LANES:(s + 1) * LANES]


def _load_token_tiles(ref, rows, pitch=ROW_TILES):
    return jnp.concatenate([ref[pl.ds(s, rows, stride=pitch), :] for s in range(ROW_TILES)], axis=1)


def _params(n_axes=1):
    return pltpu.CompilerParams(dimension_semantics=("arbitrary",) * n_axes, vmem_limit_bytes=VMEM_LIMIT)


def _full(shape):
    return pl.BlockSpec(shape, lambda *_: (0,) * len(shape))


class _Seq:
    def __init__(self, n_prompt_seq, prompt_len, n_sample_seq):
        self.tiles_per_seq = prompt_len // CHUNK
        self.n_prompt_seq = n_prompt_seq
        self.n_prompt_tiles = n_prompt_seq * self.tiles_per_seq
        self.n_tiles = self.n_prompt_tiles + n_sample_seq
        self.n_seq = n_prompt_seq + n_sample_seq

    def is_sample(self, i):
        return i >= self.n_prompt_tiles

    def tile_in_seq(self, i):
        return jnp.where(self.is_sample(i), 0, i % self.tiles_per_seq)

    def seq(self, i):
        return jnp.where(self.is_sample(i), self.n_prompt_seq + i - self.n_prompt_tiles, i // self.tiles_per_seq)

    def last(self, i):
        return jnp.logical_or(self.is_sample(i), i % self.tiles_per_seq == self.tiles_per_seq - 1)

    def pos0(self, i):
        return jnp.where(self.is_sample(i), PAST_LEN, self.tile_in_seq(i) * CHUNK)


class _Group:
    def __init__(self, n_seq, seq_len, tile, row0, pos_base):
        assert seq_len % tile == 0 and row0 % tile == 0
        self.n_seq, self.tile, self.pos_base = n_seq, tile, pos_base
        self.tiles_per_seq = seq_len // tile
        self.n_tiles = n_seq * self.tiles_per_seq
        self.block0 = row0 // tile

    def block(self, i):
        return self.block0 + i

    def seq(self, i):
        return i // self.tiles_per_seq

    def tile_in_seq(self, i):
        return i % self.tiles_per_seq

    def last(self, i):
        return i % self.tiles_per_seq == self.tiles_per_seq - 1

    def pos0(self, i):
        return self.pos_base + self.tile_in_seq(i) * self.tile


SEQ_TILE = 256


def _split_specs(tile, width, n_prompt, same_array):
    npt = n_prompt // tile
    off = npt if same_array else 0
    return [pl.BlockSpec((tile, width), lambda i: (jnp.minimum(i, npt - 1), 0)),
            pl.BlockSpec((tile, width), lambda i: (jnp.maximum(i - npt, 0) + off, 0))]


def _split_rows(prompt_ref, sample_ref, n_prompt_tiles):
    return jnp.where(pl.program_id(0) < n_prompt_tiles, prompt_ref[...], sample_ref[...])


def _even_in_kernel(hp_ref, hs_ref, g_ref, w_ref, wg_ref, bg_ref, out_ref, *, npt):
    xn = _rms(_split_rows(hp_ref, hs_ref, npt), g_ref[...])
    proj = _dot(xn, w_ref[...])
    z = proj[:, IN_EVEN_MAIN:]
    a = _dot(z, wg_ref[...]) + bg_ref[...]
    log_alpha = (jnp.minimum(a, 0.0) - jnp.log1p(jnp.exp(-jnp.abs(a)))) * (1.0 / GLA_TAU)
    out_ref[:, :IN_EVEN_MAIN] = proj[:, :IN_EVEN_MAIN]
    out_ref[:, IN_EVEN_MAIN:] = log_alpha


def _even_in(h_parts, n, n_p, g, w_in, wg, bg):
    wp = w_in.shape[1]
    return pl.pallas_call(
        functools.partial(_even_in_kernel, npt=n_p // TOKEN_TILE),
        out_shape=jax.ShapeDtypeStruct((n, PROJ_WIDTH), F32),
        grid=(n // TOKEN_TILE,),
        in_specs=_split_specs(TOKEN_TILE, D_MODEL, n_p, h_parts[0] is h_parts[1]) + [
            _full((1, D_MODEL)), _full((D_MODEL, wp)), _full((wp - IN_EVEN_MAIN, GLA_KEY_WIDTH)),
            _full((1, GLA_KEY_WIDTH))],
        out_specs=pl.BlockSpec((TOKEN_TILE, PROJ_WIDTH), lambda i: (i, 0)),
        compiler_params=_params(), name="even_in")(*h_parts, g, w_in, wg, bg)


def _pool_kernel(u_ref, init_ref, pw_ref, ps_ref, y_ref, st_ref, buf, *, grp):
    i = pl.program_id(0)
    rows = grp.tile

    @pl.when(grp.tile_in_seq(i) == 0)
    def _():
        buf[0:16, :] = init_ref[0]

    u = u_ref[...]
    buf[16:16 + rows, :] = u
    pos = grp.pos0(i) + lax.broadcasted_iota(jnp.int32, (rows, 1), 0)
    for g, w in enumerate(POOL_WINDOWS):
        sl = slice(g * POOL_GROUP_DIM, (g + 1) * POOL_GROUP_DIM)
        acc = u[:, sl]
        for j in range(1, w):
            acc = acc + buf[16 - j:16 - j + rows, sl]
        cnt = jnp.minimum(w, pos + 1).astype(F32)
        d = acc / cnt - u[:, sl]
        y_ref[:, sl] = _dot(d, pw_ref[g]) * ps_ref[:, sl]
    tail = buf[rows:rows + 16, :]
    st_ref[0] = tail
    buf[0:16, :] = tail


def _pool(proj, init, pool_w, pool_scale, grp):
    return pl.pallas_call(
        functools.partial(_pool_kernel, grp=grp),
        out_shape=(jax.ShapeDtypeStruct((grp.n_tiles * grp.tile, POOL_WIDTH), F32),
                   jax.ShapeDtypeStruct((grp.n_seq, 16, POOL_WIDTH), F32)),
        grid=(grp.n_tiles,),
        in_specs=[pl.BlockSpec((grp.tile, POOL_WIDTH), lambda i: (grp.block(i), 0)),
                  pl.BlockSpec((1, 16, POOL_WIDTH), lambda i: (grp.seq(i), 0, 0)),
                  _full((len(POOL_WINDOWS), POOL_GROUP_DIM, POOL_GROUP_DIM)), _full((1, POOL_WIDTH))],
        out_specs=(pl.BlockSpec((grp.tile, POOL_WIDTH), lambda i: (i, 0)),
                   pl.BlockSpec((1, 16, POOL_WIDTH), lambda i: (grp.seq(i), 0, 0))),
        scratch_shapes=[pltpu.VMEM((grp.tile + 16, POOL_WIDTH), F32)],
        compiler_params=_params(), name="pool")(proj, init, pool_w, pool_scale)


def _gla_kernel(q_ref, k_ref, v_ref, g_ref, s0_ref, o_ref, sout_ref, state, before, *, grp):
    i = pl.program_id(0)
    tile = grp.tile
    n_blk = tile // GLA_BLOCK

    @pl.when(grp.tile_in_seq(i) == 0)
    def _():
        state[...] = s0_ref[0]

    g = g_ref[...]
    row = lax.broadcasted_iota(jnp.int32, (tile, tile), 0)
    col = lax.broadcasted_iota(jnp.int32, (tile, tile), 1)
    same = (row >> 4) == (col >> 4)
    causal = jnp.logical_and(same, col <= row)
    tri = jnp.where(causal, 1.0, 0.0).astype(MXU_DTYPE)
    ones = jnp.where(same, 1.0, 0.0).astype(MXU_DTYPE)
    g1, g2, g3 = _split3(g)
    b = _dot(tri, g1) + _dot(tri, g2) + _dot(tri, g3)
    b_last = _dot(ones, g1) + _dot(ones, g2) + _dot(ones, g3)
    q_t = q_ref[...] * (GLA_DK ** -0.5) * jnp.exp(b)
    k = k_ref[...]
    k_t = k * jnp.exp(-b)
    k_dec_t = (k * jnp.exp(b_last - b)).T
    sel = jnp.where((lax.broadcasted_iota(jnp.int32, (tile, LANES), 0) >> 4)
                    == lax.broadcasted_iota(jnp.int32, (tile, LANES), 1), 1.0, 0.0).astype(MXU_DTYPE)
    t1, t2, t3 = _split3(g.T)
    blk_decay = jnp.exp(_dot(t1, sel) + _dot(t2, sel) + _dot(t3, sel))
    v = v_ref[...]
    stacked = n_blk * GLA_DK
    dk_bits, blk_bits = GLA_DK.bit_length() - 1, GLA_BLOCK.bit_length() - 1
    upd_live = (lax.broadcasted_iota(jnp.int32, (stacked, tile), 0) >> dk_bits
                == lax.broadcasted_iota(jnp.int32, (stacked, tile), 1) >> blk_bits)
    qry_live = (lax.broadcasted_iota(jnp.int32, (tile, stacked), 0) >> blk_bits
                == lax.broadcasted_iota(jnp.int32, (tile, stacked), 1) >> dk_bits)
    low_half = lax.broadcasted_iota(jnp.int32, (tile, LANES), 1) < GLA_DK
    for h in range(GLA_HEADS):
        ks = slice(h * GLA_DK, (h + 1) * GLA_DK)
        vs = slice(h * GLA_DV, (h + 1) * GLA_DV)
        vh = v[:, vs]
        scores = jnp.where(causal, _dot_nt(q_t[:, ks], k_t[:, ks]), 0.0)
        o = _dot(scores, vh)
        upd = _dot(jnp.where(upd_live, jnp.concatenate([k_dec_t[ks, :]] * n_blk, axis=0), 0.0), vh)
        s = state[h]
        for j in range(n_blk):
            rows = slice(j * GLA_DK, (j + 1) * GLA_DK)
            before[rows, :] = s
            s = blk_decay[ks, j:j + 1] * s + upd[rows, :]
        state[h] = s
        pair = q_t[:, (h // 2) * LANES:(h // 2 + 1) * LANES]
        swapped = pltpu.roll(pair, GLA_DK, axis=1)
        both = jnp.where(low_half, pair, swapped) if h % 2 == 0 else jnp.where(low_half, swapped, pair)
        q_exp = jnp.where(qry_live, jnp.concatenate([both] * (stacked // LANES), axis=1), 0.0)
        o_ref[:, vs] = o + _dot(q_exp, before[...])

    @pl.when(grp.last(i))
    def _():
        sout_ref[0] = state[...]


def _gla(proj, s0, grp):
    st_shape = (grp.n_seq, GLA_HEADS, GLA_DK, GLA_DV)
    st_spec = pl.BlockSpec((1, GLA_HEADS, GLA_DK, GLA_DV), lambda i: (grp.seq(i), 0, 0, 0))
    kw = GLA_KEY_WIDTH
    cols = lambda width, c: pl.BlockSpec((grp.tile, width), lambda i: (grp.block(i), c))
    return pl.pallas_call(
        functools.partial(_gla_kernel, grp=grp),
        out_shape=(jax.ShapeDtypeStruct((grp.n_tiles * grp.tile, GLA_WIDTH), F32), jax.ShapeDtypeStruct(st_shape, F32)),
        grid=(grp.n_tiles,),
        in_specs=[cols(kw, POOL_WIDTH // kw), cols(kw, POOL_WIDTH // kw + 1),
                  cols(GLA_WIDTH, (POOL_WIDTH + 2 * kw) // GLA_WIDTH), cols(kw, IN_EVEN_MAIN // kw), st_spec],
        out_specs=(pl.BlockSpec((grp.tile, GLA_WIDTH), lambda i: (i, 0)), st_spec),
        scratch_shapes=[pltpu.VMEM((GLA_HEADS, GLA_DK, GLA_DV), F32),
                        pltpu.VMEM((grp.tile // GLA_BLOCK * GLA_DK, GLA_DV), F32)],
        compiler_params=_params(), name="gla")(proj, proj, proj, proj, s0)


def _route(h1, nffn_ref, wr_hi_ref, wr_lo_ref, br_ref, xn_ref, info_ref, cnt_ref, carry):
    tm = h1.shape[0]

    @pl.when(pl.program_id(0) == 0)
    def _():
        carry[...] = jnp.zeros_like(carry)

    xn = _rms(h1, nffn_ref[...])
    x_hi = xn.astype(MXU_DTYPE)
    xn_ref[:, :ROW_TILES, :] = x_hi.reshape(tm, ROW_TILES, LANES)
    x_lo = (xn - x_hi.astype(F32)).astype(MXU_DTYPE)
    logits = (_dot(x_hi, wr_hi_ref[...]) + _dot(x_lo, wr_hi_ref[...]) + _dot(x_hi, wr_lo_ref[...])
              + br_ref[...])
    lane = lax.broadcasted_iota(jnp.int32, (tm, LANES), 1)
    lane_f = lane.astype(F32)
    vals, ids, hots = [], [], []
    for _ in range(TOP_K):
        m = jnp.max(logits, axis=-1, keepdims=True)
        ix = jnp.min(jnp.where(logits == m, lane_f, float(LANES)), axis=-1, keepdims=True)
        hot = lane_f == ix
        vals.append(m)
        ids.append(ix)
        hots.append(hot)
        logits = jnp.where(hot, -jnp.inf, logits)
    es = [jnp.exp(v - vals[0]) for v in vals]
    den = es[0] + es[1] + es[2] + es[3]
    chosen = jnp.zeros((tm, LANES), F32)
    for hot in hots:
        chosen = chosen + jnp.where(hot, 1.0, 0.0)
    before = (lax.broadcasted_iota(jnp.int32, (tm, tm), 1) < lax.broadcasted_iota(jnp.int32, (tm, tm), 0))
    rank = _dot(jnp.where(before, 1.0, 0.0), chosen) + carry[...]
    info = jnp.zeros((tm, LANES), F32)
    for k in range(TOP_K):
        pos = jnp.sum(jnp.where(hots[k], rank, 0.0), axis=-1, keepdims=True)
        info = jnp.where(lane == k, es[k] / den, info)
        info = jnp.where(lane == TOP_K + k, ids[k].astype(F32), info)
        info = jnp.where(lane == 2 * TOP_K + k, pos, info)
    info_ref[...] = info
    token = pl.program_id(0) * tm + lax.broadcasted_iota(jnp.int32, (tm, 1), 0)
    tag = jnp.zeros((tm, LANES), F32)
    for d in range(TAG_DIGITS):
        tag = jnp.where(lane == d, ((token >> (8 * d)) & 255).astype(F32), tag)
    for k in range(TOP_K):
        tag = jnp.where(lane == TAG_DIGITS + k, ids[k].astype(F32), tag)
    tag_row = jnp.concatenate([tag, jnp.zeros((tm, D_MODEL - LANES), F32)], axis=1).astype(xn_ref.dtype)
    xn_ref[:, ROW_TILES:, :] = tag_row.reshape(tm, ROW_TILES, LANES)
    carry[...] = carry[...] + jnp.sum(chosen, axis=0, keepdims=True)
    cnt_ref[...] = carry[...]


_ROUTE_OUT_SHAPES = lambda n: (jax.ShapeDtypeStruct((n, D_MODEL), F32), jax.ShapeDtypeStruct((n, TAGGED_ROWS, LANES), MXU_DTYPE),
                               jax.ShapeDtypeStruct((n, LANES), F32), jax.ShapeDtypeStruct((1, LANES), F32))
_ROUTE_OUT_SPECS = (pl.BlockSpec((TOKEN_TILE, D_MODEL), lambda i: (i, 0)),
                    pl.BlockSpec((TOKEN_TILE, TAGGED_ROWS, LANES), lambda i: (i, 0, 0)),
                    pl.BlockSpec((TOKEN_TILE, LANES), lambda i: (i, 0)),
                    pl.BlockSpec((1, LANES), lambda i: (0, 0)))


def _route_in_specs():
    return [_full((1, D_MODEL)), _full((D_MODEL, LANES)), _full((D_MODEL, LANES)), _full((1, LANES))]


def _even_out_kernel(ypp_ref, yps_ref, op_ref, os_ref, r_ref, hp_ref, hs_ref, gn_ref, w_ref, nffn_ref, wr_hi_ref,
                     wr_lo_ref, br_ref, h1_ref, xn_ref, info_ref, cnt_ref, carry, *, npt):
    o = _split_rows(op_ref, os_ref, npt)
    r = r_ref[...]
    parts = []
    for hd in range(GLA_HEADS):
        sl = slice(hd * GLA_DV, (hd + 1) * GLA_DV)
        oh = o[:, sl]
        oh = oh * lax.rsqrt(jnp.mean(oh * oh, axis=-1, keepdims=True) + RMS_EPS) * gn_ref[...]
        rh = r[:, sl]
        parts.append(oh * (rh * jax.nn.sigmoid(rh)))
    gla = jnp.concatenate(parts, axis=1)
    mix = _dot(_split_rows(ypp_ref, yps_ref, npt), w_ref[:POOL_WIDTH, :]) + _dot(gla, w_ref[POOL_WIDTH:, :])
    h1 = _split_rows(hp_ref, hs_ref, npt) + mix
    h1_ref[...] = h1
    _route(h1, nffn_ref, wr_hi_ref, wr_lo_ref, br_ref, xn_ref, info_ref, cnt_ref, carry)


def _even_out(y_pool_parts, o_gla_parts, proj, h_parts, n_p, gla_norm, w_out, nffn, wr_hi, wr_lo, br):
    n = proj.shape[0]
    return pl.pallas_call(
        functools.partial(_even_out_kernel, npt=n_p // TOKEN_TILE),
        out_shape=_ROUTE_OUT_SHAPES(n),
        grid=(n // TOKEN_TILE,),
        in_specs=_split_specs(TOKEN_TILE, POOL_WIDTH, n_p, False) + _split_specs(TOKEN_TILE, GLA_WIDTH, n_p, False)
        + [pl.BlockSpec((TOKEN_TILE, GLA_WIDTH), lambda i: (i, (IN_EVEN_MAIN - GLA_WIDTH) // GLA_WIDTH))]
        + _split_specs(TOKEN_TILE, D_MODEL, n_p, h_parts[0] is h_parts[1])
        + [_full((1, GLA_DV)), _full((D_MODEL, D_MODEL))] + _route_in_specs(),
        out_specs=_ROUTE_OUT_SPECS,
        scratch_shapes=[pltpu.VMEM((1, LANES), F32)],
        compiler_params=_params(), name="even_out")(
            *y_pool_parts, *o_gla_parts, proj, *h_parts, gla_norm, w_out, nffn, wr_hi, wr_lo, br)


def _rope_tile(x, cos, sin, lo_half):
    swapped = jnp.where(lo_half, pltpu.roll(x, LANES - HEAD_DIM // 2, axis=1), pltpu.roll(x, HEAD_DIM // 2, axis=1))
    return x * cos + swapped * sin


def _odd_in_kernel(h_ref, g_ref, w_ref, b_ref, cos_ref, sin_ref, q_ref, kv_ref):
    xn = _rms(h_ref[...], g_ref[...])
    qkv = _dot(xn, w_ref[...]) + b_ref[...]
    cos = cos_ref[...]
    sin = sin_ref[...]
    lo_half = (lax.broadcasted_iota(jnp.int32, cos.shape, 1) % HEAD_DIM) < HEAD_DIM // 2
    for j in range(Q_WIDTH // LANES):
        sl = slice(j * LANES, (j + 1) * LANES)
        q_ref[:, sl] = _rope_tile(qkv[:, sl], cos, sin, lo_half).astype(q_ref.dtype)
    kv_ref[:, :KV_WIDTH] = _rope_tile(qkv[:, Q_WIDTH:Q_WIDTH + KV_WIDTH], cos, sin, lo_half)
    kv_ref[:, KV_WIDTH:] = qkv[:, Q_WIDTH + KV_WIDTH:]


def _odd_in(h, g, w_qkv, b_qkv, cos, sin, table_block):
    n = h.shape[0]
    wq = w_qkv.shape[1]
    tab = pl.BlockSpec((TOKEN_TILE, LANES), lambda i: (table_block(i), 0))
    return pl.pallas_call(
        _odd_in_kernel,
        out_shape=(jax.ShapeDtypeStruct((n, Q_WIDTH), MXU_DTYPE), jax.ShapeDtypeStruct((n, 2 * KV_WIDTH), F32)),
        grid=(n // TOKEN_TILE,),
        in_specs=[pl.BlockSpec((TOKEN_TILE, D_MODEL), lambda i: (i, 0)), _full((1, D_MODEL)),
                  _full((D_MODEL, wq)), _full((1, wq)), tab, tab],
        out_specs=(pl.BlockSpec((TOKEN_TILE, Q_WIDTH), lambda i: (i, 0)),
                   pl.BlockSpec((TOKEN_TILE, 2 * KV_WIDTH), lambda i: (i, 0))),
        compiler_params=_params(), name="odd_in")(h, g, w_qkv, b_qkv, cos, sin)


def _attn_kernel(sink_ref, q_ref, kv0_ref, kv1_ref, kv2_ref, ck0_ref, ck1_ref, cv0_ref, cv1_ref, o_ref, *, sq):
    i = pl.program_id(0)
    smp = sq.is_sample(i)
    t = sq.tile_in_seq(i)
    kv0 = kv0_ref[...]
    k_old = jnp.where(smp, ck0_ref[...], kv2_ref[:, :KV_WIDTH])
    k_mid = jnp.where(smp, ck1_ref[...], kv1_ref[:, :KV_WIDTH])
    v_old = jnp.where(smp, cv0_ref[...], kv2_ref[:, KV_WIDTH:])
    v_mid = jnp.where(smp, cv1_ref[...], kv1_ref[:, KV_WIDTH:])
    pad = jnp.zeros((CHUNK, KV_WIDTH), F32)
    keys = jnp.concatenate([k_old, k_mid, kv0[:, :KV_WIDTH], pad], axis=0)
    vals = jnp.concatenate([v_old, v_mid, kv0[:, KV_WIDTH:], pad], axis=0)
    n_keys = 4 * CHUNK
    lane = lax.broadcasted_iota(jnp.int32, (n_keys, KV_WIDTH), 1)
    lo = lane < HEAD_DIM
    keys_sw = pltpu.roll(keys, HEAD_DIM, axis=1)
    vals_sw = pltpu.roll(vals, HEAD_DIM, axis=1)
    kcol = lax.broadcasted_iota(jnp.int32, (1, n_keys), 1)
    first_valid = jnp.where(smp, 0, (2 - jnp.minimum(t, 2)) * CHUNK)
    key_ok = jnp.logical_and(kcol >= first_valid, kcol < 3 * CHUNK)
    n_pairs = N_Q_HEADS // N_KV_HEADS // 2
    pair_of_row = lax.broadcasted_iota(jnp.int32, (n_pairs * CHUNK, 1), 0) // CHUNK
    for g in range(N_KV_HEADS):
        own, other = (keys, keys_sw) if g == 0 else (keys_sw, keys)
        vown, vother = (vals, vals_sw) if g == 0 else (vals_sw, vals)
        kb = jnp.concatenate([jnp.where(lo, own, 0.0), jnp.where(lo, 0.0, other)], axis=0).astype(MXU_DTYPE)
        vb = jnp.concatenate([jnp.where(lo, vown, 0.0), jnp.where(lo, 0.0, vother)], axis=0).astype(MXU_DTYPE)
        pairs = [slice((g * n_pairs + pr) * LANES, (g * n_pairs + pr + 1) * LANES) for pr in range(n_pairs)]
        qg = jnp.concatenate([q_ref[:, sl] for sl in pairs], axis=0)
        s = _dot_nt(qg, kb) * (HEAD_DIM ** -0.5)
        halves = []
        for half in range(2):
            sh = jnp.where(key_ok, s[:, half * n_keys:(half + 1) * n_keys], NEG_INF)
            sink = jnp.zeros((n_pairs * CHUNK, 1), F32)
            for pr in range(n_pairs):
                sink = jnp.where(pair_of_row == pr, sink_ref[2 * (g * n_pairs + pr) + half], sink)
            m = jnp.maximum(jnp.max(sh, axis=-1, keepdims=True), sink)
            p = jnp.exp(sh - m)
            halves.append(p * (1.0 / (jnp.sum(p, axis=-1, keepdims=True) + jnp.exp(sink - m))))
        o = _dot(jnp.concatenate(halves, axis=1), vb)
        for pr, sl in enumerate(pairs):
            o_ref[:, sl] = o[pr * CHUNK:(pr + 1) * CHUNK].astype(o_ref.dtype)


def _attention(sinks, q, kv, ck, cv, sq):
    n = q.shape[0]
    npt = sq.n_prompt_tiles
    prev = lambda d: (lambda i, s: (jnp.where(sq.is_sample(i), i, jnp.maximum(i - d, 0)), 0))
    cache = lambda d: (lambda i, s: (jnp.where(sq.is_sample(i), 2 * (i - npt) + d, 0), 0))
    kvspec = lambda f: pl.BlockSpec((CHUNK, 2 * KV_WIDTH), f)
    cspec = lambda f: pl.BlockSpec((CHUNK, KV_WIDTH), f)
    grid_spec = pltpu.PrefetchScalarGridSpec(
        num_scalar_prefetch=1, grid=(sq.n_tiles,),
        in_specs=[pl.BlockSpec((CHUNK, Q_WIDTH), lambda i, s: (i, 0)),
                  kvspec(prev(0)), kvspec(prev(1)), kvspec(prev(2)),
                  cspec(cache(0)), cspec(cache(1)), cspec(cache(0)), cspec(cache(1))],
        out_specs=pl.BlockSpec((CHUNK, Q_WIDTH), lambda i, s: (i, 0)))
    return pl.pallas_call(
        functools.partial(_attn_kernel, sq=sq),
        out_shape=jax.ShapeDtypeStruct((n, Q_WIDTH), MXU_DTYPE),
        grid_spec=grid_spec, compiler_params=_params(), name="attention")(sinks, q, kv, kv, kv, ck, ck, cv, cv)


def _odd_out_kernel(o_ref, h_ref, w_ref, b_ref, nffn_ref, wr_hi_ref, wr_lo_ref, br_ref,
                    h1_ref, xn_ref, info_ref, cnt_ref, carry):
    h1 = h_ref[...] + _dot(o_ref[...], w_ref[...]) + b_ref[...]
    h1_ref[...] = h1
    _route(h1, nffn_ref, wr_hi_ref, wr_lo_ref, br_ref, xn_ref, info_ref, cnt_ref, carry)


def _odd_out(o, h, w_out, b_out, nffn, wr_hi, wr_lo, br):
    n = h.shape[0]
    row = pl.BlockSpec((TOKEN_TILE, D_MODEL), lambda i: (i, 0))
    return pl.pallas_call(
        _odd_out_kernel,
        out_shape=_ROUTE_OUT_SHAPES(n),
        grid=(n // TOKEN_TILE,),
        in_specs=[row, row, _full((D_MODEL, D_MODEL)), _full((1, D_MODEL))] + _route_in_specs(),
        out_specs=_ROUTE_OUT_SPECS,
        scratch_shapes=[pltpu.VMEM((1, LANES), F32)],
        compiler_params=_params(), name="odd_out")(o, h, w_out, b_out, nffn, wr_hi, wr_lo, br)


def _token_copy(src, s, dst, d, sem, rows=ROW_TILES):
    return pltpu.make_async_copy(src.at[pl.ds(pl.multiple_of(s, rows), rows)],
                                 dst.at[pl.ds(pl.multiple_of(d, rows), rows)], sem)


ZERO_TOKENS = EXPERT_TILE // 2
HIDDEN_BLOCK = 512


def _zero_fill(fill_ref, zeros, xs_out, zsem):
    zeros[...] = jnp.zeros_like(zeros)

    def sweep(wait):
        def go(cp):
            cp.wait() if wait else cp.start()

        def tail(e, c):
            off, length = fill_ref[e], fill_ref[N_EXPERTS + e]
            for bit in range(EXPERT_TILE.bit_length() - 1):
                rows = (1 << bit) * TAGGED_ROWS

                @pl.when((length >> bit) & 1 == 1)
                def _():
                    o = pl.multiple_of(off + (length & ((1 << bit) - 1)) * TAGGED_ROWS, TAGGED_ROWS)
                    go(pltpu.make_async_copy(zeros.at[pl.ds(0, rows)], xs_out.at[pl.ds(o, rows)], zsem))
            return c

        lax.fori_loop(0, N_EXPERTS, tail, 0)

        def unused(t, c):
            o = pl.multiple_of(fill_ref[2 * N_EXPERTS] + t * ZERO_TOKENS * TAGGED_ROWS, TAGGED_ROWS)
            go(pltpu.make_async_copy(zeros, xs_out.at[pl.ds(o, ZERO_TOKENS * TAGGED_ROWS)], zsem))
            return c

        lax.fori_loop(0, fill_ref[2 * N_EXPERTS + 1], unused, 0)

    sweep(wait=False)
    sweep(wait=True)


def _dispatch_kernel(fill_ref, dest_hbm, x_ref, xs_out, dest_even, dest_odd, zeros, sem, dsems, zsem, *, tile):
    i = pl.program_id(0)
    n_slots = tile * TOP_K

    @pl.when(i == 0)
    def _():
        _zero_fill(fill_ref, zeros, xs_out, zsem)

    def run(parity, mine, other):
        def dest_rows(step, buf, s):
            return pltpu.make_async_copy(dest_hbm.at[pl.ds(step * n_slots, n_slots)], buf, dsems.at[s])

        if parity == 0:
            @pl.when(i == 0)
            def _():
                dest_rows(0, mine, parity).start()

        dest_rows(i, mine, parity).wait()

        @pl.when(i + 1 < pl.num_programs(0))
        def _():
            dest_rows(i + 1, other, 1 - parity).start()

        def issue(t, c):
            for k in range(TOP_K):
                _token_copy(x_ref, t * TAGGED_ROWS, xs_out, mine[t * TOP_K + k], sem, TAGGED_ROWS).start(priority=k % 2)
            return c

        lax.fori_loop(0, tile, issue, 0, unroll=2)

    @pl.when(i % 2 == 0)
    def _():
        run(0, dest_even, dest_odd)

    @pl.when(i % 2 == 1)
    def _():
        run(1, dest_odd, dest_even)

    whole = xs_out.at[pl.ds(0, n_slots * TAGGED_ROWS)]
    pltpu.make_async_copy(whole, whole, sem).wait()


def _dispatch(fill, dest, xt, n_rows):
    n = xt.shape[0] // TAGGED_ROWS
    tile = next(t for t in DISPATCH_TILES if n % t == 0)
    grid_spec = pltpu.PrefetchScalarGridSpec(
        num_scalar_prefetch=1, grid=(n // tile,),
        in_specs=[pl.BlockSpec(memory_space=pl.ANY),
                  pl.BlockSpec((tile * TAGGED_ROWS, LANES), lambda i, f: (i, 0))],
        out_specs=pl.BlockSpec(memory_space=pl.ANY),
        scratch_shapes=[pltpu.SMEM((tile * TOP_K,), jnp.int32), pltpu.SMEM((tile * TOP_K,), jnp.int32),
                        pltpu.VMEM((ZERO_TOKENS * TAGGED_ROWS, LANES), xt.dtype),
                        pltpu.SemaphoreType.DMA, pltpu.SemaphoreType.DMA((2,)), pltpu.SemaphoreType.DMA])
    return pl.pallas_call(
        functools.partial(_dispatch_kernel, tile=tile),
        out_shape=jax.ShapeDtypeStruct((n_rows * TAGGED_ROWS, LANES), xt.dtype),
        grid_spec=grid_spec, compiler_params=_params(), name="moe_dispatch")(fill, dest, xt)


def _experts_kernel(te_ref, tr_ref, x_ref, wgu_ref, bgu_ref, wd_ref, bd_ref, yt_ref,
                    wgu_mxu, wd_mxu, ybuf, place_vmem, place_smem, ysems, psem, *, n_tokens):
    i = pl.program_id(0)
    slot = i % 2
    prev = 1 - slot
    tile_rows = EXPERT_TILE * ROW_TILES
    spare = yt_ref.shape[0] - 2 * tile_rows
    col = lax.broadcasted_iota(jnp.int32, (1, EXPERT_TILE), 1)

    def rows_done(s):
        whole = yt_ref.at[pl.ds(0, tile_rows)]
        pltpu.make_async_copy(whole, whole, ysems.at[s]).wait()

    def places_to_smem(first_rows, s):
        place_vmem[...] = jnp.broadcast_to(first_rows, place_vmem.shape)
        return pltpu.make_async_copy(place_vmem, place_smem.at[s], psem)

    def start_row_copy(j, s, queue=0):
        _token_copy(ybuf.at[s], j * ROW_TILES, yt_ref, place_smem[s, 0, j], ysems.at[s]).start(priority=queue)

    @pl.when(i == 0)
    def _():
        ybuf[...] = jnp.zeros(ybuf.shape, F32)
        for s in (1, 0):
            cp = pltpu.make_async_copy(ybuf.at[0], yt_ref.at[pl.ds(spare + s * tile_rows, tile_rows)], ysems.at[s])
            cp.start()
            if s == 1:
                cp.wait()
        cp = places_to_smem(spare + tile_rows + col * ROW_TILES, 1)
        cp.start()
        cp.wait()

    @pl.when(jnp.logical_or(i == 0, te_ref[i] != te_ref[jnp.maximum(i - 1, 0)]))
    def _():
        wgu_mxu[...] = wgu_ref[0, 0].astype(MXU_DTYPE)
        wd_mxu[...] = wd_ref[0, 0].astype(MXU_DTYPE)

    n_valid = tr_ref[i]
    after_last = jnp.logical_and(n_valid == 0, jnp.logical_and(i >= 1, tr_ref[jnp.maximum(i - 1, 0)] > 0))

    @pl.when(n_valid > 0)
    def _():
        rows_done(slot)
        for j in range(EXPERT_TILE):
            start_row_copy(j, prev, j % 2)
        wide = x_ref[...].reshape(EXPERT_TILE, TAGGED_ROWS * LANES)
        tag = wide[:, D_MODEL:D_MODEL + LANES].astype(F32)
        lane = lax.broadcasted_iota(jnp.int32, (EXPERT_TILE, LANES), 1)
        is_id = jnp.logical_and(lane >= TAG_DIGITS, lane < TAG_DIGITS + TOP_K)
        mine = jnp.logical_and(is_id, tag == te_ref[i].astype(F32))
        weight = jnp.where(lane == 0, 1.0, jnp.where(lane == 1, 256.0, 65536.0))
        terms = jnp.where(lane < TAG_DIGITS, tag * weight, jnp.where(mine, ((lane - TAG_DIGITS) * n_tokens).astype(F32), 0.0))
        place = jnp.sum(terms, axis=-1, keepdims=True)
        hi = jnp.floor(place * (1.0 / 65536.0))
        mid = jnp.floor((place - hi * 65536.0) * (1.0 / 256.0))
        digits = jnp.where(lane == 0, place - hi * 65536.0 - mid * 256.0, jnp.where(lane == 1, mid, jnp.where(lane == 2, hi, 0.0)))
        pick = jnp.where(lax.broadcasted_iota(jnp.int32, (ROW_TILES, LANES), 0)
                         == lax.broadcasted_iota(jnp.int32, (ROW_TILES, LANES), 1), 1.0, 0.0)
        planes = _dot_nt(pick, digits)
        place_row = (planes[0:1] + 256.0 * planes[1:2] + 65536.0 * planes[2:3]).astype(jnp.int32) * ROW_TILES
        to_smem = places_to_smem(jnp.where(col < n_valid, place_row, spare + slot * tile_rows + col * ROW_TILES), slot)
        to_smem.start()

        x = wide[:, :D_MODEL]
        out = None
        for c in range(D_MODEL // HIDDEN_BLOCK):
            cols = slice(c * HIDDEN_BLOCK, (c + 1) * HIDDEN_BLOCK)
            ucols = slice(D_MODEL + c * HIDDEN_BLOCK, D_MODEL + (c + 1) * HIDDEN_BLOCK)
            gate = jnp.minimum(_dot(x, wgu_mxu[:, cols]) + bgu_ref[0, 0, :, cols], SWIGLU_LIMIT)
            up = jnp.clip(_dot(x, wgu_mxu[:, ucols]) + bgu_ref[0, 0, :, ucols], -SWIGLU_LIMIT, SWIGLU_LIMIT)
            act = (up + 1.0) * gate * jax.nn.sigmoid(SWIGLU_ALPHA * gate)
            part = _dot(act, wd_mxu[cols, :])
            out = part if out is None else out + part
        _store_token_tiles(ybuf.at[slot], out + bd_ref[0, 0])
        to_smem.wait()

    @pl.when(after_last)
    def _():
        rows_done(slot)
        def issue(j, c):
            start_row_copy(j, prev)
            return c

        lax.fori_loop(0, EXPERT_TILE, issue, 0, unroll=8)
        rows_done(prev)


def _experts(layer, tile_expert, tile_rows, xs, wgu, bgu, wd, bd, n_tokens):
    rows = xs.shape[0]
    w = lambda shape: pl.BlockSpec((1, 1) + shape, lambda i, te, tr: (layer, te[i], 0, 0))
    grid_spec = pltpu.PrefetchScalarGridSpec(
        num_scalar_prefetch=2, grid=(rows // EXPERT_TILE,),
        in_specs=[pl.BlockSpec((EXPERT_TILE, TAGGED_ROWS, LANES), lambda i, te, tr: (i, 0, 0)),
                  w((D_MODEL, 2 * D_MODEL)), w((1, 2 * D_MODEL)), w((D_MODEL, D_MODEL)), w((1, D_MODEL))],
        out_specs=pl.BlockSpec(memory_space=pl.ANY),
        scratch_shapes=[pltpu.VMEM((D_MODEL, 2 * D_MODEL), MXU_DTYPE), pltpu.VMEM((D_MODEL, D_MODEL), MXU_DTYPE),
                        pltpu.VMEM((2, EXPERT_TILE * ROW_TILES, LANES), F32),
                        pltpu.VMEM((ROW_TILES, EXPERT_TILE), jnp.int32), pltpu.SMEM((2, ROW_TILES, EXPERT_TILE), jnp.int32),
                        pltpu.SemaphoreType.DMA((2,)), pltpu.SemaphoreType.DMA])
    yt_rows = (n_tokens * TOP_K + 2 * EXPERT_TILE) * ROW_TILES
    return pl.pallas_call(
        functools.partial(_experts_kernel, n_tokens=n_tokens), out_shape=jax.ShapeDtypeStruct((yt_rows, LANES), F32),
        grid_spec=grid_spec, compiler_params=_params(), name="moe_experts")(
            tile_expert, tile_rows, xs, wgu, bgu, wd, bd)


def _combine_kernel(*refs, final, npt):
    y_refs, outs = refs[:TOP_K], refs[TOP_K + 7:]
    info_ref, h_ref, p_prompt_ref, p_sample_ref, pp_ref, pg_ref, nf_ref = refs[TOP_K:TOP_K + 7]
    i = pl.program_id(0)
    gates = info_ref[...]
    moe = None
    for k in range(TOP_K):
        rows = _load_token_tiles(y_refs[k], COMBINE_TILE)
        moe = gates[:, k:k + 1] * rows if moe is None else moe + gates[:, k:k + 1] * rows
    h2 = h_ref[...] + moe
    p = jnp.where(i < npt, p_prompt_ref[0], p_sample_ref[0])
    h3 = h2 + jax.nn.sigmoid(_dot(h2, pg_ref[...])) * _dot(p, pp_ref[...])
    if not final:
        outs[0][...] = h3
        return
    y = _rms(h3, nf_ref[...])

    @pl.when(i < npt)
    def _():
        outs[0][...] = y

    @pl.when(i >= npt)
    def _():
        outs[1][...] = y


def _combine(layer, yt, info, h1, p_prompt, p_sample, ple_proj, ple_gate, norm_final, final):
    n = h1.shape[0]
    n_p = p_prompt.shape[1]
    npt = n_p // COMBINE_TILE
    steps = n // COMBINE_TILE
    row = lambda w: pl.BlockSpec((COMBINE_TILE, w), lambda i: (i, 0))
    rank_rows = lambda k: pl.BlockSpec((COMBINE_TILE * ROW_TILES, LANES), lambda i: (k * steps + i, 0))
    prompt_rows = lambda i: jnp.minimum(i, npt - 1)
    sample_rows = lambda i: jnp.maximum(i - npt, 0)
    if final:
        out_shape = (jax.ShapeDtypeStruct((n_p, D_MODEL), F32), jax.ShapeDtypeStruct((n - n_p, D_MODEL), F32))
        out_specs = (pl.BlockSpec((COMBINE_TILE, D_MODEL), lambda i: (prompt_rows(i), 0)),
                     pl.BlockSpec((COMBINE_TILE, D_MODEL), lambda i: (sample_rows(i), 0)))
    else:
        out_shape, out_specs = jax.ShapeDtypeStruct((n, D_MODEL), F32), row(D_MODEL)
    return pl.pallas_call(
        functools.partial(_combine_kernel, final=final, npt=npt),
        out_shape=out_shape,
        grid=(steps,),
        in_specs=[rank_rows(k) for k in range(TOP_K)] + [
            row(LANES), row(D_MODEL),
            pl.BlockSpec((1, COMBINE_TILE, PLE_DIM), lambda i: (layer, prompt_rows(i), 0)),
            pl.BlockSpec((1, COMBINE_TILE, PLE_DIM), lambda i: (layer, sample_rows(i), 0)),
            _full((PLE_DIM, D_MODEL)), _full((D_MODEL, D_MODEL)), _full((1, D_MODEL))],
        out_specs=out_specs,
        compiler_params=_params(), name="moe_combine")(
            *([yt] * TOP_K), info, h1, p_prompt, p_sample, ple_proj, ple_gate, norm_final)


def _moe_and_embed(layer, h1, xt, info, counts, p_prompt, p_sample, wgu, bgu, wd, bd, ple_proj, ple_gate,
                   norm_final, final):
    n = h1.shape[0]
    n_tiles = (n * TOP_K + N_EXPERTS * (EXPERT_TILE - 1)) // EXPERT_TILE + 1
    ids = info[:, TOP_K:2 * TOP_K].astype(jnp.int32)
    rank = info[:, 2 * TOP_K:3 * TOP_K].astype(jnp.int32)
    cnt = counts[0, :N_EXPERTS].astype(jnp.int32)
    padded = ((cnt + EXPERT_TILE - 1) // EXPERT_TILE) * EXPERT_TILE
    ends = jnp.cumsum(padded)
    starts = ends - padded
    dest = ((starts[ids] + rank) * TAGGED_ROWS).reshape(-1)
    tile_start = jnp.arange(n_tiles, dtype=jnp.int32) * EXPERT_TILE
    tile_expert = jnp.minimum(jnp.sum((tile_start[:, None] >= ends[None, :]).astype(jnp.int32), axis=1), N_EXPERTS - 1)
    tile_rows = jnp.clip(cnt[tile_expert] - (tile_start - starts[tile_expert]), 0, EXPERT_TILE)
    tile_rows = jnp.where(tile_start < ends[-1], tile_rows, 0).astype(jnp.int32)
    fill = jnp.concatenate([(starts + cnt) * TAGGED_ROWS, padded - cnt,
                            jnp.stack([ends[-1] * TAGGED_ROWS,
                                       (n_tiles * EXPERT_TILE - ends[-1]) // ZERO_TOKENS])]).astype(jnp.int32)
    xs = _dispatch(fill, dest, xt.reshape(n * TAGGED_ROWS, LANES), n_tiles * EXPERT_TILE)
    yt = _experts(layer, tile_expert, tile_rows, xs.reshape(-1, TAGGED_ROWS, LANES), wgu, bgu, wd, bd, n)
    return _combine(layer, yt, info, h1, p_prompt, p_sample, ple_proj, ple_gate, norm_final, final)


def _rope_tables(prompt_len, n_sample_seq, sample_len):
    half = HEAD_DIM // 2
    inv = jnp.power(jnp.float32(ROPE_THETA), -jnp.arange(half, dtype=F32) / half)
    pos = jnp.concatenate([jnp.arange(prompt_len), jnp.tile(PAST_LEN + jnp.arange(sample_len), n_sample_seq)])
    ang = pos.astype(F32)[:, None] * inv[None, :]
    cos = jnp.tile(jnp.cos(ang), (1, LANES // half))
    sin = jnp.tile(jnp.concatenate([-jnp.sin(ang), jnp.sin(ang)], axis=1), (1, LANES // HEAD_DIM))
    return cos, sin


def kernel(x_prompt, x_sample, state_pool, state_gla, cache_k, cache_v, p_prompt, p_sample, norm_mix, norm_ffn, norm_final, w_in_even, pool_w, pool_scale, gla_w_gate, gla_b_gate, gla_norm, w_out_even, w_qkv_odd, b_qkv_odd, attn_sinks, w_out_odd, b_out_odd, w_router, b_router, w_gate_up, b_gate_up, w_down, b_down, ple_proj, ple_gate):
    bsz, t_len, _ = x_prompt.shape
    dec_bsz, dec_len, _ = x_sample.shape
    depth = norm_mix.shape[0]
    n_p, n_s = bsz * t_len, dec_bsz * dec_len
    n = n_p + n_s
    assert dec_len == CHUNK and n_s == TOKEN_TILE and t_len % TOKEN_TILE == 0 and cache_k.shape[2] == WINDOW
    sq = _Seq(bsz, t_len, dec_bsz)
    grp_p = _Group(bsz, t_len, min(SEQ_TILE, t_len), 0, 0)
    grp_s = _Group(dec_bsz, dec_len, dec_len, n_p, PAST_LEN)
    bf = lambda a: a.astype(MXU_DTYPE)
    row = lambda a: a.reshape(1, -1)

    h_parts = (x_prompt.reshape(n_p, D_MODEL), x_sample.reshape(n_s, D_MODEL))
    p_parts = (p_prompt.reshape(depth, n_p, PLE_DIM), p_sample.reshape(depth, n_s, PLE_DIM))
    b_gu = b_gate_up.reshape(depth, N_EXPERTS, 1, 2 * D_MODEL)
    b_dn = b_down.reshape(depth, N_EXPERTS, 1, D_MODEL)
    cos, sin = _rope_tables(t_len, dec_bsz, dec_len)
    tiles_per_seq = t_len // TOKEN_TILE
    table_block = lambda i: jnp.where(i < bsz * tiles_per_seq, i % tiles_per_seq, tiles_per_seq)

    pools, glas, new_k, new_v = [], [], [], []
    for i in range(depth):
        if i % 2 == 0:
            e = i // 2
            w_in = jnp.pad(bf(w_in_even[e]), ((0, 0), (0, (-IN_EVEN) % LANES)))
            wg = jnp.pad(bf(gla_w_gate[e]), ((0, w_in.shape[1] - IN_EVEN_MAIN - GLA_RANK), (0, 0)))
            proj = _even_in(h_parts, n, n_p, row(norm_mix[i]), w_in, wg, row(gla_b_gate[e]))
            pw, ps = bf(pool_w[e]), row(pool_scale[e])
            y_pool_p, pool_p = _pool(proj, jnp.zeros((bsz, 16, POOL_WIDTH), F32), pw, ps, grp_p)
            y_pool_s, pool_s = _pool(proj, jnp.pad(state_pool[e], ((0, 0), (1, 0), (0, 0))), pw, ps, grp_s)
            o_gla_p, gla_p = _gla(proj, jnp.zeros((bsz,) + state_gla.shape[2:], F32), grp_p)
            o_gla_s, gla_s = _gla(proj, state_gla[e], grp_s)
            pools.append((pool_p[:, 1:], pool_s[:, 1:]))
            glas.append((gla_p, gla_s))
            wr = jnp.pad(w_router[i], ((0, 0), (0, LANES - N_EXPERTS)))
            wr_hi = bf(wr)
            wr_lo = bf(wr - wr_hi.astype(F32))
            br = jnp.pad(row(b_router[i]), ((0, 0), (0, LANES - N_EXPERTS)), constant_values=NEG_INF)
            h1, xn, info, counts = _even_out((y_pool_p, y_pool_s), (o_gla_p, o_gla_s), proj, h_parts, n_p,
                                             row(gla_norm[e]), bf(w_out_even[e]), row(norm_ffn[i]), wr_hi, wr_lo, br)
        else:
            o = i // 2
            h = h_parts[0]
            q, kv = _odd_in(h, row(norm_mix[i]), bf(w_qkv_odd[o]), row(b_qkv_odd[o]), cos, sin, table_block)
            ck = cache_k[o].reshape(dec_bsz * WINDOW, KV_WIDTH)
            cv = cache_v[o].reshape(dec_bsz * WINDOW, KV_WIDTH)
            att = _attention(attn_sinks[o], q, kv, ck, cv, sq)
            kv_p = kv[:n_p].reshape(bsz, t_len, 2 * KV_WIDTH)[:, -WINDOW:]
            kv_s = kv[n_p:].reshape(dec_bsz, dec_len, 2 * KV_WIDTH)
            hd = (N_KV_HEADS, HEAD_DIM)
            new_k.append((kv_p[..., :KV_WIDTH].reshape(bsz, WINDOW, *hd),
                          jnp.concatenate([cache_k[o], kv_s[..., :KV_WIDTH].reshape(dec_bsz, dec_len, *hd)], axis=1)[:, -WINDOW:]))
            new_v.append((kv_p[..., KV_WIDTH:].reshape(bsz, WINDOW, *hd),
                          jnp.concatenate([cache_v[o], kv_s[..., KV_WIDTH:].reshape(dec_bsz, dec_len, *hd)], axis=1)[:, -WINDOW:]))
            wr = jnp.pad(w_router[i], ((0, 0), (0, LANES - N_EXPERTS)))
            wr_hi = bf(wr)
            wr_lo = bf(wr - wr_hi.astype(F32))
            br = jnp.pad(row(b_router[i]), ((0, 0), (0, LANES - N_EXPERTS)), constant_values=NEG_INF)
            h1, xn, info, counts = _odd_out(att, h, bf(w_out_odd[o]), row(b_out_odd[o]),
                                            row(norm_ffn[i]), wr_hi, wr_lo, br)
        final = i == depth - 1
        out = _moe_and_embed(i, h1, xn, info, counts, *p_parts, w_gate_up, b_gu, w_down, b_dn,
                             bf(ple_proj[i]), bf(ple_gate[i]), row(norm_final), final)
        h_parts = out if final else (out, out)

    y_prompt = h_parts[0].reshape(bsz, t_len, D_MODEL)
    y_sample = h_parts[1].reshape(dec_bsz, dec_len, D_MODEL)
    part = lambda pairs, j: jnp.stack([p[j] for p in pairs])
    return (y_prompt, y_sample, part(pools, 0), part(glas, 0), part(new_k, 0), part(new_v, 0),
            part(pools, 1), part(glas, 1), part(new_k, 1), part(new_v, 1))
```
